```python
import jax
import jax.numpy as jnp
from jax import lax
import numpy as np

D_MODEL = 1024
BATCH = 8
SEQ = 2048
DEPTH = 2
DEC_BATCH = 128
DEC_SEQ = 1
PAST_LEN = 16384
PAGE_SIZE = 128

N_AB_LAYERS = (DEPTH + 1) // 2
N_C_LAYERS = DEPTH // 2
POOL_WINDOWS = (2, 4, 8, 16)
POOL_GROUPS = len(POOL_WINDOWS)
POOL_GROUP_WIDTH = D_MODEL // 8
POOL_WIDTH = POOL_GROUPS * POOL_GROUP_WIDTH
POOL_BUF = max(POOL_WINDOWS) - 1
RWKV_HEAD = 64
RWKV_WIDTH = D_MODEL // 2
RWKV_HEADS = RWKV_WIDTH // RWKV_HEAD
W_LORA = 64
A_LORA = 64
G_LORA = 128
RWKV_IN = 3 * RWKV_WIDTH + W_LORA + A_LORA + G_LORA
AB_IN = POOL_WIDTH + RWKV_IN
AB_OUT = POOL_WIDTH + RWKV_WIDTH
CHUNK = 128
SGU_WIDTH = 2 * D_MODEL
SGU_GROUPS = 4
MEM_LEN = 256
XA_HEADS = 4
XA_HEAD_DIM = D_MODEL // XA_HEADS
D_FF = 4 * D_MODEL
RMS_EPS = 1e-5
LN_EPS = 1e-5
GN_EPS = RWKV_HEAD * 1e-5
L2_EPS = 1e-12

kernel_name = 'hybrid_pool_rwkv7_gmlp_memxattn_step'


def rmsnorm(x, g):
    xf = x.astype(jnp.float32)
    y = xf * lax.rsqrt(jnp.mean(xf * xf, axis=-1, keepdims=True) + RMS_EPS)
    return (y * g).astype(x.dtype)


def layernorm(x, g, b):
    xf = x.astype(jnp.float32)
    m = jnp.mean(xf, axis=-1, keepdims=True)
    var = jnp.mean(jnp.square(xf - m), axis=-1, keepdims=True)
    return ((xf - m) * lax.rsqrt(var + LN_EPS) * g + b).astype(x.dtype)


def pool_mix(z, buf, pos0, w_group, scale):
    b, l, _ = z.shape
    full = jnp.concatenate([buf.astype(z.dtype), z], axis=1)
    cs = jnp.cumsum(full.astype(jnp.float32), axis=1)
    cs = jnp.concatenate([jnp.zeros((b, 1, POOL_WIDTH), jnp.float32), cs], axis=1)
    pos = pos0 + jnp.arange(l, dtype=jnp.int32)
    hi = cs[:, POOL_BUF + 1:POOL_BUF + 1 + l]
    pooled = []
    for gi, win in enumerate(POOL_WINDOWS):
        cols = slice(gi * POOL_GROUP_WIDTH, (gi + 1) * POOL_GROUP_WIDTH)
        lo = cs[:, POOL_BUF + 1 - win:POOL_BUF + 1 - win + l, cols]
        count = jnp.minimum(pos + 1, win).astype(jnp.float32)[None, :, None]
        pooled.append((hi[..., cols] - lo) / count)
    d = jnp.concatenate(pooled, axis=-1) - z.astype(jnp.float32)
    d = d.reshape(b, l, POOL_GROUPS, POOL_GROUP_WIDTH)
    y = jnp.einsum('blgc,gcd->blgd', d, w_group.astype(jnp.float32)).reshape(b, l, POOL_WIDTH) * scale
    return y.astype(z.dtype), full[:, -POOL_BUF:]


def rwkv7_mix(z, prev, s0, mu, w0, w2, a0, a2, g2, k_k, k_a, r_k, gn_g, gn_b):
    b, l, _ = z.shape
    zf = z.astype(jnp.float32)
    shifted = jnp.concatenate([prev.astype(jnp.float32)[:, None], zf[:, :-1]], axis=1)
    zs = zf + (shifted - zf) * mu
    w = RWKV_WIDTH
    r, k, v, wl, al, gl = jnp.split(zs, [w, 2 * w, 3 * w, 3 * w + W_LORA, 3 * w + W_LORA + A_LORA], axis=-1)
    wlog = -jax.nn.softplus(-(w0 + jnp.tanh(wl) @ w2)) - 0.5
    decay = jnp.exp(-jnp.exp(wlog))
    a = jax.nn.sigmoid(a0 + al @ a2)
    g = jax.nn.sigmoid(gl) @ g2
    kk = k * k_k
    k = k * (1.0 + (a - 1.0) * k_a)

    def heads(t):
        return t.reshape(b, l, RWKV_HEADS, RWKV_HEAD)

    rh, kh, vh, dh, ah, kkh = heads(r), heads(k), heads(v), heads(decay), heads(a), heads(kk)
    kkh = kkh / jnp.maximum(jnp.sqrt(jnp.sum(kkh * kkh, axis=-1, keepdims=True)), L2_EPS)

    def step(s, inp):
        r_t, k_t, v_t, d_t, kk_t, a_t = inp
        sa = jnp.einsum('bhvk,bhk->bhv', s, -kk_t)
        s = (s * d_t[:, :, None, :] + sa[..., None] * (kk_t * a_t)[:, :, None, :]
             + v_t[..., None] * k_t[:, :, None, :])
        return s, jnp.einsum('bhvk,bhk->bhv', s, r_t)

    xs = tuple(jnp.moveaxis(t, 1, 0) for t in (rh, kh, vh, dh, kkh, ah))
    s_last, ys = lax.scan(step, s0.astype(jnp.float32), xs)
    y = jnp.moveaxis(ys, 0, 1)
    m = jnp.mean(y, axis=-1, keepdims=True)
    var = jnp.mean(jnp.square(y - m), axis=-1, keepdims=True)
    yn = ((y - m) * lax.rsqrt(var + GN_EPS)).reshape(b, l, RWKV_WIDTH) * gn_g + gn_b
    bonus = (jnp.sum(rh * kh * r_k, axis=-1, keepdims=True) * vh).reshape(b, l, RWKV_WIDTH)
    out = (yn + bonus) * g
    return out.astype(z.dtype), z[:, -1], s_last.astype(s0.dtype)


def chunk_spatial(v, w_s, b_s):
    b, l, e = v.shape
    blk = min(l, CHUNK)
    n = -(-l // blk)
    lp = n * blk
    vp = jnp.pad(v, ((0, 0), (0, lp - l), (0, 0))).reshape(b, n, blk, SGU_GROUPS, e // SGU_GROUPS)
    wm = jnp.tril(w_s[:, :blk, :blk])
    out = jnp.einsum('gij,bcjgd->bcigd', wm, vp) + jnp.transpose(b_s[:, :blk])[None, None, :, :, None]
    return out.reshape(b, lp, e)[:, :l]


def sgu_mix(h, w_in, ln_g, ln_b, w_s, b_s, w_out):
    zc = jax.nn.gelu(h @ w_in, approximate=False)
    u, v = jnp.split(zc, 2, axis=-1)
    v = layernorm(v, ln_g, ln_b)
    return (u * chunk_spatial(v, w_s, b_s)) @ w_out, v


def memory_kv(mem, g, w_k, w_v):
    b, m, _ = mem.shape
    mn = rmsnorm(mem, g)
    return ((mn @ w_k).reshape(b, m, XA_HEADS, XA_HEAD_DIM),
            (mn @ w_v).reshape(b, m, XA_HEADS, XA_HEAD_DIM))


def cross_attn(h, k, v, w_q, w_o):
    b, l, _ = h.shape
    q = (h @ w_q).reshape(b, l, XA_HEADS, XA_HEAD_DIM)
    s = jnp.einsum('blhd,bmhd->bhlm', q, k.astype(q.dtype)).astype(jnp.float32) * (XA_HEAD_DIM ** -0.5)
    p = jax.nn.softmax(s, axis=-1).astype(q.dtype)
    o = jnp.einsum('bhlm,bmhd->blhd', p, v.astype(q.dtype)).reshape(b, l, D_MODEL)
    return o @ w_o


def sq_relu_mlp(h, w_up, w_down):
    return jnp.square(jax.nn.relu(h @ w_up)) @ w_down


def setup_inputs(seed: int = 0) -> dict:
    key = jax.random.key(seed)
    ks = iter(jax.random.split(key, 48))
    f32 = jnp.float32

    def nrm(shape, scale=1.0):
        return jax.random.normal(next(ks), shape, f32) * scale

    def gain(shape):
        return 1.0 + 0.02 * jax.random.normal(next(ks), shape, f32)

    d = D_MODEL
    return {
        'x_prompt': nrm((BATCH, SEQ, d)),
        'x_sample': nrm((DEC_BATCH, DEC_SEQ, d)),
        'mem_prompt': nrm((BATCH, MEM_LEN, d)),
        'cache_mem_k': nrm((DEPTH, DEC_BATCH, MEM_LEN, XA_HEADS, XA_HEAD_DIM)),
        'cache_mem_v': nrm((DEPTH, DEC_BATCH, MEM_LEN, XA_HEADS, XA_HEAD_DIM)),
        'state_pool': nrm((N_AB_LAYERS, DEC_BATCH, POOL_BUF, POOL_WIDTH)),
        'state_shift': nrm((N_AB_LAYERS, DEC_BATCH, RWKV_IN)),
        'state_wkv': nrm((N_AB_LAYERS, DEC_BATCH, RWKV_HEADS, RWKV_HEAD, RWKV_HEAD)),
        'norm_mix_g': gain((DEPTH, d)),
        'norm_xa_g': gain((DEPTH, d)),
        'norm_mem_g': gain((DEPTH, d)),
        'norm_ffn_g': gain((DEPTH, d)),
        'norm_final_g': gain((d,)),
        'w_in_ab': nrm((N_AB_LAYERS, d, AB_IN), d ** -0.5),
        'w_out_ab': nrm((N_AB_LAYERS, AB_OUT, d), AB_OUT ** -0.5),
        'pool_w': nrm((N_AB_LAYERS, POOL_GROUPS, POOL_GROUP_WIDTH, POOL_GROUP_WIDTH), POOL_GROUP_WIDTH ** -0.5),
        'pool_scale': gain((N_AB_LAYERS, POOL_WIDTH)),
        'rwkv_mu': jax.random.uniform(next(ks), (N_AB_LAYERS, RWKV_IN), f32),
        'rwkv_w0': nrm((N_AB_LAYERS, RWKV_WIDTH), 0.5),
        'rwkv_w2': nrm((N_AB_LAYERS, W_LORA, RWKV_WIDTH), 0.5 * W_LORA ** -0.5),
        'rwkv_a0': nrm((N_AB_LAYERS, RWKV_WIDTH), 0.1),
        'rwkv_a2': nrm((N_AB_LAYERS, A_LORA, RWKV_WIDTH), 0.5 * A_LORA ** -0.5),
        'rwkv_g2': nrm((N_AB_LAYERS, G_LORA, RWKV_WIDTH), G_LORA ** -0.5),
        'rwkv_k_k': 0.85 + nrm((N_AB_LAYERS, RWKV_WIDTH), 0.02),
        'rwkv_k_a': gain((N_AB_LAYERS, RWKV_WIDTH)),
        'rwkv_r_k': nrm((N_AB_LAYERS, RWKV_HEADS, RWKV_HEAD), 0.1),
        'rwkv_gn_g': gain((N_AB_LAYERS, RWKV_WIDTH)),
        'rwkv_gn_b': nrm((N_AB_LAYERS, RWKV_WIDTH), 0.02),
        'w_in_c': nrm((N_C_LAYERS, d, 2 * SGU_WIDTH), d ** -0.5),
        'sgu_ln_g': gain((N_C_LAYERS, SGU_WIDTH)),
        'sgu_ln_b': nrm((N_C_LAYERS, SGU_WIDTH), 0.02),
        'sgu_w_s': nrm((N_C_LAYERS, SGU_GROUPS, CHUNK, CHUNK), CHUNK ** -0.5),
        'sgu_b_s': gain((N_C_LAYERS, SGU_GROUPS, CHUNK)),
        'w_out_c': nrm((N_C_LAYERS, SGU_WIDTH, d), SGU_WIDTH ** -0.5),
        'w_xq': nrm((DEPTH, d, d), d ** -0.5),
        'w_xk': nrm((DEPTH, d, d), d ** -0.5),
        'w_xv': nrm((DEPTH, d, d), d ** -0.5),
        'w_xo': nrm((DEPTH, d, d), d ** -0.5),
        'w_ff_up': nrm((DEPTH, d, D_FF), d ** -0.5),
        'w_ff_down': nrm((DEPTH, D_FF, d), D_FF ** -0.5),
    }


def reference(x_prompt, x_sample, mem_prompt, cache_mem_k, cache_mem_v, state_pool, state_shift, state_wkv,
              norm_mix_g, norm_xa_g, norm_mem_g, norm_ffn_g, norm_final_g,
              w_in_ab, w_out_ab, pool_w, pool_scale,
              rwkv_mu, rwkv_w0, rwkv_w2, rwkv_a0, rwkv_a2, rwkv_g2, rwkv_k_k, rwkv_k_a, rwkv_r_k,
              rwkv_gn_g, rwkv_gn_b,
              w_in_c, sgu_ln_g, sgu_ln_b, sgu_w_s, sgu_b_s, w_out_c,
              w_xq, w_xk, w_xv, w_xo, w_ff_up, w_ff_down):

    def trunk(x, pos0, mem_k, mem_v, pool_in, shift_in, wkv_in):
        pool_out, shift_out, wkv_out, sgu_v_out = [], [], [], []
        for l in range(DEPTH):
            j = l // 2
            h = rmsnorm(x, norm_mix_g[l])
            if l % 2 == 0:
                z = h @ w_in_ab[j]
                y_pool, pool_new = pool_mix(z[..., :POOL_WIDTH], pool_in[j], pos0, pool_w[j], pool_scale[j])
                y_rwkv, shift_new, wkv_new = rwkv7_mix(
                    z[..., POOL_WIDTH:], shift_in[j], wkv_in[j], rwkv_mu[j], rwkv_w0[j], rwkv_w2[j],
                    rwkv_a0[j], rwkv_a2[j], rwkv_g2[j], rwkv_k_k[j], rwkv_k_a[j], rwkv_r_k[j],
                    rwkv_gn_g[j], rwkv_gn_b[j])
                mix = jnp.concatenate([y_pool, y_rwkv], axis=-1) @ w_out_ab[j]
                pool_out.append(pool_new)
                shift_out.append(shift_new)
                wkv_out.append(wkv_new)
            else:
                mix, v_rows = sgu_mix(h, w_in_c[j], sgu_ln_g[j], sgu_ln_b[j], sgu_w_s[j], sgu_b_s[j], w_out_c[j])
                sgu_v_out.append(v_rows)
            x = x + mix
            x = x + cross_attn(rmsnorm(x, norm_xa_g[l]), mem_k[l], mem_v[l], w_xq[l], w_xo[l])
            x = x + sq_relu_mlp(rmsnorm(x, norm_ffn_g[l]), w_ff_up[l], w_ff_down[l])
        return (rmsnorm(x, norm_final_g), jnp.stack(pool_out), jnp.stack(shift_out),
                jnp.stack(wkv_out), sgu_v_out)

    kv = [memory_kv(mem_prompt, norm_mem_g[l], w_xk[l], w_xv[l]) for l in range(DEPTH)]
    mem_k_prompt = jnp.stack([kv_l[0] for kv_l in kv])
    mem_v_prompt = jnp.stack([kv_l[1] for kv_l in kv])
    bp = x_prompt.shape[0]
    dt = x_prompt.dtype
    pool0 = jnp.zeros((N_AB_LAYERS, bp, POOL_BUF, POOL_WIDTH), dt)
    shift0 = jnp.zeros((N_AB_LAYERS, bp, RWKV_IN), dt)
    wkv0 = jnp.zeros((N_AB_LAYERS, bp, RWKV_HEADS, RWKV_HEAD, RWKV_HEAD), dt)
    y_prompt, pool_prompt, shift_prompt, wkv_prompt, _ = trunk(
        x_prompt, 0, mem_k_prompt, mem_v_prompt, pool0, shift0, wkv0)

    y_sample, pool_sample, shift_sample, wkv_sample, sgu_v = trunk(
        x_sample, PAST_LEN, cache_mem_k, cache_mem_v, state_pool, state_shift, state_wkv)
    sgu_v_sample = jnp.stack(sgu_v)

    return (y_prompt, y_sample, mem_k_prompt, mem_v_prompt, pool_prompt, pool_sample,
            shift_prompt, shift_sample, wkv_prompt, wkv_sample, sgu_v_sample)
```

```python
import functools

import jax
import jax.numpy as jnp
from jax import lax
from jax.experimental import pallas as pl
from jax.experimental.pallas import tpu as pltpu

D_MODEL = 1024
DEPTH = 2
PAST_LEN = 16384
POOL_WINDOWS = (2, 4, 8, 16)
POOL_GROUP_WIDTH = 128
POOL_WIDTH = 512
POOL_BUF = 15
RWKV_HEAD = 64
RWKV_WIDTH = 512
RWKV_HEADS = 8
W_LORA = 64
A_LORA = 64
G_LORA = 128
RWKV_IN = 1792
LORA_IN = W_LORA + A_LORA + G_LORA
CHUNK = 128
SGU_WIDTH = 2048
SGU_GROUPS = 4
MEM_LEN = 256
XA_HEADS = 4
XA_HEAD_DIM = 256
D_FF = 4096
RMS_EPS = 1e-5
LN_EPS = 1e-5
GN_EPS = RWKV_HEAD * 1e-5
L2_EPS = 1e-12

LANES = 128
VMEM_LIMIT = 48 * 1024 * 1024

F32 = jnp.float32
BF16 = jnp.bfloat16


def _params(*sem):
    return pltpu.CompilerParams(dimension_semantics=sem, vmem_limit_bytes=VMEM_LIMIT)


def _dot(a, b):
    return jnp.dot(a.astype(BF16), b.astype(BF16), preferred_element_type=F32)


def _dot_nt(a, b):
    return lax.dot_general(a.astype(BF16), b.astype(BF16), (((1,), (1,)), ((), ())),
                           preferred_element_type=F32)


def _rms(x, g):
    return x * lax.rsqrt(jnp.mean(x * x, axis=-1, keepdims=True) + RMS_EPS) * g


def _head_ones(width, head):
    r = lax.broadcasted_iota(jnp.int32, (width, width), 0) // head
    c = lax.broadcasted_iota(jnp.int32, (width, width), 1) // head
    return (r == c).astype(BF16)


def _head_sum(x, ones):
    hi = x.astype(BF16)
    lo = (x - hi.astype(F32)).astype(BF16)
    return (jnp.dot(hi, ones, preferred_element_type=F32)
            + jnp.dot(lo, ones, preferred_element_type=F32))


def _linear_kernel(*refs, nx, has_gain, act, has_res):
    x_refs = refs[:nx]
    pos = nx
    g_ref = refs[pos] if has_gain else None
    pos += int(has_gain)
    w_refs = refs[pos:pos + nx]
    pos += nx
    res_ref = refs[pos] if has_res else None
    pos += int(has_res)
    o_ref = refs[pos]
    if has_gain:
        xn_ref = refs[pos + 1]

        @pl.when(pl.program_id(1) == 0)
        def _():
            xn_ref[...] = _rms(x_refs[0][...].astype(F32), g_ref[...]).astype(BF16)

        lhs = [xn_ref[...]]
    else:
        lhs = [r[...] for r in x_refs]
    acc = _dot(lhs[0], w_refs[0][...])
    for l, w in zip(lhs[1:], w_refs[1:]):
        acc = acc + _dot(l, w[...])
    if act == "gelu":
        acc = 0.5 * acc * (1.0 + lax.erf(acc * 0.7071067811865476))
    if has_res:
        acc = acc + res_ref[...]
    o_ref[...] = acc.astype(o_ref.dtype)


def _linear(xs, ws, *, gain=None, act=None, res=None, out_dtype=F32, tm, tn, name):
    m = xs[0].shape[0]
    n = ws[0].shape[1]
    assert m % tm == 0 and n % tn == 0, (m, tm, n, tn)
    assert gain is None or len(xs) == 1
    in_specs = [pl.BlockSpec((tm, x.shape[1]), lambda i, j: (i, 0)) for x in xs]
    args = list(xs)
    if gain is not None:
        in_specs.append(pl.BlockSpec((1, gain.shape[-1]), lambda i, j: (0, 0)))
        args.append(gain.reshape(1, -1))
    in_specs += [pl.BlockSpec((w.shape[0], tn), lambda i, j: (0, j)) for w in ws]
    args += list(ws)
    if res is not None:
        in_specs.append(pl.BlockSpec((tm, tn), lambda i, j: (i, j)))
        args.append(res)
    scratch = [pltpu.VMEM((tm, xs[0].shape[1]), BF16)] if gain is not None else []
    kern = functools.partial(_linear_kernel, nx=len(xs), has_gain=gain is not None, act=act,
                             has_res=res is not None)
    return pl.pallas_call(
        kern,
        grid=(m // tm, n // tn),
        in_specs=in_specs,
        out_specs=pl.BlockSpec((tm, tn), lambda i, j: (i, j)),
        out_shape=jax.ShapeDtypeStruct((m, n), out_dtype),
        scratch_shapes=scratch,
        compiler_params=_params("parallel", "arbitrary"),
        name=name,
    )(*args)


def _mlp_kernel(*refs, final_norm):
    if final_norm:
        x_ref, g_ref, wu_ref, wd_ref, gf_ref, o_ref, xn_ref, acc_ref = refs
    else:
        x_ref, g_ref, wu_ref, wd_ref, o_ref, xn_ref, acc_ref = refs
    j = pl.program_id(1)

    @pl.when(j == 0)
    def _():
        xn_ref[...] = _rms(x_ref[...], g_ref[...]).astype(BF16)
        acc_ref[...] = jnp.zeros_like(acc_ref)

    h = jnp.dot(xn_ref[...], wu_ref[...], preferred_element_type=F32)
    h = jnp.square(jnp.maximum(h, 0.0)).astype(BF16)
    acc_ref[...] += jnp.dot(h, wd_ref[...], preferred_element_type=F32)

    @pl.when(j == pl.num_programs(1) - 1)
    def _():
        y = x_ref[...] + acc_ref[...]
        if final_norm:
            y = _rms(y, gf_ref[...])
        o_ref[...] = y


def _mlp(x, gain, w_up, w_down, *, final_gain=None, tm, tf):
    m, d = x.shape
    f = w_up.shape[1]
    assert m % tm == 0 and f % tf == 0
    in_specs = [
        pl.BlockSpec((tm, d), lambda i, j: (i, 0)),
        pl.BlockSpec((1, d), lambda i, j: (0, 0)),
        pl.BlockSpec((d, tf), lambda i, j: (0, j)),
        pl.BlockSpec((tf, d), lambda i, j: (j, 0)),
    ]
    args = [x, gain.reshape(1, d), w_up, w_down]
    if final_gain is not None:
        in_specs.append(pl.BlockSpec((1, d), lambda i, j: (0, 0)))
        args.append(final_gain.reshape(1, d))
    return pl.pallas_call(
        functools.partial(_mlp_kernel, final_norm=final_gain is not None),
        grid=(m // tm, f // tf),
        in_specs=in_specs,
        out_specs=pl.BlockSpec((tm, d), lambda i, j: (i, 0)),
        out_shape=jax.ShapeDtypeStruct((m, d), F32),
        scratch_shapes=[pltpu.VMEM((tm, d), BF16), pltpu.VMEM((tm, d), F32)],
        compiler_params=_params("parallel", "arbitrary"),
        name="mlp",
    )(*args)


def _xattn_prompt_kernel(x_ref, g_ref, k_ref, v_ref, wq_ref, wo_ref, o_ref, att_ref):
    x = x_ref[0]
    xn = _rms(x, g_ref[...]).astype(BF16)
    q = jnp.dot(xn, wq_ref[...], preferred_element_type=F32) * (XA_HEAD_DIM ** -0.5)
    for h in range(XA_HEADS):
        cols = slice(h * XA_HEAD_DIM, (h + 1) * XA_HEAD_DIM)
        s = _dot_nt(q[:, cols], k_ref[0, :, cols])
        s = s - jnp.max(s, axis=-1, keepdims=True)
        p = jnp.exp(s)
        p = p / jnp.sum(p, axis=-1, keepdims=True)
        att_ref[:, cols] = _dot(p, v_ref[0, :, cols]).astype(BF16)
    o_ref[0] = x + jnp.dot(att_ref[...], wo_ref[...], preferred_element_type=F32)


def _xattn_prompt(x, gain, mem_k, mem_v, w_q, w_o, *, tm):
    b, l, d = x.shape
    assert l % tm == 0
    return pl.pallas_call(
        _xattn_prompt_kernel,
        grid=(b, l // tm),
        in_specs=[
            pl.BlockSpec((1, tm, d), lambda bi, i: (bi, i, 0)),
            pl.BlockSpec((1, d), lambda bi, i: (0, 0)),
            pl.BlockSpec((1, MEM_LEN, d), lambda bi, i: (bi, 0, 0)),
            pl.BlockSpec((1, MEM_LEN, d), lambda bi, i: (bi, 0, 0)),
            pl.BlockSpec((d, d), lambda bi, i: (0, 0)),
            pl.BlockSpec((d, d), lambda bi, i: (0, 0)),
        ],
        out_specs=pl.BlockSpec((1, tm, d), lambda bi, i: (bi, i, 0)),
        out_shape=jax.ShapeDtypeStruct((b, l, d), F32),
        scratch_shapes=[pltpu.VMEM((tm, d), BF16)],
        compiler_params=_params("parallel", "parallel"),
        name="xattn_prompt",
    )(x, gain.reshape(1, d), mem_k, mem_v, w_q, w_o)


def _xattn_sample_kernel(q_ref, k_ref, v_ref, o_ref, *, bs):
    rows = 8
    for h in range(XA_HEADS):
        cols = slice(h * XA_HEAD_DIM, (h + 1) * XA_HEAD_DIM)
        q = q_ref[:, :, cols] * (XA_HEAD_DIM ** -0.5)
        q = jnp.broadcast_to(q, (bs, rows, XA_HEAD_DIM)).astype(BF16)
        k = k_ref[:, :, cols].astype(BF16)
        s = jnp.einsum("bqd,bmd->bqm", q, k, preferred_element_type=F32)
        s = s - jnp.max(s, axis=-1, keepdims=True)
        p = jnp.exp(s)
        p = p / jnp.sum(p, axis=-1, keepdims=True)
        v = v_ref[:, :, cols].astype(BF16)
        o = jnp.einsum("bqm,bmd->bqd", p.astype(BF16), v, preferred_element_type=F32)
        o_ref[:, :, cols] = o[:, 0:1, :].astype(o_ref.dtype)


def _xattn_sample_core(q, mem_k, mem_v, *, bs):
    b, _, d = q.shape
    assert b % bs == 0
    return pl.pallas_call(
        functools.partial(_xattn_sample_kernel, bs=bs),
        grid=(b // bs,),
        in_specs=[
            pl.BlockSpec((bs, 1, d), lambda i: (i, 0, 0)),
            pl.BlockSpec((bs, MEM_LEN, d), lambda i: (i, 0, 0)),
            pl.BlockSpec((bs, MEM_LEN, d), lambda i: (i, 0, 0)),
        ],
        out_specs=pl.BlockSpec((bs, 1, d), lambda i: (i, 0, 0)),
        out_shape=jax.ShapeDtypeStruct((b, 1, d), BF16),
        compiler_params=_params("parallel"),
        name="xattn_sample",
    )(q, mem_k, mem_v)


HALO = 16


def _pool_prompt_kernel(z_ref, halo_ref, w_ref, scale_ref, o_ref, ext_ref, *, tm, tiles_per_seq):
    t = pl.program_id(1)
    ext_ref[HALO:HALO + tm, :] = z_ref[0]

    @pl.when(t == 0)
    def _():
        ext_ref[0:HALO, :] = jnp.zeros((HALO, POOL_WIDTH), F32)

    @pl.when(t > 0)
    def _():
        ext_ref[0:HALO, :] = halo_ref[0]

    pos = t * tm + lax.broadcasted_iota(jnp.int32, (tm, 1), 0)
    for gi, win in enumerate(POOL_WINDOWS):
        cols = slice(gi * POOL_GROUP_WIDTH, (gi + 1) * POOL_GROUP_WIDTH)
        tok = ext_ref[HALO:HALO + tm, cols]
        acc = tok
        for k in range(1, win):
            acc = acc + ext_ref[HALO - k:HALO - k + tm, cols]
        count = jnp.minimum(pos + 1, win).astype(F32)
        dlt = acc / count - tok
        o_ref[0, :, cols] = (_dot(dlt, w_ref[gi]) * scale_ref[:, cols]).astype(o_ref.dtype)


def _pool_prompt(z, w_group, scale, *, tm):
    b, l, _ = z.shape
    assert l % tm == 0 and tm % HALO == 0
    tiles = l // tm
    hb = tm // HALO
    return pl.pallas_call(
        functools.partial(_pool_prompt_kernel, tm=tm, tiles_per_seq=tiles),
        grid=(b, tiles),
        in_specs=[
            pl.BlockSpec((1, tm, POOL_WIDTH), lambda bi, i: (bi, i, 0)),
            pl.BlockSpec((1, HALO, POOL_WIDTH), lambda bi, i: (bi, jnp.maximum(i * hb - 1, 0), 0)),
            pl.BlockSpec((len(POOL_WINDOWS), POOL_GROUP_WIDTH, POOL_GROUP_WIDTH), lambda bi, i: (0, 0, 0)),
            pl.BlockSpec((1, POOL_WIDTH), lambda bi, i: (0, 0)),
        ],
        out_specs=pl.BlockSpec((1, tm, POOL_WIDTH), lambda bi, i: (bi, i, 0)),
        out_shape=jax.ShapeDtypeStruct((b, l, POOL_WIDTH), BF16),
        scratch_shapes=[pltpu.VMEM((HALO + tm, POOL_WIDTH), F32)],
        compiler_params=_params("parallel", "parallel"),
        name="pool_prompt",
    )(z, z, w_group, scale.reshape(1, POOL_WIDTH))


def _pool_sample_kernel(z_ref, buf_ref, w_ref, scale_ref, o_ref):
    for gi, win in enumerate(POOL_WINDOWS):
        cols = slice(gi * POOL_GROUP_WIDTH, (gi + 1) * POOL_GROUP_WIDTH)
        tok = z_ref[:, cols]
        acc = tok
        for k in range(1, win):
            acc = acc + buf_ref[:, POOL_BUF - k, cols]
        count = float(min(PAST_LEN + 1, win))
        dlt = acc / count - tok
        o_ref[:, cols] = (_dot(dlt, w_ref[gi]) * scale_ref[:, cols]).astype(o_ref.dtype)


def _pool_sample(z, buf, w_group, scale):
    b = z.shape[0]
    return pl.pallas_call(
        _pool_sample_kernel,
        grid=(1,),
        in_specs=[
            pl.BlockSpec((b, POOL_WIDTH), lambda i: (0, 0)),
            pl.BlockSpec((b, POOL_BUF, POOL_WIDTH), lambda i: (0, 0, 0)),
            pl.BlockSpec((len(POOL_WINDOWS), POOL_GROUP_WIDTH, POOL_GROUP_WIDTH), lambda i: (0, 0, 0)),
            pl.BlockSpec((1, POOL_WIDTH), lambda i: (0, 0)),
        ],
        out_specs=pl.BlockSpec((b, POOL_WIDTH), lambda i: (0, 0)),
        out_shape=jax.ShapeDtypeStruct((b, POOL_WIDTH), BF16),
        compiler_params=_params("arbitrary"),
        name="pool_sample",
    )(z, buf, w_group, scale.reshape(1, POOL_WIDTH))


def _rwkv_prep_math(z4, s4, p, out_refs, sl):
    zr, zk, zv, zl = z4
    sr, sk, sv, sq = s4
    mu_r, mu_k, mu_v, mu_l, w0, a0, k_k, k_a, r_k, w_la, w_g = p
    r = zr + (sr - zr) * mu_r
    k = zk + (sk - zk) * mu_k
    v = zv + (sv - zv) * mu_v
    lo = zl + (sq - zl) * mu_l
    lane = lax.broadcasted_iota(jnp.int32, lo.shape, 1)
    feat = jnp.where(lane < W_LORA, jnp.tanh(lo), jnp.where(lane < W_LORA + A_LORA, lo, jax.nn.sigmoid(lo)))
    wa = _dot(feat[:, :W_LORA + A_LORA], w_la)
    g = _dot(feat[:, W_LORA + A_LORA:], w_g)
    wlog = -jax.nn.softplus(-(w0 + wa[:, :RWKV_WIDTH])) - 0.5
    decay = jnp.exp(-jnp.exp(wlog))
    a = jax.nn.sigmoid(a0 + wa[:, RWKV_WIDTH:])
    kk = k * k_k
    k2 = k * (1.0 + (a - 1.0) * k_a)
    ones = _head_ones(RWKV_WIDTH, RWKV_HEAD)
    kk = kk / jnp.maximum(jnp.sqrt(_head_sum(kk * kk, ones)), L2_EPS)
    bonus = _head_sum(r * k2 * r_k, ones) * v
    r_o, k_o, v_o, kap_o, b_o, d_o, g_o, bonus_o = out_refs
    r_o[sl] = r
    k_o[sl] = k2
    v_o[sl] = v
    kap_o[sl] = kk
    b_o[sl] = kk * a
    d_o[sl] = decay
    g_o[sl] = g
    bonus_o[sl] = bonus


def _load_params(refs):
    return tuple(r[...] for r in refs)


def _rwkv_prep_prompt_kernel(zr_ref, zk_ref, zv_ref, zl_ref, *refs, tm):
    p_refs = refs[:11]
    out_refs = refs[11:19]
    er_ref, ek_ref, ev_ref, el_ref = refs[19:23]
    t = pl.program_id(1)
    row = 7
    z4, s4 = [], []
    for z_ref, e_ref in ((zr_ref, er_ref), (zk_ref, ek_ref), (zv_ref, ev_ref), (zl_ref, el_ref)):
        z = z_ref[0]

        @pl.when(t == 0)
        def _():
            e_ref[0:8, :] = jnp.zeros((8, e_ref.shape[1]), F32)

        @pl.when(t > 0)
        def _():
            e_ref[row:row + 1, :] = e_ref[row + tm:row + tm + 1, :]

        e_ref[8:8 + tm, :] = z
        z4.append(z)
        s4.append(e_ref[row:row + tm, :])
    _rwkv_prep_math(z4, s4, _load_params(p_refs), out_refs, (0,))


def _prep_param_args(mu, w0, a0, k_k, k_a, r_k, w_la, w_g):
    w = RWKV_WIDTH
    row = lambda x: x.reshape(1, -1)
    return [row(mu[:w]), row(mu[w:2 * w]), row(mu[2 * w:3 * w]), row(mu[3 * w:]),
            row(w0), row(a0), row(k_k), row(k_a), row(r_k), w_la, w_g]


def _prep_out_shapes(lead):
    return [jax.ShapeDtypeStruct(lead + (RWKV_WIDTH,), F32) for _ in range(8)]


def _rwkv_prep_prompt(z, params, *, tm):
    b, l, _ = z.shape
    assert l % tm == 0
    pw = POOL_WIDTH // RWKV_WIDTH
    full = lambda a: pl.BlockSpec(a.shape, lambda bi, i: (0,) * a.ndim)
    in_specs = [
        pl.BlockSpec((1, tm, RWKV_WIDTH), lambda bi, i: (bi, i, pw)),
        pl.BlockSpec((1, tm, RWKV_WIDTH), lambda bi, i: (bi, i, pw + 1)),
        pl.BlockSpec((1, tm, RWKV_WIDTH), lambda bi, i: (bi, i, pw + 2)),
        pl.BlockSpec((1, tm, LORA_IN), lambda bi, i: (bi, i, (POOL_WIDTH + 3 * RWKV_WIDTH) // LORA_IN)),
    ] + [full(a) for a in params]
    out_spec = pl.BlockSpec((1, tm, RWKV_WIDTH), lambda bi, i: (bi, i, 0))
    return pl.pallas_call(
        functools.partial(_rwkv_prep_prompt_kernel, tm=tm),
        grid=(b, l // tm),
        in_specs=in_specs,
        out_specs=[out_spec] * 8,
        out_shape=_prep_out_shapes((b, l)),
        scratch_shapes=[pltpu.VMEM((tm + 8, RWKV_WIDTH), F32)] * 3 + [pltpu.VMEM((tm + 8, LORA_IN), F32)],
        compiler_params=_params("parallel", "arbitrary"),
        name="rwkv_prep_prompt",
    )(z, z, z, z, *params)


def _rwkv_prep_sample_kernel(zr_ref, zk_ref, zv_ref, zl_ref, sr_ref, sk_ref, sv_ref, sl_ref, *refs):
    p_refs = refs[:11]
    out_refs = refs[11:19]
    z4 = [zr_ref[...], zk_ref[...], zv_ref[...], zl_ref[...]]
    s4 = [sr_ref[...], sk_ref[...], sv_ref[...], sl_ref[...]]
    _rwkv_prep_math(z4, s4, _load_params(p_refs), out_refs, (Ellipsis,))


def _rwkv_prep_sample(z, shift, params):
    b = z.shape[0]
    pw = POOL_WIDTH // RWKV_WIDTH
    full = lambda a: pl.BlockSpec(a.shape, lambda i: (0,) * a.ndim)
    in_specs = [
        pl.BlockSpec((b, RWKV_WIDTH), lambda i: (0, pw)),
        pl.BlockSpec((b, RWKV_WIDTH), lambda i: (0, pw + 1)),
        pl.BlockSpec((b, RWKV_WIDTH), lambda i: (0, pw + 2)),
        pl.BlockSpec((b, LORA_IN), lambda i: (0, (POOL_WIDTH + 3 * RWKV_WIDTH) // LORA_IN)),
        pl.BlockSpec((b, RWKV_WIDTH), lambda i: (0, 0)),
        pl.BlockSpec((b, RWKV_WIDTH), lambda i: (0, 1)),
        pl.BlockSpec((b, RWKV_WIDTH), lambda i: (0, 2)),
        pl.BlockSpec((b, LORA_IN), lambda i: (0, 3 * RWKV_WIDTH // LORA_IN)),
    ] + [full(a) for a in params]
    out_spec = pl.BlockSpec((b, RWKV_WIDTH), lambda i: (0, 0))
    return pl.pallas_call(
        _rwkv_prep_sample_kernel,
        grid=(1,),
        in_specs=in_specs,
        out_specs=[out_spec] * 8,
        out_shape=_prep_out_shapes((b,)),
        compiler_params=_params("arbitrary"),
        name="rwkv_prep_sample",
    )(z, z, z, z, shift, shift, shift, shift, *params)


def _rwkv_scan_kernel(kap_ref, d_ref, b_ref, k_ref, r_ref, v_ref, s0_ref, y_ref, sout_ref, s_ref, *, tt, ni):
    tb = pl.program_id(1)

    @pl.when(tb == 0)
    def _():
        s_ref[...] = s0_ref[0]

    def step(t, carry):
        kap = kap_ref[0, t]
        dec = d_ref[0, t]
        bb = b_ref[0, t]
        kk = k_ref[0, t]
        rr = r_ref[0, t]
        for i in range(ni):
            s = s_ref[i]
            sa = -jnp.sum(s * kap, axis=0, keepdims=True)
            s = s * dec + sa * bb + v_ref[0, t, i:i + 1, :] * kk
            s_ref[i] = s
            y_ref[0, t, i:i + 1, :] = jnp.sum(s * rr, axis=0, keepdims=True)
        return carry

    lax.fori_loop(0, tt, step, 0)

    @pl.when(tb == pl.num_programs(1) - 1)
    def _():
        sout_ref[0] = s_ref[...]


def _rwkv_scan(kap, dec, bb, kk, rr, v, s0, *, tt):
    g, l, n, _ = kap.shape
    ni = v.shape[2]
    assert l % tt == 0
    op_spec = pl.BlockSpec((1, tt, n, LANES), lambda gi, i: (gi, i, 0, 0))
    v_spec = pl.BlockSpec((1, tt, ni, LANES), lambda gi, i: (gi, i, 0, 0))
    s_spec = pl.BlockSpec((1, ni, n, LANES), lambda gi, i: (gi, 0, 0, 0))
    return pl.pallas_call(
        functools.partial(_rwkv_scan_kernel, tt=tt, ni=ni),
        grid=(g, l // tt),
        in_specs=[op_spec] * 5 + [v_spec, s_spec],
        out_specs=[v_spec, s_spec],
        out_shape=[jax.ShapeDtypeStruct((g, l, ni, LANES), F32), jax.ShapeDtypeStruct((g, ni, n, LANES), F32)],
        scratch_shapes=[pltpu.VMEM((ni, n, LANES), F32)],
        compiler_params=_params("parallel", "arbitrary"),
        name="rwkv_scan",
    )(kap, dec, bb, kk, rr, v, s0)


def _rwkv_post_kernel(y_ref, bonus_ref, g_ref, gng_ref, gnb_ref, o_ref):
    y = y_ref[...]
    ones = _head_ones(RWKV_WIDTH, RWKV_HEAD)
    m = _head_sum(y, ones) * (1.0 / RWKV_HEAD)
    c = y - m
    var = _head_sum(c * c, ones) * (1.0 / RWKV_HEAD)
    yn = c * lax.rsqrt(var + GN_EPS) * gng_ref[...] + gnb_ref[...]
    o_ref[...] = ((yn + bonus_ref[...]) * g_ref[...]).astype(o_ref.dtype)


def _rwkv_post(y, bonus, g, gn_g, gn_b, *, tm):
    m = y.shape[0]
    assert m % tm == 0
    spec = pl.BlockSpec((tm, RWKV_WIDTH), lambda i: (i, 0))
    pspec = pl.BlockSpec((1, RWKV_WIDTH), lambda i: (0, 0))
    return pl.pallas_call(
        _rwkv_post_kernel,
        grid=(m // tm,),
        in_specs=[spec, spec, spec, pspec, pspec],
        out_specs=spec,
        out_shape=jax.ShapeDtypeStruct((m, RWKV_WIDTH), BF16),
        compiler_params=_params("parallel"),
        name="rwkv_post",
    )(y, bonus, g, gn_g.reshape(1, -1), gn_b.reshape(1, -1))


def _layernorm(v, g, b):
    m = jnp.mean(v, axis=-1, keepdims=True)
    c = v - m
    var = jnp.mean(c * c, axis=-1, keepdims=True)
    return c * lax.rsqrt(var + LN_EPS) * g + b


def _sgu_prompt_kernel(u_ref, v_ref, x_ref, lng_ref, lnb_ref, ws_ref, bias_ref, wo_ref, o_ref, vn_ref, gate_ref, *, tm):
    vn_ref[...] = _layernorm(v_ref[...].astype(F32), lng_ref[...], lnb_ref[...]).astype(BF16)
    gw = SGU_WIDTH // SGU_GROUPS
    r = lax.broadcasted_iota(jnp.int32, (CHUNK, CHUNK), 0)
    c = lax.broadcasted_iota(jnp.int32, (CHUNK, CHUNK), 1)
    for gi in range(SGU_GROUPS):
        cols = slice(gi * gw, (gi + 1) * gw)
        wm = jnp.where(r >= c, ws_ref[gi], 0.0).astype(BF16)
        for ci in range(tm // CHUNK):
            rows = slice(ci * CHUNK, (ci + 1) * CHUNK)
            sp = jnp.dot(wm, vn_ref[rows, cols], preferred_element_type=F32) + bias_ref[:, cols]
            gate_ref[rows, cols] = (u_ref[rows, cols].astype(F32) * sp).astype(BF16)
    o_ref[...] = x_ref[...] + jnp.dot(gate_ref[...], wo_ref[...], preferred_element_type=F32)


def _sgu_prompt(zc, x, ln_g, ln_b, w_s, bias, w_out, *, tm):
    t, d = x.shape
    assert t % tm == 0 and tm % CHUNK == 0
    row = lambda a: a.reshape(1, -1)
    return pl.pallas_call(
        functools.partial(_sgu_prompt_kernel, tm=tm),
        grid=(t // tm,),
        in_specs=[
            pl.BlockSpec((tm, SGU_WIDTH), lambda i: (i, 0)),
            pl.BlockSpec((tm, SGU_WIDTH), lambda i: (i, 1)),
            pl.BlockSpec((tm, d), lambda i: (i, 0)),
            pl.BlockSpec((1, SGU_WIDTH), lambda i: (0, 0)),
            pl.BlockSpec((1, SGU_WIDTH), lambda i: (0, 0)),
            pl.BlockSpec((SGU_GROUPS, CHUNK, CHUNK), lambda i: (0, 0, 0)),
            pl.BlockSpec((CHUNK, SGU_WIDTH), lambda i: (0, 0)),
            pl.BlockSpec((SGU_WIDTH, d), lambda i: (0, 0)),
        ],
        out_specs=pl.BlockSpec((tm, d), lambda i: (i, 0)),
        out_shape=jax.ShapeDtypeStruct((t, d), F32),
        scratch_shapes=[pltpu.VMEM((tm, SGU_WIDTH), BF16), pltpu.VMEM((tm, SGU_WIDTH), BF16)],
        compiler_params=_params("parallel"),
        name="sgu_prompt",
    )(zc, zc, x, row(ln_g), row(ln_b), w_s, bias, w_out)


def _sgu_sample_kernel(u_ref, v_ref, lng_ref, lnb_ref, coef_ref, bias_ref, gate_ref, vn_ref):
    vn = _layernorm(v_ref[...], lng_ref[...], lnb_ref[...])
    vn_ref[...] = vn
    gate_ref[...] = (u_ref[...] * (vn * coef_ref[...] + bias_ref[...])).astype(gate_ref.dtype)


def _sgu_sample(zc, ln_g, ln_b, coef, bias):
    b = zc.shape[0]
    row = lambda a: a.reshape(1, -1)
    pspec = pl.BlockSpec((1, SGU_WIDTH), lambda i: (0, 0))
    return pl.pallas_call(
        _sgu_sample_kernel,
        grid=(1,),
        in_specs=[pl.BlockSpec((b, SGU_WIDTH), lambda i: (0, 0)), pl.BlockSpec((b, SGU_WIDTH), lambda i: (0, 1)),
                  pspec, pspec, pspec, pspec],
        out_specs=[pl.BlockSpec((b, SGU_WIDTH), lambda i: (0, 0))] * 2,
        out_shape=[jax.ShapeDtypeStruct((b, SGU_WIDTH), BF16), jax.ShapeDtypeStruct((b, SGU_WIDTH), F32)],
        compiler_params=_params("arbitrary"),
        name="sgu_sample",
    )(zc, zc, row(ln_g), row(ln_b), row(coef), row(bias))


def _to_lanes_prompt(x):
    b, l, _ = x.shape
    x = x.reshape(b, l, RWKV_HEADS, RWKV_HEAD).transpose(1, 3, 0, 2).reshape(l, RWKV_HEAD, b * RWKV_HEADS)
    return jnp.concatenate([x, x], axis=-1)[None]


def _v_to_lanes_prompt(v):
    b, l, _ = v.shape
    v = v.reshape(b, l, RWKV_HEADS, 2, RWKV_HEAD // 2).transpose(1, 4, 3, 0, 2)
    return v.reshape(1, l, RWKV_HEAD // 2, 2 * b * RWKV_HEADS)


def _y_from_lanes_prompt(y, b):
    l = y.shape[1]
    y = y.reshape(l, RWKV_HEAD // 2, 2, b, RWKV_HEADS).transpose(3, 0, 4, 2, 1)
    return y.reshape(b, l, RWKV_WIDTH)


def _state_from_lanes_prompt(s, b):
    s = s.reshape(RWKV_HEAD // 2, RWKV_HEAD, 2, b, RWKV_HEADS).transpose(3, 4, 2, 0, 1)
    return s.reshape(b, RWKV_HEADS, RWKV_HEAD, RWKV_HEAD)


def _to_lanes_sample(x):
    b = x.shape[0]
    g = b * RWKV_HEADS // LANES
    x = x.reshape(g, LANES // RWKV_HEADS, RWKV_HEADS, RWKV_HEAD).transpose(0, 3, 1, 2)
    return x.reshape(g, 1, RWKV_HEAD, LANES)


def _from_lanes_sample(y):
    g = y.shape[0]
    y = y.reshape(g, RWKV_HEAD, LANES // RWKV_HEADS, RWKV_HEADS).transpose(0, 2, 3, 1)
    return y.reshape(g * LANES // RWKV_HEADS, RWKV_WIDTH)


def _state_to_lanes_sample(s):
    b = s.shape[0]
    g = b * RWKV_HEADS // LANES
    s = s.reshape(g, LANES // RWKV_HEADS, RWKV_HEADS, RWKV_HEAD, RWKV_HEAD).transpose(0, 3, 4, 1, 2)
    return s.reshape(g, RWKV_HEAD, RWKV_HEAD, LANES)


def _state_from_lanes_sample(s):
    g = s.shape[0]
    s = s.reshape(g, RWKV_HEAD, RWKV_HEAD, LANES // RWKV_HEADS, RWKV_HEADS).transpose(0, 3, 4, 1, 2)
    return s.reshape(g * LANES // RWKV_HEADS, RWKV_HEADS, RWKV_HEAD, RWKV_HEAD)


def kernel(x_prompt, x_sample, mem_prompt, cache_mem_k, cache_mem_v, state_pool, state_shift, state_wkv, norm_mix_g, norm_xa_g, norm_mem_g, norm_ffn_g, norm_final_g, w_in_ab, w_out_ab, pool_w, pool_scale, rwkv_mu, rwkv_w0, rwkv_w2, rwkv_a0, rwkv_a2, rwkv_g2, rwkv_k_k, rwkv_k_a, rwkv_r_k, rwkv_gn_g, rwkv_gn_b, w_in_c, sgu_ln_g, sgu_ln_b, sgu_w_s, sgu_b_s, w_out_c, w_xq, w_xk, w_xv, w_xo, w_ff_up, w_ff_down):
    bp, lp, d = x_prompt.shape
    bs = x_sample.shape[0]
    tp = bp * lp
    bf = lambda w: w.astype(BF16)

    w_in_ab_b, w_out_ab_b, pool_w_b = bf(w_in_ab), bf(w_out_ab), bf(pool_w)
    w_in_c_b, w_out_c_b = bf(w_in_c), bf(w_out_c)
    w_xq_b, w_xk_b, w_xv_b, w_xo_b = bf(w_xq), bf(w_xk), bf(w_xv), bf(w_xo)
    w_up_b, w_down_b = bf(w_ff_up), bf(w_ff_down)

    mem2 = mem_prompt.reshape(bp * MEM_LEN, d)
    mem_k_p, mem_v_p = [], []
    for l in range(DEPTH):
        mem_k_p.append(_linear([mem2], [w_xk_b[l]], gain=norm_mem_g[l], tm=1024, tn=512, name="mem_k"))
        mem_v_p.append(_linear([mem2], [w_xv_b[l]], gain=norm_mem_g[l], tm=1024, tn=512, name="mem_v"))

    xp = x_prompt.reshape(tp, d)
    xs = x_sample.reshape(bs, d)
    pool_out_p, pool_out_s, shift_out_p, shift_out_s, wkv_out_p, wkv_out_s, sgu_v_s = [], [], [], [], [], [], []

    for l in range(DEPTH):
        j = l // 2
        if l % 2 == 0:
            w_la = jnp.zeros((W_LORA + A_LORA, 2 * RWKV_WIDTH), F32)
            w_la = w_la.at[:W_LORA, :RWKV_WIDTH].set(rwkv_w2[j]).at[W_LORA:, RWKV_WIDTH:].set(rwkv_a2[j])
            params = _prep_param_args(rwkv_mu[j], rwkv_w0[j], rwkv_a0[j], rwkv_k_k[j], rwkv_k_a[j],
                                      rwkv_r_k[j].reshape(-1), bf(w_la), bf(rwkv_g2[j]))
            zp = _linear([xp], [w_in_ab_b[j]], gain=norm_mix_g[l], tm=1024, tn=256, name="ab_in_prompt")
            zp = zp.reshape(bp, lp, -1)
            y_pool = _pool_prompt(zp, pool_w_b[j], pool_scale[j], tm=512).reshape(tp, POOL_WIDTH)
            r, k2, v, kap, bb, dec, g, bonus = _rwkv_prep_prompt(zp, params, tm=512)
            s0 = jnp.zeros((1, RWKV_HEAD // 2, RWKV_HEAD, LANES), F32)
            y_l, s_l = _rwkv_scan(_to_lanes_prompt(kap), _to_lanes_prompt(dec), _to_lanes_prompt(bb),
                                  _to_lanes_prompt(k2), _to_lanes_prompt(r), _v_to_lanes_prompt(v), s0, tt=32)
            y = _y_from_lanes_prompt(y_l, bp).reshape(tp, RWKV_WIDTH)
            y_rwkv = _rwkv_post(y, bonus.reshape(tp, -1), g.reshape(tp, -1), rwkv_gn_g[j], rwkv_gn_b[j], tm=1024)
            xp = _linear([y_pool, y_rwkv], [w_out_ab_b[j][:POOL_WIDTH], w_out_ab_b[j][POOL_WIDTH:]], res=xp,
                         tm=1024, tn=512, name="ab_out_prompt")
            pool_out_p.append(zp[:, lp - POOL_BUF:, :POOL_WIDTH])
            shift_out_p.append(zp[:, lp - 1, POOL_WIDTH:])
            wkv_out_p.append(_state_from_lanes_prompt(s_l, bp))
            zs = _linear([xs], [w_in_ab_b[j]], gain=norm_mix_g[l], tm=bs, tn=256, name="ab_in_sample")
            y_pool = _pool_sample(zs, state_pool[j], pool_w_b[j], pool_scale[j])
            r, k2, v, kap, bb, dec, g, bonus = _rwkv_prep_sample(zs, state_shift[j], params)
            y_l, s_l = _rwkv_scan(_to_lanes_sample(kap), _to_lanes_sample(dec), _to_lanes_sample(bb),
                                  _to_lanes_sample(k2), _to_lanes_sample(r), _to_lanes_sample(v),
                                  _state_to_lanes_sample(state_wkv[j]), tt=1)
            y_rwkv = _rwkv_post(_from_lanes_sample(y_l), bonus, g, rwkv_gn_g[j], rwkv_gn_b[j], tm=bs)
            xs = _linear([y_pool, y_rwkv], [w_out_ab_b[j][:POOL_WIDTH], w_out_ab_b[j][POOL_WIDTH:]], res=xs,
                         tm=bs, tn=512, name="ab_out_sample")
            pool_out_s.append(jnp.concatenate([state_pool[j][:, 1:], zs[:, None, :POOL_WIDTH]], axis=1))
            shift_out_s.append(zs[:, POOL_WIDTH:])
            wkv_out_s.append(_state_from_lanes_sample(s_l))
        else:
            gw = SGU_WIDTH // SGU_GROUPS
            zc = _linear([xp], [w_in_c_b[j]], gain=norm_mix_g[l], act="gelu", out_dtype=BF16, tm=1024, tn=512,
                         name="sgu_in_prompt")
            bias = jnp.repeat(sgu_b_s[j].T, gw, axis=1)
            xp = _sgu_prompt(zc, xp, sgu_ln_g[j], sgu_ln_b[j], sgu_w_s[j], bias, w_out_c_b[j], tm=512)
            zc = _linear([xs], [w_in_c_b[j]], gain=norm_mix_g[l], act="gelu", tm=bs, tn=512, name="sgu_in_sample")
            gate, vn = _sgu_sample(zc, sgu_ln_g[j], sgu_ln_b[j], jnp.repeat(sgu_w_s[j][:, 0, 0], gw),
                                   jnp.repeat(sgu_b_s[j][:, 0], gw))
            xs = _linear([gate], [w_out_c_b[j]], res=xs, tm=bs, tn=512, name="sgu_out_sample")
            sgu_v_s.append(vn.reshape(bs, 1, SGU_WIDTH))

        xp = _xattn_prompt(xp.reshape(bp, lp, d), norm_xa_g[l], mem_k_p[l].reshape(bp, MEM_LEN, d),
                           mem_v_p[l].reshape(bp, MEM_LEN, d), w_xq_b[l], w_xo_b[l], tm=512).reshape(tp, d)
        q = _linear([xs], [w_xq_b[l]], gain=norm_xa_g[l], tm=bs, tn=512, name="xattn_q_sample")
        att = _xattn_sample_core(q.reshape(bs, 1, d), cache_mem_k[l].reshape(bs, MEM_LEN, d),
                                 cache_mem_v[l].reshape(bs, MEM_LEN, d), bs=4)
        xs = _linear([att.reshape(bs, d)], [w_xo_b[l]], res=xs, tm=bs, tn=512, name="xattn_o_sample")

        fg = norm_final_g if l == DEPTH - 1 else None
        xp = _mlp(xp, norm_ffn_g[l], w_up_b[l], w_down_b[l], final_gain=fg, tm=1024, tf=512)
        xs = _mlp(xs, norm_ffn_g[l], w_up_b[l], w_down_b[l], final_gain=fg, tm=bs, tf=512)

    head_shape = (bp, MEM_LEN, XA_HEADS, XA_HEAD_DIM)
    return (xp.reshape(bp, lp, d),
            xs.reshape(bs, 1, d),
            jnp.stack([m.reshape(head_shape) for m in mem_k_p]),
            jnp.stack([m.reshape(head_shape) for m in mem_v_p]),
            jnp.stack(pool_out_p),
            jnp.stack(pool_out_s),
            jnp.stack(shift_out_p),
            jnp.stack(shift_out_s),
            jnp.stack(wkv_out_p),
            jnp.stack(wkv_out_s),
            jnp.stack(sgu_v_s))
```

```python
import functools

import jax
import jax.numpy as jnp
from jax import lax
from jax.experimental import pallas as pl
from jax.experimental.pallas import tpu as pltpu

D_MODEL = 1024
DEPTH = 2
PAST_LEN = 16384
POOL_WINDOWS = (2, 4, 8, 16)
POOL_GROUP_WIDTH = 128
POOL_WIDTH = 512
POOL_BUF = 15
RWKV_HEAD = 64
RWKV_WIDTH = 512
RWKV_HEADS = 8
W_LORA = 64
A_LORA = 64
G_LORA = 128
RWKV_IN = 1792
LORA_IN = W_LORA + A_LORA + G_LORA
CHUNK = 128
SGU_WIDTH = 2048
SGU_GROUPS = 4
MEM_LEN = 256
XA_HEADS = 4
XA_HEAD_DIM = 256
D_FF = 4096
RMS_EPS = 1e-5
LN_EPS = 1e-5
GN_EPS = RWKV_HEAD * 1e-5
L2_EPS = 1e-12

LANES = 128
F32_SUBLANES = 8
VMEM_LIMIT = 48 * 1024 * 1024

F32 = jnp.float32
BF16 = jnp.bfloat16


def _params(*sem):
    return pltpu.CompilerParams(dimension_semantics=sem, vmem_limit_bytes=VMEM_LIMIT)


def _dot(a, b):
    return jnp.dot(a.astype(BF16), b.astype(BF16), preferred_element_type=F32)


def _dot_nt(a, b):
    return lax.dot_general(a.astype(BF16), b.astype(BF16), (((1,), (1,)), ((), ())),
                           preferred_element_type=F32)


def _rms(x, g):
    return x * lax.rsqrt(jnp.mean(x * x, axis=-1, keepdims=True) + RMS_EPS) * g


def _head_ones(width, head):
    r = lax.broadcasted_iota(jnp.int32, (width, width), 0) // head
    c = lax.broadcasted_iota(jnp.int32, (width, width), 1) // head
    return (r == c).astype(BF16)


def _head_sum(x, ones):
    hi = x.astype(BF16)
    lo = (x - hi.astype(F32)).astype(BF16)
    return (jnp.dot(hi, ones, preferred_element_type=F32)
            + jnp.dot(lo, ones, preferred_element_type=F32))


def _linear_kernel(*refs, nx, has_gain, act, has_res):
    x_refs = refs[:nx]
    pos = nx
    g_ref = refs[pos] if has_gain else None
    pos += int(has_gain)
    w_refs = refs[pos:pos + nx]
    pos += nx
    res_ref = refs[pos] if has_res else None
    pos += int(has_res)
    o_ref = refs[pos]
    if has_gain:
        xn_ref = refs[pos + 1]

        @pl.when(pl.program_id(1) == 0)
        def _():
            xn_ref[...] = _rms(x_refs[0][...].astype(F32), g_ref[...]).astype(BF16)

        lhs = [xn_ref[...]]
    else:
        lhs = [r[...] for r in x_refs]
    acc = _dot(lhs[0], w_refs[0][...])
    for l, w in zip(lhs[1:], w_refs[1:]):
        acc = acc + _dot(l, w[...])
    if act == "gelu":
        acc = 0.5 * acc * (1.0 + lax.erf(acc * 0.7071067811865476))
    if has_res:
        acc = acc + res_ref[...]
    o_ref[...] = acc.astype(o_ref.dtype)


def _linear(xs, ws, *, gain=None, act=None, res=None, out_dtype=F32, tm, tn, name):
    m = xs[0].shape[0]
    n = ws[0].shape[1]
    assert m % tm == 0 and n % tn == 0, (m, tm, n, tn)
    assert gain is None or len(xs) == 1
    in_specs = [pl.BlockSpec((tm, x.shape[1]), lambda i, j: (i, 0)) for x in xs]
    args = list(xs)
    if gain is not None:
        in_specs.append(pl.BlockSpec((1, gain.shape[-1]), lambda i, j: (0, 0)))
        args.append(gain.reshape(1, -1))
    in_specs += [pl.BlockSpec((w.shape[0], tn), lambda i, j: (0, j)) for w in ws]
    args += list(ws)
    if res is not None:
        in_specs.append(pl.BlockSpec((tm, tn), lambda i, j: (i, j)))
        args.append(res)
    scratch = [pltpu.VMEM((tm, xs[0].shape[1]), BF16)] if gain is not None else []
    kern = functools.partial(_linear_kernel, nx=len(xs), has_gain=gain is not None, act=act,
                             has_res=res is not None)
    return pl.pallas_call(
        kern,
        grid=(m // tm, n // tn),
        in_specs=in_specs,
        out_specs=pl.BlockSpec((tm, tn), lambda i, j: (i, j)),
        out_shape=jax.ShapeDtypeStruct((m, n), out_dtype),
        scratch_shapes=scratch,
        compiler_params=_params("parallel", "arbitrary"),
        name=name,
    )(*args)


def _mlp_kernel(*refs, final_norm):
    if final_norm:
        x_ref, g_ref, wu_ref, wd_ref, gf_ref, o_ref, xn_ref, acc_ref = refs
    else:
        x_ref, g_ref, wu_ref, wd_ref, o_ref, xn_ref, acc_ref = refs
    j = pl.program_id(1)

    @pl.when(j == 0)
    def _():
        xn_ref[...] = _rms(x_ref[...], g_ref[...]).astype(BF16)
        acc_ref[...] = jnp.zeros_like(acc_ref)

    h = jnp.dot(xn_ref[...], wu_ref[...], preferred_element_type=F32)
    h = jnp.square(jnp.maximum(h, 0.0)).astype(BF16)
    acc_ref[...] += jnp.dot(h, wd_ref[...], preferred_element_type=F32)

    @pl.when(j == pl.num_programs(1) - 1)
    def _():
        y = x_ref[...] + acc_ref[...]
        if final_norm:
            y = _rms(y, gf_ref[...])
        o_ref[...] = y


def _mlp(x, gain, w_up, w_down, *, final_gain=None, tm, tf):
    m, d = x.shape
    f = w_up.shape[1]
    assert m % tm == 0 and f % tf == 0
    in_specs = [
        pl.BlockSpec((tm, d), lambda i, j: (i, 0)),
        pl.BlockSpec((1, d), lambda i, j: (0, 0)),
        pl.BlockSpec((d, tf), lambda i, j: (0, j)),
        pl.BlockSpec((tf, d), lambda i, j: (j, 0)),
    ]
    args = [x, gain.reshape(1, d), w_up, w_down]
    if final_gain is not None:
        in_specs.append(pl.BlockSpec((1, d), lambda i, j: (0, 0)))
        args.append(final_gain.reshape(1, d))
    return pl.pallas_call(
        functools.partial(_mlp_kernel, final_norm=final_gain is not None),
        grid=(m // tm, f // tf),
        in_specs=in_specs,
        out_specs=pl.BlockSpec((tm, d), lambda i, j: (i, 0)),
        out_shape=jax.ShapeDtypeStruct((m, d), F32),
        scratch_shapes=[pltpu.VMEM((tm, d), BF16), pltpu.VMEM((tm, d), F32)],
        compiler_params=_params("parallel", "arbitrary"),
        name="mlp",
    )(*args)


def _xattn_prompt_kernel(x_ref, g_ref, k_ref, v_ref, wq_ref, wo_ref, o_ref, att_ref):
    x = x_ref[0]
    xn = _rms(x, g_ref[...]).astype(BF16)
    q = jnp.dot(xn, wq_ref[...], preferred_element_type=F32) * (XA_HEAD_DIM ** -0.5)
    for h in range(XA_HEADS):
        cols = slice(h * XA_HEAD_DIM, (h + 1) * XA_HEAD_DIM)
        s = _dot_nt(q[:, cols], k_ref[0, :, cols])
        s = s - jnp.max(s, axis=-1, keepdims=True)
        p = jnp.exp(s)
        p = p / jnp.sum(p, axis=-1, keepdims=True)
        att_ref[:, cols] = _dot(p, v_ref[0, :, cols]).astype(BF16)
    o_ref[0] = x + jnp.dot(att_ref[...], wo_ref[...], preferred_element_type=F32)


def _xattn_prompt(x, gain, mem_k, mem_v, w_q, w_o, *, tm):
    b, l, d = x.shape
    assert l % tm == 0
    return pl.pallas_call(
        _xattn_prompt_kernel,
        grid=(b, l // tm),
        in_specs=[
            pl.BlockSpec((1, tm, d), lambda bi, i: (bi, i, 0)),
            pl.BlockSpec((1, d), lambda bi, i: (0, 0)),
            pl.BlockSpec((1, MEM_LEN, d), lambda bi, i: (bi, 0, 0)),
            pl.BlockSpec((1, MEM_LEN, d), lambda bi, i: (bi, 0, 0)),
            pl.BlockSpec((d, d), lambda bi, i: (0, 0)),
            pl.BlockSpec((d, d), lambda bi, i: (0, 0)),
        ],
        out_specs=pl.BlockSpec((1, tm, d), lambda bi, i: (bi, i, 0)),
        out_shape=jax.ShapeDtypeStruct((b, l, d), F32),
        scratch_shapes=[pltpu.VMEM((tm, d), BF16)],
        compiler_params=_params("parallel", "parallel"),
        name="xattn_prompt",
    )(x, gain.reshape(1, d), mem_k, mem_v, w_q, w_o)


def _xattn_sample_kernel(q_ref, k_ref, v_ref, o_ref, *, bs):
    for h in range(XA_HEADS):
        cols = slice(h * XA_HEAD_DIM, (h + 1) * XA_HEAD_DIM)
        q = q_ref[:, :, cols] * (XA_HEAD_DIM ** -0.5)
        q = jnp.broadcast_to(q, (bs, F32_SUBLANES, XA_HEAD_DIM)).astype(BF16)
        k = k_ref[0, :, :, h, :].astype(BF16)
        s = jnp.einsum("bqd,bmd->bqm", q, k, preferred_element_type=F32)
        s = s - jnp.max(s, axis=-1, keepdims=True)
        p = jnp.exp(s)
        p = p / jnp.sum(p, axis=-1, keepdims=True)
        v = v_ref[0, :, :, h, :].astype(BF16)
        o = jnp.einsum("bqm,bmd->bqd", p.astype(BF16), v, preferred_element_type=F32)
        o_ref[:, :, cols] = o[:, 0:1, :].astype(o_ref.dtype)


def _xattn_sample_core(q, cache_k, cache_v, layer, *, bs):
    b, _, d = q.shape
    assert b % bs == 0
    kv_spec = pl.BlockSpec((1, bs, MEM_LEN, XA_HEADS, XA_HEAD_DIM), lambda i: (layer, i, 0, 0, 0))
    return pl.pallas_call(
        functools.partial(_xattn_sample_kernel, bs=bs),
        grid=(b // bs,),
        in_specs=[pl.BlockSpec((bs, 1, d), lambda i: (i, 0, 0)), kv_spec, kv_spec],
        out_specs=pl.BlockSpec((bs, 1, d), lambda i: (i, 0, 0)),
        out_shape=jax.ShapeDtypeStruct((b, 1, d), BF16),
        compiler_params=_params("parallel"),
        name="xattn_sample",
    )(q, cache_k, cache_v)


HALO = 16


def _pool_prompt_kernel(z_ref, halo_ref, w_ref, scale_ref, o_ref, ext_ref, *, tm, tiles_per_seq):
    t = pl.program_id(1)
    ext_ref[HALO:HALO + tm, :] = z_ref[0]

    @pl.when(t == 0)
    def _():
        ext_ref[0:HALO, :] = jnp.zeros((HALO, POOL_WIDTH), F32)

    @pl.when(t > 0)
    def _():
        ext_ref[0:HALO, :] = halo_ref[0]

    pos = t * tm + lax.broadcasted_iota(jnp.int32, (tm, 1), 0)
    for gi, win in enumerate(POOL_WINDOWS):
        cols = slice(gi * POOL_GROUP_WIDTH, (gi + 1) * POOL_GROUP_WIDTH)
        tok = ext_ref[HALO:HALO + tm, cols]
        acc = tok
        for k in range(1, win):
            acc = acc + ext_ref[HALO - k:HALO - k + tm, cols]
        count = jnp.minimum(pos + 1, win).astype(F32)
        dlt = acc / count - tok
        o_ref[0, :, cols] = (_dot(dlt, w_ref[gi]) * scale_ref[:, cols]).astype(o_ref.dtype)


def _pool_prompt(z, w_group, scale, *, tm):
    b, l, _ = z.shape
    assert l % tm == 0 and tm % HALO == 0
    tiles = l // tm
    hb = tm // HALO
    return pl.pallas_call(
        functools.partial(_pool_prompt_kernel, tm=tm, tiles_per_seq=tiles),
        grid=(b, tiles),
        in_specs=[
            pl.BlockSpec((1, tm, POOL_WIDTH), lambda bi, i: (bi, i, 0)),
            pl.BlockSpec((1, HALO, POOL_WIDTH), lambda bi, i: (bi, jnp.maximum(i * hb - 1, 0), 0)),
            pl.BlockSpec((len(POOL_WINDOWS), POOL_GROUP_WIDTH, POOL_GROUP_WIDTH), lambda bi, i: (0, 0, 0)),
            pl.BlockSpec((1, POOL_WIDTH), lambda bi, i: (0, 0)),
        ],
        out_specs=pl.BlockSpec((1, tm, POOL_WIDTH), lambda bi, i: (bi, i, 0)),
        out_shape=jax.ShapeDtypeStruct((b, l, POOL_WIDTH), BF16),
        scratch_shapes=[pltpu.VMEM((HALO + tm, POOL_WIDTH), F32)],
        compiler_params=_params("parallel", "parallel"),
        name="pool_prompt",
    )(z, z, w_group, scale.reshape(1, POOL_WIDTH))


def _pool_sample_kernel(z_ref, buf_ref, w_ref, scale_ref, o_ref):
    for gi, win in enumerate(POOL_WINDOWS):
        cols = slice(gi * POOL_GROUP_WIDTH, (gi + 1) * POOL_GROUP_WIDTH)
        tok = z_ref[:, cols]
        acc = tok
        for k in range(1, win):
            acc = acc + buf_ref[:, POOL_BUF - k, cols]
        count = float(min(PAST_LEN + 1, win))
        dlt = acc / count - tok
        o_ref[:, cols] = (_dot(dlt, w_ref[gi]) * scale_ref[:, cols]).astype(o_ref.dtype)


def _pool_sample(z, buf, w_group, scale):
    b = z.shape[0]
    return pl.pallas_call(
        _pool_sample_kernel,
        grid=(1,),
        in_specs=[
            pl.BlockSpec((b, POOL_WIDTH), lambda i: (0, 0)),
            pl.BlockSpec((b, POOL_BUF, POOL_WIDTH), lambda i: (0, 0, 0)),
            pl.BlockSpec((len(POOL_WINDOWS), POOL_GROUP_WIDTH, POOL_GROUP_WIDTH), lambda i: (0, 0, 0)),
            pl.BlockSpec((1, POOL_WIDTH), lambda i: (0, 0)),
        ],
        out_specs=pl.BlockSpec((b, POOL_WIDTH), lambda i: (0, 0)),
        out_shape=jax.ShapeDtypeStruct((b, POOL_WIDTH), BF16),
        compiler_params=_params("arbitrary"),
        name="pool_sample",
    )(z, buf, w_group, scale.reshape(1, POOL_WIDTH))


def _rwkv_prep_math(z4, s4, p, out_refs, sl):
    zr, zk, zv, zl = z4
    sr, sk, sv, sq = s4
    mu_r, mu_k, mu_v, mu_l, w0, a0, k_k, k_a, r_k, w_la, w_g = p
    r = zr + (sr - zr) * mu_r
    k = zk + (sk - zk) * mu_k
    v = zv + (sv - zv) * mu_v
    lo = zl + (sq - zl) * mu_l
    lane = lax.broadcasted_iota(jnp.int32, lo.shape, 1)
    feat = jnp.where(lane < W_LORA, jnp.tanh(lo), jnp.where(lane < W_LORA + A_LORA, lo, jax.nn.sigmoid(lo)))
    wa = _dot(feat[:, :W_LORA + A_LORA], w_la)
    g = _dot(feat[:, W_LORA + A_LORA:], w_g)
    wlog = -jax.nn.softplus(-(w0 + wa[:, :RWKV_WIDTH])) - 0.5
    decay = jnp.exp(-jnp.exp(wlog))
    a = jax.nn.sigmoid(a0 + wa[:, RWKV_WIDTH:])
    kk = k * k_k
    k2 = k * (1.0 + (a - 1.0) * k_a)
    ones = _head_ones(RWKV_WIDTH, RWKV_HEAD)
    kk = kk / jnp.maximum(jnp.sqrt(_head_sum(kk * kk, ones)), L2_EPS)
    bonus = _head_sum(r * k2 * r_k, ones) * v
    r_o, k_o, v_o, kap_o, b_o, d_o, g_o, bonus_o = out_refs
    r_o[sl] = r
    k_o[sl] = k2
    v_o[sl] = v
    kap_o[sl] = kk
    b_o[sl] = kk * a
    d_o[sl] = decay
    g_o[sl] = g
    bonus_o[sl] = bonus


def _load_params(refs):
    return tuple(r[...] for r in refs)


def _rwkv_prep_prompt_kernel(zr_ref, zk_ref, zv_ref, zl_ref, *refs, tm):
    p_refs = refs[:11]
    out_refs = refs[11:19]
    er_ref, ek_ref, ev_ref, el_ref = refs[19:23]
    t = pl.program_id(1)
    row = 7
    z4, s4 = [], []
    for z_ref, e_ref in ((zr_ref, er_ref), (zk_ref, ek_ref), (zv_ref, ev_ref), (zl_ref, el_ref)):
        z = z_ref[0]

        @pl.when(t == 0)
        def _():
            e_ref[0:8, :] = jnp.zeros((8, e_ref.shape[1]), F32)

        @pl.when(t > 0)
        def _():
            e_ref[row:row + 1, :] = e_ref[row + tm:row + tm + 1, :]

        e_ref[8:8 + tm, :] = z
        z4.append(z)
        s4.append(e_ref[row:row + tm, :])
    _rwkv_prep_math(z4, s4, _load_params(p_refs), out_refs, (0,))


def _prep_param_args(mu, w0, a0, k_k, k_a, r_k, w_la, w_g):
    w = RWKV_WIDTH
    row = lambda x: x.reshape(1, -1)
    return [row(mu[:w]), row(mu[w:2 * w]), row(mu[2 * w:3 * w]), row(mu[3 * w:]),
            row(w0), row(a0), row(k_k), row(k_a), row(r_k), w_la, w_g]


def _prep_out_shapes(lead):
    return [jax.ShapeDtypeStruct(lead + (RWKV_WIDTH,), F32) for _ in range(8)]


def _rwkv_prep_prompt(z, params, *, tm):
    b, l, _ = z.shape
    assert l % tm == 0
    pw = POOL_WIDTH // RWKV_WIDTH
    full = lambda a: pl.BlockSpec(a.shape, lambda bi, i: (0,) * a.ndim)
    in_specs = [
        pl.BlockSpec((1, tm, RWKV_WIDTH), lambda bi, i: (bi, i, pw)),
        pl.BlockSpec((1, tm, RWKV_WIDTH), lambda bi, i: (bi, i, pw + 1)),
        pl.BlockSpec((1, tm, RWKV_WIDTH), lambda bi, i: (bi, i, pw + 2)),
        pl.BlockSpec((1, tm, LORA_IN), lambda bi, i: (bi, i, (POOL_WIDTH + 3 * RWKV_WIDTH) // LORA_IN)),
    ] + [full(a) for a in params]
    out_spec = pl.BlockSpec((1, tm, RWKV_WIDTH), lambda bi, i: (bi, i, 0))
    return pl.pallas_call(
        functools.partial(_rwkv_prep_prompt_kernel, tm=tm),
        grid=(b, l // tm),
        in_specs=in_specs,
        out_specs=[out_spec] * 8,
        out_shape=_prep_out_shapes((b, l)),
        scratch_shapes=[pltpu.VMEM((tm + 8, RWKV_WIDTH), F32)] * 3 + [pltpu.VMEM((tm + 8, LORA_IN), F32)],
        compiler_params=_params("parallel", "arbitrary"),
        name="rwkv_prep_prompt",
    )(z, z, z, z, *params)


def _rwkv_prep_sample_kernel(zr_ref, zk_ref, zv_ref, zl_ref, sr_ref, sk_ref, sv_ref, sl_ref, *refs):
    p_refs = refs[:11]
    out_refs = refs[11:19]
    z4 = [zr_ref[...], zk_ref[...], zv_ref[...], zl_ref[...]]
    s4 = [sr_ref[...], sk_ref[...], sv_ref[...], sl_ref[...]]
    _rwkv_prep_math(z4, s4, _load_params(p_refs), out_refs, (Ellipsis,))


def _rwkv_prep_sample(z, shift, params):
    b = z.shape[0]
    pw = POOL_WIDTH // RWKV_WIDTH
    full = lambda a: pl.BlockSpec(a.shape, lambda i: (0,) * a.ndim)
    in_specs = [
        pl.BlockSpec((b, RWKV_WIDTH), lambda i: (0, pw)),
        pl.BlockSpec((b, RWKV_WIDTH), lambda i: (0, pw + 1)),
        pl.BlockSpec((b, RWKV_WIDTH), lambda i: (0, pw + 2)),
        pl.BlockSpec((b, LORA_IN), lambda i: (0, (POOL_WIDTH + 3 * RWKV_WIDTH) // LORA_IN)),
        pl.BlockSpec((b, RWKV_WIDTH), lambda i: (0, 0)),
        pl.BlockSpec((b, RWKV_WIDTH), lambda i: (0, 1)),
        pl.BlockSpec((b, RWKV_WIDTH), lambda i: (0, 2)),
        pl.BlockSpec((b, LORA_IN), lambda i: (0, 3 * RWKV_WIDTH // LORA_IN)),
    ] + [full(a) for a in params]
    out_spec = pl.BlockSpec((b, RWKV_WIDTH), lambda i: (0, 0))
    return pl.pallas_call(
        _rwkv_prep_sample_kernel,
        grid=(1,),
        in_specs=in_specs,
        out_specs=[out_spec] * 8,
        out_shape=_prep_out_shapes((b,)),
        compiler_params=_params("arbitrary"),
        name="rwkv_prep_sample",
    )(z, z, z, z, shift, shift, shift, shift, *params)


def _rwkv_scan_kernel(kap_ref, d_ref, b_ref, k_ref, r_ref, v_ref, s0_ref, y_ref, sout_ref, s_ref, *, tt, ni):
    tb = pl.program_id(1)

    @pl.when(tb == 0)
    def _():
        s_ref[...] = s0_ref[0]

    def step(t, carry):
        kap = kap_ref[0, t]
        dec = d_ref[0, t]
        bb = b_ref[0, t]
        kk = k_ref[0, t]
        rr = r_ref[0, t]
        for i in range(ni):
            s = s_ref[i]
            sa = -jnp.sum(s * kap, axis=0, keepdims=True)
            s = s * dec + sa * bb + v_ref[0, t, i:i + 1, :] * kk
            s_ref[i] = s
            y_ref[0, t, i:i + 1, :] = jnp.sum(s * rr, axis=0, keepdims=True)
        return carry

    lax.fori_loop(0, tt, step, 0)

    @pl.when(tb == pl.num_programs(1) - 1)
    def _():
        sout_ref[0] = s_ref[...]


def _rwkv_scan(kap, dec, bb, kk, rr, v, s0, *, tt):
    g, l, n, _ = kap.shape
    ni = v.shape[2]
    assert l % tt == 0
    op_spec = pl.BlockSpec((1, tt, n, LANES), lambda gi, i: (gi, i, 0, 0))
    v_spec = pl.BlockSpec((1, tt, ni, LANES), lambda gi, i: (gi, i, 0, 0))
    s_spec = pl.BlockSpec((1, ni, n, LANES), lambda gi, i: (gi, 0, 0, 0))
    return pl.pallas_call(
        functools.partial(_rwkv_scan_kernel, tt=tt, ni=ni),
        grid=(g, l // tt),
        in_specs=[op_spec] * 5 + [v_spec, s_spec],
        out_specs=[v_spec, s_spec],
        out_shape=[jax.ShapeDtypeStruct((g, l, ni, LANES), F32), jax.ShapeDtypeStruct((g, ni, n, LANES), F32)],
        scratch_shapes=[pltpu.VMEM((ni, n, LANES), F32)],
        compiler_params=_params("parallel", "arbitrary"),
        name="rwkv_scan",
    )(kap, dec, bb, kk, rr, v, s0)


def _rwkv_post_kernel(y_ref, bonus_ref, g_ref, gng_ref, gnb_ref, o_ref):
    y = y_ref[...]
    ones = _head_ones(RWKV_WIDTH, RWKV_HEAD)
    m = _head_sum(y, ones) * (1.0 / RWKV_HEAD)
    c = y - m
    var = _head_sum(c * c, ones) * (1.0 / RWKV_HEAD)
    yn = c * lax.rsqrt(var + GN_EPS) * gng_ref[...] + gnb_ref[...]
    o_ref[...] = ((yn + bonus_ref[...]) * g_ref[...]).astype(o_ref.dtype)


def _rwkv_post(y, bonus, g, gn_g, gn_b, *, tm):
    m = y.shape[0]
    assert m % tm == 0
    spec = pl.BlockSpec((tm, RWKV_WIDTH), lambda i: (i, 0))
    pspec = pl.BlockSpec((1, RWKV_WIDTH), lambda i: (0, 0))
    return pl.pallas_call(
        _rwkv_post_kernel,
        grid=(m // tm,),
        in_specs=[spec, spec, spec, pspec, pspec],
        out_specs=spec,
        out_shape=jax.ShapeDtypeStruct((m, RWKV_WIDTH), BF16),
        compiler_params=_params("parallel"),
        name="rwkv_post",
    )(y, bonus, g, gn_g.reshape(1, -1), gn_b.reshape(1, -1))


def _layernorm(v, g, b):
    m = jnp.mean(v, axis=-1, keepdims=True)
    c = v - m
    var = jnp.mean(c * c, axis=-1, keepdims=True)
    return c * lax.rsqrt(var + LN_EPS) * g + b


def _sgu_prompt_kernel(u_ref, v_ref, x_ref, lng_ref, lnb_ref, ws_ref, bias_ref, wo_ref, o_ref, vn_ref, gate_ref, *, tm):
    vn_ref[...] = _layernorm(v_ref[...].astype(F32), lng_ref[...], lnb_ref[...]).astype(BF16)
    gw = SGU_WIDTH // SGU_GROUPS
    r = lax.broadcasted_iota(jnp.int32, (CHUNK, CHUNK), 0)
    c = lax.broadcasted_iota(jnp.int32, (CHUNK, CHUNK), 1)
    for gi in range(SGU_GROUPS):
        cols = slice(gi * gw, (gi + 1) * gw)
        wm = jnp.where(r >= c, ws_ref[gi], 0.0).astype(BF16)
        for ci in range(tm // CHUNK):
            rows = slice(ci * CHUNK, (ci + 1) * CHUNK)
            sp = jnp.dot(wm, vn_ref[rows, cols], preferred_element_type=F32) + bias_ref[:, cols]
            gate_ref[rows, cols] = (u_ref[rows, cols].astype(F32) * sp).astype(BF16)
    o_ref[...] = x_ref[...] + jnp.dot(gate_ref[...], wo_ref[...], preferred_element_type=F32)


def _sgu_prompt(zc, x, ln_g, ln_b, w_s, bias, w_out, *, tm):
    t, d = x.shape
    assert t % tm == 0 and tm % CHUNK == 0
    row = lambda a: a.reshape(1, -1)
    return pl.pallas_call(
        functools.partial(_sgu_prompt_kernel, tm=tm),
        grid=(t // tm,),
        in_specs=[
            pl.BlockSpec((tm, SGU_WIDTH), lambda i: (i, 0)),
            pl.BlockSpec((tm, SGU_WIDTH), lambda i: (i, 1)),
            pl.BlockSpec((tm, d), lambda i: (i, 0)),
            pl.BlockSpec((1, SGU_WIDTH), lambda i: (0, 0)),
            pl.BlockSpec((1, SGU_WIDTH), lambda i: (0, 0)),
            pl.BlockSpec((SGU_GROUPS, CHUNK, CHUNK), lambda i: (0, 0, 0)),
            pl.BlockSpec((CHUNK, SGU_WIDTH), lambda i: (0, 0)),
            pl.BlockSpec((SGU_WIDTH, d), lambda i: (0, 0)),
        ],
        out_specs=pl.BlockSpec((tm, d), lambda i: (i, 0)),
        out_shape=jax.ShapeDtypeStruct((t, d), F32),
        scratch_shapes=[pltpu.VMEM((tm, SGU_WIDTH), BF16), pltpu.VMEM((tm, SGU_WIDTH), BF16)],
        compiler_params=_params("parallel"),
        name="sgu_prompt",
    )(zc, zc, x, row(ln_g), row(ln_b), w_s, bias, w_out)


def _sgu_sample_kernel(u_ref, v_ref, lng_ref, lnb_ref, coef_ref, bias_ref, gate_ref, vn_ref):
    vn = _layernorm(v_ref[...], lng_ref[...], lnb_ref[...])
    vn_ref[...] = vn
    gate_ref[...] = (u_ref[...] * (vn * coef_ref[...] + bias_ref[...])).astype(gate_ref.dtype)


def _sgu_sample(zc, ln_g, ln_b, coef, bias):
    b = zc.shape[0]
    row = lambda a: a.reshape(1, -1)
    pspec = pl.BlockSpec((1, SGU_WIDTH), lambda i: (0, 0))
    return pl.pallas_call(
        _sgu_sample_kernel,
        grid=(1,),
        in_specs=[pl.BlockSpec((b, SGU_WIDTH), lambda i: (0, 0)), pl.BlockSpec((b, SGU_WIDTH), lambda i: (0, 1)),
                  pspec, pspec, pspec, pspec],
        out_specs=[pl.BlockSpec((b, SGU_WIDTH), lambda i: (0, 0))] * 2,
        out_shape=[jax.ShapeDtypeStruct((b, SGU_WIDTH), BF16), jax.ShapeDtypeStruct((b, SGU_WIDTH), F32)],
        compiler_params=_params("arbitrary"),
        name="sgu_sample",
    )(zc, zc, row(ln_g), row(ln_b), row(coef), row(bias))


def _to_lanes_prompt(x):
    b, l, _ = x.shape
    x = x.reshape(b, l, RWKV_HEADS, RWKV_HEAD).transpose(1, 3, 0, 2).reshape(l, RWKV_HEAD, b * RWKV_HEADS)
    return jnp.concatenate([x, x], axis=-1)[None]


def _v_to_lanes_prompt(v):
    b, l, _ = v.shape
    v = v.reshape(b, l, RWKV_HEADS, 2, RWKV_HEAD // 2).transpose(1, 4, 3, 0, 2)
    return v.reshape(1, l, RWKV_HEAD // 2, 2 * b * RWKV_HEADS)


def _y_from_lanes_prompt(y, b):
    l = y.shape[1]
    y = y.reshape(l, RWKV_HEAD // 2, 2, b, RWKV_HEADS).transpose(3, 0, 4, 2, 1)
    return y.reshape(b, l, RWKV_WIDTH)


def _state_from_lanes_prompt(s, b):
    s = s.reshape(RWKV_HEAD // 2, RWKV_HEAD, 2, b, RWKV_HEADS).transpose(3, 4, 2, 0, 1)
    return s.reshape(b, RWKV_HEADS, RWKV_HEAD, RWKV_HEAD)


def _to_lanes_sample(x):
    b = x.shape[0]
    g = b * RWKV_HEADS // LANES
    x = x.reshape(g, LANES // RWKV_HEADS, RWKV_HEADS, RWKV_HEAD).transpose(0, 3, 1, 2)
    return x.reshape(g, 1, RWKV_HEAD, LANES)


def _from_lanes_sample(y):
    g = y.shape[0]
    y = y.reshape(g, RWKV_HEAD, LANES // RWKV_HEADS, RWKV_HEADS).transpose(0, 2, 3, 1)
    return y.reshape(g * LANES // RWKV_HEADS, RWKV_WIDTH)


def _state_to_lanes_sample(s):
    b = s.shape[0]
    g = b * RWKV_HEADS // LANES
    s = s.reshape(g, LANES // RWKV_HEADS, RWKV_HEADS, RWKV_HEAD, RWKV_HEAD).transpose(0, 3, 4, 1, 2)
    return s.reshape(g, RWKV_HEAD, RWKV_HEAD, LANES)


def _state_from_lanes_sample(s):
    g = s.shape[0]
    s = s.reshape(g, RWKV_HEAD, RWKV_HEAD, LANES // RWKV_HEADS, RWKV_HEADS).transpose(0, 3, 4, 1, 2)
    return s.reshape(g * LANES // RWKV_HEADS, RWKV_HEADS, RWKV_HEAD, RWKV_HEAD)


def kernel(x_prompt, x_sample, mem_prompt, cache_mem_k, cache_mem_v, state_pool, state_shift, state_wkv, norm_mix_g, norm_xa_g, norm_mem_g, norm_ffn_g, norm_final_g, w_in_ab, w_out_ab, pool_w, pool_scale, rwkv_mu, rwkv_w0, rwkv_w2, rwkv_a0, rwkv_a2, rwkv_g2, rwkv_k_k, rwkv_k_a, rwkv_r_k, rwkv_gn_g, rwkv_gn_b, w_in_c, sgu_ln_g, sgu_ln_b, sgu_w_s, sgu_b_s, w_out_c, w_xq, w_xk, w_xv, w_xo, w_ff_up, w_ff_down):
    bp, lp, d = x_prompt.shape
    bs = x_sample.shape[0]
    tp = bp * lp
    bf = lambda w: w.astype(BF16)

    w_in_ab_b, w_out_ab_b, pool_w_b = bf(w_in_ab), bf(w_out_ab), bf(pool_w)
    w_in_c_b, w_out_c_b = bf(w_in_c), bf(w_out_c)
    w_xq_b, w_xk_b, w_xv_b, w_xo_b = bf(w_xq), bf(w_xk), bf(w_xv), bf(w_xo)
    w_up_b, w_down_b = bf(w_ff_up), bf(w_ff_down)

    mem2 = mem_prompt.reshape(bp * MEM_LEN, d)
    mem_k_p, mem_v_p = [], []
    for l in range(DEPTH):
        mem_k_p.append(_linear([mem2], [w_xk_b[l]], gain=norm_mem_g[l], tm=1024, tn=512, name="mem_k"))
        mem_v_p.append(_linear([mem2], [w_xv_b[l]], gain=norm_mem_g[l], tm=1024, tn=512, name="mem_v"))

    xp = x_prompt.reshape(tp, d)
    xs = x_sample.reshape(bs, d)
    pool_out_p, pool_out_s, shift_out_p, shift_out_s, wkv_out_p, wkv_out_s, sgu_v_s = [], [], [], [], [], [], []

    for l in range(DEPTH):
        j = l // 2
        if l % 2 == 0:
            w_la = jnp.zeros((W_LORA + A_LORA, 2 * RWKV_WIDTH), F32)
            w_la = w_la.at[:W_LORA, :RWKV_WIDTH].set(rwkv_w2[j]).at[W_LORA:, RWKV_WIDTH:].set(rwkv_a2[j])
            params = _prep_param_args(rwkv_mu[j], rwkv_w0[j], rwkv_a0[j], rwkv_k_k[j], rwkv_k_a[j],
                                      rwkv_r_k[j].reshape(-1), bf(w_la), bf(rwkv_g2[j]))
            zp = _linear([xp], [w_in_ab_b[j]], gain=norm_mix_g[l], tm=1024, tn=256, name="ab_in_prompt")
            zp = zp.reshape(bp, lp, -1)
            y_pool = _pool_prompt(zp, pool_w_b[j], pool_scale[j], tm=512).reshape(tp, POOL_WIDTH)
            r, k2, v, kap, bb, dec, g, bonus = _rwkv_prep_prompt(zp, params, tm=512)
            s0 = jnp.zeros((1, RWKV_HEAD // 2, RWKV_HEAD, LANES), F32)
            y_l, s_l = _rwkv_scan(_to_lanes_prompt(kap), _to_lanes_prompt(dec), _to_lanes_prompt(bb),
                                  _to_lanes_prompt(k2), _to_lanes_prompt(r), _v_to_lanes_prompt(v), s0, tt=32)
            y = _y_from_lanes_prompt(y_l, bp).reshape(tp, RWKV_WIDTH)
            y_rwkv = _rwkv_post(y, bonus.reshape(tp, -1), g.reshape(tp, -1), rwkv_gn_g[j], rwkv_gn_b[j], tm=1024)
            xp = _linear([y_pool, y_rwkv], [w_out_ab_b[j][:POOL_WIDTH], w_out_ab_b[j][POOL_WIDTH:]], res=xp,
                         tm=1024, tn=512, name="ab_out_prompt")
            pool_out_p.append(zp[:, lp - POOL_BUF:, :POOL_WIDTH])
            shift_out_p.append(zp[:, lp - 1, POOL_WIDTH:])
            wkv_out_p.append(_state_from_lanes_prompt(s_l, bp))
            zs = _linear([xs], [w_in_ab_b[j]], gain=norm_mix_g[l], tm=bs, tn=256, name="ab_in_sample")
            y_pool = _pool_sample(zs, state_pool[j], pool_w_b[j], pool_scale[j])
            r, k2, v, kap, bb, dec, g, bonus = _rwkv_prep_sample(zs, state_shift[j], params)
            y_l, s_l = _rwkv_scan(_to_lanes_sample(kap), _to_lanes_sample(dec), _to_lanes_sample(bb),
                                  _to_lanes_sample(k2), _to_lanes_sample(r), _to_lanes_sample(v),
                                  _state_to_lanes_sample(state_wkv[j]), tt=1)
            y_rwkv = _rwkv_post(_from_lanes_sample(y_l), bonus, g, rwkv_gn_g[j], rwkv_gn_b[j], tm=bs)
            xs = _linear([y_pool, y_rwkv], [w_out_ab_b[j][:POOL_WIDTH], w_out_ab_b[j][POOL_WIDTH:]], res=xs,
                         tm=bs, tn=512, name="ab_out_sample")
            pool_out_s.append(jnp.concatenate([state_pool[j][:, 1:], zs[:, None, :POOL_WIDTH]], axis=1))
            shift_out_s.append(zs[:, POOL_WIDTH:])
            wkv_out_s.append(_state_from_lanes_sample(s_l))
        else:
            gw = SGU_WIDTH // SGU_GROUPS
            zc = _linear([xp], [w_in_c_b[j]], gain=norm_mix_g[l], act="gelu", out_dtype=BF16, tm=1024, tn=512,
                         name="sgu_in_prompt")
            bias = jnp.repeat(sgu_b_s[j].T, gw, axis=1)
            xp = _sgu_prompt(zc, xp, sgu_ln_g[j], sgu_ln_b[j], sgu_w_s[j], bias, w_out_c_b[j], tm=512)
            zc = _linear([xs], [w_in_c_b[j]], gain=norm_mix_g[l], act="gelu", tm=bs, tn=512, name="sgu_in_sample")
            gate, vn = _sgu_sample(zc, sgu_ln_g[j], sgu_ln_b[j], jnp.repeat(sgu_w_s[j][:, 0, 0], gw),
                                   jnp.repeat(sgu_b_s[j][:, 0], gw))
            xs = _linear([gate], [w_out_c_b[j]], res=xs, tm=bs, tn=512, name="sgu_out_sample")
            sgu_v_s.append(vn.reshape(bs, 1, SGU_WIDTH))

        xp = _xattn_prompt(xp.reshape(bp, lp, d), norm_xa_g[l], mem_k_p[l].reshape(bp, MEM_LEN, d),
                           mem_v_p[l].reshape(bp, MEM_LEN, d), w_xq_b[l], w_xo_b[l], tm=512).reshape(tp, d)
        q = _linear([xs], [w_xq_b[l]], gain=norm_xa_g[l], tm=bs, tn=512, name="xattn_q_sample")
        att = _xattn_sample_core(q.reshape(bs, 1, d), cache_mem_k, cache_mem_v, l, bs=4)
        xs = _linear([att.reshape(bs, d)], [w_xo_b[l]], res=xs, tm=bs, tn=512, name="xattn_o_sample")

        fg = norm_final_g if l == DEPTH - 1 else None
        xp = _mlp(xp, norm_ffn_g[l], w_up_b[l], w_down_b[l], final_gain=fg, tm=1024, tf=512)
        xs = _mlp(xs, norm_ffn_g[l], w_up_b[l], w_down_b[l], final_gain=fg, tm=bs, tf=512)

    head_shape = (bp, MEM_LEN, XA_HEADS, XA_HEAD_DIM)
    return (xp.reshape(bp, lp, d),
            xs.reshape(bs, 1, d),
            jnp.stack([m.reshape(head_shape) for m in mem_k_p]),
            jnp.stack([m.reshape(head_shape) for m in mem_v_p]),
            jnp.stack(pool_out_p),
            jnp.stack(pool_out_s),
            jnp.stack(shift_out_p),
            jnp.stack(shift_out_s),
            jnp.stack(wkv_out_p),
            jnp.stack(wkv_out_s),
            jnp.stack(sgu_v_s))
```

```python
import functools

import jax
import jax.numpy as jnp
from jax import lax
from jax.experimental import pallas as pl
from jax.experimental.pallas import tpu as pltpu

D_MODEL = 1024
DEPTH = 2
PAST_LEN = 16384
POOL_WINDOWS = (2, 4, 8, 16)
POOL_GROUP_WIDTH = 128
POOL_WIDTH = 512
POOL_BUF = 15
RWKV_HEAD = 64
RWKV_WIDTH = 512
RWKV_HEADS = 8
W_LORA = 64
A_LORA = 64
G_LORA = 128
RWKV_IN = 1792
LORA_IN = W_LORA + A_LORA + G_LORA
CHUNK = 128
SGU_WIDTH = 2048
SGU_GROUPS = 4
MEM_LEN = 256
XA_HEADS = 4
XA_HEAD_DIM = 256
D_FF = 4096
RMS_EPS = 1e-5
LN_EPS = 1e-5
GN_EPS = RWKV_HEAD * 1e-5
L2_EPS = 1e-12

LANES = 128
F32_SUBLANES = 8
VMEM_LIMIT = 56 * 1024 * 1024

F32 = jnp.float32
BF16 = jnp.bfloat16


def _params(*sem):
    return pltpu.CompilerParams(dimension_semantics=sem, vmem_limit_bytes=VMEM_LIMIT)


def _dot(a, b):
    return jnp.dot(a.astype(BF16), b.astype(BF16), preferred_element_type=F32)


def _dot_nt(a, b):
    return lax.dot_general(a.astype(BF16), b.astype(BF16), (((1,), (1,)), ((), ())),
                           preferred_element_type=F32)


def _rms(x, g):
    return x * lax.rsqrt(jnp.mean(x * x, axis=-1, keepdims=True) + RMS_EPS) * g


def _head_ones(width, head):
    r = lax.broadcasted_iota(jnp.int32, (width, width), 0) // head
    c = lax.broadcasted_iota(jnp.int32, (width, width), 1) // head
    return (r == c).astype(BF16)


def _head_sum(x, ones):
    hi = x.astype(BF16)
    lo = (x - hi.astype(F32)).astype(BF16)
    return (jnp.dot(hi, ones, preferred_element_type=F32)
            + jnp.dot(lo, ones, preferred_element_type=F32))


def _linear_kernel(*refs, nx, has_gain, act, has_res):
    x_refs = refs[:nx]
    pos = nx
    g_ref = refs[pos] if has_gain else None
    pos += int(has_gain)
    w_refs = refs[pos:pos + nx]
    pos += nx
    res_ref = refs[pos] if has_res else None
    pos += int(has_res)
    o_ref = refs[pos]
    if has_gain:
        xn_ref = refs[pos + 1]

        @pl.when(pl.program_id(1) == 0)
        def _():
            xn_ref[...] = _rms(x_refs[0][...].astype(F32), g_ref[...]).astype(BF16)

        lhs = [xn_ref[...]]
    else:
        lhs = [r[...] for r in x_refs]
    tn = o_ref.shape[1]
    cw = next(c for c in (256, 384, 128) if tn % c == 0)
    for c0 in range(0, tn, cw):
        cols = slice(c0, c0 + cw)
        acc = _dot(lhs[0], w_refs[0][:, cols])
        for l, w in zip(lhs[1:], w_refs[1:]):
            acc = acc + _dot(l, w[:, cols])
        if act == "gelu":
            acc = 0.5 * acc * (1.0 + lax.erf(acc * 0.7071067811865476))
        if has_res:
            acc = acc + res_ref[:, cols]
        o_ref[:, cols] = acc.astype(o_ref.dtype)


def _linear(xs, ws, *, gain=None, act=None, res=None, out_dtype=F32, tm, tn, name):
    m = xs[0].shape[0]
    n = ws[0].shape[1]
    assert m % tm == 0 and n % tn == 0, (m, tm, n, tn)
    assert gain is None or len(xs) == 1
    in_specs = [pl.BlockSpec((tm, x.shape[1]), lambda i, j: (i, 0)) for x in xs]
    args = list(xs)
    if gain is not None:
        in_specs.append(pl.BlockSpec((1, gain.shape[-1]), lambda i, j: (0, 0)))
        args.append(gain.reshape(1, -1))
    in_specs += [pl.BlockSpec((w.shape[0], tn), lambda i, j: (0, j)) for w in ws]
    args += list(ws)
    if res is not None:
        in_specs.append(pl.BlockSpec((tm, tn), lambda i, j: (i, j)))
        args.append(res)
    scratch = [pltpu.VMEM((tm, xs[0].shape[1]), BF16)] if gain is not None else []
    kern = functools.partial(_linear_kernel, nx=len(xs), has_gain=gain is not None, act=act,
                             has_res=res is not None)
    return pl.pallas_call(
        kern,
        grid=(m // tm, n // tn),
        in_specs=in_specs,
        out_specs=pl.BlockSpec((tm, tn), lambda i, j: (i, j)),
        out_shape=jax.ShapeDtypeStruct((m, n), out_dtype),
        scratch_shapes=scratch,
        compiler_params=_params("parallel", "arbitrary"),
        name=name,
    )(*args)


def _mlp_kernel(*refs, final_norm):
    if final_norm:
        x_ref, g_ref, wu_ref, wd_ref, gf_ref, o_ref, xn_ref, acc_ref = refs
    else:
        x_ref, g_ref, wu_ref, wd_ref, o_ref, xn_ref, acc_ref = refs
    j = pl.program_id(1)

    @pl.when(j == 0)
    def _():
        xn_ref[...] = _rms(x_ref[...], g_ref[...]).astype(BF16)
        acc_ref[...] = jnp.zeros_like(acc_ref)

    h = jnp.dot(xn_ref[...], wu_ref[...], preferred_element_type=F32)
    h = jnp.square(jnp.maximum(h, 0.0)).astype(BF16)
    acc_ref[...] += jnp.dot(h, wd_ref[...], preferred_element_type=F32)

    @pl.when(j == pl.num_programs(1) - 1)
    def _():
        y = x_ref[...] + acc_ref[...]
        if final_norm:
            y = _rms(y, gf_ref[...])
        o_ref[...] = y


def _mlp(x, gain, w_up, w_down, *, final_gain=None, tm, tf):
    m, d = x.shape
    f = w_up.shape[1]
    assert m % tm == 0 and f % tf == 0
    in_specs = [
        pl.BlockSpec((tm, d), lambda i, j: (i, 0)),
        pl.BlockSpec((1, d), lambda i, j: (0, 0)),
        pl.BlockSpec((d, tf), lambda i, j: (0, j)),
        pl.BlockSpec((tf, d), lambda i, j: (j, 0)),
    ]
    args = [x, gain.reshape(1, d), w_up, w_down]
    if final_gain is not None:
        in_specs.append(pl.BlockSpec((1, d), lambda i, j: (0, 0)))
        args.append(final_gain.reshape(1, d))
    return pl.pallas_call(
        functools.partial(_mlp_kernel, final_norm=final_gain is not None),
        grid=(m // tm, f // tf),
        in_specs=in_specs,
        out_specs=pl.BlockSpec((tm, d), lambda i, j: (i, 0)),
        out_shape=jax.ShapeDtypeStruct((m, d), F32),
        scratch_shapes=[pltpu.VMEM((tm, d), BF16), pltpu.VMEM((tm, d), F32)],
        compiler_params=_params("parallel", "arbitrary"),
        name="mlp",
    )(*args)


def _xattn_prompt_kernel(x_ref, g_ref, k_ref, v_ref, wq_ref, wo_ref, o_ref, att_ref):
    x = x_ref[0]
    xn = _rms(x, g_ref[...]).astype(BF16)
    q = jnp.dot(xn, wq_ref[...], preferred_element_type=F32) * (XA_HEAD_DIM ** -0.5)
    for h in range(XA_HEADS):
        cols = slice(h * XA_HEAD_DIM, (h + 1) * XA_HEAD_DIM)
        s = _dot_nt(q[:, cols], k_ref[0, :, cols])
        s = s - jnp.max(s, axis=-1, keepdims=True)
        p = jnp.exp(s)
        p = p / jnp.sum(p, axis=-1, keepdims=True)
        att_ref[:, cols] = _dot(p, v_ref[0, :, cols]).astype(BF16)
    o_ref[0] = x + jnp.dot(att_ref[...], wo_ref[...], preferred_element_type=F32)


def _xattn_prompt(x, gain, mem_k, mem_v, w_q, w_o, *, tm):
    b, l, d = x.shape
    assert l % tm == 0
    return pl.pallas_call(
        _xattn_prompt_kernel,
        grid=(b, l // tm),
        in_specs=[
            pl.BlockSpec((1, tm, d), lambda bi, i: (bi, i, 0)),
            pl.BlockSpec((1, d), lambda bi, i: (0, 0)),
            pl.BlockSpec((1, MEM_LEN, d), lambda bi, i: (bi, 0, 0)),
            pl.BlockSpec((1, MEM_LEN, d), lambda bi, i: (bi, 0, 0)),
            pl.BlockSpec((d, d), lambda bi, i: (0, 0)),
            pl.BlockSpec((d, d), lambda bi, i: (0, 0)),
        ],
        out_specs=pl.BlockSpec((1, tm, d), lambda bi, i: (bi, i, 0)),
        out_shape=jax.ShapeDtypeStruct((b, l, d), F32),
        scratch_shapes=[pltpu.VMEM((tm, d), BF16)],
        compiler_params=_params("parallel", "parallel"),
        name="xattn_prompt",
    )(x, gain.reshape(1, d), mem_k, mem_v, w_q, w_o)


def _xattn_sample_kernel(q_ref, k_ref, v_ref, o_ref, *, bs):
    for b in range(bs):
        q = q_ref[b] * (XA_HEAD_DIM ** -0.5)
        s = jnp.sum(k_ref[0, b] * q[None], axis=-1, keepdims=True)
        p = jnp.exp(s - jnp.max(s, axis=0, keepdims=True))
        den = jnp.sum(p, axis=0)
        o_ref[b] = jnp.sum(p * v_ref[0, b], axis=0) / den


def _xattn_sample_core(q, cache_k, cache_v, layer, *, bs):
    b = q.shape[0]
    assert b % bs == 0
    q_spec = pl.BlockSpec((bs, XA_HEADS, XA_HEAD_DIM), lambda i: (i, 0, 0))
    kv_spec = pl.BlockSpec((1, bs, MEM_LEN, XA_HEADS, XA_HEAD_DIM), lambda i: (layer, i, 0, 0, 0))
    return pl.pallas_call(
        functools.partial(_xattn_sample_kernel, bs=bs),
        grid=(b // bs,),
        in_specs=[q_spec, kv_spec, kv_spec],
        out_specs=q_spec,
        out_shape=jax.ShapeDtypeStruct((b, XA_HEADS, XA_HEAD_DIM), F32),
        compiler_params=_params("parallel"),
        name="xattn_sample",
    )(q, cache_k, cache_v)


HALO = 16


def _pool_prompt_kernel(z_ref, halo_ref, w_ref, scale_ref, o_ref, ext_ref, *, tm, tiles_per_seq):
    t = pl.program_id(1)
    ext_ref[HALO:HALO + tm, :] = z_ref[0]

    @pl.when(t == 0)
    def _():
        ext_ref[0:HALO, :] = jnp.zeros((HALO, POOL_WIDTH), F32)

    @pl.when(t > 0)
    def _():
        ext_ref[0:HALO, :] = halo_ref[0]

    pos = t * tm + lax.broadcasted_iota(jnp.int32, (tm, 1), 0)
    for gi, win in enumerate(POOL_WINDOWS):
        cols = slice(gi * POOL_GROUP_WIDTH, (gi + 1) * POOL_GROUP_WIDTH)
        tok = ext_ref[HALO:HALO + tm, cols]
        acc = tok
        for k in range(1, win):
            acc = acc + ext_ref[HALO - k:HALO - k + tm, cols]
        count = jnp.minimum(pos + 1, win).astype(F32)
        dlt = acc / count - tok
        o_ref[0, :, cols] = (_dot(dlt, w_ref[gi]) * scale_ref[:, cols]).astype(o_ref.dtype)


def _pool_prompt(z, w_group, scale, *, tm):
    b, l, _ = z.shape
    assert l % tm == 0 and tm % HALO == 0
    tiles = l // tm
    hb = tm // HALO
    return pl.pallas_call(
        functools.partial(_pool_prompt_kernel, tm=tm, tiles_per_seq=tiles),
        grid=(b, tiles),
        in_specs=[
            pl.BlockSpec((1, tm, POOL_WIDTH), lambda bi, i: (bi, i, 0)),
            pl.BlockSpec((1, HALO, POOL_WIDTH), lambda bi, i: (bi, jnp.maximum(i * hb - 1, 0), 0)),
            pl.BlockSpec((len(POOL_WINDOWS), POOL_GROUP_WIDTH, POOL_GROUP_WIDTH), lambda bi, i: (0, 0, 0)),
            pl.BlockSpec((1, POOL_WIDTH), lambda bi, i: (0, 0)),
        ],
        out_specs=pl.BlockSpec((1, tm, POOL_WIDTH), lambda bi, i: (bi, i, 0)),
        out_shape=jax.ShapeDtypeStruct((b, l, POOL_WIDTH), BF16),
        scratch_shapes=[pltpu.VMEM((HALO + tm, POOL_WIDTH), F32)],
        compiler_params=_params("parallel", "parallel"),
        name="pool_prompt",
    )(z, z, w_group, scale.reshape(1, POOL_WIDTH))


def _pool_sample_kernel(z_ref, buf_ref, w_ref, scale_ref, o_ref):
    for gi, win in enumerate(POOL_WINDOWS):
        cols = slice(gi * POOL_GROUP_WIDTH, (gi + 1) * POOL_GROUP_WIDTH)
        tok = z_ref[:, cols]
        acc = tok
        for k in range(1, win):
            acc = acc + buf_ref[:, POOL_BUF - k, cols]
        count = float(min(PAST_LEN + 1, win))
        dlt = acc / count - tok
        o_ref[:, cols] = (_dot(dlt, w_ref[gi]) * scale_ref[:, cols]).astype(o_ref.dtype)


def _pool_sample(z, buf, w_group, scale):
    b = z.shape[0]
    return pl.pallas_call(
        _pool_sample_kernel,
        grid=(1,),
        in_specs=[
            pl.BlockSpec((b, POOL_WIDTH), lambda i: (0, 0)),
            pl.BlockSpec((b, POOL_BUF, POOL_WIDTH), lambda i: (0, 0, 0)),
            pl.BlockSpec((len(POOL_WINDOWS), POOL_GROUP_WIDTH, POOL_GROUP_WIDTH), lambda i: (0, 0, 0)),
            pl.BlockSpec((1, POOL_WIDTH), lambda i: (0, 0)),
        ],
        out_specs=pl.BlockSpec((b, POOL_WIDTH), lambda i: (0, 0)),
        out_shape=jax.ShapeDtypeStruct((b, POOL_WIDTH), BF16),
        compiler_params=_params("arbitrary"),
        name="pool_sample",
    )(z, buf, w_group, scale.reshape(1, POOL_WIDTH))


def _rwkv_prep_math(z4, s4, p, out_refs, sl):
    zr, zk, zv, zl = z4
    sr, sk, sv, sq = s4
    mu_r, mu_k, mu_v, mu_l, w0, a0, k_k, k_a, r_k, w_la, w_g = p
    r = zr + (sr - zr) * mu_r
    k = zk + (sk - zk) * mu_k
    v = zv + (sv - zv) * mu_v
    lo = zl + (sq - zl) * mu_l
    lane = lax.broadcasted_iota(jnp.int32, lo.shape, 1)
    feat = jnp.where(lane < W_LORA, jnp.tanh(lo), jnp.where(lane < W_LORA + A_LORA, lo, jax.nn.sigmoid(lo)))
    wa = _dot(feat[:, :W_LORA + A_LORA], w_la)
    g = _dot(feat[:, W_LORA + A_LORA:], w_g)
    wlog = -jax.nn.softplus(-(w0 + wa[:, :RWKV_WIDTH])) - 0.5
    decay = jnp.exp(-jnp.exp(wlog))
    a = jax.nn.sigmoid(a0 + wa[:, RWKV_WIDTH:])
    kk = k * k_k
    k2 = k * (1.0 + (a - 1.0) * k_a)
    ones = _head_ones(RWKV_WIDTH, RWKV_HEAD)
    kk = kk / jnp.maximum(jnp.sqrt(_head_sum(kk * kk, ones)), L2_EPS)
    bonus = _head_sum(r * k2 * r_k, ones) * v
    r_o, k_o, v_o, kap_o, b_o, d_o, g_o, bonus_o = out_refs
    r_o[sl] = r
    k_o[sl] = k2
    v_o[sl] = v
    kap_o[sl] = kk
    b_o[sl] = kk * a
    d_o[sl] = decay
    g_o[sl] = g
    bonus_o[sl] = bonus


def _load_params(refs):
    return tuple(r[...] for r in refs)


def _rwkv_prep_prompt_kernel(zr_ref, zk_ref, zv_ref, zl_ref, *refs, tm):
    p_refs = refs[:11]
    out_refs = refs[11:19]
    er_ref, ek_ref, ev_ref, el_ref = refs[19:23]
    t = pl.program_id(1)
    row = 7
    z4, s4 = [], []
    for z_ref, e_ref in ((zr_ref, er_ref), (zk_ref, ek_ref), (zv_ref, ev_ref), (zl_ref, el_ref)):
        z = z_ref[0]

        @pl.when(t == 0)
        def _():
            e_ref[0:8, :] = jnp.zeros((8, e_ref.shape[1]), F32)

        @pl.when(t > 0)
        def _():
            e_ref[row:row + 1, :] = e_ref[row + tm:row + tm + 1, :]

        e_ref[8:8 + tm, :] = z
        z4.append(z)
        s4.append(e_ref[row:row + tm, :])
    _rwkv_prep_math(z4, s4, _load_params(p_refs), out_refs, (0,))


def _prep_param_args(mu, w0, a0, k_k, k_a, r_k, w_la, w_g):
    w = RWKV_WIDTH
    row = lambda x: x.reshape(1, -1)
    return [row(mu[:w]), row(mu[w:2 * w]), row(mu[2 * w:3 * w]), row(mu[3 * w:]),
            row(w0), row(a0), row(k_k), row(k_a), row(r_k), w_la, w_g]


def _prep_out_shapes(lead):
    return [jax.ShapeDtypeStruct(lead + (RWKV_WIDTH,), F32) for _ in range(8)]


def _rwkv_prep_prompt(z, params, *, tm):
    b, l, _ = z.shape
    assert l % tm == 0
    pw = POOL_WIDTH // RWKV_WIDTH
    full = lambda a: pl.BlockSpec(a.shape, lambda bi, i: (0,) * a.ndim)
    in_specs = [
        pl.BlockSpec((1, tm, RWKV_WIDTH), lambda bi, i: (bi, i, pw)),
        pl.BlockSpec((1, tm, RWKV_WIDTH), lambda bi, i: (bi, i, pw + 1)),
        pl.BlockSpec((1, tm, RWKV_WIDTH), lambda bi, i: (bi, i, pw + 2)),
        pl.BlockSpec((1, tm, LORA_IN), lambda bi, i: (bi, i, (POOL_WIDTH + 3 * RWKV_WIDTH) // LORA_IN)),
    ] + [full(a) for a in params]
    out_spec = pl.BlockSpec((1, tm, RWKV_WIDTH), lambda bi, i: (bi, i, 0))
    return pl.pallas_call(
        functools.partial(_rwkv_prep_prompt_kernel, tm=tm),
        grid=(b, l // tm),
        in_specs=in_specs,
        out_specs=[out_spec] * 8,
        out_shape=_prep_out_shapes((b, l)),
        scratch_shapes=[pltpu.VMEM((tm + 8, RWKV_WIDTH), F32)] * 3 + [pltpu.VMEM((tm + 8, LORA_IN), F32)],
        compiler_params=_params("parallel", "arbitrary"),
        name="rwkv_prep_prompt",
    )(z, z, z, z, *params)


def _rwkv_prep_sample_kernel(zr_ref, zk_ref, zv_ref, zl_ref, sr_ref, sk_ref, sv_ref, sl_ref, *refs):
    p_refs = refs[:11]
    out_refs = refs[11:19]
    z4 = [zr_ref[...], zk_ref[...], zv_ref[...], zl_ref[...]]
    s4 = [sr_ref[...], sk_ref[...], sv_ref[...], sl_ref[...]]
    _rwkv_prep_math(z4, s4, _load_params(p_refs), out_refs, (Ellipsis,))


def _rwkv_prep_sample(z, shift, params):
    b = z.shape[0]
    pw = POOL_WIDTH // RWKV_WIDTH
    full = lambda a: pl.BlockSpec(a.shape, lambda i: (0,) * a.ndim)
    in_specs = [
        pl.BlockSpec((b, RWKV_WIDTH), lambda i: (0, pw)),
        pl.BlockSpec((b, RWKV_WIDTH), lambda i: (0, pw + 1)),
        pl.BlockSpec((b, RWKV_WIDTH), lambda i: (0, pw + 2)),
        pl.BlockSpec((b, LORA_IN), lambda i: (0, (POOL_WIDTH + 3 * RWKV_WIDTH) // LORA_IN)),
        pl.BlockSpec((b, RWKV_WIDTH), lambda i: (0, 0)),
        pl.BlockSpec((b, RWKV_WIDTH), lambda i: (0, 1)),
        pl.BlockSpec((b, RWKV_WIDTH), lambda i: (0, 2)),
        pl.BlockSpec((b, LORA_IN), lambda i: (0, 3 * RWKV_WIDTH // LORA_IN)),
    ] + [full(a) for a in params]
    out_spec = pl.BlockSpec((b, RWKV_WIDTH), lambda i: (0, 0))
    return pl.pallas_call(
        _rwkv_prep_sample_kernel,
        grid=(1,),
        in_specs=in_specs,
        out_specs=[out_spec] * 8,
        out_shape=_prep_out_shapes((b,)),
        compiler_params=_params("arbitrary"),
        name="rwkv_prep_sample",
    )(z, z, z, z, shift, shift, shift, shift, *params)


def _rwkv_scan_kernel(kap_ref, d_ref, b_ref, k_ref, r_ref, v_ref, s0_ref, y_ref, sout_ref, s_ref, *, tt, nib, vp):
    tb = pl.program_id(1)
    sub = F32_SUBLANES

    @pl.when(tb == 0)
    def _():
        s_ref[...] = s0_ref[0]

    for i in range(nib * sub if vp > tt else 0):
        y_ref[0, i * vp + tt:(i + 1) * vp, :] = jnp.zeros((vp - tt, LANES), F32)

    nparts = 4

    def add_to(acc, slot, term):
        acc[slot] = term if acc[slot] is None else acc[slot] + term

    def tree_sum(xs):
        while len(xs) > 1:
            xs = [xs[k] + xs[k + 1] for k in range(0, len(xs), 2)]
        return xs[0]

    def step(t, carry):
        vy_rows = [pl.ds(ib * sub * vp + t, sub, stride=vp) if vp > 1 else pl.ds(ib * sub, sub) for ib in range(nib)]
        v = [v_ref[0, rows, :] for rows in vy_rows]
        acc = [None] * (nparts * nib)
        for j in range(RWKV_HEAD):
            kap = kap_ref[0, pl.ds(j * vp + t, 1), :]
            for ib in range(nib):
                add_to(acc, nparts * ib + j % nparts, s_ref[ib, j] * kap)
        sa = [-tree_sum(acc[nparts * ib:nparts * (ib + 1)]) for ib in range(nib)]
        acc = [None] * (nparts * nib)
        for j in range(RWKV_HEAD):
            row = pl.ds(j * vp + t, 1)
            dec = d_ref[0, row, :]
            bb = b_ref[0, row, :]
            kk = k_ref[0, row, :]
            rr = r_ref[0, row, :]
            for ib in range(nib):
                s = s_ref[ib, j] * dec + sa[ib] * bb + v[ib] * kk
                s_ref[ib, j] = s
                add_to(acc, nparts * ib + j % nparts, s * rr)
        for ib in range(nib):
            y_ref[0, vy_rows[ib], :] = tree_sum(acc[nparts * ib:nparts * (ib + 1)])
        return carry

    lax.fori_loop(0, tt, step, 0)

    @pl.when(tb == pl.num_programs(1) - 1)
    def _():
        sout_ref[0] = s_ref[...]


def _rwkv_scan(kap, dec, bb, kk, rr, v, s0, *, tt, vp):
    g, nb, rows, _ = kap.shape
    n = RWKV_HEAD
    nib = s0.shape[1]
    assert rows == n * vp and v.shape[2] == nib * F32_SUBLANES * vp and tt <= vp
    op_spec = pl.BlockSpec((None, 1, n * vp, LANES), lambda gi, i: (gi, i, 0, 0))
    v_spec = pl.BlockSpec((None, 1, v.shape[2], LANES), lambda gi, i: (gi, i, 0, 0))
    s_spec = pl.BlockSpec((1, nib, n, F32_SUBLANES, LANES), lambda gi, i: (gi, 0, 0, 0, 0))
    return pl.pallas_call(
        functools.partial(_rwkv_scan_kernel, tt=tt, nib=nib, vp=vp),
        grid=(g, nb),
        in_specs=[op_spec] * 5 + [v_spec, s_spec],
        out_specs=[v_spec, s_spec],
        out_shape=[jax.ShapeDtypeStruct(v.shape, F32), jax.ShapeDtypeStruct(s0.shape, F32)],
        scratch_shapes=[pltpu.VMEM((nib, n, F32_SUBLANES, LANES), F32)],
        compiler_params=_params("parallel", "arbitrary"),
        name="rwkv_scan",
    )(kap, dec, bb, kk, rr, v, s0)


def _rwkv_post_kernel(y_ref, bonus_ref, g_ref, gng_ref, gnb_ref, o_ref):
    y = y_ref[...]
    ones = _head_ones(RWKV_WIDTH, RWKV_HEAD)
    m = _head_sum(y, ones) * (1.0 / RWKV_HEAD)
    c = y - m
    var = _head_sum(c * c, ones) * (1.0 / RWKV_HEAD)
    yn = c * lax.rsqrt(var + GN_EPS) * gng_ref[...] + gnb_ref[...]
    o_ref[...] = ((yn + bonus_ref[...]) * g_ref[...]).astype(o_ref.dtype)


def _rwkv_post(y, bonus, g, gn_g, gn_b, *, tm):
    m = y.shape[0]
    assert m % tm == 0
    spec = pl.BlockSpec((tm, RWKV_WIDTH), lambda i: (i, 0))
    pspec = pl.BlockSpec((1, RWKV_WIDTH), lambda i: (0, 0))
    return pl.pallas_call(
        _rwkv_post_kernel,
        grid=(m // tm,),
        in_specs=[spec, spec, spec, pspec, pspec],
        out_specs=spec,
        out_shape=jax.ShapeDtypeStruct((m, RWKV_WIDTH), BF16),
        compiler_params=_params("parallel"),
        name="rwkv_post",
    )(y, bonus, g, gn_g.reshape(1, -1), gn_b.reshape(1, -1))


def _layernorm(v, g, b):
    m = jnp.mean(v, axis=-1, keepdims=True)
    c = v - m
    var = jnp.mean(c * c, axis=-1, keepdims=True)
    return c * lax.rsqrt(var + LN_EPS) * g + b


def _sgu_prompt_kernel(u_ref, v_ref, x_ref, lng_ref, lnb_ref, ws_ref, bias_ref, wo_ref, o_ref, vn_ref, gate_ref, *, tm):
    vn_ref[...] = _layernorm(v_ref[...].astype(F32), lng_ref[...], lnb_ref[...]).astype(BF16)
    gw = SGU_WIDTH // SGU_GROUPS
    r = lax.broadcasted_iota(jnp.int32, (CHUNK, CHUNK), 0)
    c = lax.broadcasted_iota(jnp.int32, (CHUNK, CHUNK), 1)
    for gi in range(SGU_GROUPS):
        cols = slice(gi * gw, (gi + 1) * gw)
        wm = jnp.where(r >= c, ws_ref[gi], 0.0).astype(BF16)
        for ci in range(tm // CHUNK):
            rows = slice(ci * CHUNK, (ci + 1) * CHUNK)
            sp = jnp.dot(wm, vn_ref[rows, cols], preferred_element_type=F32) + bias_ref[:, cols]
            gate_ref[rows, cols] = (u_ref[rows, cols].astype(F32) * sp).astype(BF16)
    o_ref[...] = x_ref[...] + jnp.dot(gate_ref[...], wo_ref[...], preferred_element_type=F32)


def _sgu_prompt(zc, x, ln_g, ln_b, w_s, bias, w_out, *, tm):
    t, d = x.shape
    assert t % tm == 0 and tm % CHUNK == 0
    row = lambda a: a.reshape(1, -1)
    return pl.pallas_call(
        functools.partial(_sgu_prompt_kernel, tm=tm),
        grid=(t // tm,),
        in_specs=[
            pl.BlockSpec((tm, SGU_WIDTH), lambda i: (i, 0)),
            pl.BlockSpec((tm, SGU_WIDTH), lambda i: (i, 1)),
            pl.BlockSpec((tm, d), lambda i: (i, 0)),
            pl.BlockSpec((1, SGU_WIDTH), lambda i: (0, 0)),
            pl.BlockSpec((1, SGU_WIDTH), lambda i: (0, 0)),
            pl.BlockSpec((SGU_GROUPS, CHUNK, CHUNK), lambda i: (0, 0, 0)),
            pl.BlockSpec((CHUNK, SGU_WIDTH), lambda i: (0, 0)),
            pl.BlockSpec((SGU_WIDTH, d), lambda i: (0, 0)),
        ],
        out_specs=pl.BlockSpec((tm, d), lambda i: (i, 0)),
        out_shape=jax.ShapeDtypeStruct((t, d), F32),
        scratch_shapes=[pltpu.VMEM((tm, SGU_WIDTH), BF16), pltpu.VMEM((tm, SGU_WIDTH), BF16)],
        compiler_params=_params("parallel"),
        name="sgu_prompt",
    )(zc, zc, x, row(ln_g), row(ln_b), w_s, bias, w_out)


def _sgu_sample_kernel(u_ref, v_ref, lng_ref, lnb_ref, coef_ref, bias_ref, gate_ref, vn_ref):
    vn = _layernorm(v_ref[...], lng_ref[...], lnb_ref[...])
    vn_ref[...] = vn
    gate_ref[...] = (u_ref[...] * (vn * coef_ref[...] + bias_ref[...])).astype(gate_ref.dtype)


def _sgu_sample(zc, ln_g, ln_b, coef, bias):
    b = zc.shape[0]
    row = lambda a: a.reshape(1, -1)
    pspec = pl.BlockSpec((1, SGU_WIDTH), lambda i: (0, 0))
    return pl.pallas_call(
        _sgu_sample_kernel,
        grid=(1,),
        in_specs=[pl.BlockSpec((b, SGU_WIDTH), lambda i: (0, 0)), pl.BlockSpec((b, SGU_WIDTH), lambda i: (0, 1)),
                  pspec, pspec, pspec, pspec],
        out_specs=[pl.BlockSpec((b, SGU_WIDTH), lambda i: (0, 0))] * 2,
        out_shape=[jax.ShapeDtypeStruct((b, SGU_WIDTH), BF16), jax.ShapeDtypeStruct((b, SGU_WIDTH), F32)],
        compiler_params=_params("arbitrary"),
        name="sgu_sample",
    )(zc, zc, row(ln_g), row(ln_b), row(coef), row(bias))


SCAN_TT = 64
RELAYOUT_T = 2 * SCAN_TT
V_PITCH = 72
HALF = RWKV_HEAD // 2


def _split_heads(z_ref, b, xt):
    for h in range(RWKV_HEADS):
        z_ref[b, h * V_PITCH:h * V_PITCH + RWKV_HEAD, :] = xt[h * RWKV_HEAD:(h + 1) * RWKV_HEAD]


def _head_rows(z_ref, b, n):
    return z_ref[b, pl.ds(n, RWKV_HEADS, stride=V_PITCH), :]


def _store_slab(o_ref, n, wt):
    pad = jnp.zeros((V_PITCH - SCAN_TT, LANES), F32)
    for blk in range(2):
        o_ref[blk, n * V_PITCH:n * V_PITCH + SCAN_TT, :] = wt[blk * SCAN_TT:(blk + 1) * SCAN_TT]
        o_ref[blk, n * V_PITCH + SCAN_TT:(n + 1) * V_PITCH, :] = pad


def _key_to_lanes_kernel(x_ref, o_ref, z_ref):
    nb = x_ref.shape[0]
    for b in range(nb):
        _split_heads(z_ref, b, x_ref[b].T)
    for j in range(RWKV_HEAD):
        rows = [_head_rows(z_ref, b, j) for b in range(nb)]
        _store_slab(o_ref, j, jnp.concatenate(rows + rows, axis=0).T)


def _value_to_lanes_kernel(x_ref, o_ref, z_ref):
    nb = x_ref.shape[0]
    for b in range(nb):
        _split_heads(z_ref, b, x_ref[b].T)
    for i in range(HALF):
        rows = [_head_rows(z_ref, b, half * HALF + i) for half in range(2) for b in range(nb)]
        _store_slab(o_ref, i, jnp.concatenate(rows, axis=0).T)


def _value_from_lanes_kernel(y_ref, o_ref, z_ref):
    nb = o_ref.shape[0]
    for i in range(HALF):
        w = jnp.concatenate([y_ref[blk, i * V_PITCH:i * V_PITCH + SCAN_TT, :] for blk in range(2)], axis=0)
        wt = w.T
        for half in range(2):
            for b in range(nb):
                r0 = (half * nb + b) * RWKV_HEADS
                z_ref[b, pl.ds(half * HALF + i, RWKV_HEADS, stride=V_PITCH), :] = wt[r0:r0 + RWKV_HEADS]
    for b in range(nb):
        zt = jnp.concatenate([z_ref[b, h * V_PITCH:h * V_PITCH + RWKV_HEAD, :] for h in range(RWKV_HEADS)], axis=0)
        o_ref[b] = zt.T


def _key_to_lanes(x):
    b, l, w = x.shape
    assert 2 * b * RWKV_HEADS == LANES and l % RELAYOUT_T == 0 and w == RWKV_WIDTH
    out = pl.pallas_call(
        _key_to_lanes_kernel,
        grid=(l // RELAYOUT_T,),
        in_specs=[pl.BlockSpec((b, RELAYOUT_T, w), lambda i: (0, i, 0))],
        out_specs=pl.BlockSpec((2, RWKV_HEAD * V_PITCH, LANES), lambda i: (i, 0, 0)),
        out_shape=jax.ShapeDtypeStruct((l // SCAN_TT, RWKV_HEAD * V_PITCH, LANES), F32),
        scratch_shapes=[pltpu.VMEM((b, RWKV_HEADS * V_PITCH, RELAYOUT_T), F32)],
        compiler_params=_params("parallel"),
        name="key_to_lanes",
    )(x)
    return out[None]


def _value_to_lanes(v):
    b, l, w = v.shape
    assert 2 * b * RWKV_HEADS == LANES and l % RELAYOUT_T == 0 and w == RWKV_WIDTH
    out = pl.pallas_call(
        _value_to_lanes_kernel,
        grid=(l // RELAYOUT_T,),
        in_specs=[pl.BlockSpec((b, RELAYOUT_T, w), lambda i: (0, i, 0))],
        out_specs=pl.BlockSpec((2, HALF * V_PITCH, LANES), lambda i: (i, 0, 0)),
        out_shape=jax.ShapeDtypeStruct((l // SCAN_TT, HALF * V_PITCH, LANES), F32),
        scratch_shapes=[pltpu.VMEM((b, RWKV_HEADS * V_PITCH, RELAYOUT_T), F32)],
        compiler_params=_params("parallel"),
        name="value_to_lanes",
    )(v)
    return out[None]


def _value_from_lanes(y, b):
    l = y.shape[1] * SCAN_TT
    return pl.pallas_call(
        _value_from_lanes_kernel,
        grid=(l // RELAYOUT_T,),
        in_specs=[pl.BlockSpec((2, HALF * V_PITCH, LANES), lambda i: (i, 0, 0))],
        out_specs=pl.BlockSpec((b, RELAYOUT_T, RWKV_WIDTH), lambda i: (0, i, 0)),
        out_shape=jax.ShapeDtypeStruct((b, l, RWKV_WIDTH), F32),
        scratch_shapes=[pltpu.VMEM((b, RWKV_HEADS * V_PITCH, RELAYOUT_T), F32)],
        compiler_params=_params("parallel"),
        name="value_from_lanes",
    )(y[0])


def _state_from_lanes_prompt(s, b):
    s = s.reshape(HALF // F32_SUBLANES, RWKV_HEAD, F32_SUBLANES, 2, b, RWKV_HEADS).transpose(4, 5, 3, 0, 2, 1)
    return s.reshape(b, RWKV_HEADS, RWKV_HEAD, RWKV_HEAD)


def _to_lanes_sample(x):
    b = x.shape[0]
    g = b * RWKV_HEADS // LANES
    x = x.reshape(g, LANES // RWKV_HEADS, RWKV_HEADS, RWKV_HEAD).transpose(0, 3, 1, 2)
    return x.reshape(g, 1, RWKV_HEAD, LANES)


def _from_lanes_sample(y):
    g = y.shape[0]
    y = y.reshape(g, RWKV_HEAD, LANES // RWKV_HEADS, RWKV_HEADS).transpose(0, 2, 3, 1)
    return y.reshape(g * LANES // RWKV_HEADS, RWKV_WIDTH)


def _state_to_lanes_sample(s):
    b = s.shape[0]
    g = b * RWKV_HEADS // LANES
    nib = RWKV_HEAD // F32_SUBLANES
    s = s.reshape(g, LANES // RWKV_HEADS, RWKV_HEADS, nib, F32_SUBLANES, RWKV_HEAD).transpose(0, 3, 5, 4, 1, 2)
    return s.reshape(g, nib, RWKV_HEAD, F32_SUBLANES, LANES)


def _state_from_lanes_sample(s):
    g, nib = s.shape[:2]
    s = s.reshape(g, nib, RWKV_HEAD, F32_SUBLANES, LANES // RWKV_HEADS, RWKV_HEADS).transpose(0, 4, 5, 1, 3, 2)
    return s.reshape(g * LANES // RWKV_HEADS, RWKV_HEADS, RWKV_HEAD, RWKV_HEAD)


def kernel(x_prompt, x_sample, mem_prompt, cache_mem_k, cache_mem_v, state_pool, state_shift, state_wkv, norm_mix_g, norm_xa_g, norm_mem_g, norm_ffn_g, norm_final_g, w_in_ab, w_out_ab, pool_w, pool_scale, rwkv_mu, rwkv_w0, rwkv_w2, rwkv_a0, rwkv_a2, rwkv_g2, rwkv_k_k, rwkv_k_a, rwkv_r_k, rwkv_gn_g, rwkv_gn_b, w_in_c, sgu_ln_g, sgu_ln_b, sgu_w_s, sgu_b_s, w_out_c, w_xq, w_xk, w_xv, w_xo, w_ff_up, w_ff_down):
    bp, lp, d = x_prompt.shape
    bs = x_sample.shape[0]
    tp = bp * lp
    bf = lambda w: w.astype(BF16)

    w_in_ab_b, w_out_ab_b, pool_w_b = bf(w_in_ab), bf(w_out_ab), bf(pool_w)
    w_in_c_b, w_out_c_b = bf(w_in_c), bf(w_out_c)
    w_xq_b, w_xk_b, w_xv_b, w_xo_b = bf(w_xq), bf(w_xk), bf(w_xv), bf(w_xo)
    w_up_b, w_down_b = bf(w_ff_up), bf(w_ff_down)

    mem2 = mem_prompt.reshape(bp * MEM_LEN, d)
    mem_k_p, mem_v_p = [], []
    for l in range(DEPTH):
        mem_k_p.append(_linear([mem2], [w_xk_b[l]], gain=norm_mem_g[l], tm=1024, tn=512, name="mem_k"))
        mem_v_p.append(_linear([mem2], [w_xv_b[l]], gain=norm_mem_g[l], tm=1024, tn=512, name="mem_v"))

    xp = x_prompt.reshape(tp, d)
    xs = x_sample.reshape(bs, d)
    pool_out_p, pool_out_s, shift_out_p, shift_out_s, wkv_out_p, wkv_out_s, sgu_v_s = [], [], [], [], [], [], []

    for l in range(DEPTH):
        j = l // 2
        if l % 2 == 0:
            w_la = jnp.zeros((W_LORA + A_LORA, 2 * RWKV_WIDTH), F32)
            w_la = w_la.at[:W_LORA, :RWKV_WIDTH].set(rwkv_w2[j]).at[W_LORA:, RWKV_WIDTH:].set(rwkv_a2[j])
            params = _prep_param_args(rwkv_mu[j], rwkv_w0[j], rwkv_a0[j], rwkv_k_k[j], rwkv_k_a[j],
                                      rwkv_r_k[j].reshape(-1), bf(w_la), bf(rwkv_g2[j]))
            zp = _linear([xp], [w_in_ab_b[j]], gain=norm_mix_g[l], tm=1024, tn=1152, name="ab_in_prompt")
            zp = zp.reshape(bp, lp, -1)
            y_pool = _pool_prompt(zp, pool_w_b[j], pool_scale[j], tm=512).reshape(tp, POOL_WIDTH)
            r, k2, v, kap, bb, dec, g, bonus = _rwkv_prep_prompt(zp, params, tm=512)
            s0 = jnp.zeros((1, HALF // F32_SUBLANES, RWKV_HEAD, F32_SUBLANES, LANES), F32)
            y_l, s_l = _rwkv_scan(_key_to_lanes(kap), _key_to_lanes(dec), _key_to_lanes(bb), _key_to_lanes(k2),
                                  _key_to_lanes(r), _value_to_lanes(v), s0, tt=SCAN_TT, vp=V_PITCH)
            y = _value_from_lanes(y_l, bp).reshape(tp, RWKV_WIDTH)
            y_rwkv = _rwkv_post(y, bonus.reshape(tp, -1), g.reshape(tp, -1), rwkv_gn_g[j], rwkv_gn_b[j], tm=1024)
            xp = _linear([y_pool, y_rwkv], [w_out_ab_b[j][:POOL_WIDTH], w_out_ab_b[j][POOL_WIDTH:]], res=xp,
                         tm=1024, tn=1024, name="ab_out_prompt")
            pool_out_p.append(zp[:, lp - POOL_BUF:, :POOL_WIDTH])
            shift_out_p.append(zp[:, lp - 1, POOL_WIDTH:])
            wkv_out_p.append(_state_from_lanes_prompt(s_l, bp))
            zs = _linear([xs], [w_in_ab_b[j]], gain=norm_mix_g[l], tm=bs, tn=256, name="ab_in_sample")
            y_pool = _pool_sample(zs, state_pool[j], pool_w_b[j], pool_scale[j])
            r, k2, v, kap, bb, dec, g, bonus = _rwkv_prep_sample(zs, state_shift[j], params)
            y_l, s_l = _rwkv_scan(_to_lanes_sample(kap), _to_lanes_sample(dec), _to_lanes_sample(bb),
                                  _to_lanes_sample(k2), _to_lanes_sample(r), _to_lanes_sample(v),
                                  _state_to_lanes_sample(state_wkv[j]), tt=1, vp=1)
            y_rwkv = _rwkv_post(_from_lanes_sample(y_l), bonus, g, rwkv_gn_g[j], rwkv_gn_b[j], tm=bs)
            xs = _linear([y_pool, y_rwkv], [w_out_ab_b[j][:POOL_WIDTH], w_out_ab_b[j][POOL_WIDTH:]], res=xs,
                         tm=bs, tn=512, name="ab_out_sample")
            pool_out_s.append(jnp.concatenate([state_pool[j][:, 1:], zs[:, None, :POOL_WIDTH]], axis=1))
            shift_out_s.append(zs[:, POOL_WIDTH:])
            wkv_out_s.append(_state_from_lanes_sample(s_l))
        else:
            gw = SGU_WIDTH // SGU_GROUPS
            zc = _linear([xp], [w_in_c_b[j]], gain=norm_mix_g[l], act="gelu", out_dtype=BF16, tm=1024, tn=1024,
                         name="sgu_in_prompt")
            bias = jnp.repeat(sgu_b_s[j].T, gw, axis=1)
            xp = _sgu_prompt(zc, xp, sgu_ln_g[j], sgu_ln_b[j], sgu_w_s[j], bias, w_out_c_b[j], tm=512)
            zc = _linear([xs], [w_in_c_b[j]], gain=norm_mix_g[l], act="gelu", tm=bs, tn=512, name="sgu_in_sample")
            gate, vn = _sgu_sample(zc, sgu_ln_g[j], sgu_ln_b[j], jnp.repeat(sgu_w_s[j][:, 0, 0], gw),
                                   jnp.repeat(sgu_b_s[j][:, 0], gw))
            xs = _linear([gate], [w_out_c_b[j]], res=xs, tm=bs, tn=512, name="sgu_out_sample")
            sgu_v_s.append(vn.reshape(bs, 1, SGU_WIDTH))

        xp = _xattn_prompt(xp.reshape(bp, lp, d), norm_xa_g[l], mem_k_p[l].reshape(bp, MEM_LEN, d),
                           mem_v_p[l].reshape(bp, MEM_LEN, d), w_xq_b[l], w_xo_b[l], tm=512).reshape(tp, d)
        q = _linear([xs], [w_xq_b[l]], gain=norm_xa_g[l], tm=bs, tn=512, name="xattn_q_sample")
        att = _xattn_sample_core(q.reshape(bs, XA_HEADS, XA_HEAD_DIM), cache_mem_k, cache_mem_v, l, bs=4)
        xs = _linear([att.reshape(bs, d)], [w_xo_b[l]], res=xs, tm=bs, tn=512, name="xattn_o_sample")

        fg = norm_final_g if l == DEPTH - 1 else None
        xp = _mlp(xp, norm_ffn_g[l], w_up_b[l], w_down_b[l], final_gain=fg, tm=1024, tf=1024)
        xs = _mlp(xs, norm_ffn_g[l], w_up_b[l], w_down_b[l], final_gain=fg, tm=bs, tf=512)

    head_shape = (bp, MEM_LEN, XA_HEADS, XA_HEAD_DIM)
    return (xp.reshape(bp, lp, d),
            xs.reshape(bs, 1, d),
            jnp.stack([m.reshape(head_shape) for m in mem_k_p]),
            jnp.stack([m.reshape(head_shape) for m in mem_v_p]),
            jnp.stack(pool_out_p),
            jnp.stack(pool_out_s),
            jnp.stack(shift_out_p),
            jnp.stack(shift_out_s),
            jnp.stack(wkv_out_p),
            jnp.stack(wkv_out_s),
            jnp.stack(sgu_v_s))
```

```python
import functools

import jax
import jax.numpy as jnp
from jax import lax
from jax.experimental import pallas as pl
from jax.experimental.pallas import tpu as pltpu

D_MODEL = 1024
DEPTH = 2
PAST_LEN = 16384
POOL_WINDOWS = (2, 4, 8, 16)
POOL_GROUP_WIDTH = 128
POOL_WIDTH = 512
POOL_BUF = 15
RWKV_HEAD = 64
RWKV_WIDTH = 512
RWKV_HEADS = 8
W_LORA = 64
A_LORA = 64
G_LORA = 128
RWKV_IN = 1792
LORA_IN = W_LORA + A_LORA + G_LORA
CHUNK = 128
SGU_WIDTH = 2048
SGU_GROUPS = 4
MEM_LEN = 256
XA_HEADS = 4
XA_HEAD_DIM = 256
D_FF = 4096
RMS_EPS = 1e-5
LN_EPS = 1e-5
GN_EPS = RWKV_HEAD * 1e-5
L2_EPS = 1e-12

LANES = 128
F32_SUBLANES = 8
VMEM_LIMIT = 56 * 1024 * 1024

F32 = jnp.float32
BF16 = jnp.bfloat16


def _params(*sem):
    return pltpu.CompilerParams(dimension_semantics=sem, vmem_limit_bytes=VMEM_LIMIT)


def _dot(a, b):
    return jnp.dot(a.astype(BF16), b.astype(BF16), preferred_element_type=F32)


def _dot_nt(a, b):
    return lax.dot_general(a.astype(BF16), b.astype(BF16), (((1,), (1,)), ((), ())),
                           preferred_element_type=F32)


def _rms(x, g):
    return x * lax.rsqrt(jnp.mean(x * x, axis=-1, keepdims=True) + RMS_EPS) * g


def _head_ones(width, head):
    r = lax.broadcasted_iota(jnp.int32, (width, width), 0) // head
    c = lax.broadcasted_iota(jnp.int32, (width, width), 1) // head
    return (r == c).astype(BF16)


def _head_sum(x, ones):
    hi = x.astype(BF16)
    lo = (x - hi.astype(F32)).astype(BF16)
    return (jnp.dot(hi, ones, preferred_element_type=F32)
            + jnp.dot(lo, ones, preferred_element_type=F32))


def _linear_kernel(*refs, nx, has_gain, act, has_res):
    x_refs = refs[:nx]
    pos = nx
    g_ref = refs[pos] if has_gain else None
    pos += int(has_gain)
    w_refs = refs[pos:pos + nx]
    pos += nx
    res_ref = refs[pos] if has_res else None
    pos += int(has_res)
    o_ref = refs[pos]
    if has_gain and len(refs) == pos + 1:
        lhs = [_rms(x_refs[0][...].astype(F32), g_ref[...]).astype(BF16)]
    elif has_gain:
        xn_ref = refs[pos + 1]

        @pl.when(pl.program_id(1) == 0)
        def _():
            xn_ref[...] = _rms(x_refs[0][...].astype(F32), g_ref[...]).astype(BF16)

        lhs = [xn_ref[...]]
    else:
        lhs = [r[...] for r in x_refs]
    tn = o_ref.shape[1]
    cw = next(c for c in (256, 384, 128) if tn % c == 0)
    for c0 in range(0, tn, cw):
        cols = slice(c0, c0 + cw)
        acc = _dot(lhs[0], w_refs[0][:, cols])
        for l, w in zip(lhs[1:], w_refs[1:]):
            acc = acc + _dot(l, w[:, cols])
        if act == "gelu":
            acc = 0.5 * acc * (1.0 + lax.erf(acc * 0.7071067811865476))
        if has_res:
            acc = acc + res_ref[:, cols]
        o_ref[:, cols] = acc.astype(o_ref.dtype)


def _linear(xs, ws, *, gain=None, act=None, res=None, out_dtype=F32, tm, tn, name):
    m = xs[0].shape[0]
    n = ws[0].shape[1]
    assert m % tm == 0 and n % tn == 0, (m, tm, n, tn)
    assert gain is None or len(xs) == 1
    in_specs = [pl.BlockSpec((tm, x.shape[1]), lambda i, j: (i, 0)) for x in xs]
    args = list(xs)
    if gain is not None:
        in_specs.append(pl.BlockSpec((1, gain.shape[-1]), lambda i, j: (0, 0)))
        args.append(gain.reshape(1, -1))
    in_specs += [pl.BlockSpec((w.shape[0], tn), lambda i, j: (0, j)) for w in ws]
    args += list(ws)
    if res is not None:
        in_specs.append(pl.BlockSpec((tm, tn), lambda i, j: (i, j)))
        args.append(res)
    scratch = [pltpu.VMEM((tm, xs[0].shape[1]), BF16)] if gain is not None and n > tn else []
    kern = functools.partial(_linear_kernel, nx=len(xs), has_gain=gain is not None, act=act,
                             has_res=res is not None)
    return pl.pallas_call(
        kern,
        grid=(m // tm, n // tn),
        in_specs=in_specs,
        out_specs=pl.BlockSpec((tm, tn), lambda i, j: (i, j)),
        out_shape=jax.ShapeDtypeStruct((m, n), out_dtype),
        scratch_shapes=scratch,
        compiler_params=_params("parallel", "arbitrary"),
        name=name,
    )(*args)


def _mlp_kernel(*refs, final_norm):
    if final_norm:
        x_ref, g_ref, wu_ref, wd_ref, gf_ref, o_ref, xn_ref, acc_ref = refs
    else:
        x_ref, g_ref, wu_ref, wd_ref, o_ref, xn_ref, acc_ref = refs
    j = pl.program_id(1)

    @pl.when(j == 0)
    def _():
        xn_ref[...] = _rms(x_ref[...], g_ref[...]).astype(BF16)
        acc_ref[...] = jnp.zeros_like(acc_ref)

    h = jnp.dot(xn_ref[...], wu_ref[...], preferred_element_type=F32)
    h = jnp.square(jnp.maximum(h, 0.0)).astype(BF16)
    acc_ref[...] += jnp.dot(h, wd_ref[...], preferred_element_type=F32)

    @pl.when(j == pl.num_programs(1) - 1)
    def _():
        y = x_ref[...] + acc_ref[...]
        if final_norm:
            y = _rms(y, gf_ref[...])
        o_ref[...] = y


def _mlp(x, gain, w_up, w_down, *, final_gain=None, tm, tf):
    m, d = x.shape
    f = w_up.shape[1]
    assert m % tm == 0 and f % tf == 0
    in_specs = [
        pl.BlockSpec((tm, d), lambda i, j: (i, 0)),
        pl.BlockSpec((1, d), lambda i, j: (0, 0)),
        pl.BlockSpec((d, tf), lambda i, j: (0, j)),
        pl.BlockSpec((tf, d), lambda i, j: (j, 0)),
    ]
    args = [x, gain.reshape(1, d), w_up, w_down]
    if final_gain is not None:
        in_specs.append(pl.BlockSpec((1, d), lambda i, j: (0, 0)))
        args.append(final_gain.reshape(1, d))
    return pl.pallas_call(
        functools.partial(_mlp_kernel, final_norm=final_gain is not None),
        grid=(m // tm, f // tf),
        in_specs=in_specs,
        out_specs=pl.BlockSpec((tm, d), lambda i, j: (i, 0)),
        out_shape=jax.ShapeDtypeStruct((m, d), F32),
        scratch_shapes=[pltpu.VMEM((tm, d), BF16), pltpu.VMEM((tm, d), F32)],
        compiler_params=_params("parallel", "arbitrary"),
        name="mlp",
    )(*args)


def _xattn_prompt_kernel(x_ref, g_ref, k_ref, v_ref, wq_ref, wo_ref, o_ref, att_ref):
    x = x_ref[0]
    xn = _rms(x, g_ref[...]).astype(BF16)
    q = jnp.dot(xn, wq_ref[...], preferred_element_type=F32) * (XA_HEAD_DIM ** -0.5)
    for h in range(XA_HEADS):
        cols = slice(h * XA_HEAD_DIM, (h + 1) * XA_HEAD_DIM)
        s = _dot_nt(q[:, cols], k_ref[0, :, cols])
        s = s - jnp.max(s, axis=-1, keepdims=True)
        p = jnp.exp(s)
        p = p / jnp.sum(p, axis=-1, keepdims=True)
        att_ref[:, cols] = _dot(p, v_ref[0, :, cols]).astype(BF16)
    o_ref[0] = x + jnp.dot(att_ref[...], wo_ref[...], preferred_element_type=F32)


def _xattn_prompt(x, gain, mem_k, mem_v, w_q, w_o, *, tm):
    b, l, d = x.shape
    assert l % tm == 0
    return pl.pallas_call(
        _xattn_prompt_kernel,
        grid=(b, l // tm),
        in_specs=[
            pl.BlockSpec((1, tm, d), lambda bi, i: (bi, i, 0)),
            pl.BlockSpec((1, d), lambda bi, i: (0, 0)),
            pl.BlockSpec((1, MEM_LEN, d), lambda bi, i: (bi, 0, 0)),
            pl.BlockSpec((1, MEM_LEN, d), lambda bi, i: (bi, 0, 0)),
            pl.BlockSpec((d, d), lambda bi, i: (0, 0)),
            pl.BlockSpec((d, d), lambda bi, i: (0, 0)),
        ],
        out_specs=pl.BlockSpec((1, tm, d), lambda bi, i: (bi, i, 0)),
        out_shape=jax.ShapeDtypeStruct((b, l, d), F32),
        scratch_shapes=[pltpu.VMEM((tm, d), BF16)],
        compiler_params=_params("parallel", "parallel"),
        name="xattn_prompt",
    )(x, gain.reshape(1, d), mem_k, mem_v, w_q, w_o)


def _xattn_sample_kernel(q_ref, k_ref, v_ref, o_ref, *, bs):
    for b in range(bs):
        q = q_ref[b] * (XA_HEAD_DIM ** -0.5)
        s = jnp.sum(k_ref[0, b] * q[None], axis=-1, keepdims=True)
        p = jnp.exp(s - jnp.max(s, axis=0, keepdims=True))
        den = jnp.sum(p, axis=0)
        o_ref[b] = jnp.sum(p * v_ref[0, b], axis=0) / den


def _xattn_sample_core(q, cache_k, cache_v, layer, *, bs):
    b = q.shape[0]
    assert b % bs == 0
    q_spec = pl.BlockSpec((bs, XA_HEADS, XA_HEAD_DIM), lambda i: (i, 0, 0))
    kv_spec = pl.BlockSpec((1, bs, MEM_LEN, XA_HEADS, XA_HEAD_DIM), lambda i: (layer, i, 0, 0, 0))
    return pl.pallas_call(
        functools.partial(_xattn_sample_kernel, bs=bs),
        grid=(b // bs,),
        in_specs=[q_spec, kv_spec, kv_spec],
        out_specs=q_spec,
        out_shape=jax.ShapeDtypeStruct((b, XA_HEADS, XA_HEAD_DIM), F32),
        compiler_params=_params("parallel"),
        name="xattn_sample",
    )(q, cache_k, cache_v)


HALO = 16


def _pool_prompt_kernel(z_ref, halo_ref, w_ref, scale_ref, o_ref, ext_ref, *, tm, tiles_per_seq):
    t = pl.program_id(1)
    ext_ref[HALO:HALO + tm, :] = z_ref[0]

    @pl.when(t == 0)
    def _():
        ext_ref[0:HALO, :] = jnp.zeros((HALO, POOL_WIDTH), F32)

    @pl.when(t > 0)
    def _():
        ext_ref[0:HALO, :] = halo_ref[0]

    pos = t * tm + lax.broadcasted_iota(jnp.int32, (tm, 1), 0)
    for gi, win in enumerate(POOL_WINDOWS):
        cols = slice(gi * POOL_GROUP_WIDTH, (gi + 1) * POOL_GROUP_WIDTH)
        tok = ext_ref[HALO:HALO + tm, cols]
        acc = tok
        for k in range(1, win):
            acc = acc + ext_ref[HALO - k:HALO - k + tm, cols]
        count = jnp.minimum(pos + 1, win).astype(F32)
        dlt = acc / count - tok
        o_ref[0, :, cols] = (_dot(dlt, w_ref[gi]) * scale_ref[:, cols]).astype(o_ref.dtype)


def _pool_prompt(z, w_group, scale, *, tm):
    b, l, _ = z.shape
    assert l % tm == 0 and tm % HALO == 0
    tiles = l // tm
    hb = tm // HALO
    return pl.pallas_call(
        functools.partial(_pool_prompt_kernel, tm=tm, tiles_per_seq=tiles),
        grid=(b, tiles),
        in_specs=[
            pl.BlockSpec((1, tm, POOL_WIDTH), lambda bi, i: (bi, i, 0)),
            pl.BlockSpec((1, HALO, POOL_WIDTH), lambda bi, i: (bi, jnp.maximum(i * hb - 1, 0), 0)),
            pl.BlockSpec((len(POOL_WINDOWS), POOL_GROUP_WIDTH, POOL_GROUP_WIDTH), lambda bi, i: (0, 0, 0)),
            pl.BlockSpec((1, POOL_WIDTH), lambda bi, i: (0, 0)),
        ],
        out_specs=pl.BlockSpec((1, tm, POOL_WIDTH), lambda bi, i: (bi, i, 0)),
        out_shape=jax.ShapeDtypeStruct((b, l, POOL_WIDTH), BF16),
        scratch_shapes=[pltpu.VMEM((HALO + tm, POOL_WIDTH), F32)],
        compiler_params=_params("parallel", "parallel"),
        name="pool_prompt",
    )(z, z, w_group, scale.reshape(1, POOL_WIDTH))


def _pool_sample_kernel(z_ref, buf_ref, w_ref, scale_ref, o_ref):
    for gi, win in enumerate(POOL_WINDOWS):
        cols = slice(gi * POOL_GROUP_WIDTH, (gi + 1) * POOL_GROUP_WIDTH)
        tok = z_ref[:, cols]
        acc = tok
        for k in range(1, win):
            acc = acc + buf_ref[:, POOL_BUF - k, cols]
        count = float(min(PAST_LEN + 1, win))
        dlt = acc / count - tok
        o_ref[:, cols] = (_dot(dlt, w_ref[gi]) * scale_ref[:, cols]).astype(o_ref.dtype)


def _pool_sample(z, buf, w_group, scale):
    b = z.shape[0]
    return pl.pallas_call(
        _pool_sample_kernel,
        grid=(1,),
        in_specs=[
            pl.BlockSpec((b, POOL_WIDTH), lambda i: (0, 0)),
            pl.BlockSpec((b, POOL_BUF, POOL_WIDTH), lambda i: (0, 0, 0)),
            pl.BlockSpec((len(POOL_WINDOWS), POOL_GROUP_WIDTH, POOL_GROUP_WIDTH), lambda i: (0, 0, 0)),
            pl.BlockSpec((1, POOL_WIDTH), lambda i: (0, 0)),
        ],
        out_specs=pl.BlockSpec((b, POOL_WIDTH), lambda i: (0, 0)),
        out_shape=jax.ShapeDtypeStruct((b, POOL_WIDTH), BF16),
        compiler_params=_params("arbitrary"),
        name="pool_sample",
    )(z, buf, w_group, scale.reshape(1, POOL_WIDTH))


def _rwkv_prep_math(z4, s4, p, out_refs, sl):
    zr, zk, zv, zl = z4
    sr, sk, sv, sq = s4
    mu_r, mu_k, mu_v, mu_l, w0, a0, k_k, k_a, r_k, w_la, w_g = p
    r = zr + (sr - zr) * mu_r
    k = zk + (sk - zk) * mu_k
    v = zv + (sv - zv) * mu_v
    lo = zl + (sq - zl) * mu_l
    lane = lax.broadcasted_iota(jnp.int32, lo.shape, 1)
    feat = jnp.where(lane < W_LORA, jnp.tanh(lo), jnp.where(lane < W_LORA + A_LORA, lo, jax.nn.sigmoid(lo)))
    wa = _dot(feat[:, :W_LORA + A_LORA], w_la)
    g = _dot(feat[:, W_LORA + A_LORA:], w_g)
    wlog = -jax.nn.softplus(-(w0 + wa[:, :RWKV_WIDTH])) - 0.5
    decay = jnp.exp(-jnp.exp(wlog))
    a = jax.nn.sigmoid(a0 + wa[:, RWKV_WIDTH:])
    kk = k * k_k
    k2 = k * (1.0 + (a - 1.0) * k_a)
    ones = _head_ones(RWKV_WIDTH, RWKV_HEAD)
    kk = kk / jnp.maximum(jnp.sqrt(_head_sum(kk * kk, ones)), L2_EPS)
    bonus = _head_sum(r * k2 * r_k, ones) * v
    r_o, k_o, v_o, kap_o, b_o, d_o, g_o, bonus_o = out_refs
    r_o[sl] = r
    k_o[sl] = k2
    v_o[sl] = v
    kap_o[sl] = kk
    b_o[sl] = kk * a
    d_o[sl] = decay
    g_o[sl] = g
    bonus_o[sl] = bonus


def _load_params(refs):
    return tuple(r[...] for r in refs)


def _rwkv_prep_prompt_kernel(zr_ref, zk_ref, zv_ref, zl_ref, *refs, tm):
    p_refs = refs[:11]
    out_refs = refs[11:19]
    er_ref, ek_ref, ev_ref, el_ref = refs[19:23]
    t = pl.program_id(1)
    row = 7
    z4, s4 = [], []
    for z_ref, e_ref in ((zr_ref, er_ref), (zk_ref, ek_ref), (zv_ref, ev_ref), (zl_ref, el_ref)):
        z = z_ref[0]

        @pl.when(t == 0)
        def _():
            e_ref[0:8, :] = jnp.zeros((8, e_ref.shape[1]), F32)

        @pl.when(t > 0)
        def _():
            e_ref[row:row + 1, :] = e_ref[row + tm:row + tm + 1, :]

        e_ref[8:8 + tm, :] = z
        z4.append(z)
        s4.append(e_ref[row:row + tm, :])
    _rwkv_prep_math(z4, s4, _load_params(p_refs), out_refs, (0,))


def _prep_param_args(mu, w0, a0, k_k, k_a, r_k, w_la, w_g):
    w = RWKV_WIDTH
    row = lambda x: x.reshape(1, -1)
    return [row(mu[:w]), row(mu[w:2 * w]), row(mu[2 * w:3 * w]), row(mu[3 * w:]),
            row(w0), row(a0), row(k_k), row(k_a), row(r_k), w_la, w_g]


def _prep_out_shapes(lead):
    return [jax.ShapeDtypeStruct(lead + (RWKV_WIDTH,), F32) for _ in range(8)]


def _rwkv_prep_prompt(z, params, *, tm):
    b, l, _ = z.shape
    assert l % tm == 0
    pw = POOL_WIDTH // RWKV_WIDTH
    full = lambda a: pl.BlockSpec(a.shape, lambda bi, i: (0,) * a.ndim)
    in_specs = [
        pl.BlockSpec((1, tm, RWKV_WIDTH), lambda bi, i: (bi, i, pw)),
        pl.BlockSpec((1, tm, RWKV_WIDTH), lambda bi, i: (bi, i, pw + 1)),
        pl.BlockSpec((1, tm, RWKV_WIDTH), lambda bi, i: (bi, i, pw + 2)),
        pl.BlockSpec((1, tm, LORA_IN), lambda bi, i: (bi, i, (POOL_WIDTH + 3 * RWKV_WIDTH) // LORA_IN)),
    ] + [full(a) for a in params]
    out_spec = pl.BlockSpec((1, tm, RWKV_WIDTH), lambda bi, i: (bi, i, 0))
    return pl.pallas_call(
        functools.partial(_rwkv_prep_prompt_kernel, tm=tm),
        grid=(b, l // tm),
        in_specs=in_specs,
        out_specs=[out_spec] * 8,
        out_shape=_prep_out_shapes((b, l)),
        scratch_shapes=[pltpu.VMEM((tm + 8, RWKV_WIDTH), F32)] * 3 + [pltpu.VMEM((tm + 8, LORA_IN), F32)],
        compiler_params=_params("parallel", "arbitrary"),
        name="rwkv_prep_prompt",
    )(z, z, z, z, *params)


def _rwkv_prep_sample_kernel(zr_ref, zk_ref, zv_ref, zl_ref, sr_ref, sk_ref, sv_ref, sl_ref, *refs):
    p_refs = refs[:11]
    out_refs = refs[11:19]
    z4 = [zr_ref[...], zk_ref[...], zv_ref[...], zl_ref[...]]
    s4 = [sr_ref[...], sk_ref[...], sv_ref[...], sl_ref[...]]
    _rwkv_prep_math(z4, s4, _load_params(p_refs), out_refs, (Ellipsis,))


def _rwkv_prep_sample(z, shift, params):
    b = z.shape[0]
    pw = POOL_WIDTH // RWKV_WIDTH
    full = lambda a: pl.BlockSpec(a.shape, lambda i: (0,) * a.ndim)
    in_specs = [
        pl.BlockSpec((b, RWKV_WIDTH), lambda i: (0, pw)),
        pl.BlockSpec((b, RWKV_WIDTH), lambda i: (0, pw + 1)),
        pl.BlockSpec((b, RWKV_WIDTH), lambda i: (0, pw + 2)),
        pl.BlockSpec((b, LORA_IN), lambda i: (0, (POOL_WIDTH + 3 * RWKV_WIDTH) // LORA_IN)),
        pl.BlockSpec((b, RWKV_WIDTH), lambda i: (0, 0)),
        pl.BlockSpec((b, RWKV_WIDTH), lambda i: (0, 1)),
        pl.BlockSpec((b, RWKV_WIDTH), lambda i: (0, 2)),
        pl.BlockSpec((b, LORA_IN), lambda i: (0, 3 * RWKV_WIDTH // LORA_IN)),
    ] + [full(a) for a in params]
    out_spec = pl.BlockSpec((b, RWKV_WIDTH), lambda i: (0, 0))
    return pl.pallas_call(
        _rwkv_prep_sample_kernel,
        grid=(1,),
        in_specs=in_specs,
        out_specs=[out_spec] * 8,
        out_shape=_prep_out_shapes((b,)),
        compiler_params=_params("arbitrary"),
        name="rwkv_prep_sample",
    )(z, z, z, z, shift, shift, shift, shift, *params)


def _rwkv_scan_kernel(kap_ref, d_ref, b_ref, k_ref, r_ref, v_ref, s0_ref, y_ref, sout_ref, s_ref, *, tt, nib, vp):
    tb = pl.program_id(1)
    sub = F32_SUBLANES

    @pl.when(tb == 0)
    def _():
        s_ref[...] = s0_ref[0]

    for i in range(nib * sub if vp > tt else 0):
        y_ref[0, i * vp + tt:(i + 1) * vp, :] = jnp.zeros((vp - tt, LANES), F32)

    nparts = 4

    def add_to(acc, slot, term):
        acc[slot] = term if acc[slot] is None else acc[slot] + term

    def tree_sum(xs):
        while len(xs) > 1:
            xs = [xs[k] + xs[k + 1] for k in range(0, len(xs), 2)]
        return xs[0]

    def step(t, carry):
        vy_rows = [pl.ds(ib * sub * vp + t, sub, stride=vp) if vp > 1 else pl.ds(ib * sub, sub) for ib in range(nib)]
        v = [v_ref[0, rows, :] for rows in vy_rows]
        acc = [None] * (nparts * nib)
        for j in range(RWKV_HEAD):
            kap = kap_ref[0, pl.ds(j * vp + t, 1), :]
            for ib in range(nib):
                add_to(acc, nparts * ib + j % nparts, s_ref[ib, j] * kap)
        sa = [-tree_sum(acc[nparts * ib:nparts * (ib + 1)]) for ib in range(nib)]
        acc = [None] * (nparts * nib)
        for j in range(RWKV_HEAD):
            row = pl.ds(j * vp + t, 1)
            dec = d_ref[0, row, :]
            bb = b_ref[0, row, :]
            kk = k_ref[0, row, :]
            rr = r_ref[0, row, :]
            for ib in range(nib):
                s = s_ref[ib, j] * dec + sa[ib] * bb + v[ib] * kk
                s_ref[ib, j] = s
                add_to(acc, nparts * ib + j % nparts, s * rr)
        for ib in range(nib):
            y_ref[0, vy_rows[ib], :] = tree_sum(acc[nparts * ib:nparts * (ib + 1)])
        return carry

    lax.fori_loop(0, tt, step, 0)

    @pl.when(tb == pl.num_programs(1) - 1)
    def _():
        sout_ref[0] = s_ref[...]


def _rwkv_scan(kap, dec, bb, kk, rr, v, s0, *, tt, vp):
    g, nb, rows, _ = kap.shape
    n = RWKV_HEAD
    nib = s0.shape[1]
    assert rows == n * vp and v.shape[2] == nib * F32_SUBLANES * vp and tt <= vp
    op_spec = pl.BlockSpec((None, 1, n * vp, LANES), lambda gi, i: (gi, i, 0, 0))
    v_spec = pl.BlockSpec((None, 1, v.shape[2], LANES), lambda gi, i: (gi, i, 0, 0))
    s_spec = pl.BlockSpec((1, nib, n, F32_SUBLANES, LANES), lambda gi, i: (gi, 0, 0, 0, 0))
    return pl.pallas_call(
        functools.partial(_rwkv_scan_kernel, tt=tt, nib=nib, vp=vp),
        grid=(g, nb),
        in_specs=[op_spec] * 5 + [v_spec, s_spec],
        out_specs=[v_spec, s_spec],
        out_shape=[jax.ShapeDtypeStruct(v.shape, F32), jax.ShapeDtypeStruct(s0.shape, F32)],
        scratch_shapes=[pltpu.VMEM((nib, n, F32_SUBLANES, LANES), F32)],
        compiler_params=_params("parallel", "arbitrary"),
        name="rwkv_scan",
    )(kap, dec, bb, kk, rr, v, s0)


def _rwkv_post_kernel(y_ref, bonus_ref, g_ref, gng_ref, gnb_ref, o_ref):
    y = y_ref[...]
    ones = _head_ones(RWKV_WIDTH, RWKV_HEAD)
    m = _head_sum(y, ones) * (1.0 / RWKV_HEAD)
    c = y - m
    var = _head_sum(c * c, ones) * (1.0 / RWKV_HEAD)
    yn = c * lax.rsqrt(var + GN_EPS) * gng_ref[...] + gnb_ref[...]
    o_ref[...] = ((yn + bonus_ref[...]) * g_ref[...]).astype(o_ref.dtype)


def _rwkv_post(y, bonus, g, gn_g, gn_b, *, tm):
    m = y.shape[0]
    assert m % tm == 0
    spec = pl.BlockSpec((tm, RWKV_WIDTH), lambda i: (i, 0))
    pspec = pl.BlockSpec((1, RWKV_WIDTH), lambda i: (0, 0))
    return pl.pallas_call(
        _rwkv_post_kernel,
        grid=(m // tm,),
        in_specs=[spec, spec, spec, pspec, pspec],
        out_specs=spec,
        out_shape=jax.ShapeDtypeStruct((m, RWKV_WIDTH), BF16),
        compiler_params=_params("parallel"),
        name="rwkv_post",
    )(y, bonus, g, gn_g.reshape(1, -1), gn_b.reshape(1, -1))


def _layernorm(v, g, b):
    m = jnp.mean(v, axis=-1, keepdims=True)
    c = v - m
    var = jnp.mean(c * c, axis=-1, keepdims=True)
    return c * lax.rsqrt(var + LN_EPS) * g + b


def _sgu_prompt_kernel(u_ref, v_ref, x_ref, lng_ref, lnb_ref, ws_ref, bias_ref, wo_ref, o_ref, vn_ref, gate_ref, *, tm):
    vn_ref[...] = _layernorm(v_ref[...].astype(F32), lng_ref[...], lnb_ref[...]).astype(BF16)
    gw = SGU_WIDTH // SGU_GROUPS
    r = lax.broadcasted_iota(jnp.int32, (CHUNK, CHUNK), 0)
    c = lax.broadcasted_iota(jnp.int32, (CHUNK, CHUNK), 1)
    for gi in range(SGU_GROUPS):
        cols = slice(gi * gw, (gi + 1) * gw)
        wm = jnp.where(r >= c, ws_ref[gi], 0.0).astype(BF16)
        for ci in range(tm // CHUNK):
            rows = slice(ci * CHUNK, (ci + 1) * CHUNK)
            sp = jnp.dot(wm, vn_ref[rows, cols], preferred_element_type=F32) + bias_ref[:, cols]
            gate_ref[rows, cols] = (u_ref[rows, cols].astype(F32) * sp).astype(BF16)
    o_ref[...] = x_ref[...] + jnp.dot(gate_ref[...], wo_ref[...], preferred_element_type=F32)


def _sgu_prompt(zc, x, ln_g, ln_b, w_s, bias, w_out, *, tm):
    t, d = x.shape
    assert t % tm == 0 and tm % CHUNK == 0
    row = lambda a: a.reshape(1, -1)
    return pl.pallas_call(
        functools.partial(_sgu_prompt_kernel, tm=tm),
        grid=(t // tm,),
        in_specs=[
            pl.BlockSpec((tm, SGU_WIDTH), lambda i: (i, 0)),
            pl.BlockSpec((tm, SGU_WIDTH), lambda i: (i, 1)),
            pl.BlockSpec((tm, d), lambda i: (i, 0)),
            pl.BlockSpec((1, SGU_WIDTH), lambda i: (0, 0)),
            pl.BlockSpec((1, SGU_WIDTH), lambda i: (0, 0)),
            pl.BlockSpec((SGU_GROUPS, CHUNK, CHUNK), lambda i: (0, 0, 0)),
            pl.BlockSpec((CHUNK, SGU_WIDTH), lambda i: (0, 0)),
            pl.BlockSpec((SGU_WIDTH, d), lambda i: (0, 0)),
        ],
        out_specs=pl.BlockSpec((tm, d), lambda i: (i, 0)),
        out_shape=jax.ShapeDtypeStruct((t, d), F32),
        scratch_shapes=[pltpu.VMEM((tm, SGU_WIDTH), BF16), pltpu.VMEM((tm, SGU_WIDTH), BF16)],
        compiler_params=_params("parallel"),
        name="sgu_prompt",
    )(zc, zc, x, row(ln_g), row(ln_b), w_s, bias, w_out)


def _sgu_sample_kernel(u_ref, v_ref, lng_ref, lnb_ref, coef_ref, bias_ref, gate_ref, vn_ref):
    vn = _layernorm(v_ref[...], lng_ref[...], lnb_ref[...])
    vn_ref[...] = vn
    gate_ref[...] = (u_ref[...] * (vn * coef_ref[...] + bias_ref[...])).astype(gate_ref.dtype)


def _sgu_sample(zc, ln_g, ln_b, coef, bias):
    b = zc.shape[0]
    row = lambda a: a.reshape(1, -1)
    pspec = pl.BlockSpec((1, SGU_WIDTH), lambda i: (0, 0))
    return pl.pallas_call(
        _sgu_sample_kernel,
        grid=(1,),
        in_specs=[pl.BlockSpec((b, SGU_WIDTH), lambda i: (0, 0)), pl.BlockSpec((b, SGU_WIDTH), lambda i: (0, 1)),
                  pspec, pspec, pspec, pspec],
        out_specs=[pl.BlockSpec((b, SGU_WIDTH), lambda i: (0, 0))] * 2,
        out_shape=[jax.ShapeDtypeStruct((b, SGU_WIDTH), BF16), jax.ShapeDtypeStruct((b, SGU_WIDTH), F32)],
        compiler_params=_params("arbitrary"),
        name="sgu_sample",
    )(zc, zc, row(ln_g), row(ln_b), row(coef), row(bias))


SCAN_TT = 64
RELAYOUT_BLOCKS = 4
RELAYOUT_T = RELAYOUT_BLOCKS * SCAN_TT
V_PITCH = 72
HALF = RWKV_HEAD // 2


def _split_heads(z_ref, b, x_ref, c):
    xt = x_ref[b, c * LANES:(c + 1) * LANES, :].T
    for h in range(RWKV_HEADS):
        z_ref[b, h * V_PITCH:h * V_PITCH + RWKV_HEAD, :] = xt[h * RWKV_HEAD:(h + 1) * RWKV_HEAD]


def _head_rows(z_ref, b, n):
    return z_ref[b, pl.ds(n, RWKV_HEADS, stride=V_PITCH), :]


def _store_slab(o_ref, c, n, wt):
    pad = jnp.zeros((V_PITCH - SCAN_TT, LANES), F32)
    for k in range(LANES // SCAN_TT):
        blk = c * (LANES // SCAN_TT) + k
        o_ref[blk, n * V_PITCH:n * V_PITCH + SCAN_TT, :] = wt[k * SCAN_TT:(k + 1) * SCAN_TT]
        o_ref[blk, n * V_PITCH + SCAN_TT:(n + 1) * V_PITCH, :] = pad


def _key_to_lanes_kernel(x_ref, o_ref, z_ref):
    nb = x_ref.shape[0]
    for c in range(RELAYOUT_T // LANES):
        for b in range(nb):
            _split_heads(z_ref, b, x_ref, c)
        for j in range(RWKV_HEAD):
            rows = [_head_rows(z_ref, b, j) for b in range(nb)]
            _store_slab(o_ref, c, j, jnp.concatenate(rows + rows, axis=0).T)


def _value_to_lanes_kernel(x_ref, o_ref, z_ref):
    nb = x_ref.shape[0]
    for c in range(RELAYOUT_T // LANES):
        for b in range(nb):
            _split_heads(z_ref, b, x_ref, c)
        for i in range(HALF):
            rows = [_head_rows(z_ref, b, half * HALF + i) for half in range(2) for b in range(nb)]
            _store_slab(o_ref, c, i, jnp.concatenate(rows, axis=0).T)


def _value_from_lanes_kernel(y_ref, o_ref, z_ref):
    nb = o_ref.shape[0]
    per = LANES // SCAN_TT
    for c in range(RELAYOUT_T // LANES):
        for i in range(HALF):
            w = jnp.concatenate([y_ref[c * per + k, i * V_PITCH:i * V_PITCH + SCAN_TT, :] for k in range(per)], axis=0)
            wt = w.T
            for half in range(2):
                for b in range(nb):
                    r0 = (half * nb + b) * RWKV_HEADS
                    z_ref[b, pl.ds(half * HALF + i, RWKV_HEADS, stride=V_PITCH), :] = wt[r0:r0 + RWKV_HEADS]
        for b in range(nb):
            zt = jnp.concatenate([z_ref[b, h * V_PITCH:h * V_PITCH + RWKV_HEAD, :] for h in range(RWKV_HEADS)], axis=0)
            o_ref[b, c * LANES:(c + 1) * LANES, :] = zt.T


def _key_to_lanes(x):
    b, l, w = x.shape
    assert 2 * b * RWKV_HEADS == LANES and l % RELAYOUT_T == 0 and w == RWKV_WIDTH
    out = pl.pallas_call(
        _key_to_lanes_kernel,
        grid=(l // RELAYOUT_T,),
        in_specs=[pl.BlockSpec((b, RELAYOUT_T, w), lambda i: (0, i, 0))],
        out_specs=pl.BlockSpec((RELAYOUT_BLOCKS, RWKV_HEAD * V_PITCH, LANES), lambda i: (i, 0, 0)),
        out_shape=jax.ShapeDtypeStruct((l // SCAN_TT, RWKV_HEAD * V_PITCH, LANES), F32),
        scratch_shapes=[pltpu.VMEM((b, RWKV_HEADS * V_PITCH, LANES), F32)],
        compiler_params=_params("parallel"),
        name="key_to_lanes",
    )(x)
    return out[None]


def _value_to_lanes(v):
    b, l, w = v.shape
    assert 2 * b * RWKV_HEADS == LANES and l % RELAYOUT_T == 0 and w == RWKV_WIDTH
    out = pl.pallas_call(
        _value_to_lanes_kernel,
        grid=(l // RELAYOUT_T,),
        in_specs=[pl.BlockSpec((b, RELAYOUT_T, w), lambda i: (0, i, 0))],
        out_specs=pl.BlockSpec((RELAYOUT_BLOCKS, HALF * V_PITCH, LANES), lambda i: (i, 0, 0)),
        out_shape=jax.ShapeDtypeStruct((l // SCAN_TT, HALF * V_PITCH, LANES), F32),
        scratch_shapes=[pltpu.VMEM((b, RWKV_HEADS * V_PITCH, LANES), F32)],
        compiler_params=_params("parallel"),
        name="value_to_lanes",
    )(v)
    return out[None]


def _value_from_lanes(y, b):
    l = y.shape[1] * SCAN_TT
    return pl.pallas_call(
        _value_from_lanes_kernel,
        grid=(l // RELAYOUT_T,),
        in_specs=[pl.BlockSpec((RELAYOUT_BLOCKS, HALF * V_PITCH, LANES), lambda i: (i, 0, 0))],
        out_specs=pl.BlockSpec((b, RELAYOUT_T, RWKV_WIDTH), lambda i: (0, i, 0)),
        out_shape=jax.ShapeDtypeStruct((b, l, RWKV_WIDTH), F32),
        scratch_shapes=[pltpu.VMEM((b, RWKV_HEADS * V_PITCH, LANES), F32)],
        compiler_params=_params("parallel"),
        name="value_from_lanes",
    )(y[0])


def _state_from_lanes_prompt(s, b):
    s = s.reshape(HALF // F32_SUBLANES, RWKV_HEAD, F32_SUBLANES, 2, b, RWKV_HEADS).transpose(4, 5, 3, 0, 2, 1)
    return s.reshape(b, RWKV_HEADS, RWKV_HEAD, RWKV_HEAD)


def _to_lanes_sample(x):
    b = x.shape[0]
    g = b * RWKV_HEADS // LANES
    x = x.reshape(g, LANES // RWKV_HEADS, RWKV_HEADS, RWKV_HEAD).transpose(0, 3, 1, 2)
    return x.reshape(g, 1, RWKV_HEAD, LANES)


def _from_lanes_sample(y):
    g = y.shape[0]
    y = y.reshape(g, RWKV_HEAD, LANES // RWKV_HEADS, RWKV_HEADS).transpose(0, 2, 3, 1)
    return y.reshape(g * LANES // RWKV_HEADS, RWKV_WIDTH)


def _state_to_lanes_sample(s):
    b = s.shape[0]
    g = b * RWKV_HEADS // LANES
    nib = RWKV_HEAD // F32_SUBLANES
    s = s.reshape(g, LANES // RWKV_HEADS, RWKV_HEADS, nib, F32_SUBLANES, RWKV_HEAD).transpose(0, 3, 5, 4, 1, 2)
    return s.reshape(g, nib, RWKV_HEAD, F32_SUBLANES, LANES)


def _state_from_lanes_sample(s):
    g, nib = s.shape[:2]
    s = s.reshape(g, nib, RWKV_HEAD, F32_SUBLANES, LANES // RWKV_HEADS, RWKV_HEADS).transpose(0, 4, 5, 1, 3, 2)
    return s.reshape(g * LANES // RWKV_HEADS, RWKV_HEADS, RWKV_HEAD, RWKV_HEAD)


def kernel(x_prompt, x_sample, mem_prompt, cache_mem_k, cache_mem_v, state_pool, state_shift, state_wkv, norm_mix_g, norm_xa_g, norm_mem_g, norm_ffn_g, norm_final_g, w_in_ab, w_out_ab, pool_w, pool_scale, rwkv_mu, rwkv_w0, rwkv_w2, rwkv_a0, rwkv_a2, rwkv_g2, rwkv_k_k, rwkv_k_a, rwkv_r_k, rwkv_gn_g, rwkv_gn_b, w_in_c, sgu_ln_g, sgu_ln_b, sgu_w_s, sgu_b_s, w_out_c, w_xq, w_xk, w_xv, w_xo, w_ff_up, w_ff_down):
    bp, lp, d = x_prompt.shape
    bs = x_sample.shape[0]
    tp = bp * lp
    bf = lambda w: w.astype(BF16)

    w_in_ab_b, w_out_ab_b, pool_w_b = bf(w_in_ab), bf(w_out_ab), bf(pool_w)
    w_in_c_b, w_out_c_b = bf(w_in_c), bf(w_out_c)
    w_xq_b, w_xk_b, w_xv_b, w_xo_b = bf(w_xq), bf(w_xk), bf(w_xv), bf(w_xo)
    w_up_b, w_down_b = bf(w_ff_up), bf(w_ff_down)

    mem2 = mem_prompt.reshape(bp * MEM_LEN, d)
    mem_k_p, mem_v_p = [], []
    for l in range(DEPTH):
        mem_k_p.append(_linear([mem2], [w_xk_b[l]], gain=norm_mem_g[l], tm=1024, tn=d, name="mem_k"))
        mem_v_p.append(_linear([mem2], [w_xv_b[l]], gain=norm_mem_g[l], tm=1024, tn=d, name="mem_v"))

    xp = x_prompt.reshape(tp, d)
    xs = x_sample.reshape(bs, d)
    pool_out_p, pool_out_s, shift_out_p, shift_out_s, wkv_out_p, wkv_out_s, sgu_v_s = [], [], [], [], [], [], []

    for l in range(DEPTH):
        j = l // 2
        if l % 2 == 0:
            w_la = jnp.zeros((W_LORA + A_LORA, 2 * RWKV_WIDTH), F32)
            w_la = w_la.at[:W_LORA, :RWKV_WIDTH].set(rwkv_w2[j]).at[W_LORA:, RWKV_WIDTH:].set(rwkv_a2[j])
            params = _prep_param_args(rwkv_mu[j], rwkv_w0[j], rwkv_a0[j], rwkv_k_k[j], rwkv_k_a[j],
                                      rwkv_r_k[j].reshape(-1), bf(w_la), bf(rwkv_g2[j]))
            zp = _linear([xp], [w_in_ab_b[j]], gain=norm_mix_g[l], tm=1024, tn=w_in_ab.shape[2], name="ab_in_prompt")
            zp = zp.reshape(bp, lp, -1)
            y_pool = _pool_prompt(zp, pool_w_b[j], pool_scale[j], tm=512).reshape(tp, POOL_WIDTH)
            r, k2, v, kap, bb, dec, g, bonus = _rwkv_prep_prompt(zp, params, tm=512)
            s0 = jnp.zeros((1, HALF // F32_SUBLANES, RWKV_HEAD, F32_SUBLANES, LANES), F32)
            y_l, s_l = _rwkv_scan(_key_to_lanes(kap), _key_to_lanes(dec), _key_to_lanes(bb), _key_to_lanes(k2),
                                  _key_to_lanes(r), _value_to_lanes(v), s0, tt=SCAN_TT, vp=V_PITCH)
            y = _value_from_lanes(y_l, bp).reshape(tp, RWKV_WIDTH)
            y_rwkv = _rwkv_post(y, bonus.reshape(tp, -1), g.reshape(tp, -1), rwkv_gn_g[j], rwkv_gn_b[j], tm=1024)
            xp = _linear([y_pool, y_rwkv], [w_out_ab_b[j][:POOL_WIDTH], w_out_ab_b[j][POOL_WIDTH:]], res=xp,
                         tm=1024, tn=1024, name="ab_out_prompt")
            pool_out_p.append(zp[:, lp - POOL_BUF:, :POOL_WIDTH])
            shift_out_p.append(zp[:, lp - 1, POOL_WIDTH:])
            wkv_out_p.append(_state_from_lanes_prompt(s_l, bp))
            zs = _linear([xs], [w_in_ab_b[j]], gain=norm_mix_g[l], tm=bs, tn=256, name="ab_in_sample")
            y_pool = _pool_sample(zs, state_pool[j], pool_w_b[j], pool_scale[j])
            r, k2, v, kap, bb, dec, g, bonus = _rwkv_prep_sample(zs, state_shift[j], params)
            y_l, s_l = _rwkv_scan(_to_lanes_sample(kap), _to_lanes_sample(dec), _to_lanes_sample(bb),
                                  _to_lanes_sample(k2), _to_lanes_sample(r), _to_lanes_sample(v),
                                  _state_to_lanes_sample(state_wkv[j]), tt=1, vp=1)
            y_rwkv = _rwkv_post(_from_lanes_sample(y_l), bonus, g, rwkv_gn_g[j], rwkv_gn_b[j], tm=bs)
            xs = _linear([y_pool, y_rwkv], [w_out_ab_b[j][:POOL_WIDTH], w_out_ab_b[j][POOL_WIDTH:]], res=xs,
                         tm=bs, tn=512, name="ab_out_sample")
            pool_out_s.append(jnp.concatenate([state_pool[j][:, 1:], zs[:, None, :POOL_WIDTH]], axis=1))
            shift_out_s.append(zs[:, POOL_WIDTH:])
            wkv_out_s.append(_state_from_lanes_sample(s_l))
        else:
            gw = SGU_WIDTH // SGU_GROUPS
            zc = _linear([xp], [w_in_c_b[j]], gain=norm_mix_g[l], act="gelu", out_dtype=BF16, tm=1024,
                         tn=2 * SGU_WIDTH, name="sgu_in_prompt")
            bias = jnp.repeat(sgu_b_s[j].T, gw, axis=1)
            xp = _sgu_prompt(zc, xp, sgu_ln_g[j], sgu_ln_b[j], sgu_w_s[j], bias, w_out_c_b[j], tm=512)
            zc = _linear([xs], [w_in_c_b[j]], gain=norm_mix_g[l], act="gelu", tm=bs, tn=512, name="sgu_in_sample")
            gate, vn = _sgu_sample(zc, sgu_ln_g[j], sgu_ln_b[j], jnp.repeat(sgu_w_s[j][:, 0, 0], gw),
                                   jnp.repeat(sgu_b_s[j][:, 0], gw))
            xs = _linear([gate], [w_out_c_b[j]], res=xs, tm=bs, tn=512, name="sgu_out_sample")
            sgu_v_s.append(vn.reshape(bs, 1, SGU_WIDTH))

        xp = _xattn_prompt(xp.reshape(bp, lp, d), norm_xa_g[l], mem_k_p[l].reshape(bp, MEM_LEN, d),
                           mem_v_p[l].reshape(bp, MEM_LEN, d), w_xq_b[l], w_xo_b[l], tm=1024).reshape(tp, d)
        q = _linear([xs], [w_xq_b[l]], gain=norm_xa_g[l], tm=bs, tn=512, name="xattn_q_sample")
        att = _xattn_sample_core(q.reshape(bs, XA_HEADS, XA_HEAD_DIM), cache_mem_k, cache_mem_v, l, bs=4)
        xs = _linear([att.reshape(bs, d)], [w_xo_b[l]], res=xs, tm=bs, tn=512, name="xattn_o_sample")

        fg = norm_final_g if l == DEPTH - 1 else None
        xp = _mlp(xp, norm_ffn_g[l], w_up_b[l], w_down_b[l], final_gain=fg, tm=1024, tf=1024)
        xs = _mlp(xs, norm_ffn_g[l], w_up_b[l], w_down_b[l], final_gain=fg, tm=bs, tf=512)

    head_shape = (bp, MEM_LEN, XA_HEADS, XA_HEAD_DIM)
    return (xp.reshape(bp, lp, d),
            xs.reshape(bs, 1, d),
            jnp.stack([m.reshape(head_shape) for m in mem_k_p]),
            jnp.stack([m.reshape(head_shape) for m in mem_v_p]),
            jnp.stack(pool_out_p),
            jnp.stack(pool_out_s),
            jnp.stack(shift_out_p),
            jnp.stack(shift_out_s),
            jnp.stack(wkv_out_p),
            jnp.stack(wkv_out_s),
            jnp.stack(sgu_v_s))
```

```python
import functools

import jax
import jax.numpy as jnp
from jax import lax
from jax.experimental import pallas as pl
from jax.experimental.pallas import tpu as pltpu

D_MODEL = 1024
DEPTH = 2
PAST_LEN = 16384
POOL_WINDOWS = (2, 4, 8, 16)
POOL_GROUP_WIDTH = 128
POOL_WIDTH = 512
POOL_BUF = 15
RWKV_HEAD = 64
RWKV_WIDTH = 512
RWKV_HEADS = 8
W_LORA = 64
A_LORA = 64
G_LORA = 128
RWKV_IN = 1792
LORA_IN = W_LORA + A_LORA + G_LORA
CHUNK = 128
SGU_WIDTH = 2048
SGU_GROUPS = 4
MEM_LEN = 256
XA_HEADS = 4
XA_HEAD_DIM = 256
D_FF = 4096
RMS_EPS = 1e-5
LN_EPS = 1e-5
GN_EPS = RWKV_HEAD * 1e-5
L2_EPS = 1e-12
DECAY_SCALE = 0.6065306597126334

LANES = 128
F32_SUBLANES = 8
VMEM_LIMIT = 56 * 1024 * 1024

F32 = jnp.float32
BF16 = jnp.bfloat16


def _params(*sem):
    return pltpu.CompilerParams(dimension_semantics=sem, vmem_limit_bytes=VMEM_LIMIT)


def _dot(a, b):
    return jnp.dot(a.astype(BF16), b.astype(BF16), preferred_element_type=F32)


def _dot_nt(a, b):
    return lax.dot_general(a.astype(BF16), b.astype(BF16), (((1,), (1,)), ((), ())),
                           preferred_element_type=F32)


def _rms(x, g):
    return x * lax.rsqrt(jnp.mean(x * x, axis=-1, keepdims=True) + RMS_EPS) * g


def _head_ones(width, head):
    r = lax.broadcasted_iota(jnp.int32, (width, width), 0) // head
    c = lax.broadcasted_iota(jnp.int32, (width, width), 1) // head
    return (r == c).astype(BF16)


def _head_sum(x, ones):
    hi = x.astype(BF16)
    lo = (x - hi.astype(F32)).astype(BF16)
    return (jnp.dot(hi, ones, preferred_element_type=F32)
            + jnp.dot(lo, ones, preferred_element_type=F32))


def _linear_kernel(*refs, nx, has_gain, act, has_res):
    x_refs = refs[:nx]
    pos = nx
    g_ref = refs[pos] if has_gain else None
    pos += int(has_gain)
    w_refs = refs[pos:pos + nx]
    pos += nx
    res_ref = refs[pos] if has_res else None
    pos += int(has_res)
    o_ref = refs[pos]
    if has_gain and len(refs) == pos + 1:
        lhs = [_rms(x_refs[0][...].astype(F32), g_ref[...]).astype(BF16)]
    elif has_gain:
        xn_ref = refs[pos + 1]

        @pl.when(pl.program_id(1) == 0)
        def _():
            xn_ref[...] = _rms(x_refs[0][...].astype(F32), g_ref[...]).astype(BF16)

        lhs = [xn_ref[...]]
    else:
        lhs = [r[...] for r in x_refs]
    tn = o_ref.shape[1]
    cw = next(c for c in (256, 384, 128) if tn % c == 0)
    for c0 in range(0, tn, cw):
        cols = slice(c0, c0 + cw)
        acc = _dot(lhs[0], w_refs[0][:, cols])
        for l, w in zip(lhs[1:], w_refs[1:]):
            acc = acc + _dot(l, w[:, cols])
        if act == "gelu":
            acc = 0.5 * acc * (1.0 + lax.erf(acc * 0.7071067811865476))
        if has_res:
            acc = acc + res_ref[:, cols]
        o_ref[:, cols] = acc.astype(o_ref.dtype)


def _linear(xs, ws, *, gain=None, act=None, res=None, out_dtype=F32, tm, tn, name):
    m = xs[0].shape[0]
    n = ws[0].shape[1]
    assert m % tm == 0 and n % tn == 0, (m, tm, n, tn)
    assert gain is None or len(xs) == 1
    in_specs = [pl.BlockSpec((tm, x.shape[1]), lambda i, j: (i, 0)) for x in xs]
    args = list(xs)
    if gain is not None:
        in_specs.append(pl.BlockSpec((1, gain.shape[-1]), lambda i, j: (0, 0)))
        args.append(gain.reshape(1, -1))
    in_specs += [pl.BlockSpec((w.shape[0], tn), lambda i, j: (0, j)) for w in ws]
    args += list(ws)
    if res is not None:
        in_specs.append(pl.BlockSpec((tm, tn), lambda i, j: (i, j)))
        args.append(res)
    scratch = [pltpu.VMEM((tm, xs[0].shape[1]), BF16)] if gain is not None and n > tn else []
    kern = functools.partial(_linear_kernel, nx=len(xs), has_gain=gain is not None, act=act,
                             has_res=res is not None)
    return pl.pallas_call(
        kern,
        grid=(m // tm, n // tn),
        in_specs=in_specs,
        out_specs=pl.BlockSpec((tm, tn), lambda i, j: (i, j)),
        out_shape=jax.ShapeDtypeStruct((m, n), out_dtype),
        scratch_shapes=scratch,
        compiler_params=_params("parallel", "arbitrary"),
        name=name,
    )(*args)


def _mem_kv_kernel(m_ref, g_ref, wk_ref, wv_ref, k2_ref, v2_ref, k5_ref, v5_ref):
    mn = _rms(m_ref[...], g_ref[0]).astype(BF16)
    nb = k5_ref.shape[1]
    for w_ref, o2_ref, o5_ref in ((wk_ref, k2_ref, k5_ref), (wv_ref, v2_ref, v5_ref)):
        acc = jnp.dot(mn, w_ref[0], preferred_element_type=F32)
        o2_ref[0] = acc
        for bb in range(nb):
            for h in range(XA_HEADS):
                o5_ref[0, bb, :, h, :] = acc[bb * MEM_LEN:(bb + 1) * MEM_LEN, h * XA_HEAD_DIM:(h + 1) * XA_HEAD_DIM]


def _mem_kv(mem, gains, w_k, w_v, *, bb):
    b, m, d = mem.shape
    depth = w_k.shape[0]
    assert b % bb == 0
    tm = bb * m
    w_spec = pl.BlockSpec((1, d, d), lambda l, i: (l, 0, 0))
    o2_spec = pl.BlockSpec((1, tm, d), lambda l, i: (l, i, 0))
    o5_spec = pl.BlockSpec((1, bb, m, XA_HEADS, XA_HEAD_DIM), lambda l, i: (l, i, 0, 0, 0))
    o2_shape = jax.ShapeDtypeStruct((depth, b * m, d), F32)
    o5_shape = jax.ShapeDtypeStruct((depth, b, m, XA_HEADS, XA_HEAD_DIM), F32)
    return pl.pallas_call(
        _mem_kv_kernel,
        grid=(depth, b // bb),
        in_specs=[pl.BlockSpec((tm, d), lambda l, i: (i, 0)), pl.BlockSpec((1, 1, d), lambda l, i: (l, 0, 0)), w_spec, w_spec],
        out_specs=[o2_spec, o2_spec, o5_spec, o5_spec],
        out_shape=[o2_shape, o2_shape, o5_shape, o5_shape],
        compiler_params=_params("parallel", "parallel"),
        name="mem_kv",
    )(mem.reshape(b * m, d), gains.reshape(depth, 1, d), w_k, w_v)


def _mlp_kernel(*refs, final_norm):
    if final_norm:
        x_ref, g_ref, wu_ref, wd_ref, gf_ref, o_ref, xn_ref, acc_ref = refs
    else:
        x_ref, g_ref, wu_ref, wd_ref, o_ref, xn_ref, acc_ref = refs
    j = pl.program_id(1)

    @pl.when(j == 0)
    def _():
        xn_ref[...] = _rms(x_ref[...], g_ref[...]).astype(BF16)
        acc_ref[...] = jnp.zeros_like(acc_ref)

    h = jnp.dot(xn_ref[...], wu_ref[...], preferred_element_type=F32)
    h = jnp.square(jnp.maximum(h, 0.0)).astype(BF16)
    acc_ref[...] += jnp.dot(h, wd_ref[...], preferred_element_type=F32)

    @pl.when(j == pl.num_programs(1) - 1)
    def _():
        y = x_ref[...] + acc_ref[...]
        if final_norm:
            y = _rms(y, gf_ref[...])
        o_ref[...] = y


def _mlp(x, gain, w_up, w_down, *, final_gain=None, tm, tf):
    m, d = x.shape
    f = w_up.shape[1]
    assert m % tm == 0 and f % tf == 0
    in_specs = [
        pl.BlockSpec((tm, d), lambda i, j: (i, 0)),
        pl.BlockSpec((1, d), lambda i, j: (0, 0)),
        pl.BlockSpec((d, tf), lambda i, j: (0, j)),
        pl.BlockSpec((tf, d), lambda i, j: (j, 0)),
    ]
    args = [x, gain.reshape(1, d), w_up, w_down]
    if final_gain is not None:
        in_specs.append(pl.BlockSpec((1, d), lambda i, j: (0, 0)))
        args.append(final_gain.reshape(1, d))
    return pl.pallas_call(
        functools.partial(_mlp_kernel, final_norm=final_gain is not None),
        grid=(m // tm, f // tf),
        in_specs=in_specs,
        out_specs=pl.BlockSpec((tm, d), lambda i, j: (i, 0)),
        out_shape=jax.ShapeDtypeStruct((m, d), F32),
        scratch_shapes=[pltpu.VMEM((tm, d), BF16), pltpu.VMEM((tm, d), F32)],
        compiler_params=_params("parallel", "arbitrary"),
        name="mlp",
    )(*args)


def _xattn_prompt_kernel(x_ref, g_ref, k_ref, v_ref, wq_ref, wo_ref, o_ref, att_ref):
    x = x_ref[0]
    xn = _rms(x, g_ref[...]).astype(BF16)
    q = jnp.dot(xn, wq_ref[...], preferred_element_type=F32) * (XA_HEAD_DIM ** -0.5)
    for h in range(XA_HEADS):
        cols = slice(h * XA_HEAD_DIM, (h + 1) * XA_HEAD_DIM)
        s = _dot_nt(q[:, cols], k_ref[0, :, cols])
        s = s - jnp.max(s, axis=-1, keepdims=True)
        p = jnp.exp(s)
        p = p / jnp.sum(p, axis=-1, keepdims=True)
        att_ref[:, cols] = _dot(p, v_ref[0, :, cols]).astype(BF16)
    o_ref[0] = x + jnp.dot(att_ref[...], wo_ref[...], preferred_element_type=F32)


def _xattn_prompt(x, gain, mem_k, mem_v, w_q, w_o, *, tm):
    b, l, d = x.shape
    assert l % tm == 0
    return pl.pallas_call(
        _xattn_prompt_kernel,
        grid=(b, l // tm),
        in_specs=[
            pl.BlockSpec((1, tm, d), lambda bi, i: (bi, i, 0)),
            pl.BlockSpec((1, d), lambda bi, i: (0, 0)),
            pl.BlockSpec((1, MEM_LEN, d), lambda bi, i: (bi, 0, 0)),
            pl.BlockSpec((1, MEM_LEN, d), lambda bi, i: (bi, 0, 0)),
            pl.BlockSpec((d, d), lambda bi, i: (0, 0)),
            pl.BlockSpec((d, d), lambda bi, i: (0, 0)),
        ],
        out_specs=pl.BlockSpec((1, tm, d), lambda bi, i: (bi, i, 0)),
        out_shape=jax.ShapeDtypeStruct((b, l, d), F32),
        scratch_shapes=[pltpu.VMEM((tm, d), BF16)],
        compiler_params=_params("parallel", "parallel"),
        name="xattn_prompt",
    )(x, gain.reshape(1, d), mem_k, mem_v, w_q, w_o)


XA_TILES = XA_HEAD_DIM // LANES


def _xattn_sample_kernel(q_ref, k_ref, v_ref, o_ref, *, bs):
    for b in range(bs):
        q = q_ref[b] * (XA_HEAD_DIM ** -0.5)
        prod = k_ref[0, b] * q[None]
        prod = prod + pltpu.roll(prod, XA_HEADS, 1)
        s = jnp.sum(prod, axis=-1, keepdims=True)
        p = jnp.exp(s - jnp.max(s, axis=0, keepdims=True))
        den = jnp.sum(p, axis=0)
        o_ref[b] = jnp.sum(p * v_ref[0, b], axis=0) / den


def _head_tile_view(x):
    lead = x.shape[:-2]
    n = len(lead)
    x = x.reshape(lead + (XA_HEADS, XA_TILES, LANES))
    return x.transpose(tuple(range(n)) + (n + 1, n, n + 2)).reshape(lead + (XA_TILES * XA_HEADS, LANES))


def _xattn_sample_core(q, cache_k, cache_v, layer, *, bs):
    b = q.shape[0]
    assert b % bs == 0 and XA_TILES == 2 and XA_TILES * XA_HEADS == F32_SUBLANES
    rows = XA_TILES * XA_HEADS
    q_spec = pl.BlockSpec((bs, rows, LANES), lambda i: (i, 0, 0))
    kv_spec = pl.BlockSpec((1, bs, MEM_LEN, rows, LANES), lambda i: (layer, i, 0, 0, 0))
    out = pl.pallas_call(
        functools.partial(_xattn_sample_kernel, bs=bs),
        grid=(b // bs,),
        in_specs=[q_spec, kv_spec, kv_spec],
        out_specs=q_spec,
        out_shape=jax.ShapeDtypeStruct((b, rows, LANES), F32),
        compiler_params=_params("parallel"),
        name="xattn_sample",
    )(_head_tile_view(q.reshape(b, XA_HEADS, XA_HEAD_DIM)), _head_tile_view(cache_k), _head_tile_view(cache_v))
    return out.reshape(b, XA_TILES, XA_HEADS, LANES).transpose(0, 2, 1, 3).reshape(b, XA_HEADS * XA_HEAD_DIM)


HALO = 16


def _pool_prompt_kernel(z_ref, halo_ref, w_ref, scale_ref, o_ref, ext_ref, *, tm, tiles_per_seq):
    t = pl.program_id(1)
    ext_ref[HALO:HALO + tm, :] = z_ref[0]

    @pl.when(t == 0)
    def _():
        ext_ref[0:HALO, :] = jnp.zeros((HALO, POOL_WIDTH), F32)

    @pl.when(t > 0)
    def _():
        ext_ref[0:HALO, :] = halo_ref[0]

    pos = t * tm + lax.broadcasted_iota(jnp.int32, (tm, 1), 0)
    for gi, win in enumerate(POOL_WINDOWS):
        cols = slice(gi * POOL_GROUP_WIDTH, (gi + 1) * POOL_GROUP_WIDTH)
        tok = ext_ref[HALO:HALO + tm, cols]
        acc = tok
        for k in range(1, win):
            acc = acc + ext_ref[HALO - k:HALO - k + tm, cols]
        count = jnp.minimum(pos + 1, win).astype(F32)
        dlt = acc / count - tok
        o_ref[0, :, cols] = (_dot(dlt, w_ref[gi]) * scale_ref[:, cols]).astype(o_ref.dtype)


def _pool_prompt(z, w_group, scale, *, tm):
    b, l, _ = z.shape
    assert l % tm == 0 and tm % HALO == 0
    tiles = l // tm
    hb = tm // HALO
    return pl.pallas_call(
        functools.partial(_pool_prompt_kernel, tm=tm, tiles_per_seq=tiles),
        grid=(b, tiles),
        in_specs=[
            pl.BlockSpec((1, tm, POOL_WIDTH), lambda bi, i: (bi, i, 0)),
            pl.BlockSpec((1, HALO, POOL_WIDTH), lambda bi, i: (bi, jnp.maximum(i * hb - 1, 0), 0)),
            pl.BlockSpec((len(POOL_WINDOWS), POOL_GROUP_WIDTH, POOL_GROUP_WIDTH), lambda bi, i: (0, 0, 0)),
            pl.BlockSpec((1, POOL_WIDTH), lambda bi, i: (0, 0)),
        ],
        out_specs=pl.BlockSpec((1, tm, POOL_WIDTH), lambda bi, i: (bi, i, 0)),
        out_shape=jax.ShapeDtypeStruct((b, l, POOL_WIDTH), BF16),
        scratch_shapes=[pltpu.VMEM((HALO + tm, POOL_WIDTH), F32)],
        compiler_params=_params("parallel", "parallel"),
        name="pool_prompt",
    )(z, z, w_group, scale.reshape(1, POOL_WIDTH))


def _pool_sample_kernel(z_ref, buf_ref, w_ref, scale_ref, o_ref):
    for gi, win in enumerate(POOL_WINDOWS):
        cols = slice(gi * POOL_GROUP_WIDTH, (gi + 1) * POOL_GROUP_WIDTH)
        tok = z_ref[:, cols]
        acc = tok
        for k in range(1, win):
            acc = acc + buf_ref[:, POOL_BUF - k, cols]
        count = float(min(PAST_LEN + 1, win))
        dlt = acc / count - tok
        o_ref[:, cols] = (_dot(dlt, w_ref[gi]) * scale_ref[:, cols]).astype(o_ref.dtype)


def _pool_sample(z, buf, w_group, scale):
    b = z.shape[0]
    return pl.pallas_call(
        _pool_sample_kernel,
        grid=(1,),
        in_specs=[
            pl.BlockSpec((b, POOL_WIDTH), lambda i: (0, 0)),
            pl.BlockSpec((b, POOL_BUF, POOL_WIDTH), lambda i: (0, 0, 0)),
            pl.BlockSpec((len(POOL_WINDOWS), POOL_GROUP_WIDTH, POOL_GROUP_WIDTH), lambda i: (0, 0, 0)),
            pl.BlockSpec((1, POOL_WIDTH), lambda i: (0, 0)),
        ],
        out_specs=pl.BlockSpec((b, POOL_WIDTH), lambda i: (0, 0)),
        out_shape=jax.ShapeDtypeStruct((b, POOL_WIDTH), BF16),
        compiler_params=_params("arbitrary"),
        name="pool_sample",
    )(z, buf, w_group, scale.reshape(1, POOL_WIDTH))


def _rwkv_prep_math(z4, s4, p, out_refs, sl):
    zr, zk, zv, zl = z4
    sr, sk, sv, sq = s4
    mu_r, mu_k, mu_v, mu_l, w0, a0, k_k, k_a, r_k, w_la, w_g = p
    r = zr + (sr - zr) * mu_r
    k = zk + (sk - zk) * mu_k
    v = zv + (sv - zv) * mu_v
    lo = zl + (sq - zl) * mu_l
    lane = lax.broadcasted_iota(jnp.int32, lo.shape, 1)
    feat = jnp.where(lane < W_LORA, jnp.tanh(lo), jnp.where(lane < W_LORA + A_LORA, lo, jax.nn.sigmoid(lo)))
    wa = _dot(feat[:, :W_LORA + A_LORA], w_la)
    g = _dot(feat[:, W_LORA + A_LORA:], w_g)
    decay = jnp.exp(-DECAY_SCALE * jax.nn.sigmoid(w0 + wa[:, :RWKV_WIDTH]))
    a = jax.nn.sigmoid(a0 + wa[:, RWKV_WIDTH:])
    kk = k * k_k
    k2 = k * (1.0 + (a - 1.0) * k_a)
    ones = _head_ones(RWKV_WIDTH, RWKV_HEAD)
    kk = kk / jnp.maximum(jnp.sqrt(_head_sum(kk * kk, ones)), L2_EPS)
    bonus = _head_sum(r * k2 * r_k, ones) * v
    r_o, k_o, v_o, kap_o, b_o, d_o, g_o, bonus_o = out_refs
    r_o[sl] = r
    k_o[sl] = k2
    v_o[sl] = v
    kap_o[sl] = kk
    b_o[sl] = kk * a
    d_o[sl] = decay
    g_o[sl] = g
    bonus_o[sl] = bonus


def _load_params(refs):
    return tuple(r[...] for r in refs)


def _rwkv_prep_prompt_kernel(zr_ref, zk_ref, zv_ref, zl_ref, *refs, tm):
    p_refs = refs[:11]
    out_refs = refs[11:19]
    er_ref, ek_ref, ev_ref, el_ref = refs[19:23]
    t = pl.program_id(1)
    row = 7
    z4, s4 = [], []
    for z_ref, e_ref in ((zr_ref, er_ref), (zk_ref, ek_ref), (zv_ref, ev_ref), (zl_ref, el_ref)):
        z = z_ref[0]

        @pl.when(t == 0)
        def _():
            e_ref[0:8, :] = jnp.zeros((8, e_ref.shape[1]), F32)

        @pl.when(t > 0)
        def _():
            e_ref[row:row + 1, :] = e_ref[row + tm:row + tm + 1, :]

        e_ref[8:8 + tm, :] = z
        z4.append(z)
        s4.append(e_ref[row:row + tm, :])
    _rwkv_prep_math(z4, s4, _load_params(p_refs), out_refs, (0,))


def _prep_param_args(mu, w0, a0, k_k, k_a, r_k, w_la, w_g):
    w = RWKV_WIDTH
    row = lambda x: x.reshape(1, -1)
    return [row(mu[:w]), row(mu[w:2 * w]), row(mu[2 * w:3 * w]), row(mu[3 * w:]),
            row(w0), row(a0), row(k_k), row(k_a), row(r_k), w_la, w_g]


def _prep_out_shapes(lead):
    return [jax.ShapeDtypeStruct(lead + (RWKV_WIDTH,), F32) for _ in range(8)]


def _rwkv_prep_prompt(z, params, *, tm):
    b, l, _ = z.shape
    assert l % tm == 0
    pw = POOL_WIDTH // RWKV_WIDTH
    full = lambda a: pl.BlockSpec(a.shape, lambda bi, i: (0,) * a.ndim)
    in_specs = [
        pl.BlockSpec((1, tm, RWKV_WIDTH), lambda bi, i: (bi, i, pw)),
        pl.BlockSpec((1, tm, RWKV_WIDTH), lambda bi, i: (bi, i, pw + 1)),
        pl.BlockSpec((1, tm, RWKV_WIDTH), lambda bi, i: (bi, i, pw + 2)),
        pl.BlockSpec((1, tm, LORA_IN), lambda bi, i: (bi, i, (POOL_WIDTH + 3 * RWKV_WIDTH) // LORA_IN)),
    ] + [full(a) for a in params]
    out_spec = pl.BlockSpec((1, tm, RWKV_WIDTH), lambda bi, i: (bi, i, 0))
    return pl.pallas_call(
        functools.partial(_rwkv_prep_prompt_kernel, tm=tm),
        grid=(b, l // tm),
        in_specs=in_specs,
        out_specs=[out_spec] * 8,
        out_shape=_prep_out_shapes((b, l)),
        scratch_shapes=[pltpu.VMEM((tm + 8, RWKV_WIDTH), F32)] * 3 + [pltpu.VMEM((tm + 8, LORA_IN), F32)],
        compiler_params=_params("parallel", "arbitrary"),
        name="rwkv_prep_prompt",
    )(z, z, z, z, *params)


def _rwkv_prep_sample_kernel(zr_ref, zk_ref, zv_ref, zl_ref, sr_ref, sk_ref, sv_ref, sl_ref, *refs):
    p_refs = refs[:11]
    out_refs = refs[11:19]
    z4 = [zr_ref[...], zk_ref[...], zv_ref[...], zl_ref[...]]
    s4 = [sr_ref[...], sk_ref[...], sv_ref[...], sl_ref[...]]
    _rwkv_prep_math(z4, s4, _load_params(p_refs), out_refs, (Ellipsis,))


def _rwkv_prep_sample(z, shift, params):
    b = z.shape[0]
    pw = POOL_WIDTH // RWKV_WIDTH
    full = lambda a: pl.BlockSpec(a.shape, lambda i: (0,) * a.ndim)
    in_specs = [
        pl.BlockSpec((b, RWKV_WIDTH), lambda i: (0, pw)),
        pl.BlockSpec((b, RWKV_WIDTH), lambda i: (0, pw + 1)),
        pl.BlockSpec((b, RWKV_WIDTH), lambda i: (0, pw + 2)),
        pl.BlockSpec((b, LORA_IN), lambda i: (0, (POOL_WIDTH + 3 * RWKV_WIDTH) // LORA_IN)),
        pl.BlockSpec((b, RWKV_WIDTH), lambda i: (0, 0)),
        pl.BlockSpec((b, RWKV_WIDTH), lambda i: (0, 1)),
        pl.BlockSpec((b, RWKV_WIDTH), lambda i: (0, 2)),
        pl.BlockSpec((b, LORA_IN), lambda i: (0, 3 * RWKV_WIDTH // LORA_IN)),
    ] + [full(a) for a in params]
    out_spec = pl.BlockSpec((b, RWKV_WIDTH), lambda i: (0, 0))
    return pl.pallas_call(
        _rwkv_prep_sample_kernel,
        grid=(1,),
        in_specs=in_specs,
        out_specs=[out_spec] * 8,
        out_shape=_prep_out_shapes((b,)),
        compiler_params=_params("arbitrary"),
        name="rwkv_prep_sample",
    )(z, z, z, z, shift, shift, shift, shift, *params)


def _rwkv_scan_kernel(kap_ref, d_ref, b_ref, k_ref, r_ref, v_ref, s0_ref, y_ref, sout_ref, s_ref, *, tt, nib, vp):
    tb = pl.program_id(1)
    sub = F32_SUBLANES

    @pl.when(tb == 0)
    def _():
        s_ref[...] = s0_ref[0]

    for i in range(nib * sub if vp > tt else 0):
        y_ref[0, i * vp + tt:(i + 1) * vp, :] = jnp.zeros((vp - tt, LANES), F32)

    nparts = 4

    def add_to(acc, slot, term):
        acc[slot] = term if acc[slot] is None else acc[slot] + term

    def tree_sum(xs):
        while len(xs) > 1:
            xs = [xs[k] + xs[k + 1] for k in range(0, len(xs), 2)]
        return xs[0]

    def step(t, carry):
        vy_rows = [pl.ds(ib * sub * vp + t, sub, stride=vp) if vp > 1 else pl.ds(ib * sub, sub) for ib in range(nib)]
        v = [v_ref[0, rows, :] for rows in vy_rows]
        acc = [None] * (nparts * nib)
        for j in range(RWKV_HEAD):
            kap = kap_ref[0, pl.ds(j * vp + t, 1), :]
            for ib in range(nib):
                add_to(acc, nparts * ib + j % nparts, s_ref[ib, j] * kap)
        sa = [-tree_sum(acc[nparts * ib:nparts * (ib + 1)]) for ib in range(nib)]
        acc = [None] * (nparts * nib)
        for j in range(RWKV_HEAD):
            row = pl.ds(j * vp + t, 1)
            dec = d_ref[0, row, :]
            bb = b_ref[0, row, :]
            kk = k_ref[0, row, :]
            rr = r_ref[0, row, :]
            for ib in range(nib):
                s = s_ref[ib, j] * dec + sa[ib] * bb + v[ib] * kk
                s_ref[ib, j] = s
                add_to(acc, nparts * ib + j % nparts, s * rr)
        for ib in range(nib):
            y_ref[0, vy_rows[ib], :] = tree_sum(acc[nparts * ib:nparts * (ib + 1)])
        return carry

    lax.fori_loop(0, tt, step, 0)

    @pl.when(tb == pl.num_programs(1) - 1)
    def _():
        sout_ref[0] = s_ref[...]


def _rwkv_scan(kap, dec, bb, kk, rr, v, s0, *, tt, vp):
    g, nb, rows, _ = kap.shape
    n = RWKV_HEAD
    nib = s0.shape[1]
    assert rows == n * vp and v.shape[2] == nib * F32_SUBLANES * vp and tt <= vp
    op_spec = pl.BlockSpec((None, 1, n * vp, LANES), lambda gi, i: (gi, i, 0, 0))
    v_spec = pl.BlockSpec((None, 1, v.shape[2], LANES), lambda gi, i: (gi, i, 0, 0))
    s_spec = pl.BlockSpec((1, nib, n, F32_SUBLANES, LANES), lambda gi, i: (gi, 0, 0, 0, 0))
    return pl.pallas_call(
        functools.partial(_rwkv_scan_kernel, tt=tt, nib=nib, vp=vp),
        grid=(g, nb),
        in_specs=[op_spec] * 5 + [v_spec, s_spec],
        out_specs=[v_spec, s_spec],
        out_shape=[jax.ShapeDtypeStruct(v.shape, F32), jax.ShapeDtypeStruct(s0.shape, F32)],
        scratch_shapes=[pltpu.VMEM((nib, n, F32_SUBLANES, LANES), F32)],
        compiler_params=_params("parallel", "arbitrary"),
        name="rwkv_scan",
    )(kap, dec, bb, kk, rr, v, s0)


def _rwkv_post_kernel(y_ref, bonus_ref, g_ref, gng_ref, gnb_ref, o_ref):
    y = y_ref[...]
    ones = _head_ones(RWKV_WIDTH, RWKV_HEAD)
    m = _head_sum(y, ones) * (1.0 / RWKV_HEAD)
    c = y - m
    var = _head_sum(c * c, ones) * (1.0 / RWKV_HEAD)
    yn = c * lax.rsqrt(var + GN_EPS) * gng_ref[...] + gnb_ref[...]
    o_ref[...] = ((yn + bonus_ref[...]) * g_ref[...]).astype(o_ref.dtype)


def _rwkv_post(y, bonus, g, gn_g, gn_b, *, tm):
    m = y.shape[0]
    assert m % tm == 0
    spec = pl.BlockSpec((tm, RWKV_WIDTH), lambda i: (i, 0))
    pspec = pl.BlockSpec((1, RWKV_WIDTH), lambda i: (0, 0))
    return pl.pallas_call(
        _rwkv_post_kernel,
        grid=(m // tm,),
        in_specs=[spec, spec, spec, pspec, pspec],
        out_specs=spec,
        out_shape=jax.ShapeDtypeStruct((m, RWKV_WIDTH), BF16),
        compiler_params=_params("parallel"),
        name="rwkv_post",
    )(y, bonus, g, gn_g.reshape(1, -1), gn_b.reshape(1, -1))


def _layernorm(v, g, b):
    m = jnp.mean(v, axis=-1, keepdims=True)
    c = v - m
    var = jnp.mean(c * c, axis=-1, keepdims=True)
    return c * lax.rsqrt(var + LN_EPS) * g + b


def _sgu_prompt_kernel(u_ref, v_ref, x_ref, lng_ref, lnb_ref, ws_ref, bias_ref, wo_ref, o_ref, vn_ref, gate_ref, *, tm):
    vn_ref[...] = _layernorm(v_ref[...].astype(F32), lng_ref[...], lnb_ref[...]).astype(BF16)
    gw = SGU_WIDTH // SGU_GROUPS
    r = lax.broadcasted_iota(jnp.int32, (CHUNK, CHUNK), 0)
    c = lax.broadcasted_iota(jnp.int32, (CHUNK, CHUNK), 1)
    for gi in range(SGU_GROUPS):
        cols = slice(gi * gw, (gi + 1) * gw)
        wm = jnp.where(r >= c, ws_ref[gi], 0.0).astype(BF16)
        for ci in range(tm // CHUNK):
            rows = slice(ci * CHUNK, (ci + 1) * CHUNK)
            sp = jnp.dot(wm, vn_ref[rows, cols], preferred_element_type=F32) + bias_ref[:, cols]
            gate_ref[rows, cols] = (u_ref[rows, cols].astype(F32) * sp).astype(BF16)
    o_ref[...] = x_ref[...] + jnp.dot(gate_ref[...], wo_ref[...], preferred_element_type=F32)


def _sgu_prompt(zc, x, ln_g, ln_b, w_s, bias, w_out, *, tm):
    t, d = x.shape
    assert t % tm == 0 and tm % CHUNK == 0
    row = lambda a: a.reshape(1, -1)
    return pl.pallas_call(
        functools.partial(_sgu_prompt_kernel, tm=tm),
        grid=(t // tm,),
        in_specs=[
            pl.BlockSpec((tm, SGU_WIDTH), lambda i: (i, 0)),
            pl.BlockSpec((tm, SGU_WIDTH), lambda i: (i, 1)),
            pl.BlockSpec((tm, d), lambda i: (i, 0)),
            pl.BlockSpec((1, SGU_WIDTH), lambda i: (0, 0)),
            pl.BlockSpec((1, SGU_WIDTH), lambda i: (0, 0)),
            pl.BlockSpec((SGU_GROUPS, CHUNK, CHUNK), lambda i: (0, 0, 0)),
            pl.BlockSpec((CHUNK, SGU_WIDTH), lambda i: (0, 0)),
            pl.BlockSpec((SGU_WIDTH, d), lambda i: (0, 0)),
        ],
        out_specs=pl.BlockSpec((tm, d), lambda i: (i, 0)),
        out_shape=jax.ShapeDtypeStruct((t, d), F32),
        scratch_shapes=[pltpu.VMEM((tm, SGU_WIDTH), BF16), pltpu.VMEM((tm, SGU_WIDTH), BF16)],
        compiler_params=_params("parallel"),
        name="sgu_prompt",
    )(zc, zc, x, row(ln_g), row(ln_b), w_s, bias, w_out)


def _sgu_sample_kernel(u_ref, v_ref, lng_ref, lnb_ref, coef_ref, bias_ref, gate_ref, vn_ref):
    vn = _layernorm(v_ref[...], lng_ref[...], lnb_ref[...])
    vn_ref[...] = vn
    gate_ref[...] = (u_ref[...] * (vn * coef_ref[...] + bias_ref[...])).astype(gate_ref.dtype)


def _sgu_sample(zc, ln_g, ln_b, coef, bias):
    b = zc.shape[0]
    row = lambda a: a.reshape(1, -1)
    pspec = pl.BlockSpec((1, SGU_WIDTH), lambda i: (0, 0))
    return pl.pallas_call(
        _sgu_sample_kernel,
        grid=(1,),
        in_specs=[pl.BlockSpec((b, SGU_WIDTH), lambda i: (0, 0)), pl.BlockSpec((b, SGU_WIDTH), lambda i: (0, 1)),
                  pspec, pspec, pspec, pspec],
        out_specs=[pl.BlockSpec((b, SGU_WIDTH), lambda i: (0, 0))] * 2,
        out_shape=[jax.ShapeDtypeStruct((b, SGU_WIDTH), BF16), jax.ShapeDtypeStruct((b, SGU_WIDTH), F32)],
        compiler_params=_params("arbitrary"),
        name="sgu_sample",
    )(zc, zc, row(ln_g), row(ln_b), row(coef), row(bias))


SCAN_TT = 64
RELAYOUT_BLOCKS = 4
RELAYOUT_T = RELAYOUT_BLOCKS * SCAN_TT
V_PITCH = 72
HALF = RWKV_HEAD // 2


def _split_heads(z_ref, b, x_ref, c):
    xt = x_ref[b, c * LANES:(c + 1) * LANES, :].T
    for h in range(RWKV_HEADS):
        z_ref[b, h * V_PITCH:h * V_PITCH + RWKV_HEAD, :] = xt[h * RWKV_HEAD:(h + 1) * RWKV_HEAD]


def _head_rows(z_ref, b, n):
    return z_ref[b, pl.ds(n, RWKV_HEADS, stride=V_PITCH), :]


def _store_slab(o_ref, c, n, wt):
    pad = jnp.zeros((V_PITCH - SCAN_TT, LANES), F32)
    for k in range(LANES // SCAN_TT):
        blk = c * (LANES // SCAN_TT) + k
        o_ref[blk, n * V_PITCH:n * V_PITCH + SCAN_TT, :] = wt[k * SCAN_TT:(k + 1) * SCAN_TT]
        o_ref[blk, n * V_PITCH + SCAN_TT:(n + 1) * V_PITCH, :] = pad


def _key_to_lanes_kernel(x_ref, o_ref, z_ref):
    nb = x_ref.shape[0]
    for c in range(RELAYOUT_T // LANES):
        for b in range(nb):
            _split_heads(z_ref, b, x_ref, c)
        for j in range(RWKV_HEAD):
            rows = [_head_rows(z_ref, b, j) for b in range(nb)]
            _store_slab(o_ref, c, j, jnp.concatenate(rows + rows, axis=0).T)


def _value_to_lanes_kernel(x_ref, o_ref, z_ref):
    nb = x_ref.shape[0]
    for c in range(RELAYOUT_T // LANES):
        for b in range(nb):
            _split_heads(z_ref, b, x_ref, c)
        for i in range(HALF):
            rows = [_head_rows(z_ref, b, half * HALF + i) for half in range(2) for b in range(nb)]
            _store_slab(o_ref, c, i, jnp.concatenate(rows, axis=0).T)


def _value_from_lanes_kernel(y_ref, o_ref, z_ref):
    nb = o_ref.shape[0]
    per = LANES // SCAN_TT
    for c in range(RELAYOUT_T // LANES):
        for i in range(HALF):
            w = jnp.concatenate([y_ref[c * per + k, i * V_PITCH:i * V_PITCH + SCAN_TT, :] for k in range(per)], axis=0)
            wt = w.T
            for half in range(2):
                for b in range(nb):
                    r0 = (half * nb + b) * RWKV_HEADS
                    z_ref[b, pl.ds(half * HALF + i, RWKV_HEADS, stride=V_PITCH), :] = wt[r0:r0 + RWKV_HEADS]
        for b in range(nb):
            zt = jnp.concatenate([z_ref[b, h * V_PITCH:h * V_PITCH + RWKV_HEAD, :] for h in range(RWKV_HEADS)], axis=0)
            o_ref[b, c * LANES:(c + 1) * LANES, :] = zt.T


def _key_to_lanes(x):
    b, l, w = x.shape
    assert 2 * b * RWKV_HEADS == LANES and l % RELAYOUT_T == 0 and w == RWKV_WIDTH
    out = pl.pallas_call(
        _key_to_lanes_kernel,
        grid=(l // RELAYOUT_T,),
        in_specs=[pl.BlockSpec((b, RELAYOUT_T, w), lambda i: (0, i, 0))],
        out_specs=pl.BlockSpec((RELAYOUT_BLOCKS, RWKV_HEAD * V_PITCH, LANES), lambda i: (i, 0, 0)),
        out_shape=jax.ShapeDtypeStruct((l // SCAN_TT, RWKV_HEAD * V_PITCH, LANES), F32),
        scratch_shapes=[pltpu.VMEM((b, RWKV_HEADS * V_PITCH, LANES), F32)],
        compiler_params=_params("parallel"),
        name="key_to_lanes",
    )(x)
    return out[None]


def _value_to_lanes(v):
    b, l, w = v.shape
    assert 2 * b * RWKV_HEADS == LANES and l % RELAYOUT_T == 0 and w == RWKV_WIDTH
    out = pl.pallas_call(
        _value_to_lanes_kernel,
        grid=(l // RELAYOUT_T,),
        in_specs=[pl.BlockSpec((b, RELAYOUT_T, w), lambda i: (0, i, 0))],
        out_specs=pl.BlockSpec((RELAYOUT_BLOCKS, HALF * V_PITCH, LANES), lambda i: (i, 0, 0)),
        out_shape=jax.ShapeDtypeStruct((l // SCAN_TT, HALF * V_PITCH, LANES), F32),
        scratch_shapes=[pltpu.VMEM((b, RWKV_HEADS * V_PITCH, LANES), F32)],
        compiler_params=_params("parallel"),
        name="value_to_lanes",
    )(v)
    return out[None]


def _value_from_lanes(y, b):
    l = y.shape[1] * SCAN_TT
    return pl.pallas_call(
        _value_from_lanes_kernel,
        grid=(l // RELAYOUT_T,),
        in_specs=[pl.BlockSpec((RELAYOUT_BLOCKS, HALF * V_PITCH, LANES), lambda i: (i, 0, 0))],
        out_specs=pl.BlockSpec((b, RELAYOUT_T, RWKV_WIDTH), lambda i: (0, i, 0)),
        out_shape=jax.ShapeDtypeStruct((b, l, RWKV_WIDTH), F32),
        scratch_shapes=[pltpu.VMEM((b, RWKV_HEADS * V_PITCH, LANES), F32)],
        compiler_params=_params("parallel"),
        name="value_from_lanes",
    )(y[0])


def _state_from_lanes_prompt(s, b):
    s = s.reshape(HALF // F32_SUBLANES, RWKV_HEAD, F32_SUBLANES, 2, b, RWKV_HEADS).transpose(4, 5, 3, 0, 2, 1)
    return s.reshape(b, RWKV_HEADS, RWKV_HEAD, RWKV_HEAD)


def _to_lanes_sample(x):
    b = x.shape[0]
    g = b * RWKV_HEADS // LANES
    x = x.reshape(g, LANES // RWKV_HEADS, RWKV_HEADS, RWKV_HEAD).transpose(0, 3, 1, 2)
    return x.reshape(g, 1, RWKV_HEAD, LANES)


def _from_lanes_sample(y):
    g = y.shape[0]
    y = y.reshape(g, RWKV_HEAD, LANES // RWKV_HEADS, RWKV_HEADS).transpose(0, 2, 3, 1)
    return y.reshape(g * LANES // RWKV_HEADS, RWKV_WIDTH)


def _state_to_lanes_sample(s):
    b = s.shape[0]
    g = b * RWKV_HEADS // LANES
    nib = RWKV_HEAD // F32_SUBLANES
    s = s.reshape(g, LANES // RWKV_HEADS, RWKV_HEADS, nib, F32_SUBLANES, RWKV_HEAD).transpose(0, 3, 5, 4, 1, 2)
    return s.reshape(g, nib, RWKV_HEAD, F32_SUBLANES, LANES)


def _state_from_lanes_sample(s):
    g, nib = s.shape[:2]
    s = s.reshape(g, nib, RWKV_HEAD, F32_SUBLANES, LANES // RWKV_HEADS, RWKV_HEADS).transpose(0, 4, 5, 1, 3, 2)
    return s.reshape(g * LANES // RWKV_HEADS, RWKV_HEADS, RWKV_HEAD, RWKV_HEAD)


def kernel(x_prompt, x_sample, mem_prompt, cache_mem_k, cache_mem_v, state_pool, state_shift, state_wkv, norm_mix_g, norm_xa_g, norm_mem_g, norm_ffn_g, norm_final_g, w_in_ab, w_out_ab, pool_w, pool_scale, rwkv_mu, rwkv_w0, rwkv_w2, rwkv_a0, rwkv_a2, rwkv_g2, rwkv_k_k, rwkv_k_a, rwkv_r_k, rwkv_gn_g, rwkv_gn_b, w_in_c, sgu_ln_g, sgu_ln_b, sgu_w_s, sgu_b_s, w_out_c, w_xq, w_xk, w_xv, w_xo, w_ff_up, w_ff_down):
    bp, lp, d = x_prompt.shape
    bs = x_sample.shape[0]
    tp = bp * lp
    bf = lambda w: w.astype(BF16)

    w_in_ab_b, w_out_ab_b, pool_w_b = bf(w_in_ab), bf(w_out_ab), bf(pool_w)
    w_in_c_b, w_out_c_b = bf(w_in_c), bf(w_out_c)
    w_xq_b, w_xk_b, w_xv_b, w_xo_b = bf(w_xq), bf(w_xk), bf(w_xv), bf(w_xo)
    w_up_b, w_down_b = bf(w_ff_up), bf(w_ff_down)

    mem_k_p, mem_v_p, mem_k_out, mem_v_out = _mem_kv(mem_prompt, norm_mem_g, w_xk_b, w_xv_b, bb=4)

    xp = x_prompt.reshape(tp, d)
    xs = x_sample.reshape(bs, d)
    pool_out_p, pool_out_s, shift_out_p, shift_out_s, wkv_out_p, wkv_out_s, sgu_v_s = [], [], [], [], [], [], []

    for l in range(DEPTH):
        j = l // 2
        if l % 2 == 0:
            w_la = jnp.zeros((W_LORA + A_LORA, 2 * RWKV_WIDTH), F32)
            w_la = w_la.at[:W_LORA, :RWKV_WIDTH].set(rwkv_w2[j]).at[W_LORA:, RWKV_WIDTH:].set(rwkv_a2[j])
            params = _prep_param_args(rwkv_mu[j], rwkv_w0[j], rwkv_a0[j], rwkv_k_k[j], rwkv_k_a[j],
                                      rwkv_r_k[j].reshape(-1), bf(w_la), bf(rwkv_g2[j]))
            zp = _linear([xp], [w_in_ab_b[j]], gain=norm_mix_g[l], tm=1024, tn=w_in_ab.shape[2], name="ab_in_prompt")
            zp = zp.reshape(bp, lp, -1)
            y_pool = _pool_prompt(zp, pool_w_b[j], pool_scale[j], tm=512).reshape(tp, POOL_WIDTH)
            r, k2, v, kap, bb, dec, g, bonus = _rwkv_prep_prompt(zp, params, tm=512)
            s0 = jnp.zeros((1, HALF // F32_SUBLANES, RWKV_HEAD, F32_SUBLANES, LANES), F32)
            y_l, s_l = _rwkv_scan(_key_to_lanes(kap), _key_to_lanes(dec), _key_to_lanes(bb), _key_to_lanes(k2),
                                  _key_to_lanes(r), _value_to_lanes(v), s0, tt=SCAN_TT, vp=V_PITCH)
            y = _value_from_lanes(y_l, bp).reshape(tp, RWKV_WIDTH)
            y_rwkv = _rwkv_post(y, bonus.reshape(tp, -1), g.reshape(tp, -1), rwkv_gn_g[j], rwkv_gn_b[j], tm=1024)
            xp = _linear([y_pool, y_rwkv], [w_out_ab_b[j][:POOL_WIDTH], w_out_ab_b[j][POOL_WIDTH:]], res=xp,
                         tm=1024, tn=1024, name="ab_out_prompt")
            pool_out_p.append(zp[:, lp - POOL_BUF:, :POOL_WIDTH])
            shift_out_p.append(zp[:, lp - 1, POOL_WIDTH:])
            wkv_out_p.append(_state_from_lanes_prompt(s_l, bp))
            zs = _linear([xs], [w_in_ab_b[j]], gain=norm_mix_g[l], tm=bs, tn=256, name="ab_in_sample")
            y_pool = _pool_sample(zs, state_pool[j], pool_w_b[j], pool_scale[j])
            r, k2, v, kap, bb, dec, g, bonus = _rwkv_prep_sample(zs, state_shift[j], params)
            y_l, s_l = _rwkv_scan(_to_lanes_sample(kap), _to_lanes_sample(dec), _to_lanes_sample(bb),
                                  _to_lanes_sample(k2), _to_lanes_sample(r), _to_lanes_sample(v),
                                  _state_to_lanes_sample(state_wkv[j]), tt=1, vp=1)
            y_rwkv = _rwkv_post(_from_lanes_sample(y_l), bonus, g, rwkv_gn_g[j], rwkv_gn_b[j], tm=bs)
            xs = _linear([y_pool, y_rwkv], [w_out_ab_b[j][:POOL_WIDTH], w_out_ab_b[j][POOL_WIDTH:]], res=xs,
                         tm=bs, tn=512, name="ab_out_sample")
            pool_out_s.append(jnp.concatenate([state_pool[j][:, 1:], zs[:, None, :POOL_WIDTH]], axis=1))
            shift_out_s.append(zs[:, POOL_WIDTH:])
            wkv_out_s.append(_state_from_lanes_sample(s_l))
        else:
            gw = SGU_WIDTH // SGU_GROUPS
            zc = _linear([xp], [w_in_c_b[j]], gain=norm_mix_g[l], act="gelu", out_dtype=BF16, tm=1024,
                         tn=2 * SGU_WIDTH, name="sgu_in_prompt")
            bias = jnp.repeat(sgu_b_s[j].T, gw, axis=1)
            xp = _sgu_prompt(zc, xp, sgu_ln_g[j], sgu_ln_b[j], sgu_w_s[j], bias, w_out_c_b[j], tm=512)
            zc = _linear([xs], [w_in_c_b[j]], gain=norm_mix_g[l], act="gelu", tm=bs, tn=512, name="sgu_in_sample")
            gate, vn = _sgu_sample(zc, sgu_ln_g[j], sgu_ln_b[j], jnp.repeat(sgu_w_s[j][:, 0, 0], gw),
                                   jnp.repeat(sgu_b_s[j][:, 0], gw))
            xs = _linear([gate], [w_out_c_b[j]], res=xs, tm=bs, tn=512, name="sgu_out_sample")
            sgu_v_s.append(vn.reshape(bs, 1, SGU_WIDTH))

        xp = _xattn_prompt(xp.reshape(bp, lp, d), norm_xa_g[l], mem_k_p[l].reshape(bp, MEM_LEN, d),
                           mem_v_p[l].reshape(bp, MEM_LEN, d), w_xq_b[l], w_xo_b[l], tm=1024).reshape(tp, d)
        q = _linear([xs], [w_xq_b[l]], gain=norm_xa_g[l], tm=bs, tn=512, name="xattn_q_sample")
        att = _xattn_sample_core(q, cache_mem_k, cache_mem_v, l, bs=4)
        xs = _linear([att], [w_xo_b[l]], res=xs, tm=bs, tn=512, name="xattn_o_sample")

        fg = norm_final_g if l == DEPTH - 1 else None
        xp = _mlp(xp, norm_ffn_g[l], w_up_b[l], w_down_b[l], final_gain=fg, tm=1024, tf=1024)
        xs = _mlp(xs, norm_ffn_g[l], w_up_b[l], w_down_b[l], final_gain=fg, tm=bs, tf=512)

    return (xp.reshape(bp, lp, d),
            xs.reshape(bs, 1, d),
            mem_k_out,
            mem_v_out,
            jnp.stack(pool_out_p),
            jnp.stack(pool_out_s),
            jnp.stack(shift_out_p),
            jnp.stack(shift_out_s),
            jnp.stack(wkv_out_p),
            jnp.stack(wkv_out_s),
            jnp.stack(sgu_v_s))
```

```python
import functools

import jax
import jax.numpy as jnp
from jax import lax
from jax.experimental import pallas as pl
from jax.experimental.pallas import tpu as pltpu

D_MODEL = 1024
DEPTH = 2
PAST_LEN = 16384
POOL_WINDOWS = (2, 4, 8, 16)
POOL_GROUP_WIDTH = 128
POOL_WIDTH = 512
POOL_BUF = 15
RWKV_HEAD = 64
RWKV_WIDTH = 512
RWKV_HEADS = 8
W_LORA = 64
A_LORA = 64
G_LORA = 128
RWKV_IN = 1792
LORA_IN = W_LORA + A_LORA + G_LORA
CHUNK = 128
SGU_WIDTH = 2048
SGU_GROUPS = 4
MEM_LEN = 256
XA_HEADS = 4
XA_HEAD_DIM = 256
D_FF = 4096
RMS_EPS = 1e-5
LN_EPS = 1e-5
GN_EPS = RWKV_HEAD * 1e-5
L2_EPS = 1e-12
DECAY_SCALE = 0.6065306597126334

LANES = 128
F32_SUBLANES = 8
VMEM_LIMIT = 56 * 1024 * 1024

F32 = jnp.float32
BF16 = jnp.bfloat16


def _params(*sem):
    return pltpu.CompilerParams(dimension_semantics=sem, vmem_limit_bytes=VMEM_LIMIT)


def _dot(a, b):
    return jnp.dot(a.astype(BF16), b.astype(BF16), preferred_element_type=F32)


def _dot_nt(a, b):
    return lax.dot_general(a.astype(BF16), b.astype(BF16), (((1,), (1,)), ((), ())),
                           preferred_element_type=F32)


def _rms(x, g):
    return x * lax.rsqrt(jnp.mean(x * x, axis=-1, keepdims=True) + RMS_EPS) * g


def _head_ones(width, head):
    r = lax.broadcasted_iota(jnp.int32, (width, width), 0) // head
    c = lax.broadcasted_iota(jnp.int32, (width, width), 1) // head
    return (r == c).astype(BF16)


def _head_sum(x, ones):
    hi = x.astype(BF16)
    lo = (x - hi.astype(F32)).astype(BF16)
    return (jnp.dot(hi, ones, preferred_element_type=F32)
            + jnp.dot(lo, ones, preferred_element_type=F32))


def _linear_kernel(*refs, nx, has_gain, act, has_res):
    x_refs = refs[:nx]
    pos = nx
    g_ref = refs[pos] if has_gain else None
    pos += int(has_gain)
    w_refs = refs[pos:pos + nx]
    pos += nx
    res_ref = refs[pos] if has_res else None
    pos += int(has_res)
    o_ref = refs[pos]
    if has_gain and len(refs) == pos + 1:
        lhs = [_rms(x_refs[0][...].astype(F32), g_ref[...]).astype(BF16)]
    elif has_gain:
        xn_ref = refs[pos + 1]

        @pl.when(pl.program_id(1) == 0)
        def _():
            xn_ref[...] = _rms(x_refs[0][...].astype(F32), g_ref[...]).astype(BF16)

        lhs = [xn_ref[...]]
    else:
        lhs = [r[...] for r in x_refs]
    tn = o_ref.shape[1]
    cw = next(c for c in (256, 384, 128) if tn % c == 0)
    for c0 in range(0, tn, cw):
        cols = slice(c0, c0 + cw)
        acc = _dot(lhs[0], w_refs[0][:, cols])
        for l, w in zip(lhs[1:], w_refs[1:]):
            acc = acc + _dot(l, w[:, cols])
        if act == "gelu":
            acc = 0.5 * acc * (1.0 + lax.erf(acc * 0.7071067811865476))
        if has_res:
            acc = acc + res_ref[:, cols]
        o_ref[:, cols] = acc.astype(o_ref.dtype)


def _linear(xs, ws, *, gain=None, act=None, res=None, out_dtype=F32, tm, tn, name):
    m = xs[0].shape[0]
    n = ws[0].shape[1]
    assert m % tm == 0 and n % tn == 0, (m, tm, n, tn)
    assert gain is None or len(xs) == 1
    in_specs = [pl.BlockSpec((tm, x.shape[1]), lambda i, j: (i, 0)) for x in xs]
    args = list(xs)
    if gain is not None:
        in_specs.append(pl.BlockSpec((1, gain.shape[-1]), lambda i, j: (0, 0)))
        args.append(gain.reshape(1, -1))
    in_specs += [pl.BlockSpec((w.shape[0], tn), lambda i, j: (0, j)) for w in ws]
    args += list(ws)
    if res is not None:
        in_specs.append(pl.BlockSpec((tm, tn), lambda i, j: (i, j)))
        args.append(res)
    scratch = [pltpu.VMEM((tm, xs[0].shape[1]), BF16)] if gain is not None and n > tn else []
    kern = functools.partial(_linear_kernel, nx=len(xs), has_gain=gain is not None, act=act,
                             has_res=res is not None)
    return pl.pallas_call(
        kern,
        grid=(m // tm, n // tn),
        in_specs=in_specs,
        out_specs=pl.BlockSpec((tm, tn), lambda i, j: (i, j)),
        out_shape=jax.ShapeDtypeStruct((m, n), out_dtype),
        scratch_shapes=scratch,
        compiler_params=_params("parallel", "arbitrary"),
        name=name,
    )(*args)


def _mem_kv_kernel(m_ref, g_ref, wk_ref, wv_ref, k2_ref, v2_ref, k5_ref, v5_ref):
    mn = _rms(m_ref[...], g_ref[0]).astype(BF16)
    nb = k5_ref.shape[1]
    for w_ref, o2_ref, o5_ref in ((wk_ref, k2_ref, k5_ref), (wv_ref, v2_ref, v5_ref)):
        acc = jnp.dot(mn, w_ref[0], preferred_element_type=F32)
        o2_ref[0] = acc
        for bb in range(nb):
            for h in range(XA_HEADS):
                o5_ref[0, bb, :, h, :] = acc[bb * MEM_LEN:(bb + 1) * MEM_LEN, h * XA_HEAD_DIM:(h + 1) * XA_HEAD_DIM]


def _mem_kv(mem, gains, w_k, w_v, *, bb):
    b, m, d = mem.shape
    depth = w_k.shape[0]
    assert b % bb == 0
    tm = bb * m
    w_spec = pl.BlockSpec((1, d, d), lambda l, i: (l, 0, 0))
    o2_spec = pl.BlockSpec((1, tm, d), lambda l, i: (l, i, 0))
    o5_spec = pl.BlockSpec((1, bb, m, XA_HEADS, XA_HEAD_DIM), lambda l, i: (l, i, 0, 0, 0))
    o2_shape = jax.ShapeDtypeStruct((depth, b * m, d), F32)
    o5_shape = jax.ShapeDtypeStruct((depth, b, m, XA_HEADS, XA_HEAD_DIM), F32)
    return pl.pallas_call(
        _mem_kv_kernel,
        grid=(depth, b // bb),
        in_specs=[pl.BlockSpec((tm, d), lambda l, i: (i, 0)), pl.BlockSpec((1, 1, d), lambda l, i: (l, 0, 0)), w_spec, w_spec],
        out_specs=[o2_spec, o2_spec, o5_spec, o5_spec],
        out_shape=[o2_shape, o2_shape, o5_shape, o5_shape],
        compiler_params=_params("parallel", "parallel"),
        name="mem_kv",
    )(mem.reshape(b * m, d), gains.reshape(depth, 1, d), w_k, w_v)


def _mlp_kernel(*refs, final_norm, tf):
    if final_norm:
        x_ref, g_ref, wu_ref, wd_ref, gf_ref, o_ref = refs
    else:
        x_ref, g_ref, wu_ref, wd_ref, o_ref = refs
    x = x_ref[...]
    xn = _rms(x, g_ref[...]).astype(BF16)
    y = x
    for f0 in range(0, wu_ref.shape[1], tf):
        h = jnp.dot(xn, wu_ref[:, f0:f0 + tf], preferred_element_type=F32)
        h = jnp.square(jnp.maximum(h, 0.0)).astype(BF16)
        y = y + jnp.dot(h, wd_ref[f0:f0 + tf, :], preferred_element_type=F32)
    if final_norm:
        y = _rms(y, gf_ref[...])
    o_ref[...] = y


def _mlp(x, gain, w_up, w_down, *, final_gain=None, tm, tf):
    m, d = x.shape
    f = w_up.shape[1]
    assert m % tm == 0 and f % tf == 0
    resident = dict(pipeline_mode=pl.Buffered(1))
    in_specs = [
        pl.BlockSpec((tm, d), lambda i: (i, 0)),
        pl.BlockSpec((1, d), lambda i: (0, 0)),
        pl.BlockSpec((d, f), lambda i: (0, 0), **resident),
        pl.BlockSpec((f, d), lambda i: (0, 0), **resident),
    ]
    args = [x, gain.reshape(1, d), w_up, w_down]
    if final_gain is not None:
        in_specs.append(pl.BlockSpec((1, d), lambda i: (0, 0)))
        args.append(final_gain.reshape(1, d))
    return pl.pallas_call(
        functools.partial(_mlp_kernel, final_norm=final_gain is not None, tf=tf),
        grid=(m // tm,),
        in_specs=in_specs,
        out_specs=pl.BlockSpec((tm, d), lambda i: (i, 0)),
        out_shape=jax.ShapeDtypeStruct((m, d), F32),
        compiler_params=_params("parallel"),
        name="mlp",
    )(*args)


def _xattn_prompt_kernel(x_ref, g_ref, k_ref, v_ref, wq_ref, wo_ref, o_ref, att_ref):
    x = x_ref[0]
    xn = _rms(x, g_ref[...]).astype(BF16)
    q = jnp.dot(xn, wq_ref[...], preferred_element_type=F32) * (XA_HEAD_DIM ** -0.5)
    for h in range(XA_HEADS):
        cols = slice(h * XA_HEAD_DIM, (h + 1) * XA_HEAD_DIM)
        s = _dot_nt(q[:, cols], k_ref[0, :, cols])
        s = s - jnp.max(s, axis=-1, keepdims=True)
        p = jnp.exp(s)
        p = p / jnp.sum(p, axis=-1, keepdims=True)
        att_ref[:, cols] = _dot(p, v_ref[0, :, cols]).astype(BF16)
    o_ref[0] = x + jnp.dot(att_ref[...], wo_ref[...], preferred_element_type=F32)


def _xattn_prompt(x, gain, mem_k, mem_v, w_q, w_o, *, tm):
    b, l, d = x.shape
    assert l % tm == 0
    return pl.pallas_call(
        _xattn_prompt_kernel,
        grid=(b, l // tm),
        in_specs=[
            pl.BlockSpec((1, tm, d), lambda bi, i: (bi, i, 0)),
            pl.BlockSpec((1, d), lambda bi, i: (0, 0)),
            pl.BlockSpec((1, MEM_LEN, d), lambda bi, i: (bi, 0, 0)),
            pl.BlockSpec((1, MEM_LEN, d), lambda bi, i: (bi, 0, 0)),
            pl.BlockSpec((d, d), lambda bi, i: (0, 0)),
            pl.BlockSpec((d, d), lambda bi, i: (0, 0)),
        ],
        out_specs=pl.BlockSpec((1, tm, d), lambda bi, i: (bi, i, 0)),
        out_shape=jax.ShapeDtypeStruct((b, l, d), F32),
        scratch_shapes=[pltpu.VMEM((tm, d), BF16)],
        compiler_params=_params("parallel", "parallel"),
        name="xattn_prompt",
    )(x, gain.reshape(1, d), mem_k, mem_v, w_q, w_o)


XA_TILES = XA_HEAD_DIM // LANES


def _xattn_sample_kernel(q_ref, k_ref, v_ref, o_ref, *, bs):
    for b in range(bs):
        q = q_ref[b] * (XA_HEAD_DIM ** -0.5)
        prod = k_ref[0, b] * q[None]
        prod = prod + pltpu.roll(prod, XA_HEADS, 1)
        s = jnp.sum(prod, axis=-1, keepdims=True)
        p = jnp.exp(s - jnp.max(s, axis=0, keepdims=True))
        den = jnp.sum(p, axis=0)
        o_ref[b] = jnp.sum(p * v_ref[0, b], axis=0) / den


def _head_tile_view(x):
    lead = x.shape[:-2]
    n = len(lead)
    x = x.reshape(lead + (XA_HEADS, XA_TILES, LANES))
    return x.transpose(tuple(range(n)) + (n + 1, n, n + 2)).reshape(lead + (XA_TILES * XA_HEADS, LANES))


def _xattn_sample_core(q, cache_k, cache_v, layer, *, bs):
    b = q.shape[0]
    assert b % bs == 0 and XA_TILES == 2 and XA_TILES * XA_HEADS == F32_SUBLANES
    rows = XA_TILES * XA_HEADS
    q_spec = pl.BlockSpec((bs, rows, LANES), lambda i: (i, 0, 0))
    kv_spec = pl.BlockSpec((1, bs, MEM_LEN, rows, LANES), lambda i: (layer, i, 0, 0, 0))
    out = pl.pallas_call(
        functools.partial(_xattn_sample_kernel, bs=bs),
        grid=(b // bs,),
        in_specs=[q_spec, kv_spec, kv_spec],
        out_specs=q_spec,
        out_shape=jax.ShapeDtypeStruct((b, rows, LANES), F32),
        compiler_params=_params("parallel"),
        name="xattn_sample",
    )(_head_tile_view(q.reshape(b, XA_HEADS, XA_HEAD_DIM)), _head_tile_view(cache_k), _head_tile_view(cache_v))
    return out.reshape(b, XA_TILES, XA_HEADS, LANES).transpose(0, 2, 1, 3).reshape(b, XA_HEADS * XA_HEAD_DIM)


HALO = 16


def _pool_prompt_kernel(z_ref, halo_ref, w_ref, scale_ref, o_ref, ext_ref, *, tm, tiles_per_seq):
    t = pl.program_id(1)
    ext_ref[HALO:HALO + tm, :] = z_ref[0]

    @pl.when(t == 0)
    def _():
        ext_ref[0:HALO, :] = jnp.zeros((HALO, POOL_WIDTH), F32)

    @pl.when(t > 0)
    def _():
        ext_ref[0:HALO, :] = halo_ref[0]

    pos = t * tm + lax.broadcasted_iota(jnp.int32, (tm, 1), 0)
    for gi, win in enumerate(POOL_WINDOWS):
        cols = slice(gi * POOL_GROUP_WIDTH, (gi + 1) * POOL_GROUP_WIDTH)
        tok = ext_ref[HALO:HALO + tm, cols]
        acc = tok
        for k in range(1, win):
            acc = acc + ext_ref[HALO - k:HALO - k + tm, cols]
        count = jnp.minimum(pos + 1, win).astype(F32)
        dlt = acc / count - tok
        o_ref[0, :, cols] = (_dot(dlt, w_ref[gi]) * scale_ref[:, cols]).astype(o_ref.dtype)


def _pool_prompt(z, w_group, scale, *, tm):
    b, l, _ = z.shape
    assert l % tm == 0 and tm % HALO == 0
    tiles = l // tm
    hb = tm // HALO
    return pl.pallas_call(
        functools.partial(_pool_prompt_kernel, tm=tm, tiles_per_seq=tiles),
        grid=(b, tiles),
        in_specs=[
            pl.BlockSpec((1, tm, POOL_WIDTH), lambda bi, i: (bi, i, 0)),
            pl.BlockSpec((1, HALO, POOL_WIDTH), lambda bi, i: (bi, jnp.maximum(i * hb - 1, 0), 0)),
            pl.BlockSpec((len(POOL_WINDOWS), POOL_GROUP_WIDTH, POOL_GROUP_WIDTH), lambda bi, i: (0, 0, 0)),
            pl.BlockSpec((1, POOL_WIDTH), lambda bi, i: (0, 0)),
        ],
        out_specs=pl.BlockSpec((1, tm, POOL_WIDTH), lambda bi, i: (bi, i, 0)),
        out_shape=jax.ShapeDtypeStruct((b, l, POOL_WIDTH), BF16),
        scratch_shapes=[pltpu.VMEM((HALO + tm, POOL_WIDTH), F32)],
        compiler_params=_params("parallel", "parallel"),
        name="pool_prompt",
    )(z, z, w_group, scale.reshape(1, POOL_WIDTH))


def _pool_sample_kernel(z_ref, buf_ref, w_ref, scale_ref, o_ref):
    for gi, win in enumerate(POOL_WINDOWS):
        cols = slice(gi * POOL_GROUP_WIDTH, (gi + 1) * POOL_GROUP_WIDTH)
        tok = z_ref[:, cols]
        acc = tok
        for k in range(1, win):
            acc = acc + buf_ref[:, POOL_BUF - k, cols]
        count = float(min(PAST_LEN + 1, win))
        dlt = acc / count - tok
        o_ref[:, cols] = (_dot(dlt, w_ref[gi]) * scale_ref[:, cols]).astype(o_ref.dtype)


def _pool_sample(z, buf, w_group, scale):
    b = z.shape[0]
    return pl.pallas_call(
        _pool_sample_kernel,
        grid=(1,),
        in_specs=[
            pl.BlockSpec((b, POOL_WIDTH), lambda i: (0, 0)),
            pl.BlockSpec((b, POOL_BUF, POOL_WIDTH), lambda i: (0, 0, 0)),
            pl.BlockSpec((len(POOL_WINDOWS), POOL_GROUP_WIDTH, POOL_GROUP_WIDTH), lambda i: (0, 0, 0)),
            pl.BlockSpec((1, POOL_WIDTH), lambda i: (0, 0)),
        ],
        out_specs=pl.BlockSpec((b, POOL_WIDTH), lambda i: (0, 0)),
        out_shape=jax.ShapeDtypeStruct((b, POOL_WIDTH), BF16),
        compiler_params=_params("arbitrary"),
        name="pool_sample",
    )(z, buf, w_group, scale.reshape(1, POOL_WIDTH))


def _rwkv_prep_math(z4, s4, p, out_refs, sl):
    zr, zk, zv, zl = z4
    sr, sk, sv, sq = s4
    mu_r, mu_k, mu_v, mu_l, w0, a0, k_k, k_a, r_k, w_la, w_g = p
    r = zr + (sr - zr) * mu_r
    k = zk + (sk - zk) * mu_k
    v = zv + (sv - zv) * mu_v
    lo = zl + (sq - zl) * mu_l
    wal = lo[:, :W_LORA + A_LORA]
    lane = lax.broadcasted_iota(jnp.int32, wal.shape, 1)
    feat = jnp.where(lane < W_LORA, 2.0 * jax.nn.sigmoid(2.0 * wal) - 1.0, wal)
    wa = _dot(feat, w_la)
    g = _dot(jax.nn.sigmoid(lo[:, W_LORA + A_LORA:]), w_g)
    decay = jnp.exp(-DECAY_SCALE * jax.nn.sigmoid(w0 + wa[:, :RWKV_WIDTH]))
    a = jax.nn.sigmoid(a0 + wa[:, RWKV_WIDTH:])
    kk = k * k_k
    k2 = k * (1.0 + (a - 1.0) * k_a)
    ones = _head_ones(RWKV_WIDTH, RWKV_HEAD)
    kk = kk * jnp.minimum(lax.rsqrt(_head_sum(kk * kk, ones)), 1.0 / L2_EPS)
    bonus = _head_sum(r * k2 * r_k, ones) * v
    r_o, k_o, v_o, kap_o, b_o, d_o, g_o, bonus_o = out_refs
    r_o[sl] = r
    k_o[sl] = k2
    v_o[sl] = v
    kap_o[sl] = kk
    b_o[sl] = kk * a
    d_o[sl] = decay
    g_o[sl] = g
    bonus_o[sl] = bonus


def _load_params(refs):
    return tuple(r[...] for r in refs)


def _rwkv_prep_prompt_kernel(zr_ref, zk_ref, zv_ref, zl_ref, *refs, tm):
    p_refs = refs[:11]
    out_refs = refs[11:19]
    er_ref, ek_ref, ev_ref, el_ref = refs[19:23]
    t = pl.program_id(1)
    row = 7
    z4, s4 = [], []
    for z_ref, e_ref in ((zr_ref, er_ref), (zk_ref, ek_ref), (zv_ref, ev_ref), (zl_ref, el_ref)):
        z = z_ref[0]

        @pl.when(t == 0)
        def _():
            e_ref[0:8, :] = jnp.zeros((8, e_ref.shape[1]), F32)

        @pl.when(t > 0)
        def _():
            e_ref[row:row + 1, :] = e_ref[row + tm:row + tm + 1, :]

        e_ref[8:8 + tm, :] = z
        z4.append(z)
        s4.append(e_ref[row:row + tm, :])
    _rwkv_prep_math(z4, s4, _load_params(p_refs), out_refs, (0,))


def _prep_param_args(mu, w0, a0, k_k, k_a, r_k, w_la, w_g):
    w = RWKV_WIDTH
    row = lambda x: x.reshape(1, -1)
    return [row(mu[:w]), row(mu[w:2 * w]), row(mu[2 * w:3 * w]), row(mu[3 * w:]),
            row(w0), row(a0), row(k_k), row(k_a), row(r_k), w_la, w_g]


def _prep_out_shapes(lead):
    return [jax.ShapeDtypeStruct(lead + (RWKV_WIDTH,), F32) for _ in range(8)]


def _rwkv_prep_prompt(z, params, *, tm):
    b, l, _ = z.shape
    assert l % tm == 0
    pw = POOL_WIDTH // RWKV_WIDTH
    full = lambda a: pl.BlockSpec(a.shape, lambda bi, i: (0,) * a.ndim)
    in_specs = [
        pl.BlockSpec((1, tm, RWKV_WIDTH), lambda bi, i: (bi, i, pw)),
        pl.BlockSpec((1, tm, RWKV_WIDTH), lambda bi, i: (bi, i, pw + 1)),
        pl.BlockSpec((1, tm, RWKV_WIDTH), lambda bi, i: (bi, i, pw + 2)),
        pl.BlockSpec((1, tm, LORA_IN), lambda bi, i: (bi, i, (POOL_WIDTH + 3 * RWKV_WIDTH) // LORA_IN)),
    ] + [full(a) for a in params]
    out_spec = pl.BlockSpec((1, tm, RWKV_WIDTH), lambda bi, i: (bi, i, 0))
    return pl.pallas_call(
        functools.partial(_rwkv_prep_prompt_kernel, tm=tm),
        grid=(b, l // tm),
        in_specs=in_specs,
        out_specs=[out_spec] * 8,
        out_shape=_prep_out_shapes((b, l)),
        scratch_shapes=[pltpu.VMEM((tm + 8, RWKV_WIDTH), F32)] * 3 + [pltpu.VMEM((tm + 8, LORA_IN), F32)],
        compiler_params=_params("parallel", "arbitrary"),
        name="rwkv_prep_prompt",
    )(z, z, z, z, *params)


def _rwkv_prep_sample_kernel(zr_ref, zk_ref, zv_ref, zl_ref, sr_ref, sk_ref, sv_ref, sl_ref, *refs):
    p_refs = refs[:11]
    out_refs = refs[11:19]
    z4 = [zr_ref[...], zk_ref[...], zv_ref[...], zl_ref[...]]
    s4 = [sr_ref[...], sk_ref[...], sv_ref[...], sl_ref[...]]
    _rwkv_prep_math(z4, s4, _load_params(p_refs), out_refs, (Ellipsis,))


def _rwkv_prep_sample(z, shift, params):
    b = z.shape[0]
    pw = POOL_WIDTH // RWKV_WIDTH
    full = lambda a: pl.BlockSpec(a.shape, lambda i: (0,) * a.ndim)
    in_specs = [
        pl.BlockSpec((b, RWKV_WIDTH), lambda i: (0, pw)),
        pl.BlockSpec((b, RWKV_WIDTH), lambda i: (0, pw + 1)),
        pl.BlockSpec((b, RWKV_WIDTH), lambda i: (0, pw + 2)),
        pl.BlockSpec((b, LORA_IN), lambda i: (0, (POOL_WIDTH + 3 * RWKV_WIDTH) // LORA_IN)),
        pl.BlockSpec((b, RWKV_WIDTH), lambda i: (0, 0)),
        pl.BlockSpec((b, RWKV_WIDTH), lambda i: (0, 1)),
        pl.BlockSpec((b, RWKV_WIDTH), lambda i: (0, 2)),
        pl.BlockSpec((b, LORA_IN), lambda i: (0, 3 * RWKV_WIDTH // LORA_IN)),
    ] + [full(a) for a in params]
    out_spec = pl.BlockSpec((b, RWKV_WIDTH), lambda i: (0, 0))
    return pl.pallas_call(
        _rwkv_prep_sample_kernel,
        grid=(1,),
        in_specs=in_specs,
        out_specs=[out_spec] * 8,
        out_shape=_prep_out_shapes((b,)),
        compiler_params=_params("arbitrary"),
        name="rwkv_prep_sample",
    )(z, z, z, z, shift, shift, shift, shift, *params)


def _rwkv_scan_kernel(kap_ref, d_ref, b_ref, k_ref, r_ref, v_ref, s0_ref, y_ref, sout_ref, s_ref, *, tt, nib, vp):
    tb = pl.program_id(1)
    sub = F32_SUBLANES

    @pl.when(tb == 0)
    def _():
        s_ref[...] = s0_ref[0]

    for i in range(nib * sub if vp > tt else 0):
        y_ref[0, i * vp + tt:(i + 1) * vp, :] = jnp.zeros((vp - tt, LANES), F32)

    nparts = 4

    def add_to(acc, slot, term):
        acc[slot] = term if acc[slot] is None else acc[slot] + term

    def tree_sum(xs):
        while len(xs) > 1:
            xs = [xs[k] + xs[k + 1] for k in range(0, len(xs), 2)]
        return xs[0]

    def step(t, carry):
        vy_rows = [pl.ds(ib * sub * vp + t, sub, stride=vp) if vp > 1 else pl.ds(ib * sub, sub) for ib in range(nib)]
        v = [v_ref[0, rows, :] for rows in vy_rows]
        acc = [None] * (nparts * nib)
        for j in range(RWKV_HEAD):
            kap = kap_ref[0, pl.ds(j * vp + t, 1), :]
            for ib in range(nib):
                add_to(acc, nparts * ib + j % nparts, s_ref[ib, j] * kap)
        sa = [-tree_sum(acc[nparts * ib:nparts * (ib + 1)]) for ib in range(nib)]
        acc = [None] * (nparts * nib)
        for j in range(RWKV_HEAD):
            row = pl.ds(j * vp + t, 1)
            dec = d_ref[0, row, :]
            bb = b_ref[0, row, :]
            kk = k_ref[0, row, :]
            rr = r_ref[0, row, :]
            for ib in range(nib):
                s = s_ref[ib, j] * dec + sa[ib] * bb + v[ib] * kk
                s_ref[ib, j] = s
                add_to(acc, nparts * ib + j % nparts, s * rr)
        for ib in range(nib):
            y_ref[0, vy_rows[ib], :] = tree_sum(acc[nparts * ib:nparts * (ib + 1)])
        return carry

    lax.fori_loop(0, tt, step, 0)

    @pl.when(tb == pl.num_programs(1) - 1)
    def _():
        sout_ref[0] = s_ref[...]


def _rwkv_scan(kap, dec, bb, kk, rr, v, s0, *, tt, vp):
    g, nb, rows, _ = kap.shape
    n = RWKV_HEAD
    nib = s0.shape[1]
    assert rows == n * vp and v.shape[2] == nib * F32_SUBLANES * vp and tt <= vp
    op_spec = pl.BlockSpec((None, 1, n * vp, LANES), lambda gi, i: (gi, i, 0, 0))
    v_spec = pl.BlockSpec((None, 1, v.shape[2], LANES), lambda gi, i: (gi, i, 0, 0))
    s_spec = pl.BlockSpec((1, nib, n, F32_SUBLANES, LANES), lambda gi, i: (gi, 0, 0, 0, 0))
    return pl.pallas_call(
        functools.partial(_rwkv_scan_kernel, tt=tt, nib=nib, vp=vp),
        grid=(g, nb),
        in_specs=[op_spec] * 5 + [v_spec, s_spec],
        out_specs=[v_spec, s_spec],
        out_shape=[jax.ShapeDtypeStruct(v.shape, F32), jax.ShapeDtypeStruct(s0.shape, F32)],
        scratch_shapes=[pltpu.VMEM((nib, n, F32_SUBLANES, LANES), F32)],
        compiler_params=_params("parallel", "arbitrary"),
        name="rwkv_scan",
    )(kap, dec, bb, kk, rr, v, s0)


def _rwkv_post_kernel(y_ref, bonus_ref, g_ref, gng_ref, gnb_ref, o_ref):
    y = y_ref[...]
    ones = _head_ones(RWKV_WIDTH, RWKV_HEAD)
    m = _head_sum(y, ones) * (1.0 / RWKV_HEAD)
    c = y - m
    var = _head_sum(c * c, ones) * (1.0 / RWKV_HEAD)
    yn = c * lax.rsqrt(var + GN_EPS) * gng_ref[...] + gnb_ref[...]
    o_ref[...] = ((yn + bonus_ref[...]) * g_ref[...]).astype(o_ref.dtype)


def _rwkv_post(y, bonus, g, gn_g, gn_b, *, tm):
    m = y.shape[0]
    assert m % tm == 0
    spec = pl.BlockSpec((tm, RWKV_WIDTH), lambda i: (i, 0))
    pspec = pl.BlockSpec((1, RWKV_WIDTH), lambda i: (0, 0))
    return pl.pallas_call(
        _rwkv_post_kernel,
        grid=(m // tm,),
        in_specs=[spec, spec, spec, pspec, pspec],
        out_specs=spec,
        out_shape=jax.ShapeDtypeStruct((m, RWKV_WIDTH), BF16),
        compiler_params=_params("parallel"),
        name="rwkv_post",
    )(y, bonus, g, gn_g.reshape(1, -1), gn_b.reshape(1, -1))


def _layernorm(v, g, b):
    m = jnp.mean(v, axis=-1, keepdims=True)
    c = v - m
    var = jnp.mean(c * c, axis=-1, keepdims=True)
    return c * lax.rsqrt(var + LN_EPS) * g + b


def _sgu_prompt_kernel(u_ref, v_ref, x_ref, lng_ref, lnb_ref, ws_ref, bias_ref, wo_ref, o_ref, vn_ref, gate_ref, *, tm, ts):
    gw = SGU_WIDTH // SGU_GROUPS
    r = lax.broadcasted_iota(jnp.int32, (CHUNK, CHUNK), 0)
    c = lax.broadcasted_iota(jnp.int32, (CHUNK, CHUNK), 1)
    wms = [jnp.where(r >= c, ws_ref[gi], 0.0).astype(BF16) for gi in range(SGU_GROUPS)]
    for r0 in range(0, tm, ts):
        sub = slice(r0, r0 + ts)
        vn_ref[sub, :] = _layernorm(v_ref[sub, :].astype(F32), lng_ref[...], lnb_ref[...]).astype(BF16)
        for gi in range(SGU_GROUPS):
            cols = slice(gi * gw, (gi + 1) * gw)
            for ci in range(r0 // CHUNK, (r0 + ts) // CHUNK):
                rows = slice(ci * CHUNK, (ci + 1) * CHUNK)
                sp = jnp.dot(wms[gi], vn_ref[rows, cols], preferred_element_type=F32) + bias_ref[:, cols]
                gate_ref[rows, cols] = (u_ref[rows, cols].astype(F32) * sp).astype(BF16)
        o_ref[sub, :] = x_ref[sub, :] + jnp.dot(gate_ref[sub, :], wo_ref[...], preferred_element_type=F32)


def _sgu_prompt(zc, x, ln_g, ln_b, w_s, bias, w_out, *, tm, ts):
    t, d = x.shape
    assert t % tm == 0 and tm % ts == 0 and ts % CHUNK == 0
    row = lambda a: a.reshape(1, -1)
    resident = dict(pipeline_mode=pl.Buffered(1))
    return pl.pallas_call(
        functools.partial(_sgu_prompt_kernel, tm=tm, ts=ts),
        grid=(t // tm,),
        in_specs=[
            pl.BlockSpec((tm, SGU_WIDTH), lambda i: (i, 0)),
            pl.BlockSpec((tm, SGU_WIDTH), lambda i: (i, 1)),
            pl.BlockSpec((tm, d), lambda i: (i, 0)),
            pl.BlockSpec((1, SGU_WIDTH), lambda i: (0, 0)),
            pl.BlockSpec((1, SGU_WIDTH), lambda i: (0, 0)),
            pl.BlockSpec((SGU_GROUPS, CHUNK, CHUNK), lambda i: (0, 0, 0)),
            pl.BlockSpec((CHUNK, SGU_WIDTH), lambda i: (0, 0), **resident),
            pl.BlockSpec((SGU_WIDTH, d), lambda i: (0, 0), **resident),
        ],
        out_specs=pl.BlockSpec((tm, d), lambda i: (i, 0)),
        out_shape=jax.ShapeDtypeStruct((t, d), F32),
        scratch_shapes=[pltpu.VMEM((tm, SGU_WIDTH), BF16), pltpu.VMEM((tm, SGU_WIDTH), BF16)],
        compiler_params=_params("parallel"),
        name="sgu_prompt",
    )(zc, zc, x, row(ln_g), row(ln_b), w_s, bias, w_out)


def _sgu_sample_kernel(u_ref, v_ref, lng_ref, lnb_ref, coef_ref, bias_ref, gate_ref, vn_ref):
    vn = _layernorm(v_ref[...], lng_ref[...], lnb_ref[...])
    vn_ref[...] = vn
    gate_ref[...] = (u_ref[...] * (vn * coef_ref[...] + bias_ref[...])).astype(gate_ref.dtype)


def _sgu_sample(zc, ln_g, ln_b, coef, bias):
    b = zc.shape[0]
    row = lambda a: a.reshape(1, -1)
    pspec = pl.BlockSpec((1, SGU_WIDTH), lambda i: (0, 0))
    return pl.pallas_call(
        _sgu_sample_kernel,
        grid=(1,),
        in_specs=[pl.BlockSpec((b, SGU_WIDTH), lambda i: (0, 0)), pl.BlockSpec((b, SGU_WIDTH), lambda i: (0, 1)),
                  pspec, pspec, pspec, pspec],
        out_specs=[pl.BlockSpec((b, SGU_WIDTH), lambda i: (0, 0))] * 2,
        out_shape=[jax.ShapeDtypeStruct((b, SGU_WIDTH), BF16), jax.ShapeDtypeStruct((b, SGU_WIDTH), F32)],
        compiler_params=_params("arbitrary"),
        name="sgu_sample",
    )(zc, zc, row(ln_g), row(ln_b), row(coef), row(bias))


SCAN_TT = 64
RELAYOUT_BLOCKS = 4
RELAYOUT_T = RELAYOUT_BLOCKS * SCAN_TT
V_PITCH = 72
HALF = RWKV_HEAD // 2


def _split_heads(z_ref, b, x_ref, c):
    xt = x_ref[b, c * LANES:(c + 1) * LANES, :].T
    for h in range(RWKV_HEADS):
        z_ref[b, h * V_PITCH:h * V_PITCH + RWKV_HEAD, :] = xt[h * RWKV_HEAD:(h + 1) * RWKV_HEAD]


def _head_rows(z_ref, b, n):
    return z_ref[b, pl.ds(n, RWKV_HEADS, stride=V_PITCH), :]


def _store_slab(o_ref, c, n, wt):
    pad = jnp.zeros((V_PITCH - SCAN_TT, LANES), F32)
    for k in range(LANES // SCAN_TT):
        blk = c * (LANES // SCAN_TT) + k
        o_ref[blk, n * V_PITCH:n * V_PITCH + SCAN_TT, :] = wt[k * SCAN_TT:(k + 1) * SCAN_TT]
        o_ref[blk, n * V_PITCH + SCAN_TT:(n + 1) * V_PITCH, :] = pad


def _key_to_lanes_kernel(x_ref, o_ref, z_ref):
    nb = x_ref.shape[0]
    for c in range(RELAYOUT_T // LANES):
        for b in range(nb):
            _split_heads(z_ref, b, x_ref, c)
        for j in range(RWKV_HEAD):
            rows = [_head_rows(z_ref, b, j) for b in range(nb)]
            _store_slab(o_ref, c, j, jnp.concatenate(rows + rows, axis=0).T)


def _value_to_lanes_kernel(x_ref, o_ref, z_ref):
    nb = x_ref.shape[0]
    for c in range(RELAYOUT_T // LANES):
        for b in range(nb):
            _split_heads(z_ref, b, x_ref, c)
        for i in range(HALF):
            rows = [_head_rows(z_ref, b, half * HALF + i) for half in range(2) for b in range(nb)]
            _store_slab(o_ref, c, i, jnp.concatenate(rows, axis=0).T)


def _value_from_lanes_kernel(y_ref, o_ref, z_ref):
    nb = o_ref.shape[0]
    per = LANES // SCAN_TT
    for c in range(RELAYOUT_T // LANES):
        for i in range(HALF):
            w = jnp.concatenate([y_ref[c * per + k, i * V_PITCH:i * V_PITCH + SCAN_TT, :] for k in range(per)], axis=0)
            wt = w.T
            for half in range(2):
                for b in range(nb):
                    r0 = (half * nb + b) * RWKV_HEADS
                    z_ref[b, pl.ds(half * HALF + i, RWKV_HEADS, stride=V_PITCH), :] = wt[r0:r0 + RWKV_HEADS]
        for b in range(nb):
            zt = jnp.concatenate([z_ref[b, h * V_PITCH:h * V_PITCH + RWKV_HEAD, :] for h in range(RWKV_HEADS)], axis=0)
            o_ref[b, c * LANES:(c + 1) * LANES, :] = zt.T


def _key_to_lanes(x):
    b, l, w = x.shape
    assert 2 * b * RWKV_HEADS == LANES and l % RELAYOUT_T == 0 and w == RWKV_WIDTH
    out = pl.pallas_call(
        _key_to_lanes_kernel,
        grid=(l // RELAYOUT_T,),
        in_specs=[pl.BlockSpec((b, RELAYOUT_T, w), lambda i: (0, i, 0))],
        out_specs=pl.BlockSpec((RELAYOUT_BLOCKS, RWKV_HEAD * V_PITCH, LANES), lambda i: (i, 0, 0)),
        out_shape=jax.ShapeDtypeStruct((l // SCAN_TT, RWKV_HEAD * V_PITCH, LANES), F32),
        scratch_shapes=[pltpu.VMEM((b, RWKV_HEADS * V_PITCH, LANES), F32)],
        compiler_params=_params("parallel"),
        name="key_to_lanes",
    )(x)
    return out[None]


def _value_to_lanes(v):
    b, l, w = v.shape
    assert 2 * b * RWKV_HEADS == LANES and l % RELAYOUT_T == 0 and w == RWKV_WIDTH
    out = pl.pallas_call(
        _value_to_lanes_kernel,
        grid=(l // RELAYOUT_T,),
        in_specs=[pl.BlockSpec((b, RELAYOUT_T, w), lambda i: (0, i, 0))],
        out_specs=pl.BlockSpec((RELAYOUT_BLOCKS, HALF * V_PITCH, LANES), lambda i: (i, 0, 0)),
        out_shape=jax.ShapeDtypeStruct((l // SCAN_TT, HALF * V_PITCH, LANES), F32),
        scratch_shapes=[pltpu.VMEM((b, RWKV_HEADS * V_PITCH, LANES), F32)],
        compiler_params=_params("parallel"),
        name="value_to_lanes",
    )(v)
    return out[None]


def _value_from_lanes(y, b):
    l = y.shape[1] * SCAN_TT
    return pl.pallas_call(
        _value_from_lanes_kernel,
        grid=(l // RELAYOUT_T,),
        in_specs=[pl.BlockSpec((RELAYOUT_BLOCKS, HALF * V_PITCH, LANES), lambda i: (i, 0, 0))],
        out_specs=pl.BlockSpec((b, RELAYOUT_T, RWKV_WIDTH), lambda i: (0, i, 0)),
        out_shape=jax.ShapeDtypeStruct((b, l, RWKV_WIDTH), F32),
        scratch_shapes=[pltpu.VMEM((b, RWKV_HEADS * V_PITCH, LANES), F32)],
        compiler_params=_params("parallel"),
        name="value_from_lanes",
    )(y[0])


def _state_from_lanes_prompt(s, b):
    s = s.reshape(HALF // F32_SUBLANES, RWKV_HEAD, F32_SUBLANES, 2, b, RWKV_HEADS).transpose(4, 5, 3, 0, 2, 1)
    return s.reshape(b, RWKV_HEADS, RWKV_HEAD, RWKV_HEAD)


def _to_lanes_sample(x):
    b = x.shape[0]
    g = b * RWKV_HEADS // LANES
    x = x.reshape(g, LANES // RWKV_HEADS, RWKV_HEADS, RWKV_HEAD).transpose(0, 3, 1, 2)
    return x.reshape(g, 1, RWKV_HEAD, LANES)


def _from_lanes_sample(y):
    g = y.shape[0]
    y = y.reshape(g, RWKV_HEAD, LANES // RWKV_HEADS, RWKV_HEADS).transpose(0, 2, 3, 1)
    return y.reshape(g * LANES // RWKV_HEADS, RWKV_WIDTH)


def _state_to_lanes_sample(s):
    b = s.shape[0]
    g = b * RWKV_HEADS // LANES
    nib = RWKV_HEAD // F32_SUBLANES
    s = s.reshape(g, LANES // RWKV_HEADS, RWKV_HEADS, nib, F32_SUBLANES, RWKV_HEAD).transpose(0, 3, 5, 4, 1, 2)
    return s.reshape(g, nib, RWKV_HEAD, F32_SUBLANES, LANES)


def _state_from_lanes_sample(s):
    g, nib = s.shape[:2]
    s = s.reshape(g, nib, RWKV_HEAD, F32_SUBLANES, LANES // RWKV_HEADS, RWKV_HEADS).transpose(0, 4, 5, 1, 3, 2)
    return s.reshape(g * LANES // RWKV_HEADS, RWKV_HEADS, RWKV_HEAD, RWKV_HEAD)


def kernel(x_prompt, x_sample, mem_prompt, cache_mem_k, cache_mem_v, state_pool, state_shift, state_wkv, norm_mix_g, norm_xa_g, norm_mem_g, norm_ffn_g, norm_final_g, w_in_ab, w_out_ab, pool_w, pool_scale, rwkv_mu, rwkv_w0, rwkv_w2, rwkv_a0, rwkv_a2, rwkv_g2, rwkv_k_k, rwkv_k_a, rwkv_r_k, rwkv_gn_g, rwkv_gn_b, w_in_c, sgu_ln_g, sgu_ln_b, sgu_w_s, sgu_b_s, w_out_c, w_xq, w_xk, w_xv, w_xo, w_ff_up, w_ff_down):
    bp, lp, d = x_prompt.shape
    bs = x_sample.shape[0]
    tp = bp * lp
    bf = lambda w: w.astype(BF16)

    w_in_ab_b, w_out_ab_b, pool_w_b = bf(w_in_ab), bf(w_out_ab), bf(pool_w)
    w_in_c_b, w_out_c_b = bf(w_in_c), bf(w_out_c)
    w_xq_b, w_xk_b, w_xv_b, w_xo_b = bf(w_xq), bf(w_xk), bf(w_xv), bf(w_xo)
    w_up_b, w_down_b = bf(w_ff_up), bf(w_ff_down)

    mem_k_p, mem_v_p, mem_k_out, mem_v_out = _mem_kv(mem_prompt, norm_mem_g, w_xk_b, w_xv_b, bb=4)

    xp = x_prompt.reshape(tp, d)
    xs = x_sample.reshape(bs, d)
    pool_out_p, pool_out_s, shift_out_p, shift_out_s, wkv_out_p, wkv_out_s, sgu_v_s = [], [], [], [], [], [], []

    for l in range(DEPTH):
        j = l // 2
        if l % 2 == 0:
            w_la = jnp.zeros((W_LORA + A_LORA, 2 * RWKV_WIDTH), F32)
            w_la = w_la.at[:W_LORA, :RWKV_WIDTH].set(rwkv_w2[j]).at[W_LORA:, RWKV_WIDTH:].set(rwkv_a2[j])
            params = _prep_param_args(rwkv_mu[j], rwkv_w0[j], rwkv_a0[j], rwkv_k_k[j], rwkv_k_a[j],
                                      rwkv_r_k[j].reshape(-1), bf(w_la), bf(rwkv_g2[j]))
            zp = _linear([xp], [w_in_ab_b[j]], gain=norm_mix_g[l], tm=1024, tn=w_in_ab.shape[2], name="ab_in_prompt")
            zp = zp.reshape(bp, lp, -1)
            y_pool = _pool_prompt(zp, pool_w_b[j], pool_scale[j], tm=512).reshape(tp, POOL_WIDTH)
            r, k2, v, kap, bb, dec, g, bonus = _rwkv_prep_prompt(zp, params, tm=512)
            s0 = jnp.zeros((1, HALF // F32_SUBLANES, RWKV_HEAD, F32_SUBLANES, LANES), F32)
            y_l, s_l = _rwkv_scan(_key_to_lanes(kap), _key_to_lanes(dec), _key_to_lanes(bb), _key_to_lanes(k2),
                                  _key_to_lanes(r), _value_to_lanes(v), s0, tt=SCAN_TT, vp=V_PITCH)
            y = _value_from_lanes(y_l, bp).reshape(tp, RWKV_WIDTH)
            y_rwkv = _rwkv_post(y, bonus.reshape(tp, -1), g.reshape(tp, -1), rwkv_gn_g[j], rwkv_gn_b[j], tm=1024)
            xp = _linear([y_pool, y_rwkv], [w_out_ab_b[j][:POOL_WIDTH], w_out_ab_b[j][POOL_WIDTH:]], res=xp,
                         tm=1024, tn=1024, name="ab_out_prompt")
            pool_out_p.append(zp[:, lp - POOL_BUF:, :POOL_WIDTH])
            shift_out_p.append(zp[:, lp - 1, POOL_WIDTH:])
            wkv_out_p.append(_state_from_lanes_prompt(s_l, bp))
            zs = _linear([xs], [w_in_ab_b[j]], gain=norm_mix_g[l], tm=bs, tn=256, name="ab_in_sample")
            y_pool = _pool_sample(zs, state_pool[j], pool_w_b[j], pool_scale[j])
            r, k2, v, kap, bb, dec, g, bonus = _rwkv_prep_sample(zs, state_shift[j], params)
            y_l, s_l = _rwkv_scan(_to_lanes_sample(kap), _to_lanes_sample(dec), _to_lanes_sample(bb),
                                  _to_lanes_sample(k2), _to_lanes_sample(r), _to_lanes_sample(v),
                                  _state_to_lanes_sample(state_wkv[j]), tt=1, vp=1)
            y_rwkv = _rwkv_post(_from_lanes_sample(y_l), bonus, g, rwkv_gn_g[j], rwkv_gn_b[j], tm=bs)
            xs = _linear([y_pool, y_rwkv], [w_out_ab_b[j][:POOL_WIDTH], w_out_ab_b[j][POOL_WIDTH:]], res=xs,
                         tm=bs, tn=512, name="ab_out_sample")
            pool_out_s.append(jnp.concatenate([state_pool[j][:, 1:], zs[:, None, :POOL_WIDTH]], axis=1))
            shift_out_s.append(zs[:, POOL_WIDTH:])
            wkv_out_s.append(_state_from_lanes_sample(s_l))
        else:
            gw = SGU_WIDTH // SGU_GROUPS
            zc = _linear([xp], [w_in_c_b[j]], gain=norm_mix_g[l], act="gelu", out_dtype=BF16, tm=1024,
                         tn=2 * SGU_WIDTH, name="sgu_in_prompt")
            bias = jnp.repeat(sgu_b_s[j].T, gw, axis=1)
            xp = _sgu_prompt(zc, xp, sgu_ln_g[j], sgu_ln_b[j], sgu_w_s[j], bias, w_out_c_b[j], tm=1024, ts=256)
            zc = _linear([xs], [w_in_c_b[j]], gain=norm_mix_g[l], act="gelu", tm=bs, tn=512, name="sgu_in_sample")
            gate, vn = _sgu_sample(zc, sgu_ln_g[j], sgu_ln_b[j], jnp.repeat(sgu_w_s[j][:, 0, 0], gw),
                                   jnp.repeat(sgu_b_s[j][:, 0], gw))
            xs = _linear([gate], [w_out_c_b[j]], res=xs, tm=bs, tn=512, name="sgu_out_sample")
            sgu_v_s.append(vn.reshape(bs, 1, SGU_WIDTH))

        xp = _xattn_prompt(xp.reshape(bp, lp, d), norm_xa_g[l], mem_k_p[l].reshape(bp, MEM_LEN, d),
                           mem_v_p[l].reshape(bp, MEM_LEN, d), w_xq_b[l], w_xo_b[l], tm=1024).reshape(tp, d)
        q = _linear([xs], [w_xq_b[l]], gain=norm_xa_g[l], tm=bs, tn=512, name="xattn_q_sample")
        att = _xattn_sample_core(q, cache_mem_k, cache_mem_v, l, bs=4)
        xs = _linear([att], [w_xo_b[l]], res=xs, tm=bs, tn=512, name="xattn_o_sample")

        fg = norm_final_g if l == DEPTH - 1 else None
        xp = _mlp(xp, norm_ffn_g[l], w_up_b[l], w_down_b[l], final_gain=fg, tm=1024, tf=1024)
        xs = _mlp(xs, norm_ffn_g[l], w_up_b[l], w_down_b[l], final_gain=fg, tm=bs, tf=512)

    return (xp.reshape(bp, lp, d),
            xs.reshape(bs, 1, d),
            mem_k_out,
            mem_v_out,
            jnp.stack(pool_out_p),
            jnp.stack(pool_out_s),
            jnp.stack(shift_out_p),
            jnp.stack(shift_out_s),
            jnp.stack(wkv_out_p),
            jnp.stack(wkv_out_s),
            jnp.stack(sgu_v_s))
```

```python
import functools

import jax
import jax.numpy as jnp
from jax import lax
from jax.experimental import pallas as pl
from jax.experimental.pallas import tpu as pltpu

D_MODEL = 1024
DEPTH = 2
PAST_LEN = 16384
POOL_WINDOWS = (2, 4, 8, 16)
POOL_GROUP_WIDTH = 128
POOL_WIDTH = 512
POOL_BUF = 15
RWKV_HEAD = 64
RWKV_WIDTH = 512
RWKV_HEADS = 8
W_LORA = 64
A_LORA = 64
G_LORA = 128
RWKV_IN = 1792
LORA_IN = W_LORA + A_LORA + G_LORA
CHUNK = 128
SGU_WIDTH = 2048
SGU_GROUPS = 4
MEM_LEN = 256
XA_HEADS = 4
XA_HEAD_DIM = 256
D_FF = 4096
RMS_EPS = 1e-5
LN_EPS = 1e-5
GN_EPS = RWKV_HEAD * 1e-5
L2_EPS = 1e-12
DECAY_SCALE = 0.6065306597126334

LANES = 128
F32_SUBLANES = 8
VMEM_LIMIT = 56 * 1024 * 1024

F32 = jnp.float32
BF16 = jnp.bfloat16


def _params(*sem):
    return pltpu.CompilerParams(dimension_semantics=sem, vmem_limit_bytes=VMEM_LIMIT)


def _dot(a, b):
    return jnp.dot(a.astype(BF16), b.astype(BF16), preferred_element_type=F32)


def _dot_nt(a, b):
    return lax.dot_general(a.astype(BF16), b.astype(BF16), (((1,), (1,)), ((), ())),
                           preferred_element_type=F32)


def _rms(x, g):
    return x * lax.rsqrt(jnp.mean(x * x, axis=-1, keepdims=True) + RMS_EPS) * g


def _head_ones(width, head):
    r = lax.broadcasted_iota(jnp.int32, (width, width), 0) // head
    c = lax.broadcasted_iota(jnp.int32, (width, width), 1) // head
    return (r == c).astype(BF16)


def _head_sum(x, ones):
    hi = x.astype(BF16)
    lo = (x - hi.astype(F32)).astype(BF16)
    return (jnp.dot(hi, ones, preferred_element_type=F32)
            + jnp.dot(lo, ones, preferred_element_type=F32))


def _linear_kernel(*refs, nx, has_gain, act, has_res):
    x_refs = refs[:nx]
    pos = nx
    g_ref = refs[pos] if has_gain else None
    pos += int(has_gain)
    w_refs = refs[pos:pos + nx]
    pos += nx
    res_ref = refs[pos] if has_res else None
    pos += int(has_res)
    o_ref = refs[pos]
    if has_gain and len(refs) == pos + 1:
        lhs = [_rms(x_refs[0][...].astype(F32), g_ref[...]).astype(BF16)]
    elif has_gain:
        xn_ref = refs[pos + 1]

        @pl.when(pl.program_id(1) == 0)
        def _():
            xn_ref[...] = _rms(x_refs[0][...].astype(F32), g_ref[...]).astype(BF16)

        lhs = [xn_ref[...]]
    else:
        lhs = [r[...] for r in x_refs]
    tn = o_ref.shape[1]
    cw = next(c for c in (256, 384, 128) if tn % c == 0)
    for c0 in range(0, tn, cw):
        cols = slice(c0, c0 + cw)
        acc = _dot(lhs[0], w_refs[0][:, cols])
        for l, w in zip(lhs[1:], w_refs[1:]):
            acc = acc + _dot(l, w[:, cols])
        if act == "gelu":
            acc = 0.5 * acc * (1.0 + lax.erf(acc * 0.7071067811865476))
        if has_res:
            acc = acc + res_ref[:, cols]
        o_ref[:, cols] = acc.astype(o_ref.dtype)


def _linear(xs, ws, *, gain=None, act=None, res=None, out_dtype=F32, tm, tn, name):
    m = xs[0].shape[0]
    n = ws[0].shape[1]
    assert m % tm == 0 and n % tn == 0, (m, tm, n, tn)
    assert gain is None or len(xs) == 1
    in_specs = [pl.BlockSpec((tm, x.shape[1]), lambda i, j: (i, 0)) for x in xs]
    args = list(xs)
    if gain is not None:
        in_specs.append(pl.BlockSpec((1, gain.shape[-1]), lambda i, j: (0, 0)))
        args.append(gain.reshape(1, -1))
    in_specs += [pl.BlockSpec((w.shape[0], tn), lambda i, j: (0, j)) for w in ws]
    args += list(ws)
    if res is not None:
        in_specs.append(pl.BlockSpec((tm, tn), lambda i, j: (i, j)))
        args.append(res)
    scratch = [pltpu.VMEM((tm, xs[0].shape[1]), BF16)] if gain is not None and n > tn else []
    kern = functools.partial(_linear_kernel, nx=len(xs), has_gain=gain is not None, act=act,
                             has_res=res is not None)
    return pl.pallas_call(
        kern,
        grid=(m // tm, n // tn),
        in_specs=in_specs,
        out_specs=pl.BlockSpec((tm, tn), lambda i, j: (i, j)),
        out_shape=jax.ShapeDtypeStruct((m, n), out_dtype),
        scratch_shapes=scratch,
        compiler_params=_params("parallel", "arbitrary"),
        name=name,
    )(*args)


def _mem_kv_kernel(m_ref, g_ref, wk_ref, wv_ref, k2_ref, v2_ref, k5_ref, v5_ref):
    mn = _rms(m_ref[...], g_ref[0]).astype(BF16)
    nb = k5_ref.shape[1]
    for w_ref, o2_ref, o5_ref in ((wk_ref, k2_ref, k5_ref), (wv_ref, v2_ref, v5_ref)):
        acc = jnp.dot(mn, w_ref[0], preferred_element_type=F32)
        o2_ref[0] = acc
        for bb in range(nb):
            for h in range(XA_HEADS):
                o5_ref[0, bb, :, h, :] = acc[bb * MEM_LEN:(bb + 1) * MEM_LEN, h * XA_HEAD_DIM:(h + 1) * XA_HEAD_DIM]


def _mem_kv(mem, gains, w_k, w_v, *, bb):
    b, m, d = mem.shape
    depth = w_k.shape[0]
    assert b % bb == 0
    tm = bb * m
    w_spec = pl.BlockSpec((1, d, d), lambda l, i: (l, 0, 0))
    o2_spec = pl.BlockSpec((1, tm, d), lambda l, i: (l, i, 0))
    o5_spec = pl.BlockSpec((1, bb, m, XA_HEADS, XA_HEAD_DIM), lambda l, i: (l, i, 0, 0, 0))
    o2_shape = jax.ShapeDtypeStruct((depth, b * m, d), F32)
    o5_shape = jax.ShapeDtypeStruct((depth, b, m, XA_HEADS, XA_HEAD_DIM), F32)
    return pl.pallas_call(
        _mem_kv_kernel,
        grid=(depth, b // bb),
        in_specs=[pl.BlockSpec((tm, d), lambda l, i: (i, 0)), pl.BlockSpec((1, 1, d), lambda l, i: (l, 0, 0)), w_spec, w_spec],
        out_specs=[o2_spec, o2_spec, o5_spec, o5_spec],
        out_shape=[o2_shape, o2_shape, o5_shape, o5_shape],
        compiler_params=_params("parallel", "parallel"),
        name="mem_kv",
    )(mem.reshape(b * m, d), gains.reshape(depth, 1, d), w_k, w_v)


def _mlp_kernel(*refs, final_norm, tf):
    if final_norm:
        x_ref, g_ref, wu_ref, wd_ref, gf_ref, o_ref = refs
    else:
        x_ref, g_ref, wu_ref, wd_ref, o_ref = refs
    x = x_ref[...]
    xn = _rms(x, g_ref[...]).astype(BF16)
    y = x
    for f0 in range(0, wu_ref.shape[1], tf):
        h = jnp.dot(xn, wu_ref[:, f0:f0 + tf], preferred_element_type=F32)
        h = jnp.square(jnp.maximum(h, 0.0)).astype(BF16)
        y = y + jnp.dot(h, wd_ref[f0:f0 + tf, :], preferred_element_type=F32)
    if final_norm:
        y = _rms(y, gf_ref[...])
    o_ref[...] = y


def _mlp(x, gain, w_up, w_down, *, final_gain=None, tm, tf):
    m, d = x.shape
    f = w_up.shape[1]
    assert m % tm == 0 and f % tf == 0
    resident = dict(pipeline_mode=pl.Buffered(1))
    in_specs = [
        pl.BlockSpec((tm, d), lambda i: (i, 0)),
        pl.BlockSpec((1, d), lambda i: (0, 0)),
        pl.BlockSpec((d, f), lambda i: (0, 0), **resident),
        pl.BlockSpec((f, d), lambda i: (0, 0), **resident),
    ]
    args = [x, gain.reshape(1, d), w_up, w_down]
    if final_gain is not None:
        in_specs.append(pl.BlockSpec((1, d), lambda i: (0, 0)))
        args.append(final_gain.reshape(1, d))
    return pl.pallas_call(
        functools.partial(_mlp_kernel, final_norm=final_gain is not None, tf=tf),
        grid=(m // tm,),
        in_specs=in_specs,
        out_specs=pl.BlockSpec((tm, d), lambda i: (i, 0)),
        out_shape=jax.ShapeDtypeStruct((m, d), F32),
        compiler_params=_params("parallel"),
        name="mlp",
    )(*args)


def _xattn_prompt_kernel(x_ref, g_ref, k_ref, v_ref, wq_ref, wo_ref, o_ref, att_ref):
    x = x_ref[0]
    xn = _rms(x, g_ref[...]).astype(BF16)
    q = jnp.dot(xn, wq_ref[...], preferred_element_type=F32) * (XA_HEAD_DIM ** -0.5)
    for h in range(XA_HEADS):
        cols = slice(h * XA_HEAD_DIM, (h + 1) * XA_HEAD_DIM)
        s = _dot_nt(q[:, cols], k_ref[0, :, cols])
        s = s - jnp.max(s, axis=-1, keepdims=True)
        p = jnp.exp(s)
        p = p / jnp.sum(p, axis=-1, keepdims=True)
        att_ref[:, cols] = _dot(p, v_ref[0, :, cols]).astype(BF16)
    o_ref[0] = x + jnp.dot(att_ref[...], wo_ref[...], preferred_element_type=F32)


def _xattn_prompt(x, gain, mem_k, mem_v, w_q, w_o, *, tm):
    b, l, d = x.shape
    assert l % tm == 0
    return pl.pallas_call(
        _xattn_prompt_kernel,
        grid=(b, l // tm),
        in_specs=[
            pl.BlockSpec((1, tm, d), lambda bi, i: (bi, i, 0)),
            pl.BlockSpec((1, d), lambda bi, i: (0, 0)),
            pl.BlockSpec((1, MEM_LEN, d), lambda bi, i: (bi, 0, 0)),
            pl.BlockSpec((1, MEM_LEN, d), lambda bi, i: (bi, 0, 0)),
            pl.BlockSpec((d, d), lambda bi, i: (0, 0)),
            pl.BlockSpec((d, d), lambda bi, i: (0, 0)),
        ],
        out_specs=pl.BlockSpec((1, tm, d), lambda bi, i: (bi, i, 0)),
        out_shape=jax.ShapeDtypeStruct((b, l, d), F32),
        scratch_shapes=[pltpu.VMEM((tm, d), BF16)],
        compiler_params=_params("parallel", "parallel"),
        name="xattn_prompt",
    )(x, gain.reshape(1, d), mem_k, mem_v, w_q, w_o)


XA_TILES = XA_HEAD_DIM // LANES


def _xattn_sample_kernel(q_ref, k_ref, v_ref, o_ref, *, bs):
    for b in range(bs):
        q = q_ref[b] * (XA_HEAD_DIM ** -0.5)
        prod = k_ref[0, b] * q[None]
        prod = prod + pltpu.roll(prod, XA_HEADS, 1)
        s = jnp.sum(prod, axis=-1, keepdims=True)
        p = jnp.exp(s - jnp.max(s, axis=0, keepdims=True))
        den = jnp.sum(p, axis=0)
        o_ref[b] = jnp.sum(p * v_ref[0, b], axis=0) / den


def _head_tile_view(x):
    lead = x.shape[:-2]
    n = len(lead)
    x = x.reshape(lead + (XA_HEADS, XA_TILES, LANES))
    return x.transpose(tuple(range(n)) + (n + 1, n, n + 2)).reshape(lead + (XA_TILES * XA_HEADS, LANES))


def _xattn_sample_core(q, cache_k, cache_v, layer, *, bs):
    b = q.shape[0]
    assert b % bs == 0 and XA_TILES == 2 and XA_TILES * XA_HEADS == F32_SUBLANES
    rows = XA_TILES * XA_HEADS
    q_spec = pl.BlockSpec((bs, rows, LANES), lambda i: (i, 0, 0))
    kv_spec = pl.BlockSpec((1, bs, MEM_LEN, rows, LANES), lambda i: (layer, i, 0, 0, 0))
    out = pl.pallas_call(
        functools.partial(_xattn_sample_kernel, bs=bs),
        grid=(b // bs,),
        in_specs=[q_spec, kv_spec, kv_spec],
        out_specs=q_spec,
        out_shape=jax.ShapeDtypeStruct((b, rows, LANES), F32),
        compiler_params=_params("parallel"),
        name="xattn_sample",
    )(_head_tile_view(q.reshape(b, XA_HEADS, XA_HEAD_DIM)), _head_tile_view(cache_k), _head_tile_view(cache_v))
    return out.reshape(b, XA_TILES, XA_HEADS, LANES).transpose(0, 2, 1, 3).reshape(b, XA_HEADS * XA_HEAD_DIM)


HALO = 16


def _pool_windows(z_ref, pos, w_ref, scale_ref, o_ref, tm):
    for gi, win in enumerate(POOL_WINDOWS):
        cols = slice(gi * POOL_GROUP_WIDTH, (gi + 1) * POOL_GROUP_WIDTH)
        tok = z_ref[HALO:HALO + tm, cols]
        acc = tok
        for k in range(1, win):
            acc = acc + z_ref[HALO - k:HALO - k + tm, cols]
        count = jnp.minimum(pos + 1, win).astype(F32)
        dlt = acc / count - tok
        o_ref[0, :, cols] = (_dot(dlt, w_ref[gi]) * scale_ref[:, cols]).astype(o_ref.dtype)


def _pool_sample_kernel(z_ref, buf_ref, w_ref, scale_ref, o_ref):
    for gi, win in enumerate(POOL_WINDOWS):
        cols = slice(gi * POOL_GROUP_WIDTH, (gi + 1) * POOL_GROUP_WIDTH)
        tok = z_ref[:, cols]
        acc = tok
        for k in range(1, win):
            acc = acc + buf_ref[:, POOL_BUF - k, cols]
        count = float(min(PAST_LEN + 1, win))
        dlt = acc / count - tok
        o_ref[:, cols] = (_dot(dlt, w_ref[gi]) * scale_ref[:, cols]).astype(o_ref.dtype)


def _pool_sample(z, buf, w_group, scale):
    b = z.shape[0]
    return pl.pallas_call(
        _pool_sample_kernel,
        grid=(1,),
        in_specs=[
            pl.BlockSpec((b, POOL_WIDTH), lambda i: (0, 0)),
            pl.BlockSpec((b, POOL_BUF, POOL_WIDTH), lambda i: (0, 0, 0)),
            pl.BlockSpec((len(POOL_WINDOWS), POOL_GROUP_WIDTH, POOL_GROUP_WIDTH), lambda i: (0, 0, 0)),
            pl.BlockSpec((1, POOL_WIDTH), lambda i: (0, 0)),
        ],
        out_specs=pl.BlockSpec((b, POOL_WIDTH), lambda i: (0, 0)),
        out_shape=jax.ShapeDtypeStruct((b, POOL_WIDTH), BF16),
        compiler_params=_params("arbitrary"),
        name="pool_sample",
    )(z, buf, w_group, scale.reshape(1, POOL_WIDTH))


def _rwkv_prep_math(z4, s4, p, out_refs, sl):
    zr, zk, zv, zl = z4
    sr, sk, sv, sq = s4
    mu_r, mu_k, mu_v, mu_l, w0, a0, k_k, k_a, r_k, w_la, w_g = p
    r = zr + (sr - zr) * mu_r
    k = zk + (sk - zk) * mu_k
    v = zv + (sv - zv) * mu_v
    lo = zl + (sq - zl) * mu_l
    wal = lo[:, :W_LORA + A_LORA]
    lane = lax.broadcasted_iota(jnp.int32, wal.shape, 1)
    feat = jnp.where(lane < W_LORA, 2.0 * jax.nn.sigmoid(2.0 * wal) - 1.0, wal)
    wa = _dot(feat, w_la)
    g = _dot(jax.nn.sigmoid(lo[:, W_LORA + A_LORA:]), w_g)
    decay = jnp.exp(-DECAY_SCALE * jax.nn.sigmoid(w0 + wa[:, :RWKV_WIDTH]))
    a = jax.nn.sigmoid(a0 + wa[:, RWKV_WIDTH:])
    kk = k * k_k
    k2 = k * (1.0 + (a - 1.0) * k_a)
    ones = _head_ones(RWKV_WIDTH, RWKV_HEAD)
    kk = kk * jnp.minimum(lax.rsqrt(_head_sum(kk * kk, ones)), 1.0 / L2_EPS)
    bonus = _head_sum(r * k2 * r_k, ones) * v
    r_o, k_o, v_o, kap_o, b_o, d_o, g_o, bonus_o = out_refs
    r_o[sl] = r
    k_o[sl] = k2
    v_o[sl] = v
    kap_o[sl] = kk
    b_o[sl] = kk * a
    d_o[sl] = decay
    g_o[sl] = g
    bonus_o[sl] = bonus


def _load_params(refs):
    return tuple(r[...] for r in refs)


AB_IN = POOL_WIDTH + RWKV_IN
LORA_COL = POOL_WIDTH + 3 * RWKV_WIDTH


def _ab_front_kernel(x_ref, g_ref, w_ref, pw_ref, ps_ref, *refs, tm):
    p_refs = refs[:11]
    pool_ref = refs[11]
    out_refs = refs[12:20]
    pool_tail_ref, shift_tail_ref, z_ref = refs[20:23]
    t = pl.program_id(1)

    @pl.when(t == 0)
    def _():
        z_ref[0:HALO, :] = jnp.zeros((HALO, AB_IN), F32)

    @pl.when(t > 0)
    def _():
        z_ref[0:HALO, :] = z_ref[tm:tm + HALO, :]

    xn = _rms(x_ref[0], g_ref[...]).astype(BF16)
    cw = 256
    for c0 in range(0, AB_IN, cw):
        z_ref[HALO:HALO + tm, c0:c0 + cw] = jnp.dot(xn, w_ref[:, c0:c0 + cw], preferred_element_type=F32)

    pos = t * tm + lax.broadcasted_iota(jnp.int32, (tm, 1), 0)
    _pool_windows(z_ref, pos, pw_ref, ps_ref, pool_ref, tm)

    col = lambda c0, width, r0: z_ref[r0:r0 + tm, c0:c0 + width]
    blocks = [(POOL_WIDTH + k * RWKV_WIDTH, RWKV_WIDTH) for k in range(3)] + [(LORA_COL, LORA_IN)]
    z4 = [col(c0, width, HALO) for c0, width in blocks]
    s4 = [col(c0, width, HALO - 1) for c0, width in blocks]
    _rwkv_prep_math(z4, s4, _load_params(p_refs), out_refs, (0,))

    pool_tail_ref[0] = z_ref[tm:tm + HALO, 0:POOL_WIDTH]
    shift_tail_ref[0] = z_ref[tm + HALO - F32_SUBLANES:tm + HALO, POOL_WIDTH:]


def _prep_param_args(mu, w0, a0, k_k, k_a, r_k, w_la, w_g):
    w = RWKV_WIDTH
    row = lambda x: x.reshape(1, -1)
    return [row(mu[:w]), row(mu[w:2 * w]), row(mu[2 * w:3 * w]), row(mu[3 * w:]),
            row(w0), row(a0), row(k_k), row(k_a), row(r_k), w_la, w_g]


def _prep_out_shapes(lead):
    return [jax.ShapeDtypeStruct(lead + (RWKV_WIDTH,), F32) for _ in range(8)]


def _ab_front_prompt(x, gain, w_in, pool_w, pool_scale, params, *, tm):
    b, l, d = x.shape
    assert l % tm == 0 and tm % HALO == 0
    full = lambda a: pl.BlockSpec(a.shape, lambda bi, i: (0,) * a.ndim)
    resident = dict(pipeline_mode=pl.Buffered(1))
    in_specs = [
        pl.BlockSpec((1, tm, d), lambda bi, i: (bi, i, 0)),
        pl.BlockSpec((1, d), lambda bi, i: (0, 0)),
        pl.BlockSpec((d, AB_IN), lambda bi, i: (0, 0), **resident),
        full(pool_w),
        pl.BlockSpec((1, POOL_WIDTH), lambda bi, i: (0, 0)),
    ] + [full(a) for a in params]
    tok_spec = pl.BlockSpec((1, tm, RWKV_WIDTH), lambda bi, i: (bi, i, 0))
    out_specs = [tok_spec] * 9 + [pl.BlockSpec((1, HALO, POOL_WIDTH), lambda bi, i: (bi, 0, 0)),
                                  pl.BlockSpec((1, F32_SUBLANES, RWKV_IN), lambda bi, i: (bi, 0, 0))]
    out_shape = ([jax.ShapeDtypeStruct((b, l, POOL_WIDTH), BF16)] + _prep_out_shapes((b, l))
                 + [jax.ShapeDtypeStruct((b, HALO, POOL_WIDTH), F32), jax.ShapeDtypeStruct((b, F32_SUBLANES, RWKV_IN), F32)])
    return pl.pallas_call(
        functools.partial(_ab_front_kernel, tm=tm),
        grid=(b, l // tm),
        in_specs=in_specs,
        out_specs=out_specs,
        out_shape=out_shape,
        scratch_shapes=[pltpu.VMEM((tm + HALO, AB_IN), F32)],
        compiler_params=_params("parallel", "arbitrary"),
        name="ab_front_prompt",
    )(x, gain.reshape(1, d), w_in, pool_w, pool_scale.reshape(1, POOL_WIDTH), *params)


def _rwkv_prep_sample_kernel(zr_ref, zk_ref, zv_ref, zl_ref, sr_ref, sk_ref, sv_ref, sl_ref, *refs):
    p_refs = refs[:11]
    out_refs = refs[11:19]
    z4 = [zr_ref[...], zk_ref[...], zv_ref[...], zl_ref[...]]
    s4 = [sr_ref[...], sk_ref[...], sv_ref[...], sl_ref[...]]
    _rwkv_prep_math(z4, s4, _load_params(p_refs), out_refs, (Ellipsis,))


def _rwkv_prep_sample(z, shift, params):
    b = z.shape[0]
    pw = POOL_WIDTH // RWKV_WIDTH
    full = lambda a: pl.BlockSpec(a.shape, lambda i: (0,) * a.ndim)
    in_specs = [
        pl.BlockSpec((b, RWKV_WIDTH), lambda i: (0, pw)),
        pl.BlockSpec((b, RWKV_WIDTH), lambda i: (0, pw + 1)),
        pl.BlockSpec((b, RWKV_WIDTH), lambda i: (0, pw + 2)),
        pl.BlockSpec((b, LORA_IN), lambda i: (0, (POOL_WIDTH + 3 * RWKV_WIDTH) // LORA_IN)),
        pl.BlockSpec((b, RWKV_WIDTH), lambda i: (0, 0)),
        pl.BlockSpec((b, RWKV_WIDTH), lambda i: (0, 1)),
        pl.BlockSpec((b, RWKV_WIDTH), lambda i: (0, 2)),
        pl.BlockSpec((b, LORA_IN), lambda i: (0, 3 * RWKV_WIDTH // LORA_IN)),
    ] + [full(a) for a in params]
    out_spec = pl.BlockSpec((b, RWKV_WIDTH), lambda i: (0, 0))
    return pl.pallas_call(
        _rwkv_prep_sample_kernel,
        grid=(1,),
        in_specs=in_specs,
        out_specs=[out_spec] * 8,
        out_shape=_prep_out_shapes((b,)),
        compiler_params=_params("arbitrary"),
        name="rwkv_prep_sample",
    )(z, z, z, z, shift, shift, shift, shift, *params)


def _rwkv_scan_kernel(kap_ref, d_ref, b_ref, k_ref, r_ref, v_ref, s0_ref, y_ref, sout_ref, s_ref, *, tt, nib, vp):
    tb = pl.program_id(1)
    sub = F32_SUBLANES

    @pl.when(tb == 0)
    def _():
        s_ref[...] = s0_ref[0]

    for i in range(nib * sub if vp > tt else 0):
        y_ref[0, i * vp + tt:(i + 1) * vp, :] = jnp.zeros((vp - tt, LANES), F32)

    nparts = 2
    look_ahead = vp > tt

    def add_to(acc, slot, term):
        acc[slot] = term if acc[slot] is None else acc[slot] + term

    def block_sums(acc):
        return tuple(acc[nparts * ib] + acc[nparts * ib + 1] for ib in range(nib))

    def first_sa():
        acc = [None] * (nparts * nib)
        for j in range(RWKV_HEAD):
            kap = kap_ref[0, pl.ds(j * vp, 1), :]
            for ib in range(nib):
                add_to(acc, nparts * ib + j % nparts, s_ref[ib, j] * kap)
        return tuple(-x for x in block_sums(acc))

    def step(t, sa):
        vy_rows = [pl.ds(ib * sub * vp + t, sub, stride=vp) if vp > 1 else pl.ds(ib * sub, sub) for ib in range(nib)]
        v = [v_ref[0, rows, :] for rows in vy_rows]
        acc_y = [None] * (nparts * nib)
        acc_s = [None] * (nparts * nib)
        for j in range(RWKV_HEAD):
            row = pl.ds(j * vp + t, 1)
            dec = d_ref[0, row, :]
            bb = b_ref[0, row, :]
            kk = k_ref[0, row, :]
            rr = r_ref[0, row, :]
            kap_next = kap_ref[0, pl.ds(j * vp + t + 1, 1), :] if look_ahead else None
            for ib in range(nib):
                s = s_ref[ib, j] * dec + sa[ib] * bb + v[ib] * kk
                s_ref[ib, j] = s
                add_to(acc_y, nparts * ib + j % nparts, s * rr)
                if look_ahead:
                    add_to(acc_s, nparts * ib + j % nparts, s * kap_next)
        for ib, y in enumerate(block_sums(acc_y)):
            y_ref[0, vy_rows[ib], :] = y
        return tuple(-x for x in block_sums(acc_s)) if look_ahead else sa

    lax.fori_loop(0, tt, step, first_sa())

    @pl.when(tb == pl.num_programs(1) - 1)
    def _():
        sout_ref[0] = s_ref[...]


def _rwkv_scan(kap, dec, bb, kk, rr, v, s0, *, tt, vp):
    g, nb, rows, _ = kap.shape
    n = RWKV_HEAD
    nib = s0.shape[1]
    assert rows == n * vp and v.shape[2] == nib * F32_SUBLANES * vp and tt <= vp
    op_spec = pl.BlockSpec((None, 1, n * vp, LANES), lambda gi, i: (gi, i, 0, 0))
    v_spec = pl.BlockSpec((None, 1, v.shape[2], LANES), lambda gi, i: (gi, i, 0, 0))
    s_spec = pl.BlockSpec((1, nib, n, F32_SUBLANES, LANES), lambda gi, i: (gi, 0, 0, 0, 0))
    return pl.pallas_call(
        functools.partial(_rwkv_scan_kernel, tt=tt, nib=nib, vp=vp),
        grid=(g, nb),
        in_specs=[op_spec] * 5 + [v_spec, s_spec],
        out_specs=[v_spec, s_spec],
        out_shape=[jax.ShapeDtypeStruct(v.shape, F32), jax.ShapeDtypeStruct(s0.shape, F32)],
        scratch_shapes=[pltpu.VMEM((nib, n, F32_SUBLANES, LANES), F32)],
        compiler_params=_params("parallel", "arbitrary"),
        name="rwkv_scan",
    )(kap, dec, bb, kk, rr, v, s0)


def _rwkv_post_kernel(y_ref, bonus_ref, g_ref, gng_ref, gnb_ref, o_ref):
    y = y_ref[...]
    ones = _head_ones(RWKV_WIDTH, RWKV_HEAD)
    m = _head_sum(y, ones) * (1.0 / RWKV_HEAD)
    c = y - m
    var = _head_sum(c * c, ones) * (1.0 / RWKV_HEAD)
    yn = c * lax.rsqrt(var + GN_EPS) * gng_ref[...] + gnb_ref[...]
    o_ref[...] = ((yn + bonus_ref[...]) * g_ref[...]).astype(o_ref.dtype)


def _rwkv_post(y, bonus, g, gn_g, gn_b, *, tm):
    m = y.shape[0]
    assert m % tm == 0
    spec = pl.BlockSpec((tm, RWKV_WIDTH), lambda i: (i, 0))
    pspec = pl.BlockSpec((1, RWKV_WIDTH), lambda i: (0, 0))
    return pl.pallas_call(
        _rwkv_post_kernel,
        grid=(m // tm,),
        in_specs=[spec, spec, spec, pspec, pspec],
        out_specs=spec,
        out_shape=jax.ShapeDtypeStruct((m, RWKV_WIDTH), BF16),
        compiler_params=_params("parallel"),
        name="rwkv_post",
    )(y, bonus, g, gn_g.reshape(1, -1), gn_b.reshape(1, -1))


def _layernorm(v, g, b):
    m = jnp.mean(v, axis=-1, keepdims=True)
    c = v - m
    var = jnp.mean(c * c, axis=-1, keepdims=True)
    return c * lax.rsqrt(var + LN_EPS) * g + b


def _sgu_prompt_kernel(u_ref, v_ref, x_ref, lng_ref, lnb_ref, ws_ref, bias_ref, wo_ref, o_ref, vn_ref, gate_ref, *, tm, ts):
    gw = SGU_WIDTH // SGU_GROUPS
    r = lax.broadcasted_iota(jnp.int32, (CHUNK, CHUNK), 0)
    c = lax.broadcasted_iota(jnp.int32, (CHUNK, CHUNK), 1)
    wms = [jnp.where(r >= c, ws_ref[gi], 0.0).astype(BF16) for gi in range(SGU_GROUPS)]
    for r0 in range(0, tm, ts):
        sub = slice(r0, r0 + ts)
        vn_ref[sub, :] = _layernorm(v_ref[sub, :].astype(F32), lng_ref[...], lnb_ref[...]).astype(BF16)
        for gi in range(SGU_GROUPS):
            cols = slice(gi * gw, (gi + 1) * gw)
            for ci in range(r0 // CHUNK, (r0 + ts) // CHUNK):
                rows = slice(ci * CHUNK, (ci + 1) * CHUNK)
                sp = jnp.dot(wms[gi], vn_ref[rows, cols], preferred_element_type=F32) + bias_ref[:, cols]
                gate_ref[rows, cols] = (u_ref[rows, cols].astype(F32) * sp).astype(BF16)
        o_ref[sub, :] = x_ref[sub, :] + jnp.dot(gate_ref[sub, :], wo_ref[...], preferred_element_type=F32)


def _sgu_prompt(zc, x, ln_g, ln_b, w_s, bias, w_out, *, tm, ts):
    t, d = x.shape
    assert t % tm == 0 and tm % ts == 0 and ts % CHUNK == 0
    row = lambda a: a.reshape(1, -1)
    resident = dict(pipeline_mode=pl.Buffered(1))
    return pl.pallas_call(
        functools.partial(_sgu_prompt_kernel, tm=tm, ts=ts),
        grid=(t // tm,),
        in_specs=[
            pl.BlockSpec((tm, SGU_WIDTH), lambda i: (i, 0)),
            pl.BlockSpec((tm, SGU_WIDTH), lambda i: (i, 1)),
            pl.BlockSpec((tm, d), lambda i: (i, 0)),
            pl.BlockSpec((1, SGU_WIDTH), lambda i: (0, 0)),
            pl.BlockSpec((1, SGU_WIDTH), lambda i: (0, 0)),
            pl.BlockSpec((SGU_GROUPS, CHUNK, CHUNK), lambda i: (0, 0, 0)),
            pl.BlockSpec((CHUNK, SGU_WIDTH), lambda i: (0, 0), **resident),
            pl.BlockSpec((SGU_WIDTH, d), lambda i: (0, 0), **resident),
        ],
        out_specs=pl.BlockSpec((tm, d), lambda i: (i, 0)),
        out_shape=jax.ShapeDtypeStruct((t, d), F32),
        scratch_shapes=[pltpu.VMEM((tm, SGU_WIDTH), BF16), pltpu.VMEM((tm, SGU_WIDTH), BF16)],
        compiler_params=_params("parallel"),
        name="sgu_prompt",
    )(zc, zc, x, row(ln_g), row(ln_b), w_s, bias, w_out)


def _sgu_sample_kernel(u_ref, v_ref, lng_ref, lnb_ref, coef_ref, bias_ref, gate_ref, vn_ref):
    vn = _layernorm(v_ref[...], lng_ref[...], lnb_ref[...])
    vn_ref[...] = vn
    gate_ref[...] = (u_ref[...] * (vn * coef_ref[...] + bias_ref[...])).astype(gate_ref.dtype)


def _sgu_sample(zc, ln_g, ln_b, coef, bias):
    b = zc.shape[0]
    row = lambda a: a.reshape(1, -1)
    pspec = pl.BlockSpec((1, SGU_WIDTH), lambda i: (0, 0))
    return pl.pallas_call(
        _sgu_sample_kernel,
        grid=(1,),
        in_specs=[pl.BlockSpec((b, SGU_WIDTH), lambda i: (0, 0)), pl.BlockSpec((b, SGU_WIDTH), lambda i: (0, 1)),
                  pspec, pspec, pspec, pspec],
        out_specs=[pl.BlockSpec((b, SGU_WIDTH), lambda i: (0, 0))] * 2,
        out_shape=[jax.ShapeDtypeStruct((b, SGU_WIDTH), BF16), jax.ShapeDtypeStruct((b, SGU_WIDTH), F32)],
        compiler_params=_params("arbitrary"),
        name="sgu_sample",
    )(zc, zc, row(ln_g), row(ln_b), row(coef), row(bias))


SCAN_TT = 64
RELAYOUT_BLOCKS = 4
RELAYOUT_T = RELAYOUT_BLOCKS * SCAN_TT
V_PITCH = 72
HALF = RWKV_HEAD // 2


def _split_heads(z_ref, b, x_ref, c):
    xt = x_ref[b, c * LANES:(c + 1) * LANES, :].T
    for h in range(RWKV_HEADS):
        z_ref[b, h * V_PITCH:h * V_PITCH + RWKV_HEAD, :] = xt[h * RWKV_HEAD:(h + 1) * RWKV_HEAD]


def _head_rows(z_ref, b, n):
    return z_ref[b, pl.ds(n, RWKV_HEADS, stride=V_PITCH), :]


def _store_slab(o_ref, c, n, wt):
    pad = jnp.zeros((V_PITCH - SCAN_TT, LANES), F32)
    for k in range(LANES // SCAN_TT):
        blk = c * (LANES // SCAN_TT) + k
        o_ref[blk, n * V_PITCH:n * V_PITCH + SCAN_TT, :] = wt[k * SCAN_TT:(k + 1) * SCAN_TT]
        o_ref[blk, n * V_PITCH + SCAN_TT:(n + 1) * V_PITCH, :] = pad


def _key_to_lanes_kernel(x_ref, o_ref, z_ref):
    nb = x_ref.shape[0]
    for c in range(RELAYOUT_T // LANES):
        for b in range(nb):
            _split_heads(z_ref, b, x_ref, c)
        for j in range(RWKV_HEAD):
            rows = [_head_rows(z_ref, b, j) for b in range(nb)]
            _store_slab(o_ref, c, j, jnp.concatenate(rows + rows, axis=0).T)


def _value_to_lanes_kernel(x_ref, o_ref, z_ref):
    nb = x_ref.shape[0]
    for c in range(RELAYOUT_T // LANES):
        for b in range(nb):
            _split_heads(z_ref, b, x_ref, c)
        for i in range(HALF):
            rows = [_head_rows(z_ref, b, half * HALF + i) for half in range(2) for b in range(nb)]
            _store_slab(o_ref, c, i, jnp.concatenate(rows, axis=0).T)


def _value_from_lanes_kernel(y_ref, o_ref, z_ref):
    nb = o_ref.shape[0]
    per = LANES // SCAN_TT
    for c in range(RELAYOUT_T // LANES):
        for i in range(HALF):
            w = jnp.concatenate([y_ref[c * per + k, i * V_PITCH:i * V_PITCH + SCAN_TT, :] for k in range(per)], axis=0)
            wt = w.T
            for half in range(2):
                for b in range(nb):
                    r0 = (half * nb + b) * RWKV_HEADS
                    z_ref[b, pl.ds(half * HALF + i, RWKV_HEADS, stride=V_PITCH), :] = wt[r0:r0 + RWKV_HEADS]
        for b in range(nb):
            zt = jnp.concatenate([z_ref[b, h * V_PITCH:h * V_PITCH + RWKV_HEAD, :] for h in range(RWKV_HEADS)], axis=0)
            o_ref[b, c * LANES:(c + 1) * LANES, :] = zt.T


def _key_to_lanes(x):
    b, l, w = x.shape
    assert 2 * b * RWKV_HEADS == LANES and l % RELAYOUT_T == 0 and w == RWKV_WIDTH
    out = pl.pallas_call(
        _key_to_lanes_kernel,
        grid=(l // RELAYOUT_T,),
        in_specs=[pl.BlockSpec((b, RELAYOUT_T, w), lambda i: (0, i, 0))],
        out_specs=pl.BlockSpec((RELAYOUT_BLOCKS, RWKV_HEAD * V_PITCH, LANES), lambda i: (i, 0, 0)),
        out_shape=jax.ShapeDtypeStruct((l // SCAN_TT, RWKV_HEAD * V_PITCH, LANES), F32),
        scratch_shapes=[pltpu.VMEM((b, RWKV_HEADS * V_PITCH, LANES), F32)],
        compiler_params=_params("parallel"),
        name="key_to_lanes",
    )(x)
    return out[None]


def _value_to_lanes(v):
    b, l, w = v.shape
    assert 2 * b * RWKV_HEADS == LANES and l % RELAYOUT_T == 0 and w == RWKV_WIDTH
    out = pl.pallas_call(
        _value_to_lanes_kernel,
        grid=(l // RELAYOUT_T,),
        in_specs=[pl.BlockSpec((b, RELAYOUT_T, w), lambda i: (0, i, 0))],
        out_specs=pl.BlockSpec((RELAYOUT_BLOCKS, HALF * V_PITCH, LANES), lambda i: (i, 0, 0)),
        out_shape=jax.ShapeDtypeStruct((l // SCAN_TT, HALF * V_PITCH, LANES), F32),
        scratch_shapes=[pltpu.VMEM((b, RWKV_HEADS * V_PITCH, LANES), F32)],
        compiler_params=_params("parallel"),
        name="value_to_lanes",
    )(v)
    return out[None]


def _value_from_lanes(y, b):
    l = y.shape[1] * SCAN_TT
    return pl.pallas_call(
        _value_from_lanes_kernel,
        grid=(l // RELAYOUT_T,),
        in_specs=[pl.BlockSpec((RELAYOUT_BLOCKS, HALF * V_PITCH, LANES), lambda i: (i, 0, 0))],
        out_specs=pl.BlockSpec((b, RELAYOUT_T, RWKV_WIDTH), lambda i: (0, i, 0)),
        out_shape=jax.ShapeDtypeStruct((b, l, RWKV_WIDTH), F32),
        scratch_shapes=[pltpu.VMEM((b, RWKV_HEADS * V_PITCH, LANES), F32)],
        compiler_params=_params("parallel"),
        name="value_from_lanes",
    )(y[0])


def _state_from_lanes_prompt(s, b):
    s = s.reshape(HALF // F32_SUBLANES, RWKV_HEAD, F32_SUBLANES, 2, b, RWKV_HEADS).transpose(4, 5, 3, 0, 2, 1)
    return s.reshape(b, RWKV_HEADS, RWKV_HEAD, RWKV_HEAD)


def _to_lanes_sample(x):
    b = x.shape[0]
    g = b * RWKV_HEADS // LANES
    x = x.reshape(g, LANES // RWKV_HEADS, RWKV_HEADS, RWKV_HEAD).transpose(0, 3, 1, 2)
    return x.reshape(g, 1, RWKV_HEAD, LANES)


def _from_lanes_sample(y):
    g = y.shape[0]
    y = y.reshape(g, RWKV_HEAD, LANES // RWKV_HEADS, RWKV_HEADS).transpose(0, 2, 3, 1)
    return y.reshape(g * LANES // RWKV_HEADS, RWKV_WIDTH)


def _state_to_lanes_sample(s):
    b = s.shape[0]
    g = b * RWKV_HEADS // LANES
    nib = RWKV_HEAD // F32_SUBLANES
    s = s.reshape(g, LANES // RWKV_HEADS, RWKV_HEADS, nib, F32_SUBLANES, RWKV_HEAD).transpose(0, 3, 5, 4, 1, 2)
    return s.reshape(g, nib, RWKV_HEAD, F32_SUBLANES, LANES)


def _state_from_lanes_sample(s):
    g, nib = s.shape[:2]
    s = s.reshape(g, nib, RWKV_HEAD, F32_SUBLANES, LANES // RWKV_HEADS, RWKV_HEADS).transpose(0, 4, 5, 1, 3, 2)
    return s.reshape(g * LANES // RWKV_HEADS, RWKV_HEADS, RWKV_HEAD, RWKV_HEAD)


def kernel(x_prompt, x_sample, mem_prompt, cache_mem_k, cache_mem_v, state_pool, state_shift, state_wkv, norm_mix_g, norm_xa_g, norm_mem_g, norm_ffn_g, norm_final_g, w_in_ab, w_out_ab, pool_w, pool_scale, rwkv_mu, rwkv_w0, rwkv_w2, rwkv_a0, rwkv_a2, rwkv_g2, rwkv_k_k, rwkv_k_a, rwkv_r_k, rwkv_gn_g, rwkv_gn_b, w_in_c, sgu_ln_g, sgu_ln_b, sgu_w_s, sgu_b_s, w_out_c, w_xq, w_xk, w_xv, w_xo, w_ff_up, w_ff_down):
    bp, lp, d = x_prompt.shape
    bs = x_sample.shape[0]
    tp = bp * lp
    bf = lambda w: w.astype(BF16)

    w_in_ab_b, w_out_ab_b, pool_w_b = bf(w_in_ab), bf(w_out_ab), bf(pool_w)
    w_in_c_b, w_out_c_b = bf(w_in_c), bf(w_out_c)
    w_xq_b, w_xk_b, w_xv_b, w_xo_b = bf(w_xq), bf(w_xk), bf(w_xv), bf(w_xo)
    w_up_b, w_down_b = bf(w_ff_up), bf(w_ff_down)

    mem_k_p, mem_v_p, mem_k_out, mem_v_out = _mem_kv(mem_prompt, norm_mem_g, w_xk_b, w_xv_b, bb=4)

    xp = x_prompt.reshape(tp, d)
    xs = x_sample.reshape(bs, d)
    pool_out_p, pool_out_s, shift_out_p, shift_out_s, wkv_out_p, wkv_out_s, sgu_v_s = [], [], [], [], [], [], []

    for l in range(DEPTH):
        j = l // 2
        if l % 2 == 0:
            w_la = jnp.zeros((W_LORA + A_LORA, 2 * RWKV_WIDTH), F32)
            w_la = w_la.at[:W_LORA, :RWKV_WIDTH].set(rwkv_w2[j]).at[W_LORA:, RWKV_WIDTH:].set(rwkv_a2[j])
            params = _prep_param_args(rwkv_mu[j], rwkv_w0[j], rwkv_a0[j], rwkv_k_k[j], rwkv_k_a[j],
                                      rwkv_r_k[j].reshape(-1), bf(w_la), bf(rwkv_g2[j]))
            y_pool, r, k2, v, kap, bb, dec, g, bonus, pool_tail, shift_tail = _ab_front_prompt(
                xp.reshape(bp, lp, d), norm_mix_g[l], w_in_ab_b[j], pool_w_b[j], pool_scale[j], params, tm=512)
            y_pool = y_pool.reshape(tp, POOL_WIDTH)
            s0 =jnp.zeros((1, HALF // F32_SUBLANES, RWKV_HEAD, F32_SUBLANES, LANES), F32)
            y_l, s_l = _rwkv_scan(_key_to_lanes(kap), _key_to_lanes(dec), _key_to_lanes(bb), _key_to_lanes(k2),
                                  _key_to_lanes(r), _value_to_lanes(v), s0, tt=SCAN_TT, vp=V_PITCH)
            y = _value_from_lanes(y_l, bp).reshape(tp, RWKV_WIDTH)
            y_rwkv = _rwkv_post(y, bonus.reshape(tp, -1), g.reshape(tp, -1), rwkv_gn_g[j], rwkv_gn_b[j], tm=1024)
            xp = _linear([y_pool, y_rwkv], [w_out_ab_b[j][:POOL_WIDTH], w_out_ab_b[j][POOL_WIDTH:]], res=xp,
                         tm=1024, tn=1024, name="ab_out_prompt")
            pool_out_p.append(pool_tail[:, HALO - POOL_BUF:])
            shift_out_p.append(shift_tail[:, F32_SUBLANES - 1])
            wkv_out_p.append(_state_from_lanes_prompt(s_l, bp))
            zs = _linear([xs], [w_in_ab_b[j]], gain=norm_mix_g[l], tm=bs, tn=256, name="ab_in_sample")
            y_pool = _pool_sample(zs, state_pool[j], pool_w_b[j], pool_scale[j])
            r, k2, v, kap, bb, dec, g, bonus = _rwkv_prep_sample(zs, state_shift[j], params)
            y_l, s_l = _rwkv_scan(_to_lanes_sample(kap), _to_lanes_sample(dec), _to_lanes_sample(bb),
                                  _to_lanes_sample(k2), _to_lanes_sample(r), _to_lanes_sample(v),
                                  _state_to_lanes_sample(state_wkv[j]), tt=1, vp=1)
            y_rwkv = _rwkv_post(_from_lanes_sample(y_l), bonus, g, rwkv_gn_g[j], rwkv_gn_b[j], tm=bs)
            xs = _linear([y_pool, y_rwkv], [w_out_ab_b[j][:POOL_WIDTH], w_out_ab_b[j][POOL_WIDTH:]], res=xs,
                         tm=bs, tn=512, name="ab_out_sample")
            pool_out_s.append(jnp.concatenate([state_pool[j][:, 1:], zs[:, None, :POOL_WIDTH]], axis=1))
            shift_out_s.append(zs[:, POOL_WIDTH:])
            wkv_out_s.append(_state_from_lanes_sample(s_l))
        else:
            gw = SGU_WIDTH // SGU_GROUPS
            zc = _linear([xp], [w_in_c_b[j]], gain=norm_mix_g[l], act="gelu", out_dtype=BF16, tm=1024,
                         tn=2 * SGU_WIDTH, name="sgu_in_prompt")
            bias = jnp.repeat(sgu_b_s[j].T, gw, axis=1)
            xp = _sgu_prompt(zc, xp, sgu_ln_g[j], sgu_ln_b[j], sgu_w_s[j], bias, w_out_c_b[j], tm=1024, ts=256)
            zc = _linear([xs], [w_in_c_b[j]], gain=norm_mix_g[l], act="gelu", tm=bs, tn=512, name="sgu_in_sample")
            gate, vn = _sgu_sample(zc, sgu_ln_g[j], sgu_ln_b[j], jnp.repeat(sgu_w_s[j][:, 0, 0], gw),
                                   jnp.repeat(sgu_b_s[j][:, 0], gw))
            xs = _linear([gate], [w_out_c_b[j]], res=xs, tm=bs, tn=512, name="sgu_out_sample")
            sgu_v_s.append(vn.reshape(bs, 1, SGU_WIDTH))

        xp = _xattn_prompt(xp.reshape(bp, lp, d), norm_xa_g[l], mem_k_p[l].reshape(bp, MEM_LEN, d),
                           mem_v_p[l].reshape(bp, MEM_LEN, d), w_xq_b[l], w_xo_b[l], tm=1024).reshape(tp, d)
        q = _linear([xs], [w_xq_b[l]], gain=norm_xa_g[l], tm=bs, tn=512, name="xattn_q_sample")
        att = _xattn_sample_core(q, cache_mem_k, cache_mem_v, l, bs=4)
        xs = _linear([att], [w_xo_b[l]], res=xs, tm=bs, tn=512, name="xattn_o_sample")

        fg = norm_final_g if l == DEPTH - 1 else None
        xp = _mlp(xp, norm_ffn_g[l], w_up_b[l], w_down_b[l], final_gain=fg, tm=1024, tf=1024)
        xs = _mlp(xs, norm_ffn_g[l], w_up_b[l], w_down_b[l], final_gain=fg, tm=bs, tf=512)

    return (xp.reshape(bp, lp, d),
            xs.reshape(bs, 1, d),
            mem_k_out,
            mem_v_out,
            jnp.stack(pool_out_p),
            jnp.stack(pool_out_s),
            jnp.stack(shift_out_p),
            jnp.stack(shift_out_s),
            jnp.stack(wkv_out_p),
            jnp.stack(wkv_out_s),
            jnp.stack(sgu_v_s))
```

```python
import functools

import jax
import jax.numpy as jnp
from jax import lax
from jax.experimental import pallas as pl
from jax.experimental.pallas import tpu as pltpu

D_MODEL = 1024
DEPTH = 2
PAST_LEN = 16384
POOL_WINDOWS = (2, 4, 8, 16)
POOL_GROUP_WIDTH = 128
POOL_WIDTH = 512
POOL_BUF = 15
RWKV_HEAD = 64
RWKV_WIDTH = 512
RWKV_HEADS = 8
W_LORA = 64
A_LORA = 64
G_LORA = 128
RWKV_IN = 1792
LORA_IN = W_LORA + A_LORA + G_LORA
CHUNK = 128
SGU_WIDTH = 2048
SGU_GROUPS = 4
MEM_LEN = 256
XA_HEADS = 4
XA_HEAD_DIM = 256
D_FF = 4096
RMS_EPS = 1e-5
LN_EPS = 1e-5
GN_EPS = RWKV_HEAD * 1e-5
L2_EPS = 1e-12
DECAY_SCALE = 0.6065306597126334

LANES = 128
F32_SUBLANES = 8
VMEM_LIMIT = 56 * 1024 * 1024

F32 = jnp.float32
BF16 = jnp.bfloat16


def _params(*sem):
    return pltpu.CompilerParams(dimension_semantics=sem, vmem_limit_bytes=VMEM_LIMIT)


def _dot(a, b):
    return jnp.dot(a.astype(BF16), b.astype(BF16), preferred_element_type=F32)


def _dot_nt(a, b):
    return lax.dot_general(a.astype(BF16), b.astype(BF16), (((1,), (1,)), ((), ())),
                           preferred_element_type=F32)


def _rms(x, g):
    return x * lax.rsqrt(jnp.mean(x * x, axis=-1, keepdims=True) + RMS_EPS) * g


def _head_ones(width, head):
    r = lax.broadcasted_iota(jnp.int32, (width, width), 0) // head
    c = lax.broadcasted_iota(jnp.int32, (width, width), 1) // head
    return (r == c).astype(BF16)


def _head_sum(x, ones):
    hi = x.astype(BF16)
    lo = (x - hi.astype(F32)).astype(BF16)
    return (jnp.dot(hi, ones, preferred_element_type=F32)
            + jnp.dot(lo, ones, preferred_element_type=F32))


def _linear_kernel(*refs, nx, has_gain, act, has_res):
    x_refs = refs[:nx]
    pos = nx
    g_ref = refs[pos] if has_gain else None
    pos += int(has_gain)
    w_refs = refs[pos:pos + nx]
    pos += nx
    res_ref = refs[pos] if has_res else None
    pos += int(has_res)
    o_ref = refs[pos]
    if has_gain and len(refs) == pos + 1:
        lhs = [_rms(x_refs[0][...].astype(F32), g_ref[...]).astype(BF16)]
    elif has_gain:
        xn_ref = refs[pos + 1]

        @pl.when(pl.program_id(1) == 0)
        def _():
            xn_ref[...] = _rms(x_refs[0][...].astype(F32), g_ref[...]).astype(BF16)

        lhs = [xn_ref[...]]
    else:
        lhs = [r[...] for r in x_refs]
    tn = o_ref.shape[1]
    cw = next(c for c in (256, 384, 128) if tn % c == 0)
    for c0 in range(0, tn, cw):
        cols = slice(c0, c0 + cw)
        acc = _dot(lhs[0], w_refs[0][:, cols])
        for l, w in zip(lhs[1:], w_refs[1:]):
            acc = acc + _dot(l, w[:, cols])
        if act == "gelu":
            acc = 0.5 * acc * (1.0 + lax.erf(acc * 0.7071067811865476))
        if has_res:
            acc = acc + res_ref[:, cols]
        o_ref[:, cols] = acc.astype(o_ref.dtype)


def _linear(xs, ws, *, gain=None, act=None, res=None, out_dtype=F32, tm, tn, name):
    m = xs[0].shape[0]
    n = ws[0].shape[1]
    assert m % tm == 0 and n % tn == 0, (m, tm, n, tn)
    assert gain is None or len(xs) == 1
    in_specs = [pl.BlockSpec((tm, x.shape[1]), lambda i, j: (i, 0)) for x in xs]
    args = list(xs)
    if gain is not None:
        in_specs.append(pl.BlockSpec((1, gain.shape[-1]), lambda i, j: (0, 0)))
        args.append(gain.reshape(1, -1))
    in_specs += [pl.BlockSpec((w.shape[0], tn), lambda i, j: (0, j)) for w in ws]
    args += list(ws)
    if res is not None:
        in_specs.append(pl.BlockSpec((tm, tn), lambda i, j: (i, j)))
        args.append(res)
    scratch = [pltpu.VMEM((tm, xs[0].shape[1]), BF16)] if gain is not None and n > tn else []
    kern = functools.partial(_linear_kernel, nx=len(xs), has_gain=gain is not None, act=act,
                             has_res=res is not None)
    return pl.pallas_call(
        kern,
        grid=(m // tm, n // tn),
        in_specs=in_specs,
        out_specs=pl.BlockSpec((tm, tn), lambda i, j: (i, j)),
        out_shape=jax.ShapeDtypeStruct((m, n), out_dtype),
        scratch_shapes=scratch,
        compiler_params=_params("parallel", "arbitrary"),
        name=name,
    )(*args)


def _mem_kv_kernel(m_ref, g_ref, wk_ref, wv_ref, k2_ref, v2_ref, k5_ref, v5_ref):
    mn = _rms(m_ref[...], g_ref[0]).astype(BF16)
    nb = k5_ref.shape[1]
    for w_ref, o2_ref, o5_ref in ((wk_ref, k2_ref, k5_ref), (wv_ref, v2_ref, v5_ref)):
        acc = jnp.dot(mn, w_ref[0], preferred_element_type=F32)
        o2_ref[0] = acc
        for bb in range(nb):
            for h in range(XA_HEADS):
                o5_ref[0, bb, :, h, :] = acc[bb * MEM_LEN:(bb + 1) * MEM_LEN, h * XA_HEAD_DIM:(h + 1) * XA_HEAD_DIM]


def _mem_kv(mem, gains, w_k, w_v, *, bb):
    b, m, d = mem.shape
    depth = w_k.shape[0]
    assert b % bb == 0
    tm = bb * m
    w_spec = pl.BlockSpec((1, d, d), lambda l, i: (l, 0, 0))
    o2_spec = pl.BlockSpec((1, tm, d), lambda l, i: (l, i, 0))
    o5_spec = pl.BlockSpec((1, bb, m, XA_HEADS, XA_HEAD_DIM), lambda l, i: (l, i, 0, 0, 0))
    o2_shape = jax.ShapeDtypeStruct((depth, b * m, d), F32)
    o5_shape = jax.ShapeDtypeStruct((depth, b, m, XA_HEADS, XA_HEAD_DIM), F32)
    return pl.pallas_call(
        _mem_kv_kernel,
        grid=(depth, b // bb),
        in_specs=[pl.BlockSpec((tm, d), lambda l, i: (i, 0)), pl.BlockSpec((1, 1, d), lambda l, i: (l, 0, 0)), w_spec, w_spec],
        out_specs=[o2_spec, o2_spec, o5_spec, o5_spec],
        out_shape=[o2_shape, o2_shape, o5_shape, o5_shape],
        compiler_params=_params("parallel", "parallel"),
        name="mem_kv",
    )(mem.reshape(b * m, d), gains.reshape(depth, 1, d), w_k, w_v)


def _mlp_kernel(*refs, final_norm, tf):
    if final_norm:
        x_ref, g_ref, wu_ref, wd_ref, gf_ref, o_ref = refs
    else:
        x_ref, g_ref, wu_ref, wd_ref, o_ref = refs
    x = x_ref[...]
    xn = _rms(x, g_ref[...]).astype(BF16)
    y = x
    for f0 in range(0, wu_ref.shape[1], tf):
        h = jnp.dot(xn, wu_ref[:, f0:f0 + tf], preferred_element_type=F32)
        h = jnp.square(jnp.maximum(h, 0.0)).astype(BF16)
        y = y + jnp.dot(h, wd_ref[f0:f0 + tf, :], preferred_element_type=F32)
    if final_norm:
        y = _rms(y, gf_ref[...])
    o_ref[...] = y


def _mlp(x, gain, w_up, w_down, *, final_gain=None, tm, tf):
    m, d = x.shape
    f = w_up.shape[1]
    assert m % tm == 0 and f % tf == 0
    resident = dict(pipeline_mode=pl.Buffered(1))
    in_specs = [
        pl.BlockSpec((tm, d), lambda i: (i, 0)),
        pl.BlockSpec((1, d), lambda i: (0, 0)),
        pl.BlockSpec((d, f), lambda i: (0, 0), **resident),
        pl.BlockSpec((f, d), lambda i: (0, 0), **resident),
    ]
    args = [x, gain.reshape(1, d), w_up, w_down]
    if final_gain is not None:
        in_specs.append(pl.BlockSpec((1, d), lambda i: (0, 0)))
        args.append(final_gain.reshape(1, d))
    return pl.pallas_call(
        functools.partial(_mlp_kernel, final_norm=final_gain is not None, tf=tf),
        grid=(m // tm,),
        in_specs=in_specs,
        out_specs=pl.BlockSpec((tm, d), lambda i: (i, 0)),
        out_shape=jax.ShapeDtypeStruct((m, d), F32),
        compiler_params=_params("parallel"),
        name="mlp",
    )(*args)


def _xattn_prompt_kernel(x_ref, g_ref, k_ref, v_ref, wq_ref, wo_ref, o_ref, att_ref):
    x = x_ref[0]
    xn = _rms(x, g_ref[...]).astype(BF16)
    q = jnp.dot(xn, wq_ref[...], preferred_element_type=F32) * (XA_HEAD_DIM ** -0.5)
    for h in range(XA_HEADS):
        cols = slice(h * XA_HEAD_DIM, (h + 1) * XA_HEAD_DIM)
        s = _dot_nt(q[:, cols], k_ref[0, :, cols])
        s = s - jnp.max(s, axis=-1, keepdims=True)
        p = jnp.exp(s)
        p = p / jnp.sum(p, axis=-1, keepdims=True)
        att_ref[:, cols] = _dot(p, v_ref[0, :, cols]).astype(BF16)
    o_ref[0] = x + jnp.dot(att_ref[...], wo_ref[...], preferred_element_type=F32)


def _xattn_prompt(x, gain, mem_k, mem_v, w_q, w_o, *, tm):
    b, l, d = x.shape
    assert l % tm == 0
    return pl.pallas_call(
        _xattn_prompt_kernel,
        grid=(b, l // tm),
        in_specs=[
            pl.BlockSpec((1, tm, d), lambda bi, i: (bi, i, 0)),
            pl.BlockSpec((1, d), lambda bi, i: (0, 0)),
            pl.BlockSpec((1, MEM_LEN, d), lambda bi, i: (bi, 0, 0)),
            pl.BlockSpec((1, MEM_LEN, d), lambda bi, i: (bi, 0, 0)),
            pl.BlockSpec((d, d), lambda bi, i: (0, 0)),
            pl.BlockSpec((d, d), lambda bi, i: (0, 0)),
        ],
        out_specs=pl.BlockSpec((1, tm, d), lambda bi, i: (bi, i, 0)),
        out_shape=jax.ShapeDtypeStruct((b, l, d), F32),
        scratch_shapes=[pltpu.VMEM((tm, d), BF16)],
        compiler_params=_params("parallel", "parallel"),
        name="xattn_prompt",
    )(x, gain.reshape(1, d), mem_k, mem_v, w_q, w_o)


XA_TILES = XA_HEAD_DIM // LANES


def _xattn_sample_kernel(q_ref, k_ref, v_ref, o_ref, *, bs):
    for b in range(bs):
        q = q_ref[b] * (XA_HEAD_DIM ** -0.5)
        prod = k_ref[0, b] * q[None]
        prod = prod + pltpu.roll(prod, XA_HEADS, 1)
        s = jnp.sum(prod, axis=-1, keepdims=True)
        p = jnp.exp(s - jnp.max(s, axis=0, keepdims=True))
        den = jnp.sum(p, axis=0)
        o_ref[b] = jnp.sum(p * v_ref[0, b], axis=0) / den


def _head_tile_view(x):
    lead = x.shape[:-2]
    n = len(lead)
    x = x.reshape(lead + (XA_HEADS, XA_TILES, LANES))
    return x.transpose(tuple(range(n)) + (n + 1, n, n + 2)).reshape(lead + (XA_TILES * XA_HEADS, LANES))


def _xattn_sample_core(q, cache_k, cache_v, layer, *, bs):
    b = q.shape[0]
    assert b % bs == 0 and XA_TILES == 2 and XA_TILES * XA_HEADS == F32_SUBLANES
    rows = XA_TILES * XA_HEADS
    q_spec = pl.BlockSpec((bs, rows, LANES), lambda i: (i, 0, 0))
    kv_spec = pl.BlockSpec((1, bs, MEM_LEN, rows, LANES), lambda i: (layer, i, 0, 0, 0))
    out = pl.pallas_call(
        functools.partial(_xattn_sample_kernel, bs=bs),
        grid=(b // bs,),
        in_specs=[q_spec, kv_spec, kv_spec],
        out_specs=q_spec,
        out_shape=jax.ShapeDtypeStruct((b, rows, LANES), F32),
        compiler_params=_params("parallel"),
        name="xattn_sample",
    )(_head_tile_view(q.reshape(b, XA_HEADS, XA_HEAD_DIM)), _head_tile_view(cache_k), _head_tile_view(cache_v))
    return out.reshape(b, XA_TILES, XA_HEADS, LANES).transpose(0, 2, 1, 3).reshape(b, XA_HEADS * XA_HEAD_DIM)


HALO = 16


def _pool_windows(z_ref, pos, w_ref, scale_ref, o_ref, tm):
    for gi, win in enumerate(POOL_WINDOWS):
        cols = slice(gi * POOL_GROUP_WIDTH, (gi + 1) * POOL_GROUP_WIDTH)
        tok = z_ref[HALO:HALO + tm, cols]
        acc = tok
        for k in range(1, win):
            acc = acc + z_ref[HALO - k:HALO - k + tm, cols]
        count = jnp.minimum(pos + 1, win).astype(F32)
        dlt = acc / count - tok
        o_ref[0, :, cols] = (_dot(dlt, w_ref[gi]) * scale_ref[:, cols]).astype(o_ref.dtype)


def _pool_sample_kernel(z_ref, buf_ref, w_ref, scale_ref, o_ref):
    for gi, win in enumerate(POOL_WINDOWS):
        cols = slice(gi * POOL_GROUP_WIDTH, (gi + 1) * POOL_GROUP_WIDTH)
        tok = z_ref[:, cols]
        acc = tok
        for k in range(1, win):
            acc = acc + buf_ref[:, POOL_BUF - k, cols]
        count = float(min(PAST_LEN + 1, win))
        dlt = acc / count - tok
        o_ref[:, cols] = (_dot(dlt, w_ref[gi]) * scale_ref[:, cols]).astype(o_ref.dtype)


def _pool_sample(z, buf, w_group, scale):
    b = z.shape[0]
    return pl.pallas_call(
        _pool_sample_kernel,
        grid=(1,),
        in_specs=[
            pl.BlockSpec((b, POOL_WIDTH), lambda i: (0, 0)),
            pl.BlockSpec((b, POOL_BUF, POOL_WIDTH), lambda i: (0, 0, 0)),
            pl.BlockSpec((len(POOL_WINDOWS), POOL_GROUP_WIDTH, POOL_GROUP_WIDTH), lambda i: (0, 0, 0)),
            pl.BlockSpec((1, POOL_WIDTH), lambda i: (0, 0)),
        ],
        out_specs=pl.BlockSpec((b, POOL_WIDTH), lambda i: (0, 0)),
        out_shape=jax.ShapeDtypeStruct((b, POOL_WIDTH), BF16),
        compiler_params=_params("arbitrary"),
        name="pool_sample",
    )(z, buf, w_group, scale.reshape(1, POOL_WIDTH))


def _rwkv_prep_math(z4, s4, p, out_refs, sl):
    zr, zk, zv, zl = z4
    sr, sk, sv, sq = s4
    mu_r, mu_k, mu_v, mu_l, w0, a0, k_k, k_a, r_k, w_la, w_g = p
    r = zr + (sr - zr) * mu_r
    k = zk + (sk - zk) * mu_k
    v = zv + (sv - zv) * mu_v
    lo = zl + (sq - zl) * mu_l
    wal = lo[:, :W_LORA + A_LORA]
    lane = lax.broadcasted_iota(jnp.int32, wal.shape, 1)
    feat = jnp.where(lane < W_LORA, 2.0 * jax.nn.sigmoid(2.0 * wal) - 1.0, wal)
    wa = _dot(feat, w_la)
    g = _dot(jax.nn.sigmoid(lo[:, W_LORA + A_LORA:]), w_g)
    decay = jnp.exp(-DECAY_SCALE * jax.nn.sigmoid(w0 + wa[:, :RWKV_WIDTH]))
    a = jax.nn.sigmoid(a0 + wa[:, RWKV_WIDTH:])
    kk = k * k_k
    k2 = k * (1.0 + (a - 1.0) * k_a)
    ones = _head_ones(RWKV_WIDTH, RWKV_HEAD)
    kk = kk * jnp.minimum(lax.rsqrt(_head_sum(kk * kk, ones)), 1.0 / L2_EPS)
    bonus = _head_sum(r * k2 * r_k, ones) * v
    r_o, k_o, v_o, kap_o, b_o, d_o, g_o, bonus_o = out_refs
    r_o[sl] = r
    k_o[sl] = k2
    v_o[sl] = v
    kap_o[sl] = kk
    b_o[sl] = kk * a
    d_o[sl] = decay
    g_o[sl] = g
    bonus_o[sl] = bonus


def _load_params(refs):
    return tuple(r[...] for r in refs)


AB_IN = POOL_WIDTH + RWKV_IN
LORA_COL = POOL_WIDTH + 3 * RWKV_WIDTH


def _ab_front_kernel(x_ref, g_ref, w_ref, pw_ref, ps_ref, *refs, tm):
    p_refs = refs[:11]
    pool_ref = refs[11]
    out_refs = refs[12:20]
    pool_tail_ref, shift_tail_ref, z_ref = refs[20:23]
    t = pl.program_id(1)

    @pl.when(t == 0)
    def _():
        z_ref[0:HALO, :] = jnp.zeros((HALO, AB_IN), F32)

    @pl.when(t > 0)
    def _():
        z_ref[0:HALO, :] = z_ref[tm:tm + HALO, :]

    xn = _rms(x_ref[0], g_ref[...]).astype(BF16)
    cw = 256
    for c0 in range(0, AB_IN, cw):
        z_ref[HALO:HALO + tm, c0:c0 + cw] = jnp.dot(xn, w_ref[:, c0:c0 + cw], preferred_element_type=F32)

    pos = t * tm + lax.broadcasted_iota(jnp.int32, (tm, 1), 0)
    _pool_windows(z_ref, pos, pw_ref, ps_ref, pool_ref, tm)

    col = lambda c0, width, r0: z_ref[r0:r0 + tm, c0:c0 + width]
    blocks = [(POOL_WIDTH + k * RWKV_WIDTH, RWKV_WIDTH) for k in range(3)] + [(LORA_COL, LORA_IN)]
    z4 = [col(c0, width, HALO) for c0, width in blocks]
    s4 = [col(c0, width, HALO - 1) for c0, width in blocks]
    _rwkv_prep_math(z4, s4, _load_params(p_refs), out_refs, (0,))

    pool_tail_ref[0] = z_ref[tm:tm + HALO, 0:POOL_WIDTH]
    shift_tail_ref[0] = z_ref[tm + HALO - F32_SUBLANES:tm + HALO, POOL_WIDTH:]


def _prep_param_args(mu, w0, a0, k_k, k_a, r_k, w_la, w_g):
    w = RWKV_WIDTH
    row = lambda x: x.reshape(1, -1)
    return [row(mu[:w]), row(mu[w:2 * w]), row(mu[2 * w:3 * w]), row(mu[3 * w:]),
            row(w0), row(a0), row(k_k), row(k_a), row(r_k), w_la, w_g]


def _prep_out_shapes(lead):
    return [jax.ShapeDtypeStruct(lead + (RWKV_WIDTH,), F32) for _ in range(8)]


def _ab_front_prompt(x, gain, w_in, pool_w, pool_scale, params, *, tm):
    b, l, d = x.shape
    assert l % tm == 0 and tm % HALO == 0
    full = lambda a: pl.BlockSpec(a.shape, lambda bi, i: (0,) * a.ndim)
    resident = dict(pipeline_mode=pl.Buffered(1))
    in_specs = [
        pl.BlockSpec((1, tm, d), lambda bi, i: (bi, i, 0)),
        pl.BlockSpec((1, d), lambda bi, i: (0, 0)),
        pl.BlockSpec((d, AB_IN), lambda bi, i: (0, 0), **resident),
        full(pool_w),
        pl.BlockSpec((1, POOL_WIDTH), lambda bi, i: (0, 0)),
    ] + [full(a) for a in params]
    tok_spec = pl.BlockSpec((1, tm, RWKV_WIDTH), lambda bi, i: (bi, i, 0))
    out_specs = [tok_spec] * 9 + [pl.BlockSpec((1, HALO, POOL_WIDTH), lambda bi, i: (bi, 0, 0)),
                                  pl.BlockSpec((1, F32_SUBLANES, RWKV_IN), lambda bi, i: (bi, 0, 0))]
    out_shape = ([jax.ShapeDtypeStruct((b, l, POOL_WIDTH), BF16)] + _prep_out_shapes((b, l))
                 + [jax.ShapeDtypeStruct((b, HALO, POOL_WIDTH), F32), jax.ShapeDtypeStruct((b, F32_SUBLANES, RWKV_IN), F32)])
    return pl.pallas_call(
        functools.partial(_ab_front_kernel, tm=tm),
        grid=(b, l // tm),
        in_specs=in_specs,
        out_specs=out_specs,
        out_shape=out_shape,
        scratch_shapes=[pltpu.VMEM((tm + HALO, AB_IN), F32)],
        compiler_params=_params("parallel", "arbitrary"),
        name="ab_front_prompt",
    )(x, gain.reshape(1, d), w_in, pool_w, pool_scale.reshape(1, POOL_WIDTH), *params)


def _rwkv_prep_sample_kernel(zr_ref, zk_ref, zv_ref, zl_ref, sr_ref, sk_ref, sv_ref, sl_ref, *refs):
    p_refs = refs[:11]
    out_refs = refs[11:19]
    z4 = [zr_ref[...], zk_ref[...], zv_ref[...], zl_ref[...]]
    s4 = [sr_ref[...], sk_ref[...], sv_ref[...], sl_ref[...]]
    _rwkv_prep_math(z4, s4, _load_params(p_refs), out_refs, (Ellipsis,))


def _rwkv_prep_sample(z, shift, params):
    b = z.shape[0]
    pw = POOL_WIDTH // RWKV_WIDTH
    full = lambda a: pl.BlockSpec(a.shape, lambda i: (0,) * a.ndim)
    in_specs = [
        pl.BlockSpec((b, RWKV_WIDTH), lambda i: (0, pw)),
        pl.BlockSpec((b, RWKV_WIDTH), lambda i: (0, pw + 1)),
        pl.BlockSpec((b, RWKV_WIDTH), lambda i: (0, pw + 2)),
        pl.BlockSpec((b, LORA_IN), lambda i: (0, (POOL_WIDTH + 3 * RWKV_WIDTH) // LORA_IN)),
        pl.BlockSpec((b, RWKV_WIDTH), lambda i: (0, 0)),
        pl.BlockSpec((b, RWKV_WIDTH), lambda i: (0, 1)),
        pl.BlockSpec((b, RWKV_WIDTH), lambda i: (0, 2)),
        pl.BlockSpec((b, LORA_IN), lambda i: (0, 3 * RWKV_WIDTH // LORA_IN)),
    ] + [full(a) for a in params]
    out_spec = pl.BlockSpec((b, RWKV_WIDTH), lambda i: (0, 0))
    return pl.pallas_call(
        _rwkv_prep_sample_kernel,
        grid=(1,),
        in_specs=in_specs,
        out_specs=[out_spec] * 8,
        out_shape=_prep_out_shapes((b,)),
        compiler_params=_params("arbitrary"),
        name="rwkv_prep_sample",
    )(z, z, z, z, shift, shift, shift, shift, *params)


def _rwkv_scan_kernel(kap_ref, d_ref, b_ref, k_ref, r_ref, v_ref, s0_ref, y_ref, sout_ref, s_ref, *, tt, nib, vp):
    tb = pl.program_id(1)
    sub = F32_SUBLANES

    @pl.when(tb == 0)
    def _():
        s_ref[...] = s0_ref[0]

    for i in range(nib * sub if vp > tt else 0):
        y_ref[0, i * vp + tt:(i + 1) * vp, :] = jnp.zeros((vp - tt, LANES), F32)

    nparts = 2
    look_ahead = vp > tt

    def add_to(acc, slot, term):
        acc[slot] = term if acc[slot] is None else acc[slot] + term

    def block_sums(acc):
        return tuple(acc[nparts * ib] + acc[nparts * ib + 1] for ib in range(nib))

    def first_sa():
        acc = [None] * (nparts * nib)
        for j in range(RWKV_HEAD):
            kap = kap_ref[0, pl.ds(j * vp, 1), :]
            for ib in range(nib):
                add_to(acc, nparts * ib + j % nparts, s_ref[ib, j] * kap)
        return tuple(-x for x in block_sums(acc))

    def step(t, sa):
        vy_rows = [pl.ds(ib * sub * vp + t, sub, stride=vp) if vp > 1 else pl.ds(ib * sub, sub) for ib in range(nib)]
        v = [v_ref[0, rows, :] for rows in vy_rows]
        acc_y = [None] * (nparts * nib)
        acc_s = [None] * (nparts * nib)
        for j in range(RWKV_HEAD):
            row = pl.ds(j * vp + t, 1)
            dec = d_ref[0, row, :]
            bb = b_ref[0, row, :]
            kk = k_ref[0, row, :]
            rr = r_ref[0, row, :]
            kap_next = kap_ref[0, pl.ds(j * vp + t + 1, 1), :] if look_ahead else None
            for ib in range(nib):
                s = s_ref[ib, j] * dec + sa[ib] * bb + v[ib] * kk
                s_ref[ib, j] = s
                add_to(acc_y, nparts * ib + j % nparts, s * rr)
                if look_ahead:
                    add_to(acc_s, nparts * ib + j % nparts, s * kap_next)
        for ib, y in enumerate(block_sums(acc_y)):
            y_ref[0, vy_rows[ib], :] = y
        return tuple(-x for x in block_sums(acc_s)) if look_ahead else sa

    lax.fori_loop(0, tt, step, first_sa())

    @pl.when(tb == pl.num_programs(1) - 1)
    def _():
        sout_ref[0] = s_ref[...]


def _rwkv_scan(kap, dec, bb, kk, rr, v, s0, *, tt, vp):
    g, nb, rows, _ = kap.shape
    n = RWKV_HEAD
    nib = s0.shape[1]
    assert rows == n * vp and v.shape[2] == nib * F32_SUBLANES * vp and tt <= vp
    op_spec = pl.BlockSpec((None, 1, n * vp, LANES), lambda gi, i: (gi, i, 0, 0))
    v_spec = pl.BlockSpec((None, 1, v.shape[2], LANES), lambda gi, i: (gi, i, 0, 0))
    s_spec = pl.BlockSpec((1, nib, n, F32_SUBLANES, LANES), lambda gi, i: (gi, 0, 0, 0, 0))
    return pl.pallas_call(
        functools.partial(_rwkv_scan_kernel, tt=tt, nib=nib, vp=vp),
        grid=(g, nb),
        in_specs=[op_spec] * 5 + [v_spec, s_spec],
        out_specs=[v_spec, s_spec],
        out_shape=[jax.ShapeDtypeStruct(v.shape, F32), jax.ShapeDtypeStruct(s0.shape, F32)],
        scratch_shapes=[pltpu.VMEM((nib, n, F32_SUBLANES, LANES), F32)],
        compiler_params=_params("parallel", "arbitrary"),
        name="rwkv_scan",
    )(kap, dec, bb, kk, rr, v, s0)


def _rwkv_post_kernel(y_ref, bonus_ref, g_ref, gng_ref, gnb_ref, o_ref):
    y = y_ref[...]
    ones = _head_ones(RWKV_WIDTH, RWKV_HEAD)
    m = _head_sum(y, ones) * (1.0 / RWKV_HEAD)
    c = y - m
    var = _head_sum(c * c, ones) * (1.0 / RWKV_HEAD)
    yn = c * lax.rsqrt(var + GN_EPS) * gng_ref[...] + gnb_ref[...]
    o_ref[...] = ((yn + bonus_ref[...]) * g_ref[...]).astype(o_ref.dtype)


def _rwkv_post(y, bonus, g, gn_g, gn_b, *, tm):
    m = y.shape[0]
    assert m % tm == 0
    spec = pl.BlockSpec((tm, RWKV_WIDTH), lambda i: (i, 0))
    pspec = pl.BlockSpec((1, RWKV_WIDTH), lambda i: (0, 0))
    return pl.pallas_call(
        _rwkv_post_kernel,
        grid=(m // tm,),
        in_specs=[spec, spec, spec, pspec, pspec],
        out_specs=spec,
        out_shape=jax.ShapeDtypeStruct((m, RWKV_WIDTH), BF16),
        compiler_params=_params("parallel"),
        name="rwkv_post",
    )(y, bonus, g, gn_g.reshape(1, -1), gn_b.reshape(1, -1))


def _layernorm(v, g, b):
    m = jnp.mean(v, axis=-1, keepdims=True)
    c = v - m
    var = jnp.mean(c * c, axis=-1, keepdims=True)
    return c * lax.rsqrt(var + LN_EPS) * g + b


def _sgu_prompt_kernel(x_ref, g_ref, wi_ref, lng_ref, lnb_ref, ws_ref, bias_ref, wo_ref, o_ref, zc_ref, vn_ref, gate_ref,
                       *, tm, ts):
    xn = _rms(x_ref[...], g_ref[...]).astype(BF16)
    cw = 256
    for c0 in list(range(SGU_WIDTH, 2 * SGU_WIDTH, cw)) + list(range(0, SGU_WIDTH, cw)):
        acc = jnp.dot(xn, wi_ref[:, c0:c0 + cw], preferred_element_type=F32)
        zc_ref[:, c0:c0 + cw] = (0.5 * acc * (1.0 + lax.erf(acc * 0.7071067811865476))).astype(BF16)
    gw = SGU_WIDTH // SGU_GROUPS
    r = lax.broadcasted_iota(jnp.int32, (CHUNK, CHUNK), 0)
    c = lax.broadcasted_iota(jnp.int32, (CHUNK, CHUNK), 1)
    wms = [jnp.where(r >= c, ws_ref[gi], 0.0).astype(BF16) for gi in range(SGU_GROUPS)]
    for r0 in range(0, tm, ts):
        sub = slice(r0, r0 + ts)
        v = zc_ref[sub, SGU_WIDTH:].astype(F32)
        vn_ref[sub, :] = _layernorm(v, lng_ref[...], lnb_ref[...]).astype(BF16)
        for gi in range(SGU_GROUPS):
            cols = slice(gi * gw, (gi + 1) * gw)
            for ci in range(r0 // CHUNK, (r0 + ts) // CHUNK):
                rows = slice(ci * CHUNK, (ci + 1) * CHUNK)
                sp = jnp.dot(wms[gi], vn_ref[rows, cols], preferred_element_type=F32) + bias_ref[:, cols]
                gate_ref[rows, cols] = (zc_ref[rows, cols].astype(F32) * sp).astype(BF16)
        o_ref[sub, :] = x_ref[sub, :] + jnp.dot(gate_ref[sub, :], wo_ref[...], preferred_element_type=F32)


def _sgu_prompt(x, gain, w_in, ln_g, ln_b, w_s, bias, w_out, *, tm, ts):
    t, d = x.shape
    assert t % tm == 0 and tm % ts == 0 and ts % CHUNK == 0
    row = lambda a: a.reshape(1, -1)
    resident = dict(pipeline_mode=pl.Buffered(1))
    return pl.pallas_call(
        functools.partial(_sgu_prompt_kernel, tm=tm, ts=ts),
        grid=(t // tm,),
        in_specs=[
            pl.BlockSpec((tm, d), lambda i: (i, 0)),
            pl.BlockSpec((1, d), lambda i: (0, 0)),
            pl.BlockSpec((d, 2 * SGU_WIDTH), lambda i: (0, 0), **resident),
            pl.BlockSpec((1, SGU_WIDTH), lambda i: (0, 0)),
            pl.BlockSpec((1, SGU_WIDTH), lambda i: (0, 0)),
            pl.BlockSpec((SGU_GROUPS, CHUNK, CHUNK), lambda i: (0, 0, 0)),
            pl.BlockSpec((CHUNK, SGU_WIDTH), lambda i: (0, 0), **resident),
            pl.BlockSpec((SGU_WIDTH, d), lambda i: (0, 0), **resident),
        ],
        out_specs=pl.BlockSpec((tm, d), lambda i: (i, 0)),
        out_shape=jax.ShapeDtypeStruct((t, d), F32),
        scratch_shapes=[pltpu.VMEM((tm, 2 * SGU_WIDTH), BF16), pltpu.VMEM((tm, SGU_WIDTH), BF16),
                        pltpu.VMEM((tm, SGU_WIDTH), BF16)],
        compiler_params=_params("parallel"),
        name="sgu_prompt",
    )(x, row(gain), w_in, row(ln_g), row(ln_b), w_s, bias, w_out)


def _sgu_sample_kernel(u_ref, v_ref, lng_ref, lnb_ref, coef_ref, bias_ref, gate_ref, vn_ref):
    vn = _layernorm(v_ref[...], lng_ref[...], lnb_ref[...])
    vn_ref[...] = vn
    gate_ref[...] = (u_ref[...] * (vn * coef_ref[...] + bias_ref[...])).astype(gate_ref.dtype)


def _sgu_sample(zc, ln_g, ln_b, coef, bias):
    b = zc.shape[0]
    row = lambda a: a.reshape(1, -1)
    pspec = pl.BlockSpec((1, SGU_WIDTH), lambda i: (0, 0))
    return pl.pallas_call(
        _sgu_sample_kernel,
        grid=(1,),
        in_specs=[pl.BlockSpec((b, SGU_WIDTH), lambda i: (0, 0)), pl.BlockSpec((b, SGU_WIDTH), lambda i: (0, 1)),
                  pspec, pspec, pspec, pspec],
        out_specs=[pl.BlockSpec((b, SGU_WIDTH), lambda i: (0, 0))] * 2,
        out_shape=[jax.ShapeDtypeStruct((b, SGU_WIDTH), BF16), jax.ShapeDtypeStruct((b, SGU_WIDTH), F32)],
        compiler_params=_params("arbitrary"),
        name="sgu_sample",
    )(zc, zc, row(ln_g), row(ln_b), row(coef), row(bias))


SCAN_TT = 64
RELAYOUT_BLOCKS = 4
RELAYOUT_T = RELAYOUT_BLOCKS * SCAN_TT
V_PITCH = 72
HALF = RWKV_HEAD // 2


def _split_heads(z_ref, b, x_ref, c):
    xt = x_ref[b, c * LANES:(c + 1) * LANES, :].T
    for h in range(RWKV_HEADS):
        z_ref[b, h * V_PITCH:h * V_PITCH + RWKV_HEAD, :] = xt[h * RWKV_HEAD:(h + 1) * RWKV_HEAD]


def _head_rows(z_ref, b, n):
    return z_ref[b, pl.ds(n, RWKV_HEADS, stride=V_PITCH), :]


def _store_slab(o_ref, c, n, wt):
    pad = jnp.zeros((V_PITCH - SCAN_TT, LANES), F32)
    for k in range(LANES // SCAN_TT):
        blk = c * (LANES // SCAN_TT) + k
        o_ref[blk, n * V_PITCH:n * V_PITCH + SCAN_TT, :] = wt[k * SCAN_TT:(k + 1) * SCAN_TT]
        o_ref[blk, n * V_PITCH + SCAN_TT:(n + 1) * V_PITCH, :] = pad


def _key_to_lanes_kernel(x_ref, o_ref, z_ref):
    nb = x_ref.shape[0]
    for c in range(RELAYOUT_T // LANES):
        for b in range(nb):
            _split_heads(z_ref, b, x_ref, c)
        for j in range(RWKV_HEAD):
            rows = [_head_rows(z_ref, b, j) for b in range(nb)]
            _store_slab(o_ref, c, j, jnp.concatenate(rows + rows, axis=0).T)


def _value_to_lanes_kernel(x_ref, o_ref, z_ref):
    nb = x_ref.shape[0]
    for c in range(RELAYOUT_T // LANES):
        for b in range(nb):
            _split_heads(z_ref, b, x_ref, c)
        for i in range(HALF):
            rows = [_head_rows(z_ref, b, half * HALF + i) for half in range(2) for b in range(nb)]
            _store_slab(o_ref, c, i, jnp.concatenate(rows, axis=0).T)


def _ab_back_kernel(y_ref, bonus_ref, g_ref, pool_ref, x_ref, gng_ref, gnb_ref, w_ref, o_ref, z_ref, lhs_ref):
    nb = o_ref.shape[0]
    per = LANES // SCAN_TT
    for i in range(HALF):
        w = jnp.concatenate([y_ref[k, i * V_PITCH:i * V_PITCH + SCAN_TT, :] for k in range(per)], axis=0)
        wt = w.T
        for half in range(2):
            for b in range(nb):
                r0 = (half * nb + b) * RWKV_HEADS
                z_ref[b, pl.ds(half * HALF + i, RWKV_HEADS, stride=V_PITCH), :] = wt[r0:r0 + RWKV_HEADS]
    ones = _head_ones(RWKV_WIDTH, RWKV_HEAD)
    for b in range(nb):
        zt = jnp.concatenate([z_ref[b, h * V_PITCH:h * V_PITCH + RWKV_HEAD, :] for h in range(RWKV_HEADS)], axis=0)
        y = zt.T
        m = _head_sum(y, ones) * (1.0 / RWKV_HEAD)
        c = y - m
        var = _head_sum(c * c, ones) * (1.0 / RWKV_HEAD)
        yn = c * lax.rsqrt(var + GN_EPS) * gng_ref[...] + gnb_ref[...]
        rows = slice(b * LANES, (b + 1) * LANES)
        lhs_ref[rows, :POOL_WIDTH] = pool_ref[b]
        lhs_ref[rows, POOL_WIDTH:] = ((yn + bonus_ref[b]) * g_ref[b]).astype(BF16)
    mix = jnp.dot(lhs_ref[...], w_ref[...], preferred_element_type=F32)
    for b in range(nb):
        o_ref[b] = x_ref[b] + mix[b * LANES:(b + 1) * LANES]


def _key_to_lanes(x):
    b, l, w = x.shape
    assert 2 * b * RWKV_HEADS == LANES and l % RELAYOUT_T == 0 and w == RWKV_WIDTH
    out = pl.pallas_call(
        _key_to_lanes_kernel,
        grid=(l // RELAYOUT_T,),
        in_specs=[pl.BlockSpec((b, RELAYOUT_T, w), lambda i: (0, i, 0))],
        out_specs=pl.BlockSpec((RELAYOUT_BLOCKS, RWKV_HEAD * V_PITCH, LANES), lambda i: (i, 0, 0)),
        out_shape=jax.ShapeDtypeStruct((l // SCAN_TT, RWKV_HEAD * V_PITCH, LANES), F32),
        scratch_shapes=[pltpu.VMEM((b, RWKV_HEADS * V_PITCH, LANES), F32)],
        compiler_params=_params("parallel"),
        name="key_to_lanes",
    )(x)
    return out[None]


def _value_to_lanes(v):
    b, l, w = v.shape
    assert 2 * b * RWKV_HEADS == LANES and l % RELAYOUT_T == 0 and w == RWKV_WIDTH
    out = pl.pallas_call(
        _value_to_lanes_kernel,
        grid=(l // RELAYOUT_T,),
        in_specs=[pl.BlockSpec((b, RELAYOUT_T, w), lambda i: (0, i, 0))],
        out_specs=pl.BlockSpec((RELAYOUT_BLOCKS, HALF * V_PITCH, LANES), lambda i: (i, 0, 0)),
        out_shape=jax.ShapeDtypeStruct((l // SCAN_TT, HALF * V_PITCH, LANES), F32),
        scratch_shapes=[pltpu.VMEM((b, RWKV_HEADS * V_PITCH, LANES), F32)],
        compiler_params=_params("parallel"),
        name="value_to_lanes",
    )(v)
    return out[None]


def _ab_back_prompt(y, bonus, g, y_pool, x, gn_g, gn_b, w_out):
    b, l, d = x.shape
    per = LANES // SCAN_TT
    tok = lambda width: pl.BlockSpec((b, LANES, width), lambda i: (0, i, 0))
    pspec = pl.BlockSpec((1, RWKV_WIDTH), lambda i: (0, 0))
    return pl.pallas_call(
        _ab_back_kernel,
        grid=(l // LANES,),
        in_specs=[pl.BlockSpec((per, HALF * V_PITCH, LANES), lambda i: (i, 0, 0)),
                  tok(RWKV_WIDTH), tok(RWKV_WIDTH), tok(POOL_WIDTH), tok(d), pspec, pspec,
                  pl.BlockSpec(w_out.shape, lambda i: (0, 0), pipeline_mode=pl.Buffered(1))],
        out_specs=tok(d),
        out_shape=jax.ShapeDtypeStruct((b, l, d), F32),
        scratch_shapes=[pltpu.VMEM((b, RWKV_HEADS * V_PITCH, LANES), F32),
                        pltpu.VMEM((b * LANES, POOL_WIDTH + RWKV_WIDTH), BF16)],
        compiler_params=_params("parallel"),
        name="ab_back_prompt",
    )(y[0], bonus, g, y_pool, x, gn_g.reshape(1, -1), gn_b.reshape(1, -1), w_out)


def _state_from_lanes_prompt(s, b):
    s = s.reshape(HALF // F32_SUBLANES, RWKV_HEAD, F32_SUBLANES, 2, b, RWKV_HEADS).transpose(4, 5, 3, 0, 2, 1)
    return s.reshape(b, RWKV_HEADS, RWKV_HEAD, RWKV_HEAD)


def _to_lanes_sample(x):
    b = x.shape[0]
    g = b * RWKV_HEADS // LANES
    x = x.reshape(g, LANES // RWKV_HEADS, RWKV_HEADS, RWKV_HEAD).transpose(0, 3, 1, 2)
    return x.reshape(g, 1, RWKV_HEAD, LANES)


def _from_lanes_sample(y):
    g = y.shape[0]
    y = y.reshape(g, RWKV_HEAD, LANES // RWKV_HEADS, RWKV_HEADS).transpose(0, 2, 3, 1)
    return y.reshape(g * LANES // RWKV_HEADS, RWKV_WIDTH)


def _state_to_lanes_sample(s):
    b = s.shape[0]
    g = b * RWKV_HEADS // LANES
    nib = RWKV_HEAD // F32_SUBLANES
    s = s.reshape(g, LANES // RWKV_HEADS, RWKV_HEADS, nib, F32_SUBLANES, RWKV_HEAD).transpose(0, 3, 5, 4, 1, 2)
    return s.reshape(g, nib, RWKV_HEAD, F32_SUBLANES, LANES)


def _state_from_lanes_sample(s):
    g, nib = s.shape[:2]
    s = s.reshape(g, nib, RWKV_HEAD, F32_SUBLANES, LANES // RWKV_HEADS, RWKV_HEADS).transpose(0, 4, 5, 1, 3, 2)
    return s.reshape(g * LANES // RWKV_HEADS, RWKV_HEADS, RWKV_HEAD, RWKV_HEAD)


def kernel(x_prompt, x_sample, mem_prompt, cache_mem_k, cache_mem_v, state_pool, state_shift, state_wkv, norm_mix_g, norm_xa_g, norm_mem_g, norm_ffn_g, norm_final_g, w_in_ab, w_out_ab, pool_w, pool_scale, rwkv_mu, rwkv_w0, rwkv_w2, rwkv_a0, rwkv_a2, rwkv_g2, rwkv_k_k, rwkv_k_a, rwkv_r_k, rwkv_gn_g, rwkv_gn_b, w_in_c, sgu_ln_g, sgu_ln_b, sgu_w_s, sgu_b_s, w_out_c, w_xq, w_xk, w_xv, w_xo, w_ff_up, w_ff_down):
    bp, lp, d = x_prompt.shape
    bs = x_sample.shape[0]
    tp = bp * lp
    bf = lambda w: w.astype(BF16)

    w_in_ab_b, w_out_ab_b, pool_w_b = bf(w_in_ab), bf(w_out_ab), bf(pool_w)
    w_in_c_b, w_out_c_b = bf(w_in_c), bf(w_out_c)
    w_xq_b, w_xk_b, w_xv_b, w_xo_b = bf(w_xq), bf(w_xk), bf(w_xv), bf(w_xo)
    w_up_b, w_down_b = bf(w_ff_up), bf(w_ff_down)

    mem_k_p, mem_v_p, mem_k_out, mem_v_out = _mem_kv(mem_prompt, norm_mem_g, w_xk_b, w_xv_b, bb=4)

    xp = x_prompt.reshape(tp, d)
    xs = x_sample.reshape(bs, d)
    pool_out_p, pool_out_s, shift_out_p, shift_out_s, wkv_out_p, wkv_out_s, sgu_v_s = [], [], [], [], [], [], []

    for l in range(DEPTH):
        j = l // 2
        if l % 2 == 0:
            w_la = jnp.zeros((W_LORA + A_LORA, 2 * RWKV_WIDTH), F32)
            w_la = w_la.at[:W_LORA, :RWKV_WIDTH].set(rwkv_w2[j]).at[W_LORA:, RWKV_WIDTH:].set(rwkv_a2[j])
            params = _prep_param_args(rwkv_mu[j], rwkv_w0[j], rwkv_a0[j], rwkv_k_k[j], rwkv_k_a[j],
                                      rwkv_r_k[j].reshape(-1), bf(w_la), bf(rwkv_g2[j]))
            y_pool, r, k2, v, kap, bb, dec, g, bonus, pool_tail, shift_tail = _ab_front_prompt(
                xp.reshape(bp, lp, d), norm_mix_g[l], w_in_ab_b[j], pool_w_b[j], pool_scale[j], params, tm=512)
            s0 = jnp.zeros((1, HALF // F32_SUBLANES, RWKV_HEAD, F32_SUBLANES, LANES), F32)
            y_l, s_l = _rwkv_scan(_key_to_lanes(kap), _key_to_lanes(dec), _key_to_lanes(bb), _key_to_lanes(k2),
                                  _key_to_lanes(r), _value_to_lanes(v), s0, tt=SCAN_TT, vp=V_PITCH)
            xp = _ab_back_prompt(y_l, bonus, g, y_pool, xp.reshape(bp, lp, d), rwkv_gn_g[j], rwkv_gn_b[j],
                                 w_out_ab_b[j]).reshape(tp, d)
            pool_out_p.append(pool_tail[:, HALO - POOL_BUF:])
            shift_out_p.append(shift_tail[:, F32_SUBLANES - 1])
            wkv_out_p.append(_state_from_lanes_prompt(s_l, bp))
            zs = _linear([xs], [w_in_ab_b[j]], gain=norm_mix_g[l], tm=bs, tn=256, name="ab_in_sample")
            y_pool = _pool_sample(zs, state_pool[j], pool_w_b[j], pool_scale[j])
            r, k2, v, kap, bb, dec, g, bonus = _rwkv_prep_sample(zs, state_shift[j], params)
            y_l, s_l = _rwkv_scan(_to_lanes_sample(kap), _to_lanes_sample(dec), _to_lanes_sample(bb),
                                  _to_lanes_sample(k2), _to_lanes_sample(r), _to_lanes_sample(v),
                                  _state_to_lanes_sample(state_wkv[j]), tt=1, vp=1)
            y_rwkv = _rwkv_post(_from_lanes_sample(y_l), bonus, g, rwkv_gn_g[j], rwkv_gn_b[j], tm=bs)
            xs = _linear([y_pool, y_rwkv], [w_out_ab_b[j][:POOL_WIDTH], w_out_ab_b[j][POOL_WIDTH:]], res=xs,
                         tm=bs, tn=512, name="ab_out_sample")
            pool_out_s.append(jnp.concatenate([state_pool[j][:, 1:], zs[:, None, :POOL_WIDTH]], axis=1))
            shift_out_s.append(zs[:, POOL_WIDTH:])
            wkv_out_s.append(_state_from_lanes_sample(s_l))
        else:
            gw = SGU_WIDTH // SGU_GROUPS
            bias = jnp.repeat(sgu_b_s[j].T, gw, axis=1)
            xp = _sgu_prompt(xp, norm_mix_g[l], w_in_c_b[j], sgu_ln_g[j], sgu_ln_b[j], sgu_w_s[j], bias, w_out_c_b[j],
                             tm=512, ts=256)
            zc = _linear([xs], [w_in_c_b[j]], gain=norm_mix_g[l], act="gelu", tm=bs, tn=512, name="sgu_in_sample")
            gate, vn = _sgu_sample(zc, sgu_ln_g[j], sgu_ln_b[j], jnp.repeat(sgu_w_s[j][:, 0, 0], gw),
                                   jnp.repeat(sgu_b_s[j][:, 0], gw))
            xs = _linear([gate], [w_out_c_b[j]], res=xs, tm=bs, tn=512, name="sgu_out_sample")
            sgu_v_s.append(vn.reshape(bs, 1, SGU_WIDTH))

        xp = _xattn_prompt(xp.reshape(bp, lp, d), norm_xa_g[l], mem_k_p[l].reshape(bp, MEM_LEN, d),
                           mem_v_p[l].reshape(bp, MEM_LEN, d), w_xq_b[l], w_xo_b[l], tm=1024).reshape(tp, d)
        q = _linear([xs], [w_xq_b[l]], gain=norm_xa_g[l], tm=bs, tn=512, name="xattn_q_sample")
        att = _xattn_sample_core(q, cache_mem_k, cache_mem_v, l, bs=4)
        xs = _linear([att], [w_xo_b[l]], res=xs, tm=bs, tn=512, name="xattn_o_sample")

        fg = norm_final_g if l == DEPTH - 1 else None
        xp = _mlp(xp, norm_ffn_g[l], w_up_b[l], w_down_b[l], final_gain=fg, tm=1024, tf=1024)
        xs = _mlp(xs, norm_ffn_g[l], w_up_b[l], w_down_b[l], final_gain=fg, tm=bs, tf=512)

    return (xp.reshape(bp, lp, d),
            xs.reshape(bs, 1, d),
            mem_k_out,
            mem_v_out,
            jnp.stack(pool_out_p),
            jnp.stack(pool_out_s),
            jnp.stack(shift_out_p),
            jnp.stack(shift_out_s),
            jnp.stack(wkv_out_p),
            jnp.stack(wkv_out_s),
            jnp.stack(sgu_v_s))
```

```python
import functools

import jax
import jax.numpy as jnp
from jax import lax
from jax.experimental import pallas as pl
from jax.experimental.pallas import tpu as pltpu

D_MODEL = 1024
DEPTH = 2
PAST_LEN = 16384
POOL_WINDOWS = (2, 4, 8, 16)
POOL_GROUP_WIDTH = 128
POOL_WIDTH = 512
POOL_BUF = 15
RWKV_HEAD = 64
RWKV_WIDTH = 512
RWKV_HEADS = 8
W_LORA = 64
A_LORA = 64
G_LORA = 128
RWKV_IN = 1792
LORA_IN = W_LORA + A_LORA + G_LORA
CHUNK = 128
SGU_WIDTH = 2048
SGU_GROUPS = 4
MEM_LEN = 256
XA_HEADS = 4
XA_HEAD_DIM = 256
D_FF = 4096
RMS_EPS = 1e-5
LN_EPS = 1e-5
GN_EPS = RWKV_HEAD * 1e-5
L2_EPS = 1e-12
DECAY_SCALE = 0.6065306597126334

LANES = 128
F32_SUBLANES = 8
VMEM_LIMIT = 56 * 1024 * 1024

F32 = jnp.float32
BF16 = jnp.bfloat16


def _params(*sem):
    return pltpu.CompilerParams(dimension_semantics=sem, vmem_limit_bytes=VMEM_LIMIT)


def _dot(a, b):
    return jnp.dot(a.astype(BF16), b.astype(BF16), preferred_element_type=F32)


def _dot_nt(a, b):
    return lax.dot_general(a.astype(BF16), b.astype(BF16), (((1,), (1,)), ((), ())),
                           preferred_element_type=F32)


def _rms(x, g):
    return x * lax.rsqrt(jnp.mean(x * x, axis=-1, keepdims=True) + RMS_EPS) * g


def _head_ones(width, head):
    r = lax.broadcasted_iota(jnp.int32, (width, width), 0) // head
    c = lax.broadcasted_iota(jnp.int32, (width, width), 1) // head
    return (r == c).astype(BF16)


def _head_sum(x, ones):
    hi = x.astype(BF16)
    lo = (x - hi.astype(F32)).astype(BF16)
    return (jnp.dot(hi, ones, preferred_element_type=F32)
            + jnp.dot(lo, ones, preferred_element_type=F32))


def _linear_kernel(*refs, nx, has_gain, act, has_res):
    x_refs = refs[:nx]
    pos = nx
    g_ref = refs[pos] if has_gain else None
    pos += int(has_gain)
    w_refs = refs[pos:pos + nx]
    pos += nx
    res_ref = refs[pos] if has_res else None
    pos += int(has_res)
    o_ref = refs[pos]
    if has_gain and len(refs) == pos + 1:
        lhs = [_rms(x_refs[0][...].astype(F32), g_ref[...]).astype(BF16)]
    elif has_gain:
        xn_ref = refs[pos + 1]

        @pl.when(pl.program_id(1) == 0)
        def _():
            xn_ref[...] = _rms(x_refs[0][...].astype(F32), g_ref[...]).astype(BF16)

        lhs = [xn_ref[...]]
    else:
        lhs = [r[...] for r in x_refs]
    tn = o_ref.shape[1]
    cw = next(c for c in (256, 384, 128) if tn % c == 0)
    for c0 in range(0, tn, cw):
        cols = slice(c0, c0 + cw)
        acc = _dot(lhs[0], w_refs[0][:, cols])
        for l, w in zip(lhs[1:], w_refs[1:]):
            acc = acc + _dot(l, w[:, cols])
        if act == "gelu":
            acc = 0.5 * acc * (1.0 + lax.erf(acc * 0.7071067811865476))
        if has_res:
            acc = acc + res_ref[:, cols]
        o_ref[:, cols] = acc.astype(o_ref.dtype)


def _linear(xs, ws, *, gain=None, act=None, res=None, out_dtype=F32, tm, tn, name):
    m = xs[0].shape[0]
    n = ws[0].shape[1]
    assert m % tm == 0 and n % tn == 0, (m, tm, n, tn)
    assert gain is None or len(xs) == 1
    in_specs = [pl.BlockSpec((tm, x.shape[1]), lambda i, j: (i, 0)) for x in xs]
    args = list(xs)
    if gain is not None:
        in_specs.append(pl.BlockSpec((1, gain.shape[-1]), lambda i, j: (0, 0)))
        args.append(gain.reshape(1, -1))
    in_specs += [pl.BlockSpec((w.shape[0], tn), lambda i, j: (0, j)) for w in ws]
    args += list(ws)
    if res is not None:
        in_specs.append(pl.BlockSpec((tm, tn), lambda i, j: (i, j)))
        args.append(res)
    scratch = [pltpu.VMEM((tm, xs[0].shape[1]), BF16)] if gain is not None and n > tn else []
    kern = functools.partial(_linear_kernel, nx=len(xs), has_gain=gain is not None, act=act,
                             has_res=res is not None)
    return pl.pallas_call(
        kern,
        grid=(m // tm, n // tn),
        in_specs=in_specs,
        out_specs=pl.BlockSpec((tm, tn), lambda i, j: (i, j)),
        out_shape=jax.ShapeDtypeStruct((m, n), out_dtype),
        scratch_shapes=scratch,
        compiler_params=_params("parallel", "arbitrary"),
        name=name,
    )(*args)


def _mem_kv_kernel(m_ref, g_ref, wk_ref, wv_ref, k2_ref, v2_ref, k5_ref, v5_ref):
    mn = _rms(m_ref[...], g_ref[0]).astype(BF16)
    nb = k5_ref.shape[1]
    for w_ref, o2_ref, o5_ref in ((wk_ref, k2_ref, k5_ref), (wv_ref, v2_ref, v5_ref)):
        acc = jnp.dot(mn, w_ref[0], preferred_element_type=F32)
        o2_ref[0] = acc
        for bb in range(nb):
            for h in range(XA_HEADS):
                o5_ref[0, bb, :, h, :] = acc[bb * MEM_LEN:(bb + 1) * MEM_LEN, h * XA_HEAD_DIM:(h + 1) * XA_HEAD_DIM]


def _mem_kv(mem, gains, w_k, w_v, *, bb):
    b, m, d = mem.shape
    depth = w_k.shape[0]
    assert b % bb == 0
    tm = bb * m
    w_spec = pl.BlockSpec((1, d, d), lambda l, i: (l, 0, 0))
    o2_spec = pl.BlockSpec((1, tm, d), lambda l, i: (l, i, 0))
    o5_spec = pl.BlockSpec((1, bb, m, XA_HEADS, XA_HEAD_DIM), lambda l, i: (l, i, 0, 0, 0))
    o2_shape = jax.ShapeDtypeStruct((depth, b * m, d), F32)
    o5_shape = jax.ShapeDtypeStruct((depth, b, m, XA_HEADS, XA_HEAD_DIM), F32)
    return pl.pallas_call(
        _mem_kv_kernel,
        grid=(depth, b // bb),
        in_specs=[pl.BlockSpec((tm, d), lambda l, i: (i, 0)), pl.BlockSpec((1, 1, d), lambda l, i: (l, 0, 0)), w_spec, w_spec],
        out_specs=[o2_spec, o2_spec, o5_spec, o5_spec],
        out_shape=[o2_shape, o2_shape, o5_shape, o5_shape],
        compiler_params=_params("parallel", "parallel"),
        name="mem_kv",
    )(mem.reshape(b * m, d), gains.reshape(depth, 1, d), w_k, w_v)


def _mlp_kernel(*refs, final_norm, tf):
    if final_norm:
        x_ref, g_ref, wu_ref, wd_ref, gf_ref, o_ref = refs
    else:
        x_ref, g_ref, wu_ref, wd_ref, o_ref = refs
    x = x_ref[...]
    xn = _rms(x, g_ref[...]).astype(BF16)
    y = x
    for f0 in range(0, wu_ref.shape[1], tf):
        h = jnp.dot(xn, wu_ref[:, f0:f0 + tf], preferred_element_type=F32)
        h = jnp.square(jnp.maximum(h, 0.0)).astype(BF16)
        y = y + jnp.dot(h, wd_ref[f0:f0 + tf, :], preferred_element_type=F32)
    if final_norm:
        y = _rms(y, gf_ref[...])
    o_ref[...] = y


def _mlp(x, gain, w_up, w_down, *, final_gain=None, tm, tf):
    m, d = x.shape
    f = w_up.shape[1]
    assert m % tm == 0 and f % tf == 0
    resident = dict(pipeline_mode=pl.Buffered(1))
    in_specs = [
        pl.BlockSpec((tm, d), lambda i: (i, 0)),
        pl.BlockSpec((1, d), lambda i: (0, 0)),
        pl.BlockSpec((d, f), lambda i: (0, 0), **resident),
        pl.BlockSpec((f, d), lambda i: (0, 0), **resident),
    ]
    args = [x, gain.reshape(1, d), w_up, w_down]
    if final_gain is not None:
        in_specs.append(pl.BlockSpec((1, d), lambda i: (0, 0)))
        args.append(final_gain.reshape(1, d))
    return pl.pallas_call(
        functools.partial(_mlp_kernel, final_norm=final_gain is not None, tf=tf),
        grid=(m // tm,),
        in_specs=in_specs,
        out_specs=pl.BlockSpec((tm, d), lambda i: (i, 0)),
        out_shape=jax.ShapeDtypeStruct((m, d), F32),
        compiler_params=_params("parallel"),
        name="mlp",
    )(*args)


def _xattn_prompt_kernel(x_ref, g_ref, k_ref, v_ref, wq_ref, wo_ref, o_ref, att_ref, *, tm, ts):
    for r0 in range(0, tm, ts):
        sub = slice(r0, r0 + ts)
        x = x_ref[0, sub, :]
        xn = _rms(x, g_ref[...]).astype(BF16)
        q = jnp.dot(xn, wq_ref[...], preferred_element_type=F32) * (XA_HEAD_DIM ** -0.5)
        for h in range(XA_HEADS):
            cols = slice(h * XA_HEAD_DIM, (h + 1) * XA_HEAD_DIM)
            s = _dot_nt(q[:, cols], k_ref[0, :, cols])
            s = s - jnp.max(s, axis=-1, keepdims=True)
            p = jnp.exp(s)
            p = p / jnp.sum(p, axis=-1, keepdims=True)
            att_ref[sub, cols] = _dot(p, v_ref[0, :, cols]).astype(BF16)
        o_ref[0, sub, :] = x + jnp.dot(att_ref[sub, :], wo_ref[...], preferred_element_type=F32)


def _xattn_prompt(x, gain, mem_k, mem_v, layer, w_q, w_o, *, tm, ts):
    b, l, d = x.shape
    assert l % tm == 0 and tm % ts == 0
    kv_spec = pl.BlockSpec((1, MEM_LEN, d), lambda bi, i: (layer, bi, 0))
    w_spec = pl.BlockSpec((d, d), lambda bi, i: (0, 0), pipeline_mode=pl.Buffered(1))
    return pl.pallas_call(
        functools.partial(_xattn_prompt_kernel, tm=tm, ts=ts),
        grid=(b, l // tm),
        in_specs=[
            pl.BlockSpec((1, tm, d), lambda bi, i: (bi, i, 0)),
            pl.BlockSpec((1, d), lambda bi, i: (0, 0)),
            kv_spec, kv_spec, w_spec, w_spec,
        ],
        out_specs=pl.BlockSpec((1, tm, d), lambda bi, i: (bi, i, 0)),
        out_shape=jax.ShapeDtypeStruct((b, l, d), F32),
        scratch_shapes=[pltpu.VMEM((tm, d), BF16)],
        compiler_params=_params("parallel", "parallel"),
        name="xattn_prompt",
    )(x, gain.reshape(1, d), mem_k, mem_v, w_q, w_o)


XA_TILES = XA_HEAD_DIM // LANES


def _xattn_sample_kernel(q_ref, k_ref, v_ref, o_ref, *, bs):
    for b in range(bs):
        q = q_ref[b] * (XA_HEAD_DIM ** -0.5)
        prod = k_ref[0, b] * q[None]
        prod = prod + pltpu.roll(prod, XA_HEADS, 1)
        s = jnp.sum(prod, axis=-1, keepdims=True)
        p = jnp.exp(s - jnp.max(s, axis=0, keepdims=True))
        den = jnp.sum(p, axis=0)
        o_ref[b] = jnp.sum(p * v_ref[0, b], axis=0) / den


def _head_tile_view(x):
    lead = x.shape[:-2]
    n = len(lead)
    x = x.reshape(lead + (XA_HEADS, XA_TILES, LANES))
    return x.transpose(tuple(range(n)) + (n + 1, n, n + 2)).reshape(lead + (XA_TILES * XA_HEADS, LANES))


def _xattn_sample_core(q, cache_k, cache_v, layer, *, bs):
    b = q.shape[0]
    assert b % bs == 0 and XA_TILES == 2 and XA_TILES * XA_HEADS == F32_SUBLANES
    rows = XA_TILES * XA_HEADS
    q_spec = pl.BlockSpec((bs, rows, LANES), lambda i: (i, 0, 0))
    kv_spec = pl.BlockSpec((1, bs, MEM_LEN, rows, LANES), lambda i: (layer, i, 0, 0, 0))
    out = pl.pallas_call(
        functools.partial(_xattn_sample_kernel, bs=bs),
        grid=(b // bs,),
        in_specs=[q_spec, kv_spec, kv_spec],
        out_specs=q_spec,
        out_shape=jax.ShapeDtypeStruct((b, rows, LANES), F32),
        compiler_params=_params("parallel"),
        name="xattn_sample",
    )(_head_tile_view(q.reshape(b, XA_HEADS, XA_HEAD_DIM)), _head_tile_view(cache_k), _head_tile_view(cache_v))
    return out.reshape(b, XA_TILES, XA_HEADS, LANES).transpose(0, 2, 1, 3).reshape(b, XA_HEADS * XA_HEAD_DIM)


HALO = 16


def _pool_windows(z_ref, pos, w_ref, scale_ref, o_ref, tm):
    for gi, win in enumerate(POOL_WINDOWS):
        cols = slice(gi * POOL_GROUP_WIDTH, (gi + 1) * POOL_GROUP_WIDTH)
        tok = z_ref[HALO:HALO + tm, cols]
        acc = tok
        for k in range(1, win):
            acc = acc + z_ref[HALO - k:HALO - k + tm, cols]
        count = jnp.minimum(pos + 1, win).astype(F32)
        dlt = acc / count - tok
        o_ref[0, :, cols] = (_dot(dlt, w_ref[gi]) * scale_ref[:, cols]).astype(o_ref.dtype)


def _pool_sample_kernel(z_ref, buf_ref, w_ref, scale_ref, o_ref):
    for gi, win in enumerate(POOL_WINDOWS):
        cols = slice(gi * POOL_GROUP_WIDTH, (gi + 1) * POOL_GROUP_WIDTH)
        tok = z_ref[:, cols]
        acc = tok
        for k in range(1, win):
            acc = acc + buf_ref[:, POOL_BUF - k, cols]
        count = float(min(PAST_LEN + 1, win))
        dlt = acc / count - tok
        o_ref[:, cols] = (_dot(dlt, w_ref[gi]) * scale_ref[:, cols]).astype(o_ref.dtype)


def _pool_sample(z, buf, w_group, scale):
    b = z.shape[0]
    return pl.pallas_call(
        _pool_sample_kernel,
        grid=(1,),
        in_specs=[
            pl.BlockSpec((b, POOL_WIDTH), lambda i: (0, 0)),
            pl.BlockSpec((b, POOL_BUF, POOL_WIDTH), lambda i: (0, 0, 0)),
            pl.BlockSpec((len(POOL_WINDOWS), POOL_GROUP_WIDTH, POOL_GROUP_WIDTH), lambda i: (0, 0, 0)),
            pl.BlockSpec((1, POOL_WIDTH), lambda i: (0, 0)),
        ],
        out_specs=pl.BlockSpec((b, POOL_WIDTH), lambda i: (0, 0)),
        out_shape=jax.ShapeDtypeStruct((b, POOL_WIDTH), BF16),
        compiler_params=_params("arbitrary"),
        name="pool_sample",
    )(z, buf, w_group, scale.reshape(1, POOL_WIDTH))


def _rwkv_prep_math(z4, s4, p, out_refs, sl):
    zr, zk, zv, zl = z4
    sr, sk, sv, sq = s4
    mu_r, mu_k, mu_v, mu_l, w0, a0, k_k, k_a, r_k, w_la, w_g = p
    r = zr + (sr - zr) * mu_r
    k = zk + (sk - zk) * mu_k
    v = zv + (sv - zv) * mu_v
    lo = zl + (sq - zl) * mu_l
    wal = lo[:, :W_LORA + A_LORA]
    lane = lax.broadcasted_iota(jnp.int32, wal.shape, 1)
    feat = jnp.where(lane < W_LORA, 2.0 * jax.nn.sigmoid(2.0 * wal) - 1.0, wal)
    wa = _dot(feat, w_la)
    g = _dot(jax.nn.sigmoid(lo[:, W_LORA + A_LORA:]), w_g)
    decay = jnp.exp(-DECAY_SCALE * jax.nn.sigmoid(w0 + wa[:, :RWKV_WIDTH]))
    a = jax.nn.sigmoid(a0 + wa[:, RWKV_WIDTH:])
    kk = k * k_k
    k2 = k * (1.0 + (a - 1.0) * k_a)
    ones = _head_ones(RWKV_WIDTH, RWKV_HEAD)
    kk = kk * jnp.minimum(lax.rsqrt(_head_sum(kk * kk, ones)), 1.0 / L2_EPS)
    bonus = _head_sum(r * k2 * r_k, ones) * v
    r_o, k_o, v_o, kap_o, b_o, d_o, g_o, bonus_o = out_refs
    r_o[sl] = r
    k_o[sl] = k2
    v_o[sl] = v
    kap_o[sl] = kk
    b_o[sl] = kk * a
    d_o[sl] = decay
    g_o[sl] = g
    bonus_o[sl] = bonus


def _load_params(refs):
    return tuple(r[...] for r in refs)


AB_IN = POOL_WIDTH + RWKV_IN
LORA_COL = POOL_WIDTH + 3 * RWKV_WIDTH


def _ab_front_kernel(x_ref, g_ref, w_ref, pw_ref, ps_ref, *refs, tm):
    p_refs = refs[:11]
    pool_ref = refs[11]
    out_refs = refs[12:20]
    pool_tail_ref, shift_tail_ref, z_ref = refs[20:23]
    t = pl.program_id(1)

    @pl.when(t == 0)
    def _():
        z_ref[0:HALO, :] = jnp.zeros((HALO, AB_IN), F32)

    @pl.when(t > 0)
    def _():
        z_ref[0:HALO, :] = z_ref[tm:tm + HALO, :]

    xn = _rms(x_ref[0], g_ref[...]).astype(BF16)
    cw = 256
    for c0 in range(0, AB_IN, cw):
        z_ref[HALO:HALO + tm, c0:c0 + cw] = jnp.dot(xn, w_ref[:, c0:c0 + cw], preferred_element_type=F32)

    pos = t * tm + lax.broadcasted_iota(jnp.int32, (tm, 1), 0)
    _pool_windows(z_ref, pos, pw_ref, ps_ref, pool_ref, tm)

    col = lambda c0, width, r0: z_ref[r0:r0 + tm, c0:c0 + width]
    blocks = [(POOL_WIDTH + k * RWKV_WIDTH, RWKV_WIDTH) for k in range(3)] + [(LORA_COL, LORA_IN)]
    z4 = [col(c0, width, HALO) for c0, width in blocks]
    s4 = [col(c0, width, HALO - 1) for c0, width in blocks]
    _rwkv_prep_math(z4, s4, _load_params(p_refs), out_refs, (0,))

    pool_tail_ref[0] = z_ref[tm:tm + HALO, 0:POOL_WIDTH]
    shift_tail_ref[0] = z_ref[tm + HALO - F32_SUBLANES:tm + HALO, POOL_WIDTH:]


def _prep_param_args(mu, w0, a0, k_k, k_a, r_k, w_la, w_g):
    w = RWKV_WIDTH
    row = lambda x: x.reshape(1, -1)
    return [row(mu[:w]), row(mu[w:2 * w]), row(mu[2 * w:3 * w]), row(mu[3 * w:]),
            row(w0), row(a0), row(k_k), row(k_a), row(r_k), w_la, w_g]


def _prep_out_shapes(lead):
    return [jax.ShapeDtypeStruct(lead + (RWKV_WIDTH,), F32) for _ in range(8)]


def _ab_front_prompt(x, gain, w_in, pool_w, pool_scale, params, *, tm):
    b, l, d = x.shape
    assert l % tm == 0 and tm % HALO == 0
    full = lambda a: pl.BlockSpec(a.shape, lambda bi, i: (0,) * a.ndim)
    resident = dict(pipeline_mode=pl.Buffered(1))
    in_specs = [
        pl.BlockSpec((1, tm, d), lambda bi, i: (bi, i, 0)),
        pl.BlockSpec((1, d), lambda bi, i: (0, 0)),
        pl.BlockSpec((d, AB_IN), lambda bi, i: (0, 0), **resident),
        full(pool_w),
        pl.BlockSpec((1, POOL_WIDTH), lambda bi, i: (0, 0)),
    ] + [full(a) for a in params]
    tok_spec = pl.BlockSpec((1, tm, RWKV_WIDTH), lambda bi, i: (bi, i, 0))
    out_specs = [tok_spec] * 9 + [pl.BlockSpec((1, HALO, POOL_WIDTH), lambda bi, i: (bi, 0, 0)),
                                  pl.BlockSpec((1, F32_SUBLANES, RWKV_IN), lambda bi, i: (bi, 0, 0))]
    out_shape = ([jax.ShapeDtypeStruct((b, l, POOL_WIDTH), BF16)] + _prep_out_shapes((b, l))
                 + [jax.ShapeDtypeStruct((b, HALO, POOL_WIDTH), F32), jax.ShapeDtypeStruct((b, F32_SUBLANES, RWKV_IN), F32)])
    return pl.pallas_call(
        functools.partial(_ab_front_kernel, tm=tm),
        grid=(b, l // tm),
        in_specs=in_specs,
        out_specs=out_specs,
        out_shape=out_shape,
        scratch_shapes=[pltpu.VMEM((tm + HALO, AB_IN), F32)],
        compiler_params=_params("parallel", "arbitrary"),
        name="ab_front_prompt",
    )(x, gain.reshape(1, d), w_in, pool_w, pool_scale.reshape(1, POOL_WIDTH), *params)


def _rwkv_prep_sample_kernel(zr_ref, zk_ref, zv_ref, zl_ref, sr_ref, sk_ref, sv_ref, sl_ref, *refs):
    p_refs = refs[:11]
    out_refs = refs[11:19]
    z4 = [zr_ref[...], zk_ref[...], zv_ref[...], zl_ref[...]]
    s4 = [sr_ref[...], sk_ref[...], sv_ref[...], sl_ref[...]]
    _rwkv_prep_math(z4, s4, _load_params(p_refs), out_refs, (Ellipsis,))


def _rwkv_prep_sample(z, shift, params):
    b = z.shape[0]
    pw = POOL_WIDTH // RWKV_WIDTH
    full = lambda a: pl.BlockSpec(a.shape, lambda i: (0,) * a.ndim)
    in_specs = [
        pl.BlockSpec((b, RWKV_WIDTH), lambda i: (0, pw)),
        pl.BlockSpec((b, RWKV_WIDTH), lambda i: (0, pw + 1)),
        pl.BlockSpec((b, RWKV_WIDTH), lambda i: (0, pw + 2)),
        pl.BlockSpec((b, LORA_IN), lambda i: (0, (POOL_WIDTH + 3 * RWKV_WIDTH) // LORA_IN)),
        pl.BlockSpec((b, RWKV_WIDTH), lambda i: (0, 0)),
        pl.BlockSpec((b, RWKV_WIDTH), lambda i: (0, 1)),
        pl.BlockSpec((b, RWKV_WIDTH), lambda i: (0, 2)),
        pl.BlockSpec((b, LORA_IN), lambda i: (0, 3 * RWKV_WIDTH // LORA_IN)),
    ] + [full(a) for a in params]
    out_spec = pl.BlockSpec((b, RWKV_WIDTH), lambda i: (0, 0))
    return pl.pallas_call(
        _rwkv_prep_sample_kernel,
        grid=(1,),
        in_specs=in_specs,
        out_specs=[out_spec] * 8,
        out_shape=_prep_out_shapes((b,)),
        compiler_params=_params("arbitrary"),
        name="rwkv_prep_sample",
    )(z, z, z, z, shift, shift, shift, shift, *params)


def _rwkv_scan_kernel(kap_ref, d_ref, b_ref, k_ref, r_ref, v_ref, s0_ref, y_ref, sout_ref, s_ref, *, tt, nib, vp):
    tb = pl.program_id(1)
    sub = F32_SUBLANES

    @pl.when(tb == 0)
    def _():
        s_ref[...] = s0_ref[0]

    for i in range(nib * sub if vp > tt else 0):
        y_ref[0, i * vp + tt:(i + 1) * vp, :] = jnp.zeros((vp - tt, LANES), F32)

    nparts = 2
    look_ahead = vp > tt

    def add_to(acc, slot, term):
        acc[slot] = term if acc[slot] is None else acc[slot] + term

    def block_sums(acc):
        return tuple(acc[nparts * ib] + acc[nparts * ib + 1] for ib in range(nib))

    def first_sa():
        acc = [None] * (nparts * nib)
        for j in range(RWKV_HEAD):
            kap = kap_ref[0, pl.ds(j * vp, 1), :]
            for ib in range(nib):
                add_to(acc, nparts * ib + j % nparts, s_ref[ib, j] * kap)
        return tuple(-x for x in block_sums(acc))

    def step(t, sa):
        vy_rows = [pl.ds(ib * sub * vp + t, sub, stride=vp) if vp > 1 else pl.ds(ib * sub, sub) for ib in range(nib)]
        v = [v_ref[0, rows, :] for rows in vy_rows]
        acc_y = [None] * (nparts * nib)
        acc_s = [None] * (nparts * nib)
        for j in range(RWKV_HEAD):
            row = pl.ds(j * vp + t, 1)
            dec = d_ref[0, row, :]
            bb = b_ref[0, row, :]
            kk = k_ref[0, row, :]
            rr = r_ref[0, row, :]
            kap_next = kap_ref[0, pl.ds(j * vp + t + 1, 1), :] if look_ahead else None
            for ib in range(nib):
                s = s_ref[ib, j] * dec + sa[ib] * bb + v[ib] * kk
                s_ref[ib, j] = s
                add_to(acc_y, nparts * ib + j % nparts, s * rr)
                if look_ahead:
                    add_to(acc_s, nparts * ib + j % nparts, s * kap_next)
        for ib, y in enumerate(block_sums(acc_y)):
            y_ref[0, vy_rows[ib], :] = y
        return tuple(-x for x in block_sums(acc_s)) if look_ahead else sa

    lax.fori_loop(0, tt, step, first_sa())

    @pl.when(tb == pl.num_programs(1) - 1)
    def _():
        sout_ref[0] = s_ref[...]


def _rwkv_scan(kap, dec, bb, kk, rr, v, s0, *, tt, vp):
    g, nb, rows, _ = kap.shape
    n = RWKV_HEAD
    nib = s0.shape[1]
    assert rows == n * vp and v.shape[2] == nib * F32_SUBLANES * vp and tt <= vp
    op_spec = pl.BlockSpec((None, 1, n * vp, LANES), lambda gi, i: (gi, i, 0, 0))
    v_spec = pl.BlockSpec((None, 1, v.shape[2], LANES), lambda gi, i: (gi, i, 0, 0))
    s_spec = pl.BlockSpec((1, nib, n, F32_SUBLANES, LANES), lambda gi, i: (gi, 0, 0, 0, 0))
    return pl.pallas_call(
        functools.partial(_rwkv_scan_kernel, tt=tt, nib=nib, vp=vp),
        grid=(g, nb),
        in_specs=[op_spec] * 5 + [v_spec, s_spec],
        out_specs=[v_spec, s_spec],
        out_shape=[jax.ShapeDtypeStruct(v.shape, F32), jax.ShapeDtypeStruct(s0.shape, F32)],
        scratch_shapes=[pltpu.VMEM((nib, n, F32_SUBLANES, LANES), F32)],
        compiler_params=_params("parallel", "arbitrary"),
        name="rwkv_scan",
    )(kap, dec, bb, kk, rr, v, s0)


def _rwkv_post_kernel(y_ref, bonus_ref, g_ref, gng_ref, gnb_ref, o_ref):
    y = y_ref[...]
    ones = _head_ones(RWKV_WIDTH, RWKV_HEAD)
    m = _head_sum(y, ones) * (1.0 / RWKV_HEAD)
    c = y - m
    var = _head_sum(c * c, ones) * (1.0 / RWKV_HEAD)
    yn = c * lax.rsqrt(var + GN_EPS) * gng_ref[...] + gnb_ref[...]
    o_ref[...] = ((yn + bonus_ref[...]) * g_ref[...]).astype(o_ref.dtype)


def _rwkv_post(y, bonus, g, gn_g, gn_b, *, tm):
    m = y.shape[0]
    assert m % tm == 0
    spec = pl.BlockSpec((tm, RWKV_WIDTH), lambda i: (i, 0))
    pspec = pl.BlockSpec((1, RWKV_WIDTH), lambda i: (0, 0))
    return pl.pallas_call(
        _rwkv_post_kernel,
        grid=(m // tm,),
        in_specs=[spec, spec, spec, pspec, pspec],
        out_specs=spec,
        out_shape=jax.ShapeDtypeStruct((m, RWKV_WIDTH), BF16),
        compiler_params=_params("parallel"),
        name="rwkv_post",
    )(y, bonus, g, gn_g.reshape(1, -1), gn_b.reshape(1, -1))


def _layernorm(v, g, b):
    m = jnp.mean(v, axis=-1, keepdims=True)
    c = v - m
    var = jnp.mean(c * c, axis=-1, keepdims=True)
    return c * lax.rsqrt(var + LN_EPS) * g + b


def _sgu_prompt_kernel(x_ref, g_ref, wi_ref, lng_ref, lnb_ref, ws_ref, bias_ref, wo_ref, o_ref, zc_ref, vn_ref, gate_ref,
                       *, tm, ts):
    xn = _rms(x_ref[...], g_ref[...]).astype(BF16)
    cw = 256
    for c0 in list(range(SGU_WIDTH, 2 * SGU_WIDTH, cw)) + list(range(0, SGU_WIDTH, cw)):
        acc = jnp.dot(xn, wi_ref[:, c0:c0 + cw], preferred_element_type=F32)
        zc_ref[:, c0:c0 + cw] = (0.5 * acc * (1.0 + lax.erf(acc * 0.7071067811865476))).astype(BF16)
    gw = SGU_WIDTH // SGU_GROUPS
    r = lax.broadcasted_iota(jnp.int32, (CHUNK, CHUNK), 0)
    c = lax.broadcasted_iota(jnp.int32, (CHUNK, CHUNK), 1)
    wms = [jnp.where(r >= c, ws_ref[gi], 0.0).astype(BF16) for gi in range(SGU_GROUPS)]
    for r0 in range(0, tm, ts):
        sub = slice(r0, r0 + ts)
        v = zc_ref[sub, SGU_WIDTH:].astype(F32)
        vn_ref[sub, :] = _layernorm(v, lng_ref[...], lnb_ref[...]).astype(BF16)
        for gi in range(SGU_GROUPS):
            cols = slice(gi * gw, (gi + 1) * gw)
            for ci in range(r0 // CHUNK, (r0 + ts) // CHUNK):
                rows = slice(ci * CHUNK, (ci + 1) * CHUNK)
                sp = jnp.dot(wms[gi], vn_ref[rows, cols], preferred_element_type=F32) + bias_ref[:, cols]
                gate_ref[rows, cols] = (zc_ref[rows, cols].astype(F32) * sp).astype(BF16)
        o_ref[sub, :] = x_ref[sub, :] + jnp.dot(gate_ref[sub, :], wo_ref[...], preferred_element_type=F32)


def _sgu_prompt(x, gain, w_in, ln_g, ln_b, w_s, bias, w_out, *, tm, ts):
    t, d = x.shape
    assert t % tm == 0 and tm % ts == 0 and ts % CHUNK == 0
    row = lambda a: a.reshape(1, -1)
    resident = dict(pipeline_mode=pl.Buffered(1))
    return pl.pallas_call(
        functools.partial(_sgu_prompt_kernel, tm=tm, ts=ts),
        grid=(t // tm,),
        in_specs=[
            pl.BlockSpec((tm, d), lambda i: (i, 0)),
            pl.BlockSpec((1, d), lambda i: (0, 0)),
            pl.BlockSpec((d, 2 * SGU_WIDTH), lambda i: (0, 0), **resident),
            pl.BlockSpec((1, SGU_WIDTH), lambda i: (0, 0)),
            pl.BlockSpec((1, SGU_WIDTH), lambda i: (0, 0)),
            pl.BlockSpec((SGU_GROUPS, CHUNK, CHUNK), lambda i: (0, 0, 0)),
            pl.BlockSpec((CHUNK, SGU_WIDTH), lambda i: (0, 0), **resident),
            pl.BlockSpec((SGU_WIDTH, d), lambda i: (0, 0), **resident),
        ],
        out_specs=pl.BlockSpec((tm, d), lambda i: (i, 0)),
        out_shape=jax.ShapeDtypeStruct((t, d), F32),
        scratch_shapes=[pltpu.VMEM((tm, 2 * SGU_WIDTH), BF16), pltpu.VMEM((tm, SGU_WIDTH), BF16),
                        pltpu.VMEM((tm, SGU_WIDTH), BF16)],
        compiler_params=_params("parallel"),
        name="sgu_prompt",
    )(x, row(gain), w_in, row(ln_g), row(ln_b), w_s, bias, w_out)


def _sgu_sample_kernel(u_ref, v_ref, lng_ref, lnb_ref, coef_ref, bias_ref, gate_ref, vn_ref):
    vn = _layernorm(v_ref[...], lng_ref[...], lnb_ref[...])
    vn_ref[...] = vn
    gate_ref[...] = (u_ref[...] * (vn * coef_ref[...] + bias_ref[...])).astype(gate_ref.dtype)


def _sgu_sample(zc, ln_g, ln_b, coef, bias):
    b = zc.shape[0]
    row = lambda a: a.reshape(1, -1)
    pspec = pl.BlockSpec((1, SGU_WIDTH), lambda i: (0, 0))
    return pl.pallas_call(
        _sgu_sample_kernel,
        grid=(1,),
        in_specs=[pl.BlockSpec((b, SGU_WIDTH), lambda i: (0, 0)), pl.BlockSpec((b, SGU_WIDTH), lambda i: (0, 1)),
                  pspec, pspec, pspec, pspec],
        out_specs=[pl.BlockSpec((b, SGU_WIDTH), lambda i: (0, 0))] * 2,
        out_shape=[jax.ShapeDtypeStruct((b, SGU_WIDTH), BF16), jax.ShapeDtypeStruct((b, SGU_WIDTH), F32)],
        compiler_params=_params("arbitrary"),
        name="sgu_sample",
    )(zc, zc, row(ln_g), row(ln_b), row(coef), row(bias))


SCAN_TT = 64
RELAYOUT_BLOCKS = 4
RELAYOUT_T = RELAYOUT_BLOCKS * SCAN_TT
V_PITCH = 72
HALF = RWKV_HEAD // 2


def _split_heads(z_ref, b, x_ref, c):
    xt = x_ref[b, c * LANES:(c + 1) * LANES, :].T
    for h in range(RWKV_HEADS):
        z_ref[b, h * V_PITCH:h * V_PITCH + RWKV_HEAD, :] = xt[h * RWKV_HEAD:(h + 1) * RWKV_HEAD]


def _head_rows(z_ref, b, n):
    return z_ref[b, pl.ds(n, RWKV_HEADS, stride=V_PITCH), :]


def _store_slab(o_ref, c, n, wt):
    pad = jnp.zeros((V_PITCH - SCAN_TT, LANES), F32)
    for k in range(LANES // SCAN_TT):
        blk = c * (LANES // SCAN_TT) + k
        o_ref[blk, n * V_PITCH:n * V_PITCH + SCAN_TT, :] = wt[k * SCAN_TT:(k + 1) * SCAN_TT]
        o_ref[blk, n * V_PITCH + SCAN_TT:(n + 1) * V_PITCH, :] = pad


def _key_to_lanes_kernel(x_ref, o_ref, z_ref):
    nb = x_ref.shape[0]
    for c in range(RELAYOUT_T // LANES):
        for b in range(nb):
            _split_heads(z_ref, b, x_ref, c)
        for j in range(RWKV_HEAD):
            rows = [_head_rows(z_ref, b, j) for b in range(nb)]
            _store_slab(o_ref, c, j, jnp.concatenate(rows + rows, axis=0).T)


def _value_to_lanes_kernel(x_ref, o_ref, z_ref):
    nb = x_ref.shape[0]
    for c in range(RELAYOUT_T // LANES):
        for b in range(nb):
            _split_heads(z_ref, b, x_ref, c)
        for i in range(HALF):
            rows = [_head_rows(z_ref, b, half * HALF + i) for half in range(2) for b in range(nb)]
            _store_slab(o_ref, c, i, jnp.concatenate(rows, axis=0).T)


def _ab_back_kernel(y_ref, bonus_ref, g_ref, pool_ref, x_ref, gng_ref, gnb_ref, w_ref, o_ref, z_ref, lhs_ref):
    nb = o_ref.shape[0]
    per = LANES // SCAN_TT
    for i in range(HALF):
        w = jnp.concatenate([y_ref[k, i * V_PITCH:i * V_PITCH + SCAN_TT, :] for k in range(per)], axis=0)
        wt = w.T
        for half in range(2):
            for b in range(nb):
                r0 = (half * nb + b) * RWKV_HEADS
                z_ref[b, pl.ds(half * HALF + i, RWKV_HEADS, stride=V_PITCH), :] = wt[r0:r0 + RWKV_HEADS]
    ones = _head_ones(RWKV_WIDTH, RWKV_HEAD)
    for b in range(nb):
        zt = jnp.concatenate([z_ref[b, h * V_PITCH:h * V_PITCH + RWKV_HEAD, :] for h in range(RWKV_HEADS)], axis=0)
        y = zt.T
        m = _head_sum(y, ones) * (1.0 / RWKV_HEAD)
        c = y - m
        var = _head_sum(c * c, ones) * (1.0 / RWKV_HEAD)
        yn = c * lax.rsqrt(var + GN_EPS) * gng_ref[...] + gnb_ref[...]
        rows = slice(b * LANES, (b + 1) * LANES)
        lhs_ref[rows, :POOL_WIDTH] = pool_ref[b]
        lhs_ref[rows, POOL_WIDTH:] = ((yn + bonus_ref[b]) * g_ref[b]).astype(BF16)
    mix = jnp.dot(lhs_ref[...], w_ref[...], preferred_element_type=F32)
    for b in range(nb):
        o_ref[b] = x_ref[b] + mix[b * LANES:(b + 1) * LANES]


def _key_to_lanes(x):
    b, l, w = x.shape
    assert 2 * b * RWKV_HEADS == LANES and l % RELAYOUT_T == 0 and w == RWKV_WIDTH
    out = pl.pallas_call(
        _key_to_lanes_kernel,
        grid=(l // RELAYOUT_T,),
        in_specs=[pl.BlockSpec((b, RELAYOUT_T, w), lambda i: (0, i, 0))],
        out_specs=pl.BlockSpec((RELAYOUT_BLOCKS, RWKV_HEAD * V_PITCH, LANES), lambda i: (i, 0, 0)),
        out_shape=jax.ShapeDtypeStruct((l // SCAN_TT, RWKV_HEAD * V_PITCH, LANES), F32),
        scratch_shapes=[pltpu.VMEM((b, RWKV_HEADS * V_PITCH, LANES), F32)],
        compiler_params=_params("parallel"),
        name="key_to_lanes",
    )(x)
    return out[None]


def _value_to_lanes(v):
    b, l, w = v.shape
    assert 2 * b * RWKV_HEADS == LANES and l % RELAYOUT_T == 0 and w == RWKV_WIDTH
    out = pl.pallas_call(
        _value_to_lanes_kernel,
        grid=(l // RELAYOUT_T,),
        in_specs=[pl.BlockSpec((b, RELAYOUT_T, w), lambda i: (0, i, 0))],
        out_specs=pl.BlockSpec((RELAYOUT_BLOCKS, HALF * V_PITCH, LANES), lambda i: (i, 0, 0)),
        out_shape=jax.ShapeDtypeStruct((l // SCAN_TT, HALF * V_PITCH, LANES), F32),
        scratch_shapes=[pltpu.VMEM((b, RWKV_HEADS * V_PITCH, LANES), F32)],
        compiler_params=_params("parallel"),
        name="value_to_lanes",
    )(v)
    return out[None]


def _ab_back_prompt(y, bonus, g, y_pool, x, gn_g, gn_b, w_out):
    b, l, d = x.shape
    per = LANES // SCAN_TT
    tok = lambda width: pl.BlockSpec((b, LANES, width), lambda i: (0, i, 0))
    pspec = pl.BlockSpec((1, RWKV_WIDTH), lambda i: (0, 0))
    return pl.pallas_call(
        _ab_back_kernel,
        grid=(l // LANES,),
        in_specs=[pl.BlockSpec((per, HALF * V_PITCH, LANES), lambda i: (i, 0, 0)),
                  tok(RWKV_WIDTH), tok(RWKV_WIDTH), tok(POOL_WIDTH), tok(d), pspec, pspec,
                  pl.BlockSpec(w_out.shape, lambda i: (0, 0), pipeline_mode=pl.Buffered(1))],
        out_specs=tok(d),
        out_shape=jax.ShapeDtypeStruct((b, l, d), F32),
        scratch_shapes=[pltpu.VMEM((b, RWKV_HEADS * V_PITCH, LANES), F32),
                        pltpu.VMEM((b * LANES, POOL_WIDTH + RWKV_WIDTH), BF16)],
        compiler_params=_params("parallel"),
        name="ab_back_prompt",
    )(y[0], bonus, g, y_pool, x, gn_g.reshape(1, -1), gn_b.reshape(1, -1), w_out)


def _state_from_lanes_prompt(s, b):
    s = s.reshape(HALF // F32_SUBLANES, RWKV_HEAD, F32_SUBLANES, 2, b, RWKV_HEADS).transpose(4, 5, 3, 0, 2, 1)
    return s.reshape(b, RWKV_HEADS, RWKV_HEAD, RWKV_HEAD)


def _to_lanes_sample(x):
    b = x.shape[0]
    g = b * RWKV_HEADS // LANES
    x = x.reshape(g, LANES // RWKV_HEADS, RWKV_HEADS, RWKV_HEAD).transpose(0, 3, 1, 2)
    return x.reshape(g, 1, RWKV_HEAD, LANES)


def _from_lanes_sample(y):
    g = y.shape[0]
    y = y.reshape(g, RWKV_HEAD, LANES // RWKV_HEADS, RWKV_HEADS).transpose(0, 2, 3, 1)
    return y.reshape(g * LANES // RWKV_HEADS, RWKV_WIDTH)


def _state_to_lanes_sample(s):
    b = s.shape[0]
    g = b * RWKV_HEADS // LANES
    nib = RWKV_HEAD // F32_SUBLANES
    s = s.reshape(g, LANES // RWKV_HEADS, RWKV_HEADS, nib, F32_SUBLANES, RWKV_HEAD).transpose(0, 3, 5, 4, 1, 2)
    return s.reshape(g, nib, RWKV_HEAD, F32_SUBLANES, LANES)


def _state_from_lanes_sample(s):
    g, nib = s.shape[:2]
    s = s.reshape(g, nib, RWKV_HEAD, F32_SUBLANES, LANES // RWKV_HEADS, RWKV_HEADS).transpose(0, 4, 5, 1, 3, 2)
    return s.reshape(g * LANES // RWKV_HEADS, RWKV_HEADS, RWKV_HEAD, RWKV_HEAD)


def kernel(x_prompt, x_sample, mem_prompt, cache_mem_k, cache_mem_v, state_pool, state_shift, state_wkv, norm_mix_g, norm_xa_g, norm_mem_g, norm_ffn_g, norm_final_g, w_in_ab, w_out_ab, pool_w, pool_scale, rwkv_mu, rwkv_w0, rwkv_w2, rwkv_a0, rwkv_a2, rwkv_g2, rwkv_k_k, rwkv_k_a, rwkv_r_k, rwkv_gn_g, rwkv_gn_b, w_in_c, sgu_ln_g, sgu_ln_b, sgu_w_s, sgu_b_s, w_out_c, w_xq, w_xk, w_xv, w_xo, w_ff_up, w_ff_down):
    bp, lp, d = x_prompt.shape
    bs = x_sample.shape[0]
    tp = bp * lp
    bf = lambda w: w.astype(BF16)

    per_layer = lambda w: [bf(w[k]) for k in range(w.shape[0])]
    w_in_ab_b, w_out_ab_b, pool_w_b = per_layer(w_in_ab), per_layer(w_out_ab), per_layer(pool_w)
    w_in_c_b, w_out_c_b = per_layer(w_in_c), per_layer(w_out_c)
    w_xq_b, w_xo_b = per_layer(w_xq), per_layer(w_xo)
    w_xk_b, w_xv_b = bf(w_xk), bf(w_xv)
    w_up_b, w_down_b = per_layer(w_ff_up), per_layer(w_ff_down)

    mem_k_p, mem_v_p, mem_k_out, mem_v_out = _mem_kv(mem_prompt, norm_mem_g, w_xk_b, w_xv_b, bb=4)

    xp = x_prompt.reshape(tp, d)
    xs = x_sample.reshape(bs, d)
    pool_out_p, pool_out_s, shift_out_p, shift_out_s, wkv_out_p, wkv_out_s, sgu_v_s = [], [], [], [], [], [], []

    for l in range(DEPTH):
        j = l // 2
        if l % 2 == 0:
            w_la = jnp.zeros((W_LORA + A_LORA, 2 * RWKV_WIDTH), F32)
            w_la = w_la.at[:W_LORA, :RWKV_WIDTH].set(rwkv_w2[j]).at[W_LORA:, RWKV_WIDTH:].set(rwkv_a2[j])
            params = _prep_param_args(rwkv_mu[j], rwkv_w0[j], rwkv_a0[j], rwkv_k_k[j], rwkv_k_a[j],
                                      rwkv_r_k[j].reshape(-1), bf(w_la), bf(rwkv_g2[j]))
            y_pool, r, k2, v, kap, bb, dec, g, bonus, pool_tail, shift_tail = _ab_front_prompt(
                xp.reshape(bp, lp, d), norm_mix_g[l], w_in_ab_b[j], pool_w_b[j], pool_scale[j], params, tm=512)
            s0 = jnp.zeros((1, HALF // F32_SUBLANES, RWKV_HEAD, F32_SUBLANES, LANES), F32)
            y_l, s_l = _rwkv_scan(_key_to_lanes(kap), _key_to_lanes(dec), _key_to_lanes(bb), _key_to_lanes(k2),
                                  _key_to_lanes(r), _value_to_lanes(v), s0, tt=SCAN_TT, vp=V_PITCH)
            xp = _ab_back_prompt(y_l, bonus, g, y_pool, xp.reshape(bp, lp, d), rwkv_gn_g[j], rwkv_gn_b[j],
                                 w_out_ab_b[j]).reshape(tp, d)
            pool_out_p.append(pool_tail[:, HALO - POOL_BUF:])
            shift_out_p.append(shift_tail[:, F32_SUBLANES - 1])
            wkv_out_p.append(_state_from_lanes_prompt(s_l, bp))
            zs = _linear([xs], [w_in_ab_b[j]], gain=norm_mix_g[l], tm=bs, tn=256, name="ab_in_sample")
            y_pool = _pool_sample(zs, state_pool[j], pool_w_b[j], pool_scale[j])
            r, k2, v, kap, bb, dec, g, bonus = _rwkv_prep_sample(zs, state_shift[j], params)
            y_l, s_l = _rwkv_scan(_to_lanes_sample(kap), _to_lanes_sample(dec), _to_lanes_sample(bb),
                                  _to_lanes_sample(k2), _to_lanes_sample(r), _to_lanes_sample(v),
                                  _state_to_lanes_sample(state_wkv[j]), tt=1, vp=1)
            y_rwkv = _rwkv_post(_from_lanes_sample(y_l), bonus, g, rwkv_gn_g[j], rwkv_gn_b[j], tm=bs)
            xs = _linear([y_pool, y_rwkv], [w_out_ab_b[j][:POOL_WIDTH], w_out_ab_b[j][POOL_WIDTH:]], res=xs,
                         tm=bs, tn=512, name="ab_out_sample")
            pool_out_s.append(jnp.concatenate([state_pool[j][:, 1:], zs[:, None, :POOL_WIDTH]], axis=1))
            shift_out_s.append(zs[:, POOL_WIDTH:])
            wkv_out_s.append(_state_from_lanes_sample(s_l))
        else:
            gw = SGU_WIDTH // SGU_GROUPS
            bias = jnp.repeat(sgu_b_s[j].T, gw, axis=1)
            xp = _sgu_prompt(xp, norm_mix_g[l], w_in_c_b[j], sgu_ln_g[j], sgu_ln_b[j], sgu_w_s[j], bias, w_out_c_b[j],
                             tm=512, ts=128)
            zc = _linear([xs], [w_in_c_b[j]], gain=norm_mix_g[l], act="gelu", tm=bs, tn=512, name="sgu_in_sample")
            gate, vn = _sgu_sample(zc, sgu_ln_g[j], sgu_ln_b[j], jnp.repeat(sgu_w_s[j][:, 0, 0], gw),
                                   jnp.repeat(sgu_b_s[j][:, 0], gw))
            xs = _linear([gate], [w_out_c_b[j]], res=xs, tm=bs, tn=512, name="sgu_out_sample")
            sgu_v_s.append(vn.reshape(bs, 1, SGU_WIDTH))

        xp = _xattn_prompt(xp.reshape(bp, lp, d), norm_xa_g[l], mem_k_p, mem_v_p, l, w_xq_b[l], w_xo_b[l],
                           tm=1024, ts=512).reshape(tp, d)
        q = _linear([xs], [w_xq_b[l]], gain=norm_xa_g[l], tm=bs, tn=512, name="xattn_q_sample")
        att = _xattn_sample_core(q, cache_mem_k, cache_mem_v, l, bs=4)
        xs = _linear([att], [w_xo_b[l]], res=xs, tm=bs, tn=512, name="xattn_o_sample")

        fg = norm_final_g if l == DEPTH - 1 else None
        xp = _mlp(xp, norm_ffn_g[l], w_up_b[l], w_down_b[l], final_gain=fg, tm=1024, tf=1024)
        xs = _mlp(xs, norm_ffn_g[l], w_up_b[l], w_down_b[l], final_gain=fg, tm=bs, tf=512)

    return (xp.reshape(bp, lp, d),
            xs.reshape(bs, 1, d),
            mem_k_out,
            mem_v_out,
            jnp.stack(pool_out_p),
            jnp.stack(pool_out_s),
            jnp.stack(shift_out_p),
            jnp.stack(shift_out_s),
            jnp.stack(wkv_out_p),
            jnp.stack(wkv_out_s),
            jnp.stack(sgu_v_s))
```

```python
import functools

import jax
import jax.numpy as jnp
from jax import lax
from jax.experimental import pallas as pl
from jax.experimental.pallas import tpu as pltpu

DEPTH = 2
PAST_LEN = 16384
POOL_WINDOWS = (2, 4, 8, 16)
POOL_GROUP_WIDTH = 128
POOL_WIDTH = 512
POOL_BUF = 15
RWKV_HEAD = 64
RWKV_WIDTH = 512
RWKV_HEADS = 8
W_LORA = 64
A_LORA = 64
G_LORA = 128
RWKV_IN = 1792
LORA_IN = W_LORA + A_LORA + G_LORA
CHUNK = 128
SGU_WIDTH = 2048
SGU_GROUPS = 4
MEM_LEN = 256
XA_HEADS = 4
XA_HEAD_DIM = 256
RMS_EPS = 1e-5
LN_EPS = 1e-5
GN_EPS = RWKV_HEAD * 1e-5
L2_EPS = 1e-12
DECAY_SCALE = 0.6065306597126334

LANES = 128
F32_SUBLANES = 8
VMEM_LIMIT = 56 * 1024 * 1024

F32 = jnp.float32
BF16 = jnp.bfloat16


def _params(*sem):
    return pltpu.CompilerParams(dimension_semantics=sem, vmem_limit_bytes=VMEM_LIMIT)


def _dot(a, b):
    return jnp.dot(a.astype(BF16), b.astype(BF16), preferred_element_type=F32)


def _dot_nt(a, b):
    return lax.dot_general(a.astype(BF16), b.astype(BF16), (((1,), (1,)), ((), ())),
                           preferred_element_type=F32)


def _rms(x, g):
    return x * lax.rsqrt(jnp.mean(x * x, axis=-1, keepdims=True) + RMS_EPS) * g


def _head_ones(width, head):
    r = lax.broadcasted_iota(jnp.int32, (width, width), 0) // head
    c = lax.broadcasted_iota(jnp.int32, (width, width), 1) // head
    return (r == c).astype(BF16)


def _head_sum(x, ones):
    hi = x.astype(BF16)
    lo = (x - hi.astype(F32)).astype(BF16)
    return (jnp.dot(hi, ones, preferred_element_type=F32)
            + jnp.dot(lo, ones, preferred_element_type=F32))


def _linear_kernel(*refs, nx, has_gain, act, has_res):
    x_refs = refs[:nx]
    pos = nx
    g_ref = refs[pos] if has_gain else None
    pos += int(has_gain)
    w_refs = refs[pos:pos + nx]
    pos += nx
    res_ref = refs[pos] if has_res else None
    pos += int(has_res)
    o_ref = refs[pos]
    if has_gain and len(refs) == pos + 1:
        lhs = [_rms(x_refs[0][...].astype(F32), g_ref[...]).astype(BF16)]
    elif has_gain:
        xn_ref = refs[pos + 1]

        @pl.when(pl.program_id(1) == 0)
        def _():
            xn_ref[...] = _rms(x_refs[0][...].astype(F32), g_ref[...]).astype(BF16)

        lhs = [xn_ref[...]]
    else:
        lhs = [r[...] for r in x_refs]
    tn = o_ref.shape[1]
    cw = next(c for c in (256, 384, 128) if tn % c == 0)
    for c0 in range(0, tn, cw):
        cols = slice(c0, c0 + cw)
        acc = _dot(lhs[0], w_refs[0][:, cols])
        for l, w in zip(lhs[1:], w_refs[1:]):
            acc = acc + _dot(l, w[:, cols])
        if act == "gelu":
            acc = 0.5 * acc * (1.0 + lax.erf(acc * 0.7071067811865476))
        if has_res:
            acc = acc + res_ref[:, cols]
        o_ref[:, cols] = acc.astype(o_ref.dtype)


def _linear(xs, ws, *, gain=None, act=None, res=None, out_dtype=F32, tm, tn, name):
    m = xs[0].shape[0]
    n = ws[0].shape[1]
    assert m % tm == 0 and n % tn == 0, (m, tm, n, tn)
    assert gain is None or len(xs) == 1
    in_specs = [pl.BlockSpec((tm, x.shape[1]), lambda i, j: (i, 0)) for x in xs]
    args = list(xs)
    if gain is not None:
        in_specs.append(pl.BlockSpec((1, gain.shape[-1]), lambda i, j: (0, 0)))
        args.append(gain.reshape(1, -1))
    in_specs += [pl.BlockSpec((w.shape[0], tn), lambda i, j: (0, j)) for w in ws]
    args += list(ws)
    if res is not None:
        in_specs.append(pl.BlockSpec((tm, tn), lambda i, j: (i, j)))
        args.append(res)
    scratch = [pltpu.VMEM((tm, xs[0].shape[1]), BF16)] if gain is not None and n > tn else []
    kern = functools.partial(_linear_kernel, nx=len(xs), has_gain=gain is not None, act=act,
                             has_res=res is not None)
    return pl.pallas_call(
        kern,
        grid=(m // tm, n // tn),
        in_specs=in_specs,
        out_specs=pl.BlockSpec((tm, tn), lambda i, j: (i, j)),
        out_shape=jax.ShapeDtypeStruct((m, n), out_dtype),
        scratch_shapes=scratch,
        compiler_params=_params("parallel", "arbitrary"),
        name=name,
    )(*args)


def _mem_kv_kernel(m_ref, g_ref, wk_ref, wv_ref, k2_ref, v2_ref, k5_ref, v5_ref):
    mn = _rms(m_ref[...], g_ref[0]).astype(BF16)
    nb = k5_ref.shape[1]
    for w_ref, o2_ref, o5_ref in ((wk_ref, k2_ref, k5_ref), (wv_ref, v2_ref, v5_ref)):
        acc = jnp.dot(mn, w_ref[0], preferred_element_type=F32)
        o2_ref[0] = acc
        for bb in range(nb):
            for h in range(XA_HEADS):
                o5_ref[0, bb, :, h, :] = acc[bb * MEM_LEN:(bb + 1) * MEM_LEN, h * XA_HEAD_DIM:(h + 1) * XA_HEAD_DIM]


def _mem_kv(mem, gains, w_k, w_v, *, bb):
    b, m, d = mem.shape
    depth = w_k.shape[0]
    assert b % bb == 0
    tm = bb * m
    w_spec = pl.BlockSpec((1, d, d), lambda l, i: (l, 0, 0))
    o2_spec = pl.BlockSpec((1, tm, d), lambda l, i: (l, i, 0))
    o5_spec = pl.BlockSpec((1, bb, m, XA_HEADS, XA_HEAD_DIM), lambda l, i: (l, i, 0, 0, 0))
    o2_shape = jax.ShapeDtypeStruct((depth, b * m, d), F32)
    o5_shape = jax.ShapeDtypeStruct((depth, b, m, XA_HEADS, XA_HEAD_DIM), F32)
    return pl.pallas_call(
        _mem_kv_kernel,
        grid=(depth, b // bb),
        in_specs=[pl.BlockSpec((tm, d), lambda l, i: (i, 0)), pl.BlockSpec((1, 1, d), lambda l, i: (l, 0, 0)), w_spec, w_spec],
        out_specs=[o2_spec, o2_spec, o5_spec, o5_spec],
        out_shape=[o2_shape, o2_shape, o5_shape, o5_shape],
        compiler_params=_params("parallel", "parallel"),
        name="mem_kv",
    )(mem.reshape(b * m, d), gains.reshape(depth, 1, d), w_k, w_v)


def _mlp_kernel(*refs, final_norm, tf):
    if final_norm:
        x_ref, g_ref, wu_ref, wd_ref, gf_ref, o_ref = refs
    else:
        x_ref, g_ref, wu_ref, wd_ref, o_ref = refs
    x = x_ref[...]
    xn = _rms(x, g_ref[...]).astype(BF16)
    y = x
    for f0 in range(0, wu_ref.shape[1], tf):
        h = jnp.dot(xn, wu_ref[:, f0:f0 + tf], preferred_element_type=F32)
        h = jnp.square(jnp.maximum(h, 0.0)).astype(BF16)
        y = y + jnp.dot(h, wd_ref[f0:f0 + tf, :], preferred_element_type=F32)
    if final_norm:
        y = _rms(y, gf_ref[...])
    o_ref[...] = y


def _mlp(x, gain, w_up, w_down, *, final_gain=None, tm, tf):
    m, d = x.shape
    f = w_up.shape[1]
    assert m % tm == 0 and f % tf == 0
    resident = dict(pipeline_mode=pl.Buffered(1))
    in_specs = [
        pl.BlockSpec((tm, d), lambda i: (i, 0)),
        pl.BlockSpec((1, d), lambda i: (0, 0)),
        pl.BlockSpec((d, f), lambda i: (0, 0), **resident),
        pl.BlockSpec((f, d), lambda i: (0, 0), **resident),
    ]
    args = [x, gain.reshape(1, d), w_up, w_down]
    if final_gain is not None:
        in_specs.append(pl.BlockSpec((1, d), lambda i: (0, 0)))
        args.append(final_gain.reshape(1, d))
    return pl.pallas_call(
        functools.partial(_mlp_kernel, final_norm=final_gain is not None, tf=tf),
        grid=(m // tm,),
        in_specs=in_specs,
        out_specs=pl.BlockSpec((tm, d), lambda i: (i, 0)),
        out_shape=jax.ShapeDtypeStruct((m, d), F32),
        compiler_params=_params("parallel"),
        name="mlp",
    )(*args)


def _xattn_prompt_kernel(x_ref, g_ref, k_ref, v_ref, wq_ref, wo_ref, o_ref, att_ref, *, tm, ts):
    for r0 in range(0, tm, ts):
        sub = slice(r0, r0 + ts)
        x = x_ref[0, sub, :]
        xn = _rms(x, g_ref[...]).astype(BF16)
        q = jnp.dot(xn, wq_ref[...], preferred_element_type=F32) * (XA_HEAD_DIM ** -0.5)
        for h in range(XA_HEADS):
            cols = slice(h * XA_HEAD_DIM, (h + 1) * XA_HEAD_DIM)
            s = _dot_nt(q[:, cols], k_ref[0, :, cols])
            s = s - jnp.max(s, axis=-1, keepdims=True)
            p = jnp.exp(s)
            p = p / jnp.sum(p, axis=-1, keepdims=True)
            att_ref[sub, cols] = _dot(p, v_ref[0, :, cols]).astype(BF16)
        o_ref[0, sub, :] = x + jnp.dot(att_ref[sub, :], wo_ref[...], preferred_element_type=F32)


def _xattn_prompt(x, gain, mem_k, mem_v, layer, w_q, w_o, *, tm, ts):
    b, l, d = x.shape
    assert l % tm == 0 and tm % ts == 0
    kv_spec = pl.BlockSpec((1, MEM_LEN, d), lambda bi, i: (layer, bi, 0))
    w_spec = pl.BlockSpec((d, d), lambda bi, i: (0, 0), pipeline_mode=pl.Buffered(1))
    return pl.pallas_call(
        functools.partial(_xattn_prompt_kernel, tm=tm, ts=ts),
        grid=(b, l // tm),
        in_specs=[
            pl.BlockSpec((1, tm, d), lambda bi, i: (bi, i, 0)),
            pl.BlockSpec((1, d), lambda bi, i: (0, 0)),
            kv_spec, kv_spec, w_spec, w_spec,
        ],
        out_specs=pl.BlockSpec((1, tm, d), lambda bi, i: (bi, i, 0)),
        out_shape=jax.ShapeDtypeStruct((b, l, d), F32),
        scratch_shapes=[pltpu.VMEM((tm, d), BF16)],
        compiler_params=_params("parallel", "parallel"),
        name="xattn_prompt",
    )(x, gain.reshape(1, d), mem_k, mem_v, w_q, w_o)


XA_TILES = XA_HEAD_DIM // LANES


def _xattn_sample_kernel(q_ref, k_ref, v_ref, o_ref, *, bs):
    for b in range(bs):
        q = q_ref[b] * (XA_HEAD_DIM ** -0.5)
        prod = k_ref[0, b] * q[None]
        prod = prod + pltpu.roll(prod, XA_HEADS, 1)
        s = jnp.sum(prod, axis=-1, keepdims=True)
        p = jnp.exp(s - jnp.max(s, axis=0, keepdims=True))
        den = jnp.sum(p, axis=0)
        o_ref[b] = jnp.sum(p * v_ref[0, b], axis=0) / den


def _head_tile_view(x):
    lead = x.shape[:-2]
    n = len(lead)
    x = x.reshape(lead + (XA_HEADS, XA_TILES, LANES))
    return x.transpose(tuple(range(n)) + (n + 1, n, n + 2)).reshape(lead + (XA_TILES * XA_HEADS, LANES))


def _xattn_sample_core(q, cache_k, cache_v, layer, *, bs):
    b = q.shape[0]
    assert b % bs == 0 and XA_TILES == 2 and XA_TILES * XA_HEADS == F32_SUBLANES
    rows = XA_TILES * XA_HEADS
    q_spec = pl.BlockSpec((bs, rows, LANES), lambda i: (i, 0, 0))
    kv_spec = pl.BlockSpec((1, bs, MEM_LEN, rows, LANES), lambda i: (layer, i, 0, 0, 0))
    out = pl.pallas_call(
        functools.partial(_xattn_sample_kernel, bs=bs),
        grid=(b // bs,),
        in_specs=[q_spec, kv_spec, kv_spec],
        out_specs=q_spec,
        out_shape=jax.ShapeDtypeStruct((b, rows, LANES), F32),
        compiler_params=_params("parallel"),
        name="xattn_sample",
    )(_head_tile_view(q.reshape(b, XA_HEADS, XA_HEAD_DIM)), _head_tile_view(cache_k), _head_tile_view(cache_v))
    return out.reshape(b, XA_TILES, XA_HEADS, LANES).transpose(0, 2, 1, 3).reshape(b, XA_HEADS * XA_HEAD_DIM)


HALO = 16


def _pool_windows(z_ref, pos, w_ref, scale_ref, o_ref, tm):
    for gi, win in enumerate(POOL_WINDOWS):
        cols = slice(gi * POOL_GROUP_WIDTH, (gi + 1) * POOL_GROUP_WIDTH)
        tok = z_ref[HALO:HALO + tm, cols]
        acc = tok
        for k in range(1, win):
            acc = acc + z_ref[HALO - k:HALO - k + tm, cols]
        count = jnp.minimum(pos + 1, win).astype(F32)
        dlt = acc / count - tok
        o_ref[0, :, cols] = (_dot(dlt, w_ref[gi]) * scale_ref[:, cols]).astype(o_ref.dtype)


def _pool_sample_kernel(z_ref, buf_ref, w_ref, scale_ref, o_ref):
    for gi, win in enumerate(POOL_WINDOWS):
        cols = slice(gi * POOL_GROUP_WIDTH, (gi + 1) * POOL_GROUP_WIDTH)
        tok = z_ref[:, cols]
        acc = tok
        for k in range(1, win):
            acc = acc + buf_ref[:, POOL_BUF - k, cols]
        count = float(min(PAST_LEN + 1, win))
        dlt = acc / count - tok
        o_ref[:, cols] = (_dot(dlt, w_ref[gi]) * scale_ref[:, cols]).astype(o_ref.dtype)


def _pool_sample(z, buf, w_group, scale):
    b = z.shape[0]
    return pl.pallas_call(
        _pool_sample_kernel,
        grid=(1,),
        in_specs=[
            pl.BlockSpec((b, POOL_WIDTH), lambda i: (0, 0)),
            pl.BlockSpec((b, POOL_BUF, POOL_WIDTH), lambda i: (0, 0, 0)),
            pl.BlockSpec((len(POOL_WINDOWS), POOL_GROUP_WIDTH, POOL_GROUP_WIDTH), lambda i: (0, 0, 0)),
            pl.BlockSpec((1, POOL_WIDTH), lambda i: (0, 0)),
        ],
        out_specs=pl.BlockSpec((b, POOL_WIDTH), lambda i: (0, 0)),
        out_shape=jax.ShapeDtypeStruct((b, POOL_WIDTH), BF16),
        compiler_params=_params("arbitrary"),
        name="pool_sample",
    )(z, buf, w_group, scale.reshape(1, POOL_WIDTH))


def _rwkv_prep_math(z4, s4, p, out_refs, sl):
    zr, zk, zv, zl = z4
    sr, sk, sv, sq = s4
    mu_r, mu_k, mu_v, mu_l, w0, a0, k_k, k_a, r_k, w_la, w_g = p
    r = zr + (sr - zr) * mu_r
    k = zk + (sk - zk) * mu_k
    v = zv + (sv - zv) * mu_v
    lo = zl + (sq - zl) * mu_l
    wal = lo[:, :W_LORA + A_LORA]
    lane = lax.broadcasted_iota(jnp.int32, wal.shape, 1)
    feat = jnp.where(lane < W_LORA, 2.0 * jax.nn.sigmoid(2.0 * wal) - 1.0, wal)
    wa = _dot(feat, w_la)
    g = _dot(jax.nn.sigmoid(lo[:, W_LORA + A_LORA:]), w_g)
    decay = jnp.exp(-DECAY_SCALE * jax.nn.sigmoid(w0 + wa[:, :RWKV_WIDTH]))
    a = jax.nn.sigmoid(a0 + wa[:, RWKV_WIDTH:])
    kk = k * k_k
    k2 = k * (1.0 + (a - 1.0) * k_a)
    ones = _head_ones(RWKV_WIDTH, RWKV_HEAD)
    kk = kk * jnp.minimum(lax.rsqrt(_head_sum(kk * kk, ones)), 1.0 / L2_EPS)
    bonus = _head_sum(r * k2 * r_k, ones) * v
    r_o, k_o, v_o, kap_o, b_o, d_o, g_o, bonus_o = out_refs
    r_o[sl] = r
    k_o[sl] = k2
    v_o[sl] = v
    kap_o[sl] = kk
    b_o[sl] = kk * a
    d_o[sl] = decay
    g_o[sl] = g
    bonus_o[sl] = bonus


def _load_params(refs):
    return tuple(r[...] for r in refs)


AB_IN = POOL_WIDTH + RWKV_IN
LORA_COL = POOL_WIDTH + 3 * RWKV_WIDTH


def _ab_front_kernel(x_ref, g_ref, w_ref, pw_ref, ps_ref, *refs, tm):
    p_refs = refs[:11]
    pool_ref = refs[11]
    out_refs = refs[12:20]
    pool_tail_ref, shift_tail_ref, z_ref = refs[20:23]
    t = pl.program_id(1)

    @pl.when(t == 0)
    def _():
        z_ref[0:HALO, :] = jnp.zeros((HALO, AB_IN), F32)

    @pl.when(t > 0)
    def _():
        z_ref[0:HALO, :] = z_ref[tm:tm + HALO, :]

    xn = _rms(x_ref[0], g_ref[...]).astype(BF16)
    cw = 256
    for c0 in range(0, AB_IN, cw):
        z_ref[HALO:HALO + tm, c0:c0 + cw] = jnp.dot(xn, w_ref[:, c0:c0 + cw], preferred_element_type=F32)

    pos = t * tm + lax.broadcasted_iota(jnp.int32, (tm, 1), 0)
    _pool_windows(z_ref, pos, pw_ref, ps_ref, pool_ref, tm)

    col = lambda c0, width, r0: z_ref[r0:r0 + tm, c0:c0 + width]
    blocks = [(POOL_WIDTH + k * RWKV_WIDTH, RWKV_WIDTH) for k in range(3)] + [(LORA_COL, LORA_IN)]
    z4 = [col(c0, width, HALO) for c0, width in blocks]
    s4 = [col(c0, width, HALO - 1) for c0, width in blocks]
    _rwkv_prep_math(z4, s4, _load_params(p_refs), out_refs, (0,))

    pool_tail_ref[0] = z_ref[tm:tm + HALO, 0:POOL_WIDTH]
    shift_tail_ref[0] = z_ref[tm + HALO - F32_SUBLANES:tm + HALO, POOL_WIDTH:]


def _prep_param_args(mu, w0, a0, k_k, k_a, r_k, w_la, w_g):
    w = RWKV_WIDTH
    row = lambda x: x.reshape(1, -1)
    return [row(mu[:w]), row(mu[w:2 * w]), row(mu[2 * w:3 * w]), row(mu[3 * w:]),
            row(w0), row(a0), row(k_k), row(k_a), row(r_k), w_la, w_g]


def _prep_out_shapes(lead):
    return [jax.ShapeDtypeStruct(lead + (RWKV_WIDTH,), F32) for _ in range(8)]


def _ab_front_prompt(x, gain, w_in, pool_w, pool_scale, params, *, tm):
    b, l, d = x.shape
    assert l % tm == 0 and tm % HALO == 0
    full = lambda a: pl.BlockSpec(a.shape, lambda bi, i: (0,) * a.ndim)
    resident = dict(pipeline_mode=pl.Buffered(1))
    in_specs = [
        pl.BlockSpec((1, tm, d), lambda bi, i: (bi, i, 0)),
        pl.BlockSpec((1, d), lambda bi, i: (0, 0)),
        pl.BlockSpec((d, AB_IN), lambda bi, i: (0, 0), **resident),
        full(pool_w),
        pl.BlockSpec((1, POOL_WIDTH), lambda bi, i: (0, 0)),
    ] + [full(a) for a in params]
    tok_spec = pl.BlockSpec((1, tm, RWKV_WIDTH), lambda bi, i: (bi, i, 0))
    out_specs = [tok_spec] * 9 + [pl.BlockSpec((1, HALO, POOL_WIDTH), lambda bi, i: (bi, 0, 0)),
                                  pl.BlockSpec((1, F32_SUBLANES, RWKV_IN), lambda bi, i: (bi, 0, 0))]
    out_shape = ([jax.ShapeDtypeStruct((b, l, POOL_WIDTH), BF16)] + _prep_out_shapes((b, l))
                 + [jax.ShapeDtypeStruct((b, HALO, POOL_WIDTH), F32), jax.ShapeDtypeStruct((b, F32_SUBLANES, RWKV_IN), F32)])
    return pl.pallas_call(
        functools.partial(_ab_front_kernel, tm=tm),
        grid=(b, l // tm),
        in_specs=in_specs,
        out_specs=out_specs,
        out_shape=out_shape,
        scratch_shapes=[pltpu.VMEM((tm + HALO, AB_IN), F32)],
        compiler_params=_params("parallel", "arbitrary"),
        name="ab_front_prompt",
    )(x, gain.reshape(1, d), w_in, pool_w, pool_scale.reshape(1, POOL_WIDTH), *params)


def _rwkv_prep_sample_kernel(zr_ref, zk_ref, zv_ref, zl_ref, sr_ref, sk_ref, sv_ref, sl_ref, *refs):
    p_refs = refs[:11]
    out_refs = refs[11:19]
    z4 = [zr_ref[...], zk_ref[...], zv_ref[...], zl_ref[...]]
    s4 = [sr_ref[...], sk_ref[...], sv_ref[...], sl_ref[...]]
    _rwkv_prep_math(z4, s4, _load_params(p_refs), out_refs, (Ellipsis,))


def _rwkv_prep_sample(z, shift, params):
    b = z.shape[0]
    pw = POOL_WIDTH // RWKV_WIDTH
    full = lambda a: pl.BlockSpec(a.shape, lambda i: (0,) * a.ndim)
    in_specs = [
        pl.BlockSpec((b, RWKV_WIDTH), lambda i: (0, pw)),
        pl.BlockSpec((b, RWKV_WIDTH), lambda i: (0, pw + 1)),
        pl.BlockSpec((b, RWKV_WIDTH), lambda i: (0, pw + 2)),
        pl.BlockSpec((b, LORA_IN), lambda i: (0, (POOL_WIDTH + 3 * RWKV_WIDTH) // LORA_IN)),
        pl.BlockSpec((b, RWKV_WIDTH), lambda i: (0, 0)),
        pl.BlockSpec((b, RWKV_WIDTH), lambda i: (0, 1)),
        pl.BlockSpec((b, RWKV_WIDTH), lambda i: (0, 2)),
        pl.BlockSpec((b, LORA_IN), lambda i: (0, 3 * RWKV_WIDTH // LORA_IN)),
    ] + [full(a) for a in params]
    out_spec = pl.BlockSpec((b, RWKV_WIDTH), lambda i: (0, 0))
    return pl.pallas_call(
        _rwkv_prep_sample_kernel,
        grid=(1,),
        in_specs=in_specs,
        out_specs=[out_spec] * 8,
        out_shape=_prep_out_shapes((b,)),
        compiler_params=_params("arbitrary"),
        name="rwkv_prep_sample",
    )(z, z, z, z, shift, shift, shift, shift, *params)


def _rwkv_scan_kernel(kap_ref, d_ref, b_ref, k_ref, r_ref, v_ref, s0_ref, y_ref, sout_ref, s_ref, *, tt, nib, vp):
    tb = pl.program_id(1)
    sub = F32_SUBLANES

    @pl.when(tb == 0)
    def _():
        s_ref[...] = s0_ref[0]

    for i in range(nib * sub if vp > tt else 0):
        y_ref[0, i * vp + tt:(i + 1) * vp, :] = jnp.zeros((vp - tt, LANES), F32)

    nparts = 2
    look_ahead = vp > tt

    def add_to(acc, slot, term):
        acc[slot] = term if acc[slot] is None else acc[slot] + term

    def block_sums(acc):
        return tuple(acc[nparts * ib] + acc[nparts * ib + 1] for ib in range(nib))

    def first_sa():
        acc = [None] * (nparts * nib)
        for j in range(RWKV_HEAD):
            kap = kap_ref[0, pl.ds(j * vp, 1), :]
            for ib in range(nib):
                add_to(acc, nparts * ib + j % nparts, s_ref[ib, j] * kap)
        return tuple(-x for x in block_sums(acc))

    def step(t, sa):
        vy_rows = [pl.ds(ib * sub * vp + t, sub, stride=vp) if vp > 1 else pl.ds(ib * sub, sub) for ib in range(nib)]
        v = [v_ref[0, rows, :] for rows in vy_rows]
        acc_y = [None] * (nparts * nib)
        acc_s = [None] * (nparts * nib)
        for j in range(RWKV_HEAD):
            row = pl.ds(j * vp + t, 1)
            dec = d_ref[0, row, :]
            bb = b_ref[0, row, :]
            kk = k_ref[0, row, :]
            rr = r_ref[0, row, :]
            kap_next = kap_ref[0, pl.ds(j * vp + t + 1, 1), :] if look_ahead else None
            for ib in range(nib):
                s = s_ref[ib, j] * dec + sa[ib] * bb + v[ib] * kk
                s_ref[ib, j] = s
                add_to(acc_y, nparts * ib + j % nparts, s * rr)
                if look_ahead:
                    add_to(acc_s, nparts * ib + j % nparts, s * kap_next)
        for ib, y in enumerate(block_sums(acc_y)):
            y_ref[0, vy_rows[ib], :] = y
        return tuple(-x for x in block_sums(acc_s)) if look_ahead else sa

    lax.fori_loop(0, tt, step, first_sa())

    @pl.when(tb == pl.num_programs(1) - 1)
    def _():
        sout_ref[0] = s_ref[...]


def _rwkv_scan(kap, dec, bb, kk, rr, v, s0, *, tt, vp):
    g, nb, rows, _ = kap.shape
    n = RWKV_HEAD
    nib = s0.shape[1]
    assert rows == n * vp and v.shape[2] == nib * F32_SUBLANES * vp and tt <= vp
    op_spec = pl.BlockSpec((None, 1, n * vp, LANES), lambda gi, i: (gi, i, 0, 0))
    v_spec = pl.BlockSpec((None, 1, v.shape[2], LANES), lambda gi, i: (gi, i, 0, 0))
    s_spec = pl.BlockSpec((1, nib, n, F32_SUBLANES, LANES), lambda gi, i: (gi, 0, 0, 0, 0))
    return pl.pallas_call(
        functools.partial(_rwkv_scan_kernel, tt=tt, nib=nib, vp=vp),
        grid=(g, nb),
        in_specs=[op_spec] * 5 + [v_spec, s_spec],
        out_specs=[v_spec, s_spec],
        out_shape=[jax.ShapeDtypeStruct(v.shape, F32), jax.ShapeDtypeStruct(s0.shape, F32)],
        scratch_shapes=[pltpu.VMEM((nib, n, F32_SUBLANES, LANES), F32)],
        compiler_params=_params("parallel", "arbitrary"),
        name="rwkv_scan",
    )(kap, dec, bb, kk, rr, v, s0)


def _rwkv_post_kernel(y_ref, bonus_ref, g_ref, gng_ref, gnb_ref, o_ref):
    y = y_ref[...]
    ones = _head_ones(RWKV_WIDTH, RWKV_HEAD)
    m = _head_sum(y, ones) * (1.0 / RWKV_HEAD)
    c = y - m
    var = _head_sum(c * c, ones) * (1.0 / RWKV_HEAD)
    yn = c * lax.rsqrt(var + GN_EPS) * gng_ref[...] + gnb_ref[...]
    o_ref[...] = ((yn + bonus_ref[...]) * g_ref[...]).astype(o_ref.dtype)


def _rwkv_post(y, bonus, g, gn_g, gn_b, *, tm):
    m = y.shape[0]
    assert m % tm == 0
    spec = pl.BlockSpec((tm, RWKV_WIDTH), lambda i: (i, 0))
    pspec = pl.BlockSpec((1, RWKV_WIDTH), lambda i: (0, 0))
    return pl.pallas_call(
        _rwkv_post_kernel,
        grid=(m // tm,),
        in_specs=[spec, spec, spec, pspec, pspec],
        out_specs=spec,
        out_shape=jax.ShapeDtypeStruct((m, RWKV_WIDTH), BF16),
        compiler_params=_params("parallel"),
        name="rwkv_post",
    )(y, bonus, g, gn_g.reshape(1, -1), gn_b.reshape(1, -1))


def _layernorm(v, g, b):
    m = jnp.mean(v, axis=-1, keepdims=True)
    c = v - m
    var = jnp.mean(c * c, axis=-1, keepdims=True)
    return c * lax.rsqrt(var + LN_EPS) * g + b


def _sgu_prompt_kernel(x_ref, g_ref, wi_ref, lng_ref, lnb_ref, ws_ref, bias_ref, wo_ref, o_ref, zc_ref, vn_ref, gate_ref,
                       *, tm, ts):
    xn = _rms(x_ref[...], g_ref[...]).astype(BF16)
    cw = 256
    for c0 in list(range(SGU_WIDTH, 2 * SGU_WIDTH, cw)) + list(range(0, SGU_WIDTH, cw)):
        acc = jnp.dot(xn, wi_ref[:, c0:c0 + cw], preferred_element_type=F32)
        zc_ref[:, c0:c0 + cw] = (0.5 * acc * (1.0 + lax.erf(acc * 0.7071067811865476))).astype(BF16)
    gw = SGU_WIDTH // SGU_GROUPS
    r = lax.broadcasted_iota(jnp.int32, (CHUNK, CHUNK), 0)
    c = lax.broadcasted_iota(jnp.int32, (CHUNK, CHUNK), 1)
    wms = [jnp.where(r >= c, ws_ref[gi], 0.0).astype(BF16) for gi in range(SGU_GROUPS)]
    for r0 in range(0, tm, ts):
        sub = slice(r0, r0 + ts)
        v = zc_ref[sub, SGU_WIDTH:].astype(F32)
        vn_ref[sub, :] = _layernorm(v, lng_ref[...], lnb_ref[...]).astype(BF16)
        for gi in range(SGU_GROUPS):
            cols = slice(gi * gw, (gi + 1) * gw)
            for ci in range(r0 // CHUNK, (r0 + ts) // CHUNK):
                rows = slice(ci * CHUNK, (ci + 1) * CHUNK)
                sp = jnp.dot(wms[gi], vn_ref[rows, cols], preferred_element_type=F32) + bias_ref[:, cols]
                gate_ref[rows, cols] = (zc_ref[rows, cols].astype(F32) * sp).astype(BF16)
        o_ref[sub, :] = x_ref[sub, :] + jnp.dot(gate_ref[sub, :], wo_ref[...], preferred_element_type=F32)


def _sgu_prompt(x, gain, w_in, ln_g, ln_b, w_s, bias, w_out, *, tm, ts):
    t, d = x.shape
    assert t % tm == 0 and tm % ts == 0 and ts % CHUNK == 0
    row = lambda a: a.reshape(1, -1)
    resident = dict(pipeline_mode=pl.Buffered(1))
    return pl.pallas_call(
        functools.partial(_sgu_prompt_kernel, tm=tm, ts=ts),
        grid=(t // tm,),
        in_specs=[
            pl.BlockSpec((tm, d), lambda i: (i, 0)),
            pl.BlockSpec((1, d), lambda i: (0, 0)),
            pl.BlockSpec((d, 2 * SGU_WIDTH), lambda i: (0, 0), **resident),
            pl.BlockSpec((1, SGU_WIDTH), lambda i: (0, 0)),
            pl.BlockSpec((1, SGU_WIDTH), lambda i: (0, 0)),
            pl.BlockSpec((SGU_GROUPS, CHUNK, CHUNK), lambda i: (0, 0, 0)),
            pl.BlockSpec((CHUNK, SGU_WIDTH), lambda i: (0, 0), **resident),
            pl.BlockSpec((SGU_WIDTH, d), lambda i: (0, 0), **resident),
        ],
        out_specs=pl.BlockSpec((tm, d), lambda i: (i, 0)),
        out_shape=jax.ShapeDtypeStruct((t, d), F32),
        scratch_shapes=[pltpu.VMEM((tm, 2 * SGU_WIDTH), BF16), pltpu.VMEM((tm, SGU_WIDTH), BF16),
                        pltpu.VMEM((tm, SGU_WIDTH), BF16)],
        compiler_params=_params("parallel"),
        name="sgu_prompt",
    )(x, row(gain), w_in, row(ln_g), row(ln_b), w_s, bias, w_out)


def _sgu_sample_kernel(u_ref, v_ref, lng_ref, lnb_ref, coef_ref, bias_ref, gate_ref, vn_ref):
    vn = _layernorm(v_ref[...], lng_ref[...], lnb_ref[...])
    vn_ref[...] = vn
    gate_ref[...] = (u_ref[...] * (vn * coef_ref[...] + bias_ref[...])).astype(gate_ref.dtype)


def _sgu_sample(zc, ln_g, ln_b, coef, bias):
    b = zc.shape[0]
    row = lambda a: a.reshape(1, -1)
    pspec = pl.BlockSpec((1, SGU_WIDTH), lambda i: (0, 0))
    return pl.pallas_call(
        _sgu_sample_kernel,
        grid=(1,),
        in_specs=[pl.BlockSpec((b, SGU_WIDTH), lambda i: (0, 0)), pl.BlockSpec((b, SGU_WIDTH), lambda i: (0, 1)),
                  pspec, pspec, pspec, pspec],
        out_specs=[pl.BlockSpec((b, SGU_WIDTH), lambda i: (0, 0))] * 2,
        out_shape=[jax.ShapeDtypeStruct((b, SGU_WIDTH), BF16), jax.ShapeDtypeStruct((b, SGU_WIDTH), F32)],
        compiler_params=_params("arbitrary"),
        name="sgu_sample",
    )(zc, zc, row(ln_g), row(ln_b), row(coef), row(bias))


SCAN_TT = 64
RELAYOUT_BLOCKS = 4
RELAYOUT_T = RELAYOUT_BLOCKS * SCAN_TT
V_PITCH = 72
HALF = RWKV_HEAD // 2


def _split_heads(z_ref, b, x_ref, c):
    xt = x_ref[b, c * LANES:(c + 1) * LANES, :].T
    for h in range(RWKV_HEADS):
        z_ref[b, h * V_PITCH:h * V_PITCH + RWKV_HEAD, :] = xt[h * RWKV_HEAD:(h + 1) * RWKV_HEAD]


def _head_rows(z_ref, b, n):
    return z_ref[b, pl.ds(n, RWKV_HEADS, stride=V_PITCH), :]


def _store_slab(o_ref, c, n, wt):
    pad = jnp.zeros((V_PITCH - SCAN_TT, LANES), F32)
    for k in range(LANES // SCAN_TT):
        blk = c * (LANES // SCAN_TT) + k
        o_ref[blk, n * V_PITCH:n * V_PITCH + SCAN_TT, :] = wt[k * SCAN_TT:(k + 1) * SCAN_TT]
        o_ref[blk, n * V_PITCH + SCAN_TT:(n + 1) * V_PITCH, :] = pad


def _key_to_lanes_kernel(x_ref, o_ref, z_ref):
    nb = x_ref.shape[0]
    for c in range(RELAYOUT_T // LANES):
        for b in range(nb):
            _split_heads(z_ref, b, x_ref, c)
        for j in range(RWKV_HEAD):
            rows = [_head_rows(z_ref, b, j) for b in range(nb)]
            _store_slab(o_ref, c, j, jnp.concatenate(rows + rows, axis=0).T)


def _value_to_lanes_kernel(x_ref, o_ref, z_ref):
    nb = x_ref.shape[0]
    for c in range(RELAYOUT_T // LANES):
        for b in range(nb):
            _split_heads(z_ref, b, x_ref, c)
        for i in range(HALF):
            rows = [_head_rows(z_ref, b, half * HALF + i) for half in range(2) for b in range(nb)]
            _store_slab(o_ref, c, i, jnp.concatenate(rows, axis=0).T)


def _ab_back_kernel(y_ref, bonus_ref, g_ref, pool_ref, x_ref, gng_ref, gnb_ref, w_ref, o_ref, z_ref, lhs_ref):
    nb = o_ref.shape[0]
    per = LANES // SCAN_TT
    for i in range(HALF):
        w = jnp.concatenate([y_ref[k, i * V_PITCH:i * V_PITCH + SCAN_TT, :] for k in range(per)], axis=0)
        wt = w.T
        for half in range(2):
            for b in range(nb):
                r0 = (half * nb + b) * RWKV_HEADS
                z_ref[b, pl.ds(half * HALF + i, RWKV_HEADS, stride=V_PITCH), :] = wt[r0:r0 + RWKV_HEADS]
    ones = _head_ones(RWKV_WIDTH, RWKV_HEAD)
    for b in range(nb):
        zt = jnp.concatenate([z_ref[b, h * V_PITCH:h * V_PITCH + RWKV_HEAD, :] for h in range(RWKV_HEADS)], axis=0)
        y = zt.T
        m = _head_sum(y, ones) * (1.0 / RWKV_HEAD)
        c = y - m
        var = _head_sum(c * c, ones) * (1.0 / RWKV_HEAD)
        yn = c * lax.rsqrt(var + GN_EPS) * gng_ref[...] + gnb_ref[...]
        rows = slice(b * LANES, (b + 1) * LANES)
        lhs_ref[rows, :POOL_WIDTH] = pool_ref[b]
        lhs_ref[rows, POOL_WIDTH:] = ((yn + bonus_ref[b]) * g_ref[b]).astype(BF16)
    mix = jnp.dot(lhs_ref[...], w_ref[...], preferred_element_type=F32)
    for b in range(nb):
        o_ref[b] = x_ref[b] + mix[b * LANES:(b + 1) * LANES]


def _key_to_lanes(x):
    b, l, w = x.shape
    assert 2 * b * RWKV_HEADS == LANES and l % RELAYOUT_T == 0 and w == RWKV_WIDTH
    out = pl.pallas_call(
        _key_to_lanes_kernel,
        grid=(l // RELAYOUT_T,),
        in_specs=[pl.BlockSpec((b, RELAYOUT_T, w), lambda i: (0, i, 0))],
        out_specs=pl.BlockSpec((RELAYOUT_BLOCKS, RWKV_HEAD * V_PITCH, LANES), lambda i: (i, 0, 0)),
        out_shape=jax.ShapeDtypeStruct((l // SCAN_TT, RWKV_HEAD * V_PITCH, LANES), F32),
        scratch_shapes=[pltpu.VMEM((b, RWKV_HEADS * V_PITCH, LANES), F32)],
        compiler_params=_params("parallel"),
        name="key_to_lanes",
    )(x)
    return out[None]


def _value_to_lanes(v):
    b, l, w = v.shape
    assert 2 * b * RWKV_HEADS == LANES and l % RELAYOUT_T == 0 and w == RWKV_WIDTH
    out = pl.pallas_call(
        _value_to_lanes_kernel,
        grid=(l // RELAYOUT_T,),
        in_specs=[pl.BlockSpec((b, RELAYOUT_T, w), lambda i: (0, i, 0))],
        out_specs=pl.BlockSpec((RELAYOUT_BLOCKS, HALF * V_PITCH, LANES), lambda i: (i, 0, 0)),
        out_shape=jax.ShapeDtypeStruct((l // SCAN_TT, HALF * V_PITCH, LANES), F32),
        scratch_shapes=[pltpu.VMEM((b, RWKV_HEADS * V_PITCH, LANES), F32)],
        compiler_params=_params("parallel"),
        name="value_to_lanes",
    )(v)
    return out[None]


def _ab_back_prompt(y, bonus, g, y_pool, x, gn_g, gn_b, w_out):
    b, l, d = x.shape
    per = LANES // SCAN_TT
    tok = lambda width: pl.BlockSpec((b, LANES, width), lambda i: (0, i, 0))
    pspec = pl.BlockSpec((1, RWKV_WIDTH), lambda i: (0, 0))
    return pl.pallas_call(
        _ab_back_kernel,
        grid=(l // LANES,),
        in_specs=[pl.BlockSpec((per, HALF * V_PITCH, LANES), lambda i: (i, 0, 0)),
                  tok(RWKV_WIDTH), tok(RWKV_WIDTH), tok(POOL_WIDTH), tok(d), pspec, pspec,
                  pl.BlockSpec(w_out.shape, lambda i: (0, 0), pipeline_mode=pl.Buffered(1))],
        out_specs=tok(d),
        out_shape=jax.ShapeDtypeStruct((b, l, d), F32),
        scratch_shapes=[pltpu.VMEM((b, RWKV_HEADS * V_PITCH, LANES), F32),
                        pltpu.VMEM((b * LANES, POOL_WIDTH + RWKV_WIDTH), BF16)],
        compiler_params=_params("parallel"),
        name="ab_back_prompt",
    )(y[0], bonus, g, y_pool, x, gn_g.reshape(1, -1), gn_b.reshape(1, -1), w_out)


def _state_from_lanes_prompt(s, b):
    s = s.reshape(HALF // F32_SUBLANES, RWKV_HEAD, F32_SUBLANES, 2, b, RWKV_HEADS).transpose(4, 5, 3, 0, 2, 1)
    return s.reshape(b, RWKV_HEADS, RWKV_HEAD, RWKV_HEAD)


def _to_lanes_sample(x):
    b = x.shape[0]
    g = b * RWKV_HEADS // LANES
    x = x.reshape(g, LANES // RWKV_HEADS, RWKV_HEADS, RWKV_HEAD).transpose(0, 3, 1, 2)
    return x.reshape(g, 1, RWKV_HEAD, LANES)


def _from_lanes_sample(y):
    g = y.shape[0]
    y = y.reshape(g, RWKV_HEAD, LANES // RWKV_HEADS, RWKV_HEADS).transpose(0, 2, 3, 1)
    return y.reshape(g * LANES // RWKV_HEADS, RWKV_WIDTH)


def _state_to_lanes_sample(s):
    b = s.shape[0]
    g = b * RWKV_HEADS // LANES
    nib = RWKV_HEAD // F32_SUBLANES
    s = s.reshape(g, LANES // RWKV_HEADS, RWKV_HEADS, nib, F32_SUBLANES, RWKV_HEAD).transpose(0, 3, 5, 4, 1, 2)
    return s.reshape(g, nib, RWKV_HEAD, F32_SUBLANES, LANES)


def _state_from_lanes_sample(s):
    g, nib = s.shape[:2]
    s = s.reshape(g, nib, RWKV_HEAD, F32_SUBLANES, LANES // RWKV_HEADS, RWKV_HEADS).transpose(0, 4, 5, 1, 3, 2)
    return s.reshape(g * LANES // RWKV_HEADS, RWKV_HEADS, RWKV_HEAD, RWKV_HEAD)


def kernel(x_prompt, x_sample, mem_prompt, cache_mem_k, cache_mem_v, state_pool, state_shift, state_wkv, norm_mix_g, norm_xa_g, norm_mem_g, norm_ffn_g, norm_final_g, w_in_ab, w_out_ab, pool_w, pool_scale, rwkv_mu, rwkv_w0, rwkv_w2, rwkv_a0, rwkv_a2, rwkv_g2, rwkv_k_k, rwkv_k_a, rwkv_r_k, rwkv_gn_g, rwkv_gn_b, w_in_c, sgu_ln_g, sgu_ln_b, sgu_w_s, sgu_b_s, w_out_c, w_xq, w_xk, w_xv, w_xo, w_ff_up, w_ff_down):
    bp, lp, d = x_prompt.shape
    bs = x_sample.shape[0]
    tp = bp * lp
    bf = lambda w: w.astype(BF16)

    per_layer = lambda w: [bf(w[k]) for k in range(w.shape[0])]
    w_in_ab_b, w_out_ab_b, pool_w_b = per_layer(w_in_ab), per_layer(w_out_ab), per_layer(pool_w)
    w_in_c_b, w_out_c_b = per_layer(w_in_c), per_layer(w_out_c)
    w_xq_b, w_xo_b = per_layer(w_xq), per_layer(w_xo)
    w_xk_b, w_xv_b = bf(w_xk), bf(w_xv)
    w_up_b, w_down_b = per_layer(w_ff_up), per_layer(w_ff_down)

    mem_k_p, mem_v_p, mem_k_out, mem_v_out = _mem_kv(mem_prompt, norm_mem_g, w_xk_b, w_xv_b, bb=4)

    xp = x_prompt.reshape(tp, d)
    xs = x_sample.reshape(bs, d)
    pool_out_p, pool_out_s, shift_out_p, shift_out_s, wkv_out_p, wkv_out_s, sgu_v_s = [], [], [], [], [], [], []

    for l in range(DEPTH):
        j = l // 2
        if l % 2 == 0:
            w_la = jnp.zeros((W_LORA + A_LORA, 2 * RWKV_WIDTH), F32)
            w_la = w_la.at[:W_LORA, :RWKV_WIDTH].set(rwkv_w2[j]).at[W_LORA:, RWKV_WIDTH:].set(rwkv_a2[j])
            params = _prep_param_args(rwkv_mu[j], rwkv_w0[j], rwkv_a0[j], rwkv_k_k[j], rwkv_k_a[j],
                                      rwkv_r_k[j].reshape(-1), bf(w_la), bf(rwkv_g2[j]))
            y_pool, r, k2, v, kap, bb, dec, g, bonus, pool_tail, shift_tail = _ab_front_prompt(
                xp.reshape(bp, lp, d), norm_mix_g[l], w_in_ab_b[j], pool_w_b[j], pool_scale[j], params, tm=512)
            s0 = jnp.zeros((1, HALF // F32_SUBLANES, RWKV_HEAD, F32_SUBLANES, LANES), F32)
            y_l, s_l = _rwkv_scan(_key_to_lanes(kap), _key_to_lanes(dec), _key_to_lanes(bb), _key_to_lanes(k2),
                                  _key_to_lanes(r), _value_to_lanes(v), s0, tt=SCAN_TT, vp=V_PITCH)
            xp = _ab_back_prompt(y_l, bonus, g, y_pool, xp.reshape(bp, lp, d), rwkv_gn_g[j], rwkv_gn_b[j],
                                 w_out_ab_b[j]).reshape(tp, d)
            pool_out_p.append(pool_tail[:, HALO - POOL_BUF:])
            shift_out_p.append(shift_tail[:, F32_SUBLANES - 1])
            wkv_out_p.append(_state_from_lanes_prompt(s_l, bp))
            zs = _linear([xs], [w_in_ab_b[j]], gain=norm_mix_g[l], tm=bs, tn=256, name="ab_in_sample")
            y_pool = _pool_sample(zs, state_pool[j], pool_w_b[j], pool_scale[j])
            r, k2, v, kap, bb, dec, g, bonus = _rwkv_prep_sample(zs, state_shift[j], params)
            y_l, s_l = _rwkv_scan(_to_lanes_sample(kap), _to_lanes_sample(dec), _to_lanes_sample(bb),
                                  _to_lanes_sample(k2), _to_lanes_sample(r), _to_lanes_sample(v),
                                  _state_to_lanes_sample(state_wkv[j]), tt=1, vp=1)
            y_rwkv = _rwkv_post(_from_lanes_sample(y_l), bonus, g, rwkv_gn_g[j], rwkv_gn_b[j], tm=bs)
            xs = _linear([y_pool, y_rwkv], [w_out_ab_b[j][:POOL_WIDTH], w_out_ab_b[j][POOL_WIDTH:]], res=xs,
                         tm=bs, tn=512, name="ab_out_sample")
            pool_out_s.append(jnp.concatenate([state_pool[j][:, 1:], zs[:, None, :POOL_WIDTH]], axis=1))
            shift_out_s.append(zs[:, POOL_WIDTH:])
            wkv_out_s.append(_state_from_lanes_sample(s_l))
        else:
            gw = SGU_WIDTH // SGU_GROUPS
            bias = jnp.repeat(sgu_b_s[j].T, gw, axis=1)
            xp = _sgu_prompt(xp, norm_mix_g[l], w_in_c_b[j], sgu_ln_g[j], sgu_ln_b[j], sgu_w_s[j], bias, w_out_c_b[j],
                             tm=512, ts=128)
            zc = _linear([xs], [w_in_c_b[j]], gain=norm_mix_g[l], act="gelu", tm=bs, tn=512, name="sgu_in_sample")
            gate, vn = _sgu_sample(zc, sgu_ln_g[j], sgu_ln_b[j], jnp.repeat(sgu_w_s[j][:, 0, 0], gw),
                                   jnp.repeat(sgu_b_s[j][:, 0], gw))
            xs = _linear([gate], [w_out_c_b[j]], res=xs, tm=bs, tn=512, name="sgu_out_sample")
            sgu_v_s.append(vn.reshape(bs, 1, SGU_WIDTH))

        xp = _xattn_prompt(xp.reshape(bp, lp, d), norm_xa_g[l], mem_k_p, mem_v_p, l, w_xq_b[l], w_xo_b[l],
                           tm=1024, ts=512).reshape(tp, d)
        q = _linear([xs], [w_xq_b[l]], gain=norm_xa_g[l], tm=bs, tn=512, name="xattn_q_sample")
        att = _xattn_sample_core(q, cache_mem_k, cache_mem_v, l, bs=8)
        xs = _linear([att], [w_xo_b[l]], res=xs, tm=bs, tn=512, name="xattn_o_sample")

        fg = norm_final_g if l == DEPTH - 1 else None
        xp = _mlp(xp, norm_ffn_g[l], w_up_b[l], w_down_b[l], final_gain=fg, tm=1024, tf=1024)
        xs = _mlp(xs, norm_ffn_g[l], w_up_b[l], w_down_b[l], final_gain=fg, tm=bs, tf=512)

    return (xp.reshape(bp, lp, d),
            xs.reshape(bs, 1, d),
            mem_k_out,
            mem_v_out,
            jnp.stack(pool_out_p),
            jnp.stack(pool_out_s),
            jnp.stack(shift_out_p),
            jnp.stack(shift_out_s),
            jnp.stack(wkv_out_p),
            jnp.stack(wkv_out_s),
            jnp.stack(sgu_v_s))
```

```python
import functools

import jax
import jax.numpy as jnp
from jax import lax
from jax.experimental import pallas as pl
from jax.experimental.pallas import tpu as pltpu

DEPTH = 2
PAST_LEN = 16384
POOL_WINDOWS = (2, 4, 8, 16)
POOL_GROUP_WIDTH = 128
POOL_WIDTH = 512
POOL_BUF = 15
RWKV_HEAD = 64
RWKV_WIDTH = 512
RWKV_HEADS = 8
W_LORA = 64
A_LORA = 64
G_LORA = 128
RWKV_IN = 1792
LORA_IN = W_LORA + A_LORA + G_LORA
CHUNK = 128
SGU_WIDTH = 2048
SGU_GROUPS = 4
MEM_LEN = 256
XA_HEADS = 4
XA_HEAD_DIM = 256
RMS_EPS = 1e-5
LN_EPS = 1e-5
GN_EPS = RWKV_HEAD * 1e-5
L2_EPS = 1e-12
DECAY_SCALE = 0.6065306597126334

LANES = 128
F32_SUBLANES = 8
VMEM_LIMIT = 56 * 1024 * 1024

F32 = jnp.float32
BF16 = jnp.bfloat16


def _params(*sem):
    return pltpu.CompilerParams(dimension_semantics=sem, vmem_limit_bytes=VMEM_LIMIT)


def _dot(a, b):
    return jnp.dot(a.astype(BF16), b.astype(BF16), preferred_element_type=F32)


def _dot_nt(a, b):
    return lax.dot_general(a.astype(BF16), b.astype(BF16), (((1,), (1,)), ((), ())),
                           preferred_element_type=F32)


def _rms(x, g):
    return x * lax.rsqrt(jnp.mean(x * x, axis=-1, keepdims=True) + RMS_EPS) * g


def _head_ones(width, head):
    r = lax.broadcasted_iota(jnp.int32, (width, width), 0) // head
    c = lax.broadcasted_iota(jnp.int32, (width, width), 1) // head
    return (r == c).astype(BF16)


def _head_sum(x, ones, split=True):
    hi = x.astype(BF16)
    acc = jnp.dot(hi, ones, preferred_element_type=F32)
    if split:
        lo = (x - hi.astype(F32)).astype(BF16)
        acc = acc + jnp.dot(lo, ones, preferred_element_type=F32)
    return acc


def _linear_kernel(*refs, nx, has_gain, act, has_res):
    x_refs = refs[:nx]
    pos = nx
    g_ref = refs[pos] if has_gain else None
    pos += int(has_gain)
    w_refs = refs[pos:pos + nx]
    pos += nx
    res_ref = refs[pos] if has_res else None
    pos += int(has_res)
    o_ref = refs[pos]
    if has_gain and len(refs) == pos + 1:
        lhs = [_rms(x_refs[0][...].astype(F32), g_ref[...]).astype(BF16)]
    elif has_gain:
        xn_ref = refs[pos + 1]

        @pl.when(pl.program_id(1) == 0)
        def _():
            xn_ref[...] = _rms(x_refs[0][...].astype(F32), g_ref[...]).astype(BF16)

        lhs = [xn_ref[...]]
    else:
        lhs = [r[...] for r in x_refs]
    tn = o_ref.shape[1]
    cw = next(c for c in (256, 384, 128) if tn % c == 0)
    for c0 in range(0, tn, cw):
        cols = slice(c0, c0 + cw)
        acc = _dot(lhs[0], w_refs[0][:, cols])
        for l, w in zip(lhs[1:], w_refs[1:]):
            acc = acc + _dot(l, w[:, cols])
        if act == "gelu":
            acc = 0.5 * acc * (1.0 + lax.erf(acc * 0.7071067811865476))
        if has_res:
            acc = acc + res_ref[:, cols]
        o_ref[:, cols] = acc.astype(o_ref.dtype)


def _linear(xs, ws, *, gain=None, act=None, res=None, out_dtype=F32, tm, tn, name):
    m = xs[0].shape[0]
    n = ws[0].shape[1]
    assert m % tm == 0 and n % tn == 0, (m, tm, n, tn)
    assert gain is None or len(xs) == 1
    in_specs = [pl.BlockSpec((tm, x.shape[1]), lambda i, j: (i, 0)) for x in xs]
    args = list(xs)
    if gain is not None:
        in_specs.append(pl.BlockSpec((1, gain.shape[-1]), lambda i, j: (0, 0)))
        args.append(gain.reshape(1, -1))
    in_specs += [pl.BlockSpec((w.shape[0], tn), lambda i, j: (0, j)) for w in ws]
    args += list(ws)
    if res is not None:
        in_specs.append(pl.BlockSpec((tm, tn), lambda i, j: (i, j)))
        args.append(res)
    scratch = [pltpu.VMEM((tm, xs[0].shape[1]), BF16)] if gain is not None and n > tn else []
    kern = functools.partial(_linear_kernel, nx=len(xs), has_gain=gain is not None, act=act,
                             has_res=res is not None)
    return pl.pallas_call(
        kern,
        grid=(m // tm, n // tn),
        in_specs=in_specs,
        out_specs=pl.BlockSpec((tm, tn), lambda i, j: (i, j)),
        out_shape=jax.ShapeDtypeStruct((m, n), out_dtype),
        scratch_shapes=scratch,
        compiler_params=_params("parallel", "arbitrary"),
        name=name,
    )(*args)


def _mem_kv_kernel(m_ref, g_ref, wk_ref, wv_ref, k2_ref, v2_ref, k5_ref, v5_ref):
    mn = _rms(m_ref[...], g_ref[0]).astype(BF16)
    nb = k5_ref.shape[1]
    for w_ref, o2_ref, o5_ref in ((wk_ref, k2_ref, k5_ref), (wv_ref, v2_ref, v5_ref)):
        acc = jnp.dot(mn, w_ref[0], preferred_element_type=F32)
        o2_ref[0] = acc
        for bb in range(nb):
            for h in range(XA_HEADS):
                o5_ref[0, bb, :, h, :] = acc[bb * MEM_LEN:(bb + 1) * MEM_LEN, h * XA_HEAD_DIM:(h + 1) * XA_HEAD_DIM]


def _mem_kv(mem, gains, w_k, w_v, *, bb):
    b, m, d = mem.shape
    depth = w_k.shape[0]
    assert b % bb == 0
    tm = bb * m
    w_spec = pl.BlockSpec((1, d, d), lambda l, i: (l, 0, 0))
    o2_spec = pl.BlockSpec((1, tm, d), lambda l, i: (l, i, 0))
    o5_spec = pl.BlockSpec((1, bb, m, XA_HEADS, XA_HEAD_DIM), lambda l, i: (l, i, 0, 0, 0))
    o2_shape = jax.ShapeDtypeStruct((depth, b * m, d), F32)
    o5_shape = jax.ShapeDtypeStruct((depth, b, m, XA_HEADS, XA_HEAD_DIM), F32)
    return pl.pallas_call(
        _mem_kv_kernel,
        grid=(depth, b // bb),
        in_specs=[pl.BlockSpec((tm, d), lambda l, i: (i, 0)), pl.BlockSpec((1, 1, d), lambda l, i: (l, 0, 0)), w_spec, w_spec],
        out_specs=[o2_spec, o2_spec, o5_spec, o5_spec],
        out_shape=[o2_shape, o2_shape, o5_shape, o5_shape],
        compiler_params=_params("parallel", "parallel"),
        name="mem_kv",
    )(mem.reshape(b * m, d), gains.reshape(depth, 1, d), w_k, w_v)


def _mlp_kernel(*refs, final_norm, tf):
    if final_norm:
        x_ref, g_ref, wu_ref, wd_ref, gf_ref, o_ref = refs
    else:
        x_ref, g_ref, wu_ref, wd_ref, o_ref = refs
    x = x_ref[...]
    xn = _rms(x, g_ref[...]).astype(BF16)
    y = x
    for f0 in range(0, wu_ref.shape[1], tf):
        h = jnp.dot(xn, wu_ref[:, f0:f0 + tf], preferred_element_type=F32)
        h = jnp.square(jnp.maximum(h, 0.0)).astype(BF16)
        y = y + jnp.dot(h, wd_ref[f0:f0 + tf, :], preferred_element_type=F32)
    if final_norm:
        y = _rms(y, gf_ref[...])
    o_ref[...] = y


def _mlp(x, gain, w_up, w_down, *, final_gain=None, tm, tf):
    m, d = x.shape
    f = w_up.shape[1]
    assert m % tm == 0 and f % tf == 0
    resident = dict(pipeline_mode=pl.Buffered(1))
    in_specs = [
        pl.BlockSpec((tm, d), lambda i: (i, 0)),
        pl.BlockSpec((1, d), lambda i: (0, 0)),
        pl.BlockSpec((d, f), lambda i: (0, 0), **resident),
        pl.BlockSpec((f, d), lambda i: (0, 0), **resident),
    ]
    args = [x, gain.reshape(1, d), w_up, w_down]
    if final_gain is not None:
        in_specs.append(pl.BlockSpec((1, d), lambda i: (0, 0)))
        args.append(final_gain.reshape(1, d))
    return pl.pallas_call(
        functools.partial(_mlp_kernel, final_norm=final_gain is not None, tf=tf),
        grid=(m // tm,),
        in_specs=in_specs,
        out_specs=pl.BlockSpec((tm, d), lambda i: (i, 0)),
        out_shape=jax.ShapeDtypeStruct((m, d), F32),
        compiler_params=_params("parallel"),
        name="mlp",
    )(*args)


def _xattn_prompt_kernel(x_ref, g_ref, k_ref, v_ref, wq_ref, wo_ref, o_ref, att_ref, *, tm, ts):
    for r0 in range(0, tm, ts):
        sub = slice(r0, r0 + ts)
        x = x_ref[0, sub, :]
        xn = _rms(x, g_ref[...]).astype(BF16)
        q = jnp.dot(xn, wq_ref[...], preferred_element_type=F32) * (XA_HEAD_DIM ** -0.5)
        for h in range(XA_HEADS):
            cols = slice(h * XA_HEAD_DIM, (h + 1) * XA_HEAD_DIM)
            s = _dot_nt(q[:, cols], k_ref[0, :, cols])
            s = s - jnp.max(s, axis=-1, keepdims=True)
            p = jnp.exp(s)
            p = p / jnp.sum(p, axis=-1, keepdims=True)
            att_ref[sub, cols] = _dot(p, v_ref[0, :, cols]).astype(BF16)
        o_ref[0, sub, :] = x + jnp.dot(att_ref[sub, :], wo_ref[...], preferred_element_type=F32)


def _xattn_prompt(x, gain, mem_k, mem_v, layer, w_q, w_o, *, tm, ts):
    b, l, d = x.shape
    assert l % tm == 0 and tm % ts == 0
    kv_spec = pl.BlockSpec((1, MEM_LEN, d), lambda bi, i: (layer, bi, 0))
    w_spec = pl.BlockSpec((d, d), lambda bi, i: (0, 0), pipeline_mode=pl.Buffered(1))
    return pl.pallas_call(
        functools.partial(_xattn_prompt_kernel, tm=tm, ts=ts),
        grid=(b, l // tm),
        in_specs=[
            pl.BlockSpec((1, tm, d), lambda bi, i: (bi, i, 0)),
            pl.BlockSpec((1, d), lambda bi, i: (0, 0)),
            kv_spec, kv_spec, w_spec, w_spec,
        ],
        out_specs=pl.BlockSpec((1, tm, d), lambda bi, i: (bi, i, 0)),
        out_shape=jax.ShapeDtypeStruct((b, l, d), F32),
        scratch_shapes=[pltpu.VMEM((tm, d), BF16)],
        compiler_params=_params("parallel", "parallel"),
        name="xattn_prompt",
    )(x, gain.reshape(1, d), mem_k, mem_v, w_q, w_o)


XA_TILES = XA_HEAD_DIM // LANES


def _xattn_sample_kernel(q_ref, k_ref, v_ref, o_ref, *, bs):
    for b in range(bs):
        q = q_ref[b] * (XA_HEAD_DIM ** -0.5)
        prod = k_ref[0, b] * q[None]
        prod = prod + pltpu.roll(prod, XA_HEADS, 1)
        s = jnp.sum(prod, axis=-1, keepdims=True)
        p = jnp.exp(s - jnp.max(s, axis=0, keepdims=True))
        den = jnp.sum(p, axis=0)
        o_ref[b] = jnp.sum(p * v_ref[0, b], axis=0) / den


def _head_tile_view(x):
    lead = x.shape[:-2]
    n = len(lead)
    x = x.reshape(lead + (XA_HEADS, XA_TILES, LANES))
    return x.transpose(tuple(range(n)) + (n + 1, n, n + 2)).reshape(lead + (XA_TILES * XA_HEADS, LANES))


def _xattn_sample_core(q, cache_k, cache_v, layer, *, bs):
    b = q.shape[0]
    assert b % bs == 0 and XA_TILES == 2 and XA_TILES * XA_HEADS == F32_SUBLANES
    rows = XA_TILES * XA_HEADS
    q_spec = pl.BlockSpec((bs, rows, LANES), lambda i: (i, 0, 0))
    kv_spec = pl.BlockSpec((1, bs, MEM_LEN, rows, LANES), lambda i: (layer, i, 0, 0, 0))
    out = pl.pallas_call(
        functools.partial(_xattn_sample_kernel, bs=bs),
        grid=(b // bs,),
        in_specs=[q_spec, kv_spec, kv_spec],
        out_specs=q_spec,
        out_shape=jax.ShapeDtypeStruct((b, rows, LANES), F32),
        compiler_params=_params("parallel"),
        name="xattn_sample",
    )(_head_tile_view(q.reshape(b, XA_HEADS, XA_HEAD_DIM)), _head_tile_view(cache_k), _head_tile_view(cache_v))
    return out.reshape(b, XA_TILES, XA_HEADS, LANES).transpose(0, 2, 1, 3).reshape(b, XA_HEADS * XA_HEAD_DIM)


HALO = 16


def _pool_windows(z_ref, pos, w_ref, scale_ref, o_ref, tm):
    for gi, win in enumerate(POOL_WINDOWS):
        cols = slice(gi * POOL_GROUP_WIDTH, (gi + 1) * POOL_GROUP_WIDTH)
        tok = z_ref[HALO:HALO + tm, cols]
        acc = tok
        for k in range(1, win):
            acc = acc + z_ref[HALO - k:HALO - k + tm, cols]
        count = jnp.minimum(pos + 1, win).astype(F32)
        dlt = acc / count - tok
        o_ref[0, :, cols] = (_dot(dlt, w_ref[gi]) * scale_ref[:, cols]).astype(o_ref.dtype)


def _pool_sample_kernel(z_ref, buf_ref, w_ref, scale_ref, o_ref):
    for gi, win in enumerate(POOL_WINDOWS):
        cols = slice(gi * POOL_GROUP_WIDTH, (gi + 1) * POOL_GROUP_WIDTH)
        tok = z_ref[:, cols]
        acc = tok
        for k in range(1, win):
            acc = acc + buf_ref[:, POOL_BUF - k, cols]
        count = float(min(PAST_LEN + 1, win))
        dlt = acc / count - tok
        o_ref[:, cols] = (_dot(dlt, w_ref[gi]) * scale_ref[:, cols]).astype(o_ref.dtype)


def _pool_sample(z, buf, w_group, scale):
    b = z.shape[0]
    return pl.pallas_call(
        _pool_sample_kernel,
        grid=(1,),
        in_specs=[
            pl.BlockSpec((b, POOL_WIDTH), lambda i: (0, 0)),
            pl.BlockSpec((b, POOL_BUF, POOL_WIDTH), lambda i: (0, 0, 0)),
            pl.BlockSpec((len(POOL_WINDOWS), POOL_GROUP_WIDTH, POOL_GROUP_WIDTH), lambda i: (0, 0, 0)),
            pl.BlockSpec((1, POOL_WIDTH), lambda i: (0, 0)),
        ],
        out_specs=pl.BlockSpec((b, POOL_WIDTH), lambda i: (0, 0)),
        out_shape=jax.ShapeDtypeStruct((b, POOL_WIDTH), BF16),
        compiler_params=_params("arbitrary"),
        name="pool_sample",
    )(z, buf, w_group, scale.reshape(1, POOL_WIDTH))


def _rwkv_prep_math(z4, s4, p, out_refs, sl):
    zr, zk, zv, zl = z4
    sr, sk, sv, sq = s4
    mu_r, mu_k, mu_v, mu_l, w0, a0, k_k, k_a, r_k, w_la, w_g = p
    r = zr + (sr - zr) * mu_r
    k = zk + (sk - zk) * mu_k
    v = zv + (sv - zv) * mu_v
    lo = zl + (sq - zl) * mu_l
    wal = lo[:, :W_LORA + A_LORA]
    lane = lax.broadcasted_iota(jnp.int32, wal.shape, 1)
    feat = jnp.where(lane < W_LORA, 2.0 * jax.nn.sigmoid(2.0 * wal) - 1.0, wal)
    wa = _dot(feat, w_la)
    g = _dot(jax.nn.sigmoid(lo[:, W_LORA + A_LORA:]), w_g)
    decay = jnp.exp(-DECAY_SCALE * jax.nn.sigmoid(w0 + wa[:, :RWKV_WIDTH]))
    a = jax.nn.sigmoid(a0 + wa[:, RWKV_WIDTH:])
    kk = k * k_k
    k2 = k * (1.0 + (a - 1.0) * k_a)
    ones = _head_ones(RWKV_WIDTH, RWKV_HEAD)
    kk = kk * jnp.minimum(lax.rsqrt(_head_sum(kk * kk, ones)), 1.0 / L2_EPS)
    bonus = _head_sum(r * k2 * r_k, ones, split=False) * v
    r_o, k_o, v_o, kap_o, b_o, d_o, g_o, bonus_o = out_refs
    r_o[sl] = r
    k_o[sl] = k2
    v_o[sl] = v
    kap_o[sl] = kk
    b_o[sl] = kk * a
    d_o[sl] = decay
    g_o[sl] = g
    bonus_o[sl] = bonus


def _load_params(refs):
    return tuple(r[...] for r in refs)


AB_IN = POOL_WIDTH + RWKV_IN
LORA_COL = POOL_WIDTH + 3 * RWKV_WIDTH


def _ab_front_kernel(x_ref, g_ref, w_ref, pw_ref, ps_ref, *refs, tm):
    p_refs = refs[:11]
    pool_ref = refs[11]
    out_refs = refs[12:20]
    pool_tail_ref, shift_tail_ref, z_ref = refs[20:23]
    t = pl.program_id(1)

    @pl.when(t == 0)
    def _():
        z_ref[0:HALO, :] = jnp.zeros((HALO, AB_IN), F32)

    @pl.when(t > 0)
    def _():
        z_ref[0:HALO, :] = z_ref[tm:tm + HALO, :]

    xn = _rms(x_ref[0], g_ref[...]).astype(BF16)
    cw = 256
    for c0 in range(0, AB_IN, cw):
        z_ref[HALO:HALO + tm, c0:c0 + cw] = jnp.dot(xn, w_ref[:, c0:c0 + cw], preferred_element_type=F32)

    pos = t * tm + lax.broadcasted_iota(jnp.int32, (tm, 1), 0)
    _pool_windows(z_ref, pos, pw_ref, ps_ref, pool_ref, tm)

    col = lambda c0, width, r0: z_ref[r0:r0 + tm, c0:c0 + width]
    blocks = [(POOL_WIDTH + k * RWKV_WIDTH, RWKV_WIDTH) for k in range(3)] + [(LORA_COL, LORA_IN)]
    z4 = [col(c0, width, HALO) for c0, width in blocks]
    s4 = [col(c0, width, HALO - 1) for c0, width in blocks]
    _rwkv_prep_math(z4, s4, _load_params(p_refs), out_refs, (0,))

    pool_tail_ref[0] = z_ref[tm:tm + HALO, 0:POOL_WIDTH]
    shift_tail_ref[0] = z_ref[tm + HALO - F32_SUBLANES:tm + HALO, POOL_WIDTH:]


def _prep_param_args(mu, w0, a0, k_k, k_a, r_k, w_la, w_g):
    w = RWKV_WIDTH
    row = lambda x: x.reshape(1, -1)
    return [row(mu[:w]), row(mu[w:2 * w]), row(mu[2 * w:3 * w]), row(mu[3 * w:]),
            row(w0), row(a0), row(k_k), row(k_a), row(r_k), w_la, w_g]


def _prep_out_shapes(lead):
    return [jax.ShapeDtypeStruct(lead + (RWKV_WIDTH,), F32) for _ in range(8)]


def _ab_front_prompt(x, gain, w_in, pool_w, pool_scale, params, *, tm):
    b, l, d = x.shape
    assert l % tm == 0 and tm % HALO == 0
    full = lambda a: pl.BlockSpec(a.shape, lambda bi, i: (0,) * a.ndim)
    resident = dict(pipeline_mode=pl.Buffered(1))
    in_specs = [
        pl.BlockSpec((1, tm, d), lambda bi, i: (bi, i, 0)),
        pl.BlockSpec((1, d), lambda bi, i: (0, 0)),
        pl.BlockSpec((d, AB_IN), lambda bi, i: (0, 0), **resident),
        full(pool_w),
        pl.BlockSpec((1, POOL_WIDTH), lambda bi, i: (0, 0)),
    ] + [full(a) for a in params]
    tok_spec = pl.BlockSpec((1, tm, RWKV_WIDTH), lambda bi, i: (bi, i, 0))
    out_specs = [tok_spec] * 9 + [pl.BlockSpec((1, HALO, POOL_WIDTH), lambda bi, i: (bi, 0, 0)),
                                  pl.BlockSpec((1, F32_SUBLANES, RWKV_IN), lambda bi, i: (bi, 0, 0))]
    out_shape = ([jax.ShapeDtypeStruct((b, l, POOL_WIDTH), BF16)] + _prep_out_shapes((b, l))
                 + [jax.ShapeDtypeStruct((b, HALO, POOL_WIDTH), F32), jax.ShapeDtypeStruct((b, F32_SUBLANES, RWKV_IN), F32)])
    return pl.pallas_call(
        functools.partial(_ab_front_kernel, tm=tm),
        grid=(b, l // tm),
        in_specs=in_specs,
        out_specs=out_specs,
        out_shape=out_shape,
        scratch_shapes=[pltpu.VMEM((tm + HALO, AB_IN), F32)],
        compiler_params=_params("parallel", "arbitrary"),
        name="ab_front_prompt",
    )(x, gain.reshape(1, d), w_in, pool_w, pool_scale.reshape(1, POOL_WIDTH), *params)


def _rwkv_prep_sample_kernel(zr_ref, zk_ref, zv_ref, zl_ref, sr_ref, sk_ref, sv_ref, sl_ref, *refs):
    p_refs = refs[:11]
    out_refs = refs[11:19]
    z4 = [zr_ref[...], zk_ref[...], zv_ref[...], zl_ref[...]]
    s4 = [sr_ref[...], sk_ref[...], sv_ref[...], sl_ref[...]]
    _rwkv_prep_math(z4, s4, _load_params(p_refs), out_refs, (Ellipsis,))


def _rwkv_prep_sample(z, shift, params):
    b = z.shape[0]
    pw = POOL_WIDTH // RWKV_WIDTH
    full = lambda a: pl.BlockSpec(a.shape, lambda i: (0,) * a.ndim)
    in_specs = [
        pl.BlockSpec((b, RWKV_WIDTH), lambda i: (0, pw)),
        pl.BlockSpec((b, RWKV_WIDTH), lambda i: (0, pw + 1)),
        pl.BlockSpec((b, RWKV_WIDTH), lambda i: (0, pw + 2)),
        pl.BlockSpec((b, LORA_IN), lambda i: (0, (POOL_WIDTH + 3 * RWKV_WIDTH) // LORA_IN)),
        pl.BlockSpec((b, RWKV_WIDTH), lambda i: (0, 0)),
        pl.BlockSpec((b, RWKV_WIDTH), lambda i: (0, 1)),
        pl.BlockSpec((b, RWKV_WIDTH), lambda i: (0, 2)),
        pl.BlockSpec((b, LORA_IN), lambda i: (0, 3 * RWKV_WIDTH // LORA_IN)),
    ] + [full(a) for a in params]
    out_spec = pl.BlockSpec((b, RWKV_WIDTH), lambda i: (0, 0))
    return pl.pallas_call(
        _rwkv_prep_sample_kernel,
        grid=(1,),
        in_specs=in_specs,
        out_specs=[out_spec] * 8,
        out_shape=_prep_out_shapes((b,)),
        compiler_params=_params("arbitrary"),
        name="rwkv_prep_sample",
    )(z, z, z, z, shift, shift, shift, shift, *params)


def _rwkv_scan_kernel(kap_ref, d_ref, b_ref, k_ref, r_ref, v_ref, s0_ref, y_ref, sout_ref, s_ref, *, tt, nib, vp):
    tb = pl.program_id(1)
    sub = F32_SUBLANES

    @pl.when(tb == 0)
    def _():
        s_ref[...] = s0_ref[0]

    for i in range(nib * sub if vp > tt else 0):
        y_ref[0, i * vp + tt:(i + 1) * vp, :] = jnp.zeros((vp - tt, LANES), F32)

    nparts = 2
    look_ahead = vp > tt

    def add_to(acc, slot, term):
        acc[slot] = term if acc[slot] is None else acc[slot] + term

    def block_sums(acc):
        return tuple(acc[nparts * ib] + acc[nparts * ib + 1] for ib in range(nib))

    def first_sa():
        acc = [None] * (nparts * nib)
        for j in range(RWKV_HEAD):
            kap = kap_ref[0, pl.ds(j * vp, 1), :]
            for ib in range(nib):
                add_to(acc, nparts * ib + j % nparts, s_ref[ib, j] * kap)
        return tuple(-x for x in block_sums(acc))

    def step(t, sa):
        vy_rows = [pl.ds(ib * sub * vp + t, sub, stride=vp) if vp > 1 else pl.ds(ib * sub, sub) for ib in range(nib)]
        v = [v_ref[0, rows, :] for rows in vy_rows]
        acc_y = [None] * (nparts * nib)
        acc_s = [None] * (nparts * nib)
        for j in range(RWKV_HEAD):
            row = pl.ds(j * vp + t, 1)
            dec = d_ref[0, row, :]
            bb = b_ref[0, row, :]
            kk = k_ref[0, row, :]
            rr = r_ref[0, row, :]
            kap_next = kap_ref[0, pl.ds(j * vp + t + 1, 1), :] if look_ahead else None
            for ib in range(nib):
                s = s_ref[ib, j] * dec + sa[ib] * bb + v[ib] * kk
                s_ref[ib, j] = s
                add_to(acc_y, nparts * ib + j % nparts, s * rr)
                if look_ahead:
                    add_to(acc_s, nparts * ib + j % nparts, s * kap_next)
        for ib, y in enumerate(block_sums(acc_y)):
            y_ref[0, vy_rows[ib], :] = y
        return tuple(-x for x in block_sums(acc_s)) if look_ahead else sa

    lax.fori_loop(0, tt, step, first_sa())

    @pl.when(tb == pl.num_programs(1) - 1)
    def _():
        sout_ref[0] = s_ref[...]


def _rwkv_scan(kap, dec, bb, kk, rr, v, s0, *, tt, vp):
    g, nb, rows, _ = kap.shape
    n = RWKV_HEAD
    nib = s0.shape[1]
    assert rows == n * vp and v.shape[2] == nib * F32_SUBLANES * vp and tt <= vp
    op_spec = pl.BlockSpec((None, 1, n * vp, LANES), lambda gi, i: (gi, i, 0, 0))
    v_spec = pl.BlockSpec((None, 1, v.shape[2], LANES), lambda gi, i: (gi, i, 0, 0))
    s_spec = pl.BlockSpec((1, nib, n, F32_SUBLANES, LANES), lambda gi, i: (gi, 0, 0, 0, 0))
    return pl.pallas_call(
        functools.partial(_rwkv_scan_kernel, tt=tt, nib=nib, vp=vp),
        grid=(g, nb),
        in_specs=[op_spec] * 5 + [v_spec, s_spec],
        out_specs=[v_spec, s_spec],
        out_shape=[jax.ShapeDtypeStruct(v.shape, F32), jax.ShapeDtypeStruct(s0.shape, F32)],
        scratch_shapes=[pltpu.VMEM((nib, n, F32_SUBLANES, LANES), F32)],
        compiler_params=_params("parallel", "arbitrary"),
        name="rwkv_scan",
    )(kap, dec, bb, kk, rr, v, s0)


def _rwkv_post_kernel(y_ref, bonus_ref, g_ref, gng_ref, gnb_ref, o_ref):
    y = y_ref[...]
    ones = _head_ones(RWKV_WIDTH, RWKV_HEAD)
    m = _head_sum(y, ones, split=False) * (1.0 / RWKV_HEAD)
    c = y - m
    var = _head_sum(c * c, ones, split=False) * (1.0 / RWKV_HEAD)
    yn = c * lax.rsqrt(var + GN_EPS) * gng_ref[...] + gnb_ref[...]
    o_ref[...] = ((yn + bonus_ref[...]) * g_ref[...]).astype(o_ref.dtype)


def _rwkv_post(y, bonus, g, gn_g, gn_b, *, tm):
    m = y.shape[0]
    assert m % tm == 0
    spec = pl.BlockSpec((tm, RWKV_WIDTH), lambda i: (i, 0))
    pspec = pl.BlockSpec((1, RWKV_WIDTH), lambda i: (0, 0))
    return pl.pallas_call(
        _rwkv_post_kernel,
        grid=(m // tm,),
        in_specs=[spec, spec, spec, pspec, pspec],
        out_specs=spec,
        out_shape=jax.ShapeDtypeStruct((m, RWKV_WIDTH), BF16),
        compiler_params=_params("parallel"),
        name="rwkv_post",
    )(y, bonus, g, gn_g.reshape(1, -1), gn_b.reshape(1, -1))


def _layernorm(v, g, b):
    m = jnp.mean(v, axis=-1, keepdims=True)
    c = v - m
    var = jnp.mean(c * c, axis=-1, keepdims=True)
    return c * lax.rsqrt(var + LN_EPS) * g + b


def _sgu_prompt_kernel(x_ref, g_ref, wi_ref, lng_ref, lnb_ref, ws_ref, bias_ref, wo_ref, o_ref, zc_ref, vn_ref, gate_ref,
                       *, tm, ts):
    xn = _rms(x_ref[...], g_ref[...]).astype(BF16)
    cw = 256
    for c0 in list(range(SGU_WIDTH, 2 * SGU_WIDTH, cw)) + list(range(0, SGU_WIDTH, cw)):
        acc = jnp.dot(xn, wi_ref[:, c0:c0 + cw], preferred_element_type=F32)
        zc_ref[:, c0:c0 + cw] = (0.5 * acc * (1.0 + lax.erf(acc * 0.7071067811865476))).astype(BF16)
    gw = SGU_WIDTH // SGU_GROUPS
    r = lax.broadcasted_iota(jnp.int32, (CHUNK, CHUNK), 0)
    c = lax.broadcasted_iota(jnp.int32, (CHUNK, CHUNK), 1)
    wms = [jnp.where(r >= c, ws_ref[gi], 0.0).astype(BF16) for gi in range(SGU_GROUPS)]
    for r0 in range(0, tm, ts):
        sub = slice(r0, r0 + ts)
        v = zc_ref[sub, SGU_WIDTH:].astype(F32)
        vn_ref[sub, :] = _layernorm(v, lng_ref[...], lnb_ref[...]).astype(BF16)
        for gi in range(SGU_GROUPS):
            cols = slice(gi * gw, (gi + 1) * gw)
            for ci in range(r0 // CHUNK, (r0 + ts) // CHUNK):
                rows = slice(ci * CHUNK, (ci + 1) * CHUNK)
                sp = jnp.dot(wms[gi], vn_ref[rows, cols], preferred_element_type=F32) + bias_ref[:, cols]
                gate_ref[rows, cols] = (zc_ref[rows, cols].astype(F32) * sp).astype(BF16)
        o_ref[sub, :] = x_ref[sub, :] + jnp.dot(gate_ref[sub, :], wo_ref[...], preferred_element_type=F32)


def _sgu_prompt(x, gain, w_in, ln_g, ln_b, w_s, bias, w_out, *, tm, ts):
    t, d = x.shape
    assert t % tm == 0 and tm % ts == 0 and ts % CHUNK == 0
    row = lambda a: a.reshape(1, -1)
    resident = dict(pipeline_mode=pl.Buffered(1))
    return pl.pallas_call(
        functools.partial(_sgu_prompt_kernel, tm=tm, ts=ts),
        grid=(t // tm,),
        in_specs=[
            pl.BlockSpec((tm, d), lambda i: (i, 0)),
            pl.BlockSpec((1, d), lambda i: (0, 0)),
            pl.BlockSpec((d, 2 * SGU_WIDTH), lambda i: (0, 0), **resident),
            pl.BlockSpec((1, SGU_WIDTH), lambda i: (0, 0)),
            pl.BlockSpec((1, SGU_WIDTH), lambda i: (0, 0)),
            pl.BlockSpec((SGU_GROUPS, CHUNK, CHUNK), lambda i: (0, 0, 0)),
            pl.BlockSpec((CHUNK, SGU_WIDTH), lambda i: (0, 0), **resident),
            pl.BlockSpec((SGU_WIDTH, d), lambda i: (0, 0), **resident),
        ],
        out_specs=pl.BlockSpec((tm, d), lambda i: (i, 0)),
        out_shape=jax.ShapeDtypeStruct((t, d), F32),
        scratch_shapes=[pltpu.VMEM((tm, 2 * SGU_WIDTH), BF16), pltpu.VMEM((tm, SGU_WIDTH), BF16),
                        pltpu.VMEM((tm, SGU_WIDTH), BF16)],
        compiler_params=_params("parallel"),
        name="sgu_prompt",
    )(x, row(gain), w_in, row(ln_g), row(ln_b), w_s, bias, w_out)


def _sgu_sample_kernel(u_ref, v_ref, lng_ref, lnb_ref, coef_ref, bias_ref, gate_ref, vn_ref):
    vn = _layernorm(v_ref[...], lng_ref[...], lnb_ref[...])
    vn_ref[...] = vn
    gate_ref[...] = (u_ref[...] * (vn * coef_ref[...] + bias_ref[...])).astype(gate_ref.dtype)


def _sgu_sample(zc, ln_g, ln_b, coef, bias):
    b = zc.shape[0]
    row = lambda a: a.reshape(1, -1)
    pspec = pl.BlockSpec((1, SGU_WIDTH), lambda i: (0, 0))
    return pl.pallas_call(
        _sgu_sample_kernel,
        grid=(1,),
        in_specs=[pl.BlockSpec((b, SGU_WIDTH), lambda i: (0, 0)), pl.BlockSpec((b, SGU_WIDTH), lambda i: (0, 1)),
                  pspec, pspec, pspec, pspec],
        out_specs=[pl.BlockSpec((b, SGU_WIDTH), lambda i: (0, 0))] * 2,
        out_shape=[jax.ShapeDtypeStruct((b, SGU_WIDTH), BF16), jax.ShapeDtypeStruct((b, SGU_WIDTH), F32)],
        compiler_params=_params("arbitrary"),
        name="sgu_sample",
    )(zc, zc, row(ln_g), row(ln_b), row(coef), row(bias))


SCAN_TT = 64
RELAYOUT_BLOCKS = 4
RELAYOUT_T = RELAYOUT_BLOCKS * SCAN_TT
V_PITCH = 72
HALF = RWKV_HEAD // 2


def _split_heads(z_ref, b, x_ref, c):
    xt = x_ref[b, c * LANES:(c + 1) * LANES, :].T
    for h in range(RWKV_HEADS):
        z_ref[b, h * V_PITCH:h * V_PITCH + RWKV_HEAD, :] = xt[h * RWKV_HEAD:(h + 1) * RWKV_HEAD]


def _head_rows(z_ref, b, n):
    return z_ref[b, pl.ds(n, RWKV_HEADS, stride=V_PITCH), :]


def _store_slab(o_ref, c, n, wt):
    pad = jnp.zeros((V_PITCH - SCAN_TT, LANES), F32)
    for k in range(LANES // SCAN_TT):
        blk = c * (LANES // SCAN_TT) + k
        o_ref[blk, n * V_PITCH:n * V_PITCH + SCAN_TT, :] = wt[k * SCAN_TT:(k + 1) * SCAN_TT]
        o_ref[blk, n * V_PITCH + SCAN_TT:(n + 1) * V_PITCH, :] = pad


def _key_to_lanes_kernel(x_ref, o_ref, z_ref):
    nb = x_ref.shape[0]
    for c in range(RELAYOUT_T // LANES):
        for b in range(nb):
            _split_heads(z_ref, b, x_ref, c)
        for j in range(RWKV_HEAD):
            rows = [_head_rows(z_ref, b, j) for b in range(nb)]
            _store_slab(o_ref, c, j, jnp.concatenate(rows + rows, axis=0).T)


def _value_to_lanes_kernel(x_ref, o_ref, z_ref):
    nb = x_ref.shape[0]
    for c in range(RELAYOUT_T // LANES):
        for b in range(nb):
            _split_heads(z_ref, b, x_ref, c)
        for i in range(HALF):
            rows = [_head_rows(z_ref, b, half * HALF + i) for half in range(2) for b in range(nb)]
            _store_slab(o_ref, c, i, jnp.concatenate(rows, axis=0).T)


def _ab_back_kernel(y_ref, bonus_ref, g_ref, pool_ref, x_ref, gng_ref, gnb_ref, w_ref, o_ref, z_ref, lhs_ref):
    nb = o_ref.shape[0]
    per = LANES // SCAN_TT
    for i in range(HALF):
        w = jnp.concatenate([y_ref[k, i * V_PITCH:i * V_PITCH + SCAN_TT, :] for k in range(per)], axis=0)
        wt = w.T
        for half in range(2):
            for b in range(nb):
                r0 = (half * nb + b) * RWKV_HEADS
                z_ref[b, pl.ds(half * HALF + i, RWKV_HEADS, stride=V_PITCH), :] = wt[r0:r0 + RWKV_HEADS]
    ones = _head_ones(RWKV_WIDTH, RWKV_HEAD)
    for b in range(nb):
        zt = jnp.concatenate([z_ref[b, h * V_PITCH:h * V_PITCH + RWKV_HEAD, :] for h in range(RWKV_HEADS)], axis=0)
        y = zt.T
        m = _head_sum(y, ones, split=False) * (1.0 / RWKV_HEAD)
        c = y - m
        var = _head_sum(c * c, ones, split=False) * (1.0 / RWKV_HEAD)
        yn = c * lax.rsqrt(var + GN_EPS) * gng_ref[...] + gnb_ref[...]
        rows = slice(b * LANES, (b + 1) * LANES)
        lhs_ref[rows, :POOL_WIDTH] = pool_ref[b]
        lhs_ref[rows, POOL_WIDTH:] = ((yn + bonus_ref[b]) * g_ref[b]).astype(BF16)
    mix = jnp.dot(lhs_ref[...], w_ref[...], preferred_element_type=F32)
    for b in range(nb):
        o_ref[b] = x_ref[b] + mix[b * LANES:(b + 1) * LANES]


def _key_to_lanes(x):
    b, l, w = x.shape
    assert 2 * b * RWKV_HEADS == LANES and l % RELAYOUT_T == 0 and w == RWKV_WIDTH
    out = pl.pallas_call(
        _key_to_lanes_kernel,
        grid=(l // RELAYOUT_T,),
        in_specs=[pl.BlockSpec((b, RELAYOUT_T, w), lambda i: (0, i, 0))],
        out_specs=pl.BlockSpec((RELAYOUT_BLOCKS, RWKV_HEAD * V_PITCH, LANES), lambda i: (i, 0, 0)),
        out_shape=jax.ShapeDtypeStruct((l // SCAN_TT, RWKV_HEAD * V_PITCH, LANES), F32),
        scratch_shapes=[pltpu.VMEM((b, RWKV_HEADS * V_PITCH, LANES), F32)],
        compiler_params=_params("parallel"),
        name="key_to_lanes",
    )(x)
    return out[None]


def _value_to_lanes(v):
    b, l, w = v.shape
    assert 2 * b * RWKV_HEADS == LANES and l % RELAYOUT_T == 0 and w == RWKV_WIDTH
    out = pl.pallas_call(
        _value_to_lanes_kernel,
        grid=(l // RELAYOUT_T,),
        in_specs=[pl.BlockSpec((b, RELAYOUT_T, w), lambda i: (0, i, 0))],
        out_specs=pl.BlockSpec((RELAYOUT_BLOCKS, HALF * V_PITCH, LANES), lambda i: (i, 0, 0)),
        out_shape=jax.ShapeDtypeStruct((l // SCAN_TT, HALF * V_PITCH, LANES), F32),
        scratch_shapes=[pltpu.VMEM((b, RWKV_HEADS * V_PITCH, LANES), F32)],
        compiler_params=_params("parallel"),
        name="value_to_lanes",
    )(v)
    return out[None]


def _ab_back_prompt(y, bonus, g, y_pool, x, gn_g, gn_b, w_out):
    b, l, d = x.shape
    per = LANES // SCAN_TT
    tok = lambda width: pl.BlockSpec((b, LANES, width), lambda i: (0, i, 0))
    pspec = pl.BlockSpec((1, RWKV_WIDTH), lambda i: (0, 0))
    return pl.pallas_call(
        _ab_back_kernel,
        grid=(l // LANES,),
        in_specs=[pl.BlockSpec((per, HALF * V_PITCH, LANES), lambda i: (i, 0, 0)),
                  tok(RWKV_WIDTH), tok(RWKV_WIDTH), tok(POOL_WIDTH), tok(d), pspec, pspec,
                  pl.BlockSpec(w_out.shape, lambda i: (0, 0), pipeline_mode=pl.Buffered(1))],
        out_specs=tok(d),
        out_shape=jax.ShapeDtypeStruct((b, l, d), F32),
        scratch_shapes=[pltpu.VMEM((b, RWKV_HEADS * V_PITCH, LANES), F32),
                        pltpu.VMEM((b * LANES, POOL_WIDTH + RWKV_WIDTH), BF16)],
        compiler_params=_params("parallel"),
        name="ab_back_prompt",
    )(y[0], bonus, g, y_pool, x, gn_g.reshape(1, -1), gn_b.reshape(1, -1), w_out)


def _state_from_lanes_prompt(s, b):
    s = s.reshape(HALF // F32_SUBLANES, RWKV_HEAD, F32_SUBLANES, 2, b, RWKV_HEADS).transpose(4, 5, 3, 0, 2, 1)
    return s.reshape(b, RWKV_HEADS, RWKV_HEAD, RWKV_HEAD)


def _to_lanes_sample(x):
    b = x.shape[0]
    g = b * RWKV_HEADS // LANES
    x = x.reshape(g, LANES // RWKV_HEADS, RWKV_HEADS, RWKV_HEAD).transpose(0, 3, 1, 2)
    return x.reshape(g, 1, RWKV_HEAD, LANES)


def _from_lanes_sample(y):
    g = y.shape[0]
    y = y.reshape(g, RWKV_HEAD, LANES // RWKV_HEADS, RWKV_HEADS).transpose(0, 2, 3, 1)
    return y.reshape(g * LANES // RWKV_HEADS, RWKV_WIDTH)


def _state_to_lanes_sample(s):
    b = s.shape[0]
    g = b * RWKV_HEADS // LANES
    nib = RWKV_HEAD // F32_SUBLANES
    s = s.reshape(g, LANES // RWKV_HEADS, RWKV_HEADS, nib, F32_SUBLANES, RWKV_HEAD).transpose(0, 3, 5, 4, 1, 2)
    return s.reshape(g, nib, RWKV_HEAD, F32_SUBLANES, LANES)


def _state_from_lanes_sample(s):
    g, nib = s.shape[:2]
    s = s.reshape(g, nib, RWKV_HEAD, F32_SUBLANES, LANES // RWKV_HEADS, RWKV_HEADS).transpose(0, 4, 5, 1, 3, 2)
    return s.reshape(g * LANES // RWKV_HEADS, RWKV_HEADS, RWKV_HEAD, RWKV_HEAD)


def kernel(x_prompt, x_sample, mem_prompt, cache_mem_k, cache_mem_v, state_pool, state_shift, state_wkv, norm_mix_g, norm_xa_g, norm_mem_g, norm_ffn_g, norm_final_g, w_in_ab, w_out_ab, pool_w, pool_scale, rwkv_mu, rwkv_w0, rwkv_w2, rwkv_a0, rwkv_a2, rwkv_g2, rwkv_k_k, rwkv_k_a, rwkv_r_k, rwkv_gn_g, rwkv_gn_b, w_in_c, sgu_ln_g, sgu_ln_b, sgu_w_s, sgu_b_s, w_out_c, w_xq, w_xk, w_xv, w_xo, w_ff_up, w_ff_down):
    bp, lp, d = x_prompt.shape
    bs = x_sample.shape[0]
    tp = bp * lp
    bf = lambda w: w.astype(BF16)

    per_layer = lambda w: [bf(w[k]) for k in range(w.shape[0])]
    w_in_ab_b, w_out_ab_b, pool_w_b = per_layer(w_in_ab), per_layer(w_out_ab), per_layer(pool_w)
    w_in_c_b, w_out_c_b = per_layer(w_in_c), per_layer(w_out_c)
    w_xq_b, w_xo_b = per_layer(w_xq), per_layer(w_xo)
    w_xk_b, w_xv_b = bf(w_xk), bf(w_xv)
    w_up_b, w_down_b = per_layer(w_ff_up), per_layer(w_ff_down)

    mem_k_p, mem_v_p, mem_k_out, mem_v_out = _mem_kv(mem_prompt, norm_mem_g, w_xk_b, w_xv_b, bb=4)

    xp = x_prompt.reshape(tp, d)
    xs = x_sample.reshape(bs, d)
    pool_out_p, pool_out_s, shift_out_p, shift_out_s, wkv_out_p, wkv_out_s, sgu_v_s = [], [], [], [], [], [], []

    for l in range(DEPTH):
        j = l // 2
        if l % 2 == 0:
            w_la = jnp.zeros((W_LORA + A_LORA, 2 * RWKV_WIDTH), F32)
            w_la = w_la.at[:W_LORA, :RWKV_WIDTH].set(rwkv_w2[j]).at[W_LORA:, RWKV_WIDTH:].set(rwkv_a2[j])
            params = _prep_param_args(rwkv_mu[j], rwkv_w0[j], rwkv_a0[j], rwkv_k_k[j], rwkv_k_a[j],
                                      rwkv_r_k[j].reshape(-1), bf(w_la), bf(rwkv_g2[j]))
            y_pool, r, k2, v, kap, bb, dec, g, bonus, pool_tail, shift_tail = _ab_front_prompt(
                xp.reshape(bp, lp, d), norm_mix_g[l], w_in_ab_b[j], pool_w_b[j], pool_scale[j], params, tm=512)
            s0 = jnp.zeros((1, HALF // F32_SUBLANES, RWKV_HEAD, F32_SUBLANES, LANES), F32)
            y_l, s_l = _rwkv_scan(_key_to_lanes(kap), _key_to_lanes(dec), _key_to_lanes(bb), _key_to_lanes(k2),
                                  _key_to_lanes(r), _value_to_lanes(v), s0, tt=SCAN_TT, vp=V_PITCH)
            xp = _ab_back_prompt(y_l, bonus, g, y_pool, xp.reshape(bp, lp, d), rwkv_gn_g[j], rwkv_gn_b[j],
                                 w_out_ab_b[j]).reshape(tp, d)
            pool_out_p.append(pool_tail[:, HALO - POOL_BUF:])
            shift_out_p.append(shift_tail[:, F32_SUBLANES - 1])
            wkv_out_p.append(_state_from_lanes_prompt(s_l, bp))
            zs = _linear([xs], [w_in_ab_b[j]], gain=norm_mix_g[l], tm=bs, tn=256, name="ab_in_sample")
            y_pool = _pool_sample(zs, state_pool[j], pool_w_b[j], pool_scale[j])
            r, k2, v, kap, bb, dec, g, bonus = _rwkv_prep_sample(zs, state_shift[j], params)
            y_l, s_l = _rwkv_scan(_to_lanes_sample(kap), _to_lanes_sample(dec), _to_lanes_sample(bb),
                                  _to_lanes_sample(k2), _to_lanes_sample(r), _to_lanes_sample(v),
                                  _state_to_lanes_sample(state_wkv[j]), tt=1, vp=1)
            y_rwkv = _rwkv_post(_from_lanes_sample(y_l), bonus, g, rwkv_gn_g[j], rwkv_gn_b[j], tm=bs)
            xs = _linear([y_pool, y_rwkv], [w_out_ab_b[j][:POOL_WIDTH], w_out_ab_b[j][POOL_WIDTH:]], res=xs,
                         tm=bs, tn=512, name="ab_out_sample")
            pool_out_s.append(jnp.concatenate([state_pool[j][:, 1:], zs[:, None, :POOL_WIDTH]], axis=1))
            shift_out_s.append(zs[:, POOL_WIDTH:])
            wkv_out_s.append(_state_from_lanes_sample(s_l))
        else:
            gw = SGU_WIDTH // SGU_GROUPS
            bias = jnp.repeat(sgu_b_s[j].T, gw, axis=1)
            xp = _sgu_prompt(xp, norm_mix_g[l], w_in_c_b[j], sgu_ln_g[j], sgu_ln_b[j], sgu_w_s[j], bias, w_out_c_b[j],
                             tm=512, ts=128)
            zc = _linear([xs], [w_in_c_b[j]], gain=norm_mix_g[l], act="gelu", tm=bs, tn=512, name="sgu_in_sample")
            gate, vn = _sgu_sample(zc, sgu_ln_g[j], sgu_ln_b[j], jnp.repeat(sgu_w_s[j][:, 0, 0], gw),
                                   jnp.repeat(sgu_b_s[j][:, 0], gw))
            xs = _linear([gate], [w_out_c_b[j]], res=xs, tm=bs, tn=512, name="sgu_out_sample")
            sgu_v_s.append(vn.reshape(bs, 1, SGU_WIDTH))

        xp = _xattn_prompt(xp.reshape(bp, lp, d), norm_xa_g[l], mem_k_p, mem_v_p, l, w_xq_b[l], w_xo_b[l],
                           tm=1024, ts=512).reshape(tp, d)
        q = _linear([xs], [w_xq_b[l]], gain=norm_xa_g[l], tm=bs, tn=512, name="xattn_q_sample")
        att = _xattn_sample_core(q, cache_mem_k, cache_mem_v, l, bs=8)
        xs = _linear([att], [w_xo_b[l]], res=xs, tm=bs, tn=512, name="xattn_o_sample")

        fg = norm_final_g if l == DEPTH - 1 else None
        xp = _mlp(xp, norm_ffn_g[l], w_up_b[l], w_down_b[l], final_gain=fg, tm=1024, tf=1024)
        xs = _mlp(xs, norm_ffn_g[l], w_up_b[l], w_down_b[l], final_gain=fg, tm=bs, tf=512)

    return (xp.reshape(bp, lp, d),
            xs.reshape(bs, 1, d),
            mem_k_out,
            mem_v_out,
            jnp.stack(pool_out_p),
            jnp.stack(pool_out_s),
            jnp.stack(shift_out_p),
            jnp.stack(shift_out_s),
            jnp.stack(wkv_out_p),
            jnp.stack(wkv_out_s),
            jnp.stack(sgu_v_s))
```

```python
import functools

import jax
import jax.numpy as jnp
from jax import lax
from jax.experimental import pallas as pl
from jax.experimental.pallas import tpu as pltpu

DEPTH = 2
PAST_LEN = 16384
POOL_WINDOWS = (2, 4, 8, 16)
POOL_GROUP_WIDTH = 128
POOL_WIDTH = 512
POOL_BUF = 15
RWKV_HEAD = 64
RWKV_WIDTH = 512
RWKV_HEADS = 8
W_LORA = 64
A_LORA = 64
G_LORA = 128
RWKV_IN = 1792
LORA_IN = W_LORA + A_LORA + G_LORA
CHUNK = 128
SGU_WIDTH = 2048
SGU_GROUPS = 4
MEM_LEN = 256
XA_HEADS = 4
XA_HEAD_DIM = 256
RMS_EPS = 1e-5
LN_EPS = 1e-5
GN_EPS = RWKV_HEAD * 1e-5
L2_EPS = 1e-12
DECAY_SCALE = 0.6065306597126334

LANES = 128
F32_SUBLANES = 8
VMEM_LIMIT = 56 * 1024 * 1024

F32 = jnp.float32
BF16 = jnp.bfloat16


def _params(*sem):
    return pltpu.CompilerParams(dimension_semantics=sem, vmem_limit_bytes=VMEM_LIMIT)


def _dot(a, b):
    return jnp.dot(a.astype(BF16), b.astype(BF16), preferred_element_type=F32)


def _dot_nt(a, b):
    return lax.dot_general(a.astype(BF16), b.astype(BF16), (((1,), (1,)), ((), ())),
                           preferred_element_type=F32)


def _rms(x, g):
    return x * lax.rsqrt(jnp.mean(x * x, axis=-1, keepdims=True) + RMS_EPS) * g


def _head_ones(width, head):
    r = lax.broadcasted_iota(jnp.int32, (width, width), 0) // head
    c = lax.broadcasted_iota(jnp.int32, (width, width), 1) // head
    return (r == c).astype(BF16)


def _head_sum(x, ones, split=True):
    hi = x.astype(BF16)
    acc = jnp.dot(hi, ones, preferred_element_type=F32)
    if split:
        lo = (x - hi.astype(F32)).astype(BF16)
        acc = acc + jnp.dot(lo, ones, preferred_element_type=F32)
    return acc


def _linear_kernel(*refs, nx, has_gain, act, has_res):
    x_refs = refs[:nx]
    pos = nx
    g_ref = refs[pos] if has_gain else None
    pos += int(has_gain)
    w_refs = refs[pos:pos + nx]
    pos += nx
    res_ref = refs[pos] if has_res else None
    pos += int(has_res)
    o_ref = refs[pos]
    if has_gain and len(refs) == pos + 1:
        lhs = [_rms(x_refs[0][...].astype(F32), g_ref[...]).astype(BF16)]
    elif has_gain:
        xn_ref = refs[pos + 1]

        @pl.when(pl.program_id(1) == 0)
        def _():
            xn_ref[...] = _rms(x_refs[0][...].astype(F32), g_ref[...]).astype(BF16)

        lhs = [xn_ref[...]]
    else:
        lhs = [r[...] for r in x_refs]
    tn = o_ref.shape[1]
    cw = next(c for c in (256, 384, 128) if tn % c == 0)
    for c0 in range(0, tn, cw):
        cols = slice(c0, c0 + cw)
        acc = _dot(lhs[0], w_refs[0][:, cols])
        for l, w in zip(lhs[1:], w_refs[1:]):
            acc = acc + _dot(l, w[:, cols])
        if act == "gelu":
            acc = 0.5 * acc * (1.0 + lax.erf(acc * 0.7071067811865476))
        if has_res:
            acc = acc + res_ref[:, cols]
        o_ref[:, cols] = acc.astype(o_ref.dtype)


def _linear(xs, ws, *, gain=None, act=None, res=None, out_dtype=F32, tm, tn, name):
    m = xs[0].shape[0]
    n = ws[0].shape[1]
    assert m % tm == 0 and n % tn == 0, (m, tm, n, tn)
    assert gain is None or len(xs) == 1
    in_specs = [pl.BlockSpec((tm, x.shape[1]), lambda i, j: (i, 0)) for x in xs]
    args = list(xs)
    if gain is not None:
        in_specs.append(pl.BlockSpec((1, gain.shape[-1]), lambda i, j: (0, 0)))
        args.append(gain.reshape(1, -1))
    in_specs += [pl.BlockSpec((w.shape[0], tn), lambda i, j: (0, j)) for w in ws]
    args += list(ws)
    if res is not None:
        in_specs.append(pl.BlockSpec((tm, tn), lambda i, j: (i, j)))
        args.append(res)
    scratch = [pltpu.VMEM((tm, xs[0].shape[1]), BF16)] if gain is not None and n > tn else []
    kern = functools.partial(_linear_kernel, nx=len(xs), has_gain=gain is not None, act=act,
                             has_res=res is not None)
    return pl.pallas_call(
        kern,
        grid=(m // tm, n // tn),
        in_specs=in_specs,
        out_specs=pl.BlockSpec((tm, tn), lambda i, j: (i, j)),
        out_shape=jax.ShapeDtypeStruct((m, n), out_dtype),
        scratch_shapes=scratch,
        compiler_params=_params("parallel", "arbitrary"),
        name=name,
    )(*args)


def _mem_kv_kernel(m_ref, g_ref, wk_ref, wv_ref, k2_ref, v2_ref, k5_ref, v5_ref):
    mn = _rms(m_ref[...], g_ref[0]).astype(BF16)
    nb = k5_ref.shape[1]
    for w_ref, o2_ref, o5_ref in ((wk_ref, k2_ref, k5_ref), (wv_ref, v2_ref, v5_ref)):
        acc = jnp.dot(mn, w_ref[0], preferred_element_type=F32)
        o2_ref[0] = acc
        for bb in range(nb):
            for h in range(XA_HEADS):
                o5_ref[0, bb, :, h, :] = acc[bb * MEM_LEN:(bb + 1) * MEM_LEN, h * XA_HEAD_DIM:(h + 1) * XA_HEAD_DIM]


def _mem_kv(mem, gains, w_k, w_v, *, bb):
    b, m, d = mem.shape
    depth = w_k.shape[0]
    assert b % bb == 0
    tm = bb * m
    w_spec = pl.BlockSpec((1, d, d), lambda l, i: (l, 0, 0))
    o2_spec = pl.BlockSpec((1, tm, d), lambda l, i: (l, i, 0))
    o5_spec = pl.BlockSpec((1, bb, m, XA_HEADS, XA_HEAD_DIM), lambda l, i: (l, i, 0, 0, 0))
    o2_shape = jax.ShapeDtypeStruct((depth, b * m, d), F32)
    o5_shape = jax.ShapeDtypeStruct((depth, b, m, XA_HEADS, XA_HEAD_DIM), F32)
    return pl.pallas_call(
        _mem_kv_kernel,
        grid=(depth, b // bb),
        in_specs=[pl.BlockSpec((tm, d), lambda l, i: (i, 0)), pl.BlockSpec((1, 1, d), lambda l, i: (l, 0, 0)), w_spec, w_spec],
        out_specs=[o2_spec, o2_spec, o5_spec, o5_spec],
        out_shape=[o2_shape, o2_shape, o5_shape, o5_shape],
        compiler_params=_params("parallel", "parallel"),
        name="mem_kv",
    )(mem.reshape(b * m, d), gains.reshape(depth, 1, d), w_k, w_v)


def _mlp_kernel(*refs, final_norm, tf):
    if final_norm:
        x_ref, g_ref, wu_ref, wd_ref, gf_ref, o_ref = refs
    else:
        x_ref, g_ref, wu_ref, wd_ref, o_ref = refs
    x = x_ref[...]
    xn = _rms(x, g_ref[...]).astype(BF16)
    y = x
    for f0 in range(0, wu_ref.shape[1], tf):
        h = jnp.dot(xn, wu_ref[:, f0:f0 + tf], preferred_element_type=F32)
        h = jnp.square(jnp.maximum(h, 0.0)).astype(BF16)
        y = y + jnp.dot(h, wd_ref[f0:f0 + tf, :], preferred_element_type=F32)
    if final_norm:
        y = _rms(y, gf_ref[...])
    o_ref[...] = y


def _mlp(x, gain, w_up, w_down, layer, *, final_gain=None, tm, tf):
    m, d = x.shape
    f = w_up.shape[2]
    assert m % tm == 0 and f % tf == 0
    resident = dict(pipeline_mode=pl.Buffered(1))
    in_specs = [
        pl.BlockSpec((tm, d), lambda i: (i, 0)),
        pl.BlockSpec((1, d), lambda i: (0, 0)),
        pl.BlockSpec((None, d, f), lambda i: (layer, 0, 0), **resident),
        pl.BlockSpec((None, f, d), lambda i: (layer, 0, 0), **resident),
    ]
    args = [x, gain.reshape(1, d), w_up, w_down]
    if final_gain is not None:
        in_specs.append(pl.BlockSpec((1, d), lambda i: (0, 0)))
        args.append(final_gain.reshape(1, d))
    return pl.pallas_call(
        functools.partial(_mlp_kernel, final_norm=final_gain is not None, tf=tf),
        grid=(m // tm,),
        in_specs=in_specs,
        out_specs=pl.BlockSpec((tm, d), lambda i: (i, 0)),
        out_shape=jax.ShapeDtypeStruct((m, d), F32),
        compiler_params=_params("parallel"),
        name="mlp",
    )(*args)


def _xattn_prompt_kernel(x_ref, g_ref, k_ref, v_ref, wq_ref, wo_ref, o_ref, att_ref, *, tm, ts):
    for r0 in range(0, tm, ts):
        sub = slice(r0, r0 + ts)
        x = x_ref[0, sub, :]
        xn = _rms(x, g_ref[...]).astype(BF16)
        q = jnp.dot(xn, wq_ref[...], preferred_element_type=F32) * (XA_HEAD_DIM ** -0.5)
        for h in range(XA_HEADS):
            cols = slice(h * XA_HEAD_DIM, (h + 1) * XA_HEAD_DIM)
            s = _dot_nt(q[:, cols], k_ref[0, :, cols])
            s = s - jnp.max(s, axis=-1, keepdims=True)
            p = jnp.exp(s)
            p = p / jnp.sum(p, axis=-1, keepdims=True)
            att_ref[sub, cols] = _dot(p, v_ref[0, :, cols]).astype(BF16)
        o_ref[0, sub, :] = x + jnp.dot(att_ref[sub, :], wo_ref[...], preferred_element_type=F32)


def _xattn_prompt(x, gain, mem_k, mem_v, layer, w_q, w_o, *, tm, ts):
    b, l, d = x.shape
    assert l % tm == 0 and tm % ts == 0
    kv_spec = pl.BlockSpec((1, MEM_LEN, d), lambda bi, i: (layer, bi, 0))
    w_spec = pl.BlockSpec((d, d), lambda bi, i: (0, 0), pipeline_mode=pl.Buffered(1))
    return pl.pallas_call(
        functools.partial(_xattn_prompt_kernel, tm=tm, ts=ts),
        grid=(b, l // tm),
        in_specs=[
            pl.BlockSpec((1, tm, d), lambda bi, i: (bi, i, 0)),
            pl.BlockSpec((1, d), lambda bi, i: (0, 0)),
            kv_spec, kv_spec, w_spec, w_spec,
        ],
        out_specs=pl.BlockSpec((1, tm, d), lambda bi, i: (bi, i, 0)),
        out_shape=jax.ShapeDtypeStruct((b, l, d), F32),
        scratch_shapes=[pltpu.VMEM((tm, d), BF16)],
        compiler_params=_params("parallel", "parallel"),
        name="xattn_prompt",
    )(x, gain.reshape(1, d), mem_k, mem_v, w_q, w_o)


XA_TILES = XA_HEAD_DIM // LANES


def _xattn_sample_kernel(q_ref, k_ref, v_ref, o_ref, *, bs):
    for b in range(bs):
        q = q_ref[b] * (XA_HEAD_DIM ** -0.5)
        prod = k_ref[0, b] * q[None]
        prod = prod + pltpu.roll(prod, XA_HEADS, 1)
        s = jnp.sum(prod, axis=-1, keepdims=True)
        p = jnp.exp(s - jnp.max(s, axis=0, keepdims=True))
        den = jnp.sum(p, axis=0)
        o_ref[b] = jnp.sum(p * v_ref[0, b], axis=0) / den


def _head_tile_view(x):
    lead = x.shape[:-2]
    n = len(lead)
    x = x.reshape(lead + (XA_HEADS, XA_TILES, LANES))
    return x.transpose(tuple(range(n)) + (n + 1, n, n + 2)).reshape(lead + (XA_TILES * XA_HEADS, LANES))


def _xattn_sample_core(q, cache_k, cache_v, layer, *, bs):
    b = q.shape[0]
    assert b % bs == 0 and XA_TILES == 2 and XA_TILES * XA_HEADS == F32_SUBLANES
    rows = XA_TILES * XA_HEADS
    q_spec = pl.BlockSpec((bs, rows, LANES), lambda i: (i, 0, 0))
    kv_spec = pl.BlockSpec((1, bs, MEM_LEN, rows, LANES), lambda i: (layer, i, 0, 0, 0))
    out = pl.pallas_call(
        functools.partial(_xattn_sample_kernel, bs=bs),
        grid=(b // bs,),
        in_specs=[q_spec, kv_spec, kv_spec],
        out_specs=q_spec,
        out_shape=jax.ShapeDtypeStruct((b, rows, LANES), F32),
        compiler_params=_params("parallel"),
        name="xattn_sample",
    )(_head_tile_view(q.reshape(b, XA_HEADS, XA_HEAD_DIM)), _head_tile_view(cache_k), _head_tile_view(cache_v))
    return out.reshape(b, XA_TILES, XA_HEADS, LANES).transpose(0, 2, 1, 3).reshape(b, XA_HEADS * XA_HEAD_DIM)


HALO = 16


def _pool_windows(z_ref, pos, w_ref, scale_ref, o_ref, tm):
    for gi, win in enumerate(POOL_WINDOWS):
        cols = slice(gi * POOL_GROUP_WIDTH, (gi + 1) * POOL_GROUP_WIDTH)
        tok = z_ref[HALO:HALO + tm, cols]
        acc = tok
        for k in range(1, win):
            acc = acc + z_ref[HALO - k:HALO - k + tm, cols]
        count = jnp.minimum(pos + 1, win).astype(F32)
        dlt = acc / count - tok
        o_ref[0, :, cols] = (_dot(dlt, w_ref[gi]) * scale_ref[:, cols]).astype(o_ref.dtype)


def _pool_sample_kernel(z_ref, buf_ref, w_ref, scale_ref, o_ref):
    for gi, win in enumerate(POOL_WINDOWS):
        cols = slice(gi * POOL_GROUP_WIDTH, (gi + 1) * POOL_GROUP_WIDTH)
        tok = z_ref[:, cols]
        acc = tok
        for k in range(1, win):
            acc = acc + buf_ref[:, POOL_BUF - k, cols]
        count = float(min(PAST_LEN + 1, win))
        dlt = acc / count - tok
        o_ref[:, cols] = (_dot(dlt, w_ref[gi]) * scale_ref[:, cols]).astype(o_ref.dtype)


def _pool_sample(z, buf, w_group, scale):
    b = z.shape[0]
    return pl.pallas_call(
        _pool_sample_kernel,
        grid=(1,),
        in_specs=[
            pl.BlockSpec((b, POOL_WIDTH), lambda i: (0, 0)),
            pl.BlockSpec((b, POOL_BUF, POOL_WIDTH), lambda i: (0, 0, 0)),
            pl.BlockSpec((len(POOL_WINDOWS), POOL_GROUP_WIDTH, POOL_GROUP_WIDTH), lambda i: (0, 0, 0)),
            pl.BlockSpec((1, POOL_WIDTH), lambda i: (0, 0)),
        ],
        out_specs=pl.BlockSpec((b, POOL_WIDTH), lambda i: (0, 0)),
        out_shape=jax.ShapeDtypeStruct((b, POOL_WIDTH), BF16),
        compiler_params=_params("arbitrary"),
        name="pool_sample",
    )(z, buf, w_group, scale.reshape(1, POOL_WIDTH))


def _rwkv_prep_math(z4, s4, p, out_refs, sl):
    zr, zk, zv, zl = z4
    sr, sk, sv, sq = s4
    mu_r, mu_k, mu_v, mu_l, w0, a0, k_k, k_a, r_k, w_la, w_g = p
    r = zr + (sr - zr) * mu_r
    k = zk + (sk - zk) * mu_k
    v = zv + (sv - zv) * mu_v
    lo = zl + (sq - zl) * mu_l
    wal = lo[:, :W_LORA + A_LORA]
    lane = lax.broadcasted_iota(jnp.int32, wal.shape, 1)
    feat = jnp.where(lane < W_LORA, 2.0 * jax.nn.sigmoid(2.0 * wal) - 1.0, wal)
    wa = _dot(feat, w_la)
    g = _dot(jax.nn.sigmoid(lo[:, W_LORA + A_LORA:]), w_g)
    decay = jnp.exp(-DECAY_SCALE * jax.nn.sigmoid(w0 + wa[:, :RWKV_WIDTH]))
    a = jax.nn.sigmoid(a0 + wa[:, RWKV_WIDTH:])
    kk = k * k_k
    k2 = k * (1.0 + (a - 1.0) * k_a)
    ones = _head_ones(RWKV_WIDTH, RWKV_HEAD)
    kk = kk * jnp.minimum(lax.rsqrt(_head_sum(kk * kk, ones)), 1.0 / L2_EPS)
    bonus = _head_sum(r * k2 * r_k, ones, split=False) * v
    r_o, k_o, v_o, kap_o, b_o, d_o, g_o, bonus_o = out_refs
    r_o[sl] = r
    k_o[sl] = k2
    v_o[sl] = v
    kap_o[sl] = kk
    b_o[sl] = kk * a
    d_o[sl] = decay
    g_o[sl] = g
    bonus_o[sl] = bonus


def _load_params(refs):
    return tuple(r[...] for r in refs)


AB_IN = POOL_WIDTH + RWKV_IN
LORA_COL = POOL_WIDTH + 3 * RWKV_WIDTH


def _ab_front_kernel(x_ref, g_ref, w_ref, pw_ref, ps_ref, *refs, tm):
    p_refs = refs[:11]
    pool_ref = refs[11]
    out_refs = refs[12:20]
    pool_tail_ref, shift_tail_ref, z_ref = refs[20:23]
    t = pl.program_id(1)

    @pl.when(t == 0)
    def _():
        z_ref[0:HALO, :] = jnp.zeros((HALO, AB_IN), F32)

    @pl.when(t > 0)
    def _():
        z_ref[0:HALO, :] = z_ref[tm:tm + HALO, :]

    xn = _rms(x_ref[0], g_ref[...]).astype(BF16)
    cw = 256
    for c0 in range(0, AB_IN, cw):
        z_ref[HALO:HALO + tm, c0:c0 + cw] = jnp.dot(xn, w_ref[:, c0:c0 + cw], preferred_element_type=F32)

    pos = t * tm + lax.broadcasted_iota(jnp.int32, (tm, 1), 0)
    _pool_windows(z_ref, pos, pw_ref, ps_ref, pool_ref, tm)

    col = lambda c0, width, r0: z_ref[r0:r0 + tm, c0:c0 + width]
    blocks = [(POOL_WIDTH + k * RWKV_WIDTH, RWKV_WIDTH) for k in range(3)] + [(LORA_COL, LORA_IN)]
    z4 = [col(c0, width, HALO) for c0, width in blocks]
    s4 = [col(c0, width, HALO - 1) for c0, width in blocks]
    _rwkv_prep_math(z4, s4, _load_params(p_refs), out_refs, (0,))

    pool_tail_ref[0] = z_ref[tm:tm + HALO, 0:POOL_WIDTH]
    shift_tail_ref[0] = z_ref[tm + HALO - F32_SUBLANES:tm + HALO, POOL_WIDTH:]


def _prep_param_args(mu, w0, a0, k_k, k_a, r_k, w_la, w_g):
    w = RWKV_WIDTH
    row = lambda x: x.reshape(1, -1)
    return [row(mu[:w]), row(mu[w:2 * w]), row(mu[2 * w:3 * w]), row(mu[3 * w:]),
            row(w0), row(a0), row(k_k), row(k_a), row(r_k), w_la, w_g]


def _prep_out_shapes(lead):
    return [jax.ShapeDtypeStruct(lead + (RWKV_WIDTH,), F32) for _ in range(8)]


def _ab_front_prompt(x, gain, w_in, pool_w, pool_scale, params, *, tm):
    b, l, d = x.shape
    assert l % tm == 0 and tm % HALO == 0
    full = lambda a: pl.BlockSpec(a.shape, lambda bi, i: (0,) * a.ndim)
    resident = dict(pipeline_mode=pl.Buffered(1))
    in_specs = [
        pl.BlockSpec((1, tm, d), lambda bi, i: (bi, i, 0)),
        pl.BlockSpec((1, d), lambda bi, i: (0, 0)),
        pl.BlockSpec((d, AB_IN), lambda bi, i: (0, 0), **resident),
        full(pool_w),
        pl.BlockSpec((1, POOL_WIDTH), lambda bi, i: (0, 0)),
    ] + [full(a) for a in params]
    tok_spec = pl.BlockSpec((1, tm, RWKV_WIDTH), lambda bi, i: (bi, i, 0))
    out_specs = [tok_spec] * 9 + [pl.BlockSpec((1, HALO, POOL_WIDTH), lambda bi, i: (bi, 0, 0)),
                                  pl.BlockSpec((1, F32_SUBLANES, RWKV_IN), lambda bi, i: (bi, 0, 0))]
    out_shape = ([jax.ShapeDtypeStruct((b, l, POOL_WIDTH), BF16)] + _prep_out_shapes((b, l))
                 + [jax.ShapeDtypeStruct((b, HALO, POOL_WIDTH), F32), jax.ShapeDtypeStruct((b, F32_SUBLANES, RWKV_IN), F32)])
    return pl.pallas_call(
        functools.partial(_ab_front_kernel, tm=tm),
        grid=(b, l // tm),
        in_specs=in_specs,
        out_specs=out_specs,
        out_shape=out_shape,
        scratch_shapes=[pltpu.VMEM((tm + HALO, AB_IN), F32)],
        compiler_params=_params("parallel", "arbitrary"),
        name="ab_front_prompt",
    )(x, gain.reshape(1, d), w_in, pool_w, pool_scale.reshape(1, POOL_WIDTH), *params)


def _rwkv_prep_sample_kernel(zr_ref, zk_ref, zv_ref, zl_ref, sr_ref, sk_ref, sv_ref, sl_ref, *refs):
    p_refs = refs[:11]
    out_refs = refs[11:19]
    z4 = [zr_ref[...], zk_ref[...], zv_ref[...], zl_ref[...]]
    s4 = [sr_ref[...], sk_ref[...], sv_ref[...], sl_ref[...]]
    _rwkv_prep_math(z4, s4, _load_params(p_refs), out_refs, (Ellipsis,))


def _rwkv_prep_sample(z, shift, params):
    b = z.shape[0]
    pw = POOL_WIDTH // RWKV_WIDTH
    full = lambda a: pl.BlockSpec(a.shape, lambda i: (0,) * a.ndim)
    in_specs = [
        pl.BlockSpec((b, RWKV_WIDTH), lambda i: (0, pw)),
        pl.BlockSpec((b, RWKV_WIDTH), lambda i: (0, pw + 1)),
        pl.BlockSpec((b, RWKV_WIDTH), lambda i: (0, pw + 2)),
        pl.BlockSpec((b, LORA_IN), lambda i: (0, (POOL_WIDTH + 3 * RWKV_WIDTH) // LORA_IN)),
        pl.BlockSpec((b, RWKV_WIDTH), lambda i: (0, 0)),
        pl.BlockSpec((b, RWKV_WIDTH), lambda i: (0, 1)),
        pl.BlockSpec((b, RWKV_WIDTH), lambda i: (0, 2)),
        pl.BlockSpec((b, LORA_IN), lambda i: (0, 3 * RWKV_WIDTH // LORA_IN)),
    ] + [full(a) for a in params]
    out_spec = pl.BlockSpec((b, RWKV_WIDTH), lambda i: (0, 0))
    return pl.pallas_call(
        _rwkv_prep_sample_kernel,
        grid=(1,),
        in_specs=in_specs,
        out_specs=[out_spec] * 8,
        out_shape=_prep_out_shapes((b,)),
        compiler_params=_params("arbitrary"),
        name="rwkv_prep_sample",
    )(z, z, z, z, shift, shift, shift, shift, *params)


def _rwkv_scan_kernel(kap_ref, d_ref, b_ref, k_ref, r_ref, v_ref, s0_ref, y_ref, sout_ref, s_ref, *, tt, nib, vp):
    tb = pl.program_id(1)
    sub = F32_SUBLANES

    @pl.when(tb == 0)
    def _():
        s_ref[...] = s0_ref[0]

    for i in range(nib * sub if vp > tt else 0):
        y_ref[0, i * vp + tt:(i + 1) * vp, :] = jnp.zeros((vp - tt, LANES), F32)

    nparts = 2
    look_ahead = vp > tt

    def add_to(acc, slot, term):
        acc[slot] = term if acc[slot] is None else acc[slot] + term

    def block_sums(acc):
        return tuple(acc[nparts * ib] + acc[nparts * ib + 1] for ib in range(nib))

    def first_sa():
        acc = [None] * (nparts * nib)
        for j in range(RWKV_HEAD):
            kap = kap_ref[0, pl.ds(j * vp, 1), :]
            for ib in range(nib):
                add_to(acc, nparts * ib + j % nparts, s_ref[ib, j] * kap)
        return tuple(-x for x in block_sums(acc))

    def step(t, sa):
        vy_rows = [pl.ds(ib * sub * vp + t, sub, stride=vp) if vp > 1 else pl.ds(ib * sub, sub) for ib in range(nib)]
        v = [v_ref[0, rows, :] for rows in vy_rows]
        acc_y = [None] * (nparts * nib)
        acc_s = [None] * (nparts * nib)
        for j in range(RWKV_HEAD):
            row = pl.ds(j * vp + t, 1)
            dec = d_ref[0, row, :]
            bb = b_ref[0, row, :]
            kk = k_ref[0, row, :]
            rr = r_ref[0, row, :]
            kap_next = kap_ref[0, pl.ds(j * vp + t + 1, 1), :] if look_ahead else None
            for ib in range(nib):
                s = s_ref[ib, j] * dec + sa[ib] * bb + v[ib] * kk
                s_ref[ib, j] = s
                add_to(acc_y, nparts * ib + j % nparts, s * rr)
                if look_ahead:
                    add_to(acc_s, nparts * ib + j % nparts, s * kap_next)
        for ib, y in enumerate(block_sums(acc_y)):
            y_ref[0, vy_rows[ib], :] = y
        return tuple(-x for x in block_sums(acc_s)) if look_ahead else sa

    lax.fori_loop(0, tt, step, first_sa())

    @pl.when(tb == pl.num_programs(1) - 1)
    def _():
        sout_ref[0] = s_ref[...]


def _rwkv_scan(kap, dec, bb, kk, rr, v, s0, *, tt, vp):
    g, nb, rows, _ = kap.shape
    n = RWKV_HEAD
    nib = s0.shape[1]
    assert rows == n * vp and v.shape[2] == nib * F32_SUBLANES * vp and tt <= vp
    op_spec = pl.BlockSpec((None, 1, n * vp, LANES), lambda gi, i: (gi, i, 0, 0))
    v_spec = pl.BlockSpec((None, 1, v.shape[2], LANES), lambda gi, i: (gi, i, 0, 0))
    s_spec = pl.BlockSpec((1, nib, n, F32_SUBLANES, LANES), lambda gi, i: (gi, 0, 0, 0, 0))
    return pl.pallas_call(
        functools.partial(_rwkv_scan_kernel, tt=tt, nib=nib, vp=vp),
        grid=(g, nb),
        in_specs=[op_spec] * 5 + [v_spec, s_spec],
        out_specs=[v_spec, s_spec],
        out_shape=[jax.ShapeDtypeStruct(v.shape, F32), jax.ShapeDtypeStruct(s0.shape, F32)],
        scratch_shapes=[pltpu.VMEM((nib, n, F32_SUBLANES, LANES), F32)],
        compiler_params=_params("parallel", "arbitrary"),
        name="rwkv_scan",
    )(kap, dec, bb, kk, rr, v, s0)


def _rwkv_post_kernel(y_ref, bonus_ref, g_ref, gng_ref, gnb_ref, o_ref):
    y = y_ref[...]
    ones = _head_ones(RWKV_WIDTH, RWKV_HEAD)
    m = _head_sum(y, ones, split=False) * (1.0 / RWKV_HEAD)
    c = y - m
    var = _head_sum(c * c, ones, split=False) * (1.0 / RWKV_HEAD)
    yn = c * lax.rsqrt(var + GN_EPS) * gng_ref[...] + gnb_ref[...]
    o_ref[...] = ((yn + bonus_ref[...]) * g_ref[...]).astype(o_ref.dtype)


def _rwkv_post(y, bonus, g, gn_g, gn_b, *, tm):
    m = y.shape[0]
    assert m % tm == 0
    spec = pl.BlockSpec((tm, RWKV_WIDTH), lambda i: (i, 0))
    pspec = pl.BlockSpec((1, RWKV_WIDTH), lambda i: (0, 0))
    return pl.pallas_call(
        _rwkv_post_kernel,
        grid=(m // tm,),
        in_specs=[spec, spec, spec, pspec, pspec],
        out_specs=spec,
        out_shape=jax.ShapeDtypeStruct((m, RWKV_WIDTH), BF16),
        compiler_params=_params("parallel"),
        name="rwkv_post",
    )(y, bonus, g, gn_g.reshape(1, -1), gn_b.reshape(1, -1))


def _layernorm(v, g, b):
    m = jnp.mean(v, axis=-1, keepdims=True)
    c = v - m
    var = jnp.mean(c * c, axis=-1, keepdims=True)
    return c * lax.rsqrt(var + LN_EPS) * g + b


def _sgu_prompt_kernel(x_ref, g_ref, wi_ref, lng_ref, lnb_ref, ws_ref, bias_ref, wo_ref, o_ref, zc_ref, vn_ref, gate_ref,
                       *, tm, ts):
    xn = _rms(x_ref[...], g_ref[...]).astype(BF16)
    cw = 256
    for c0 in list(range(SGU_WIDTH, 2 * SGU_WIDTH, cw)) + list(range(0, SGU_WIDTH, cw)):
        acc = jnp.dot(xn, wi_ref[:, c0:c0 + cw], preferred_element_type=F32)
        zc_ref[:, c0:c0 + cw] = (0.5 * acc * (1.0 + lax.erf(acc * 0.7071067811865476))).astype(BF16)
    gw = SGU_WIDTH // SGU_GROUPS
    r = lax.broadcasted_iota(jnp.int32, (CHUNK, CHUNK), 0)
    c = lax.broadcasted_iota(jnp.int32, (CHUNK, CHUNK), 1)
    wms = [jnp.where(r >= c, ws_ref[gi], 0.0).astype(BF16) for gi in range(SGU_GROUPS)]
    for r0 in range(0, tm, ts):
        sub = slice(r0, r0 + ts)
        v = zc_ref[sub, SGU_WIDTH:].astype(F32)
        vn_ref[sub, :] = _layernorm(v, lng_ref[...], lnb_ref[...]).astype(BF16)
        for gi in range(SGU_GROUPS):
            cols = slice(gi * gw, (gi + 1) * gw)
            for ci in range(r0 // CHUNK, (r0 + ts) // CHUNK):
                rows = slice(ci * CHUNK, (ci + 1) * CHUNK)
                sp = jnp.dot(wms[gi], vn_ref[rows, cols], preferred_element_type=F32) + bias_ref[:, cols]
                gate_ref[rows, cols] = (zc_ref[rows, cols].astype(F32) * sp).astype(BF16)
        o_ref[sub, :] = x_ref[sub, :] + jnp.dot(gate_ref[sub, :], wo_ref[...], preferred_element_type=F32)


def _sgu_prompt(x, gain, w_in, ln_g, ln_b, w_s, bias, w_out, *, tm, ts):
    t, d = x.shape
    assert t % tm == 0 and tm % ts == 0 and ts % CHUNK == 0
    row = lambda a: a.reshape(1, -1)
    resident = dict(pipeline_mode=pl.Buffered(1))
    return pl.pallas_call(
        functools.partial(_sgu_prompt_kernel, tm=tm, ts=ts),
        grid=(t // tm,),
        in_specs=[
            pl.BlockSpec((tm, d), lambda i: (i, 0)),
            pl.BlockSpec((1, d), lambda i: (0, 0)),
            pl.BlockSpec((d, 2 * SGU_WIDTH), lambda i: (0, 0), **resident),
            pl.BlockSpec((1, SGU_WIDTH), lambda i: (0, 0)),
            pl.BlockSpec((1, SGU_WIDTH), lambda i: (0, 0)),
            pl.BlockSpec((SGU_GROUPS, CHUNK, CHUNK), lambda i: (0, 0, 0)),
            pl.BlockSpec((CHUNK, SGU_WIDTH), lambda i: (0, 0), **resident),
            pl.BlockSpec((SGU_WIDTH, d), lambda i: (0, 0), **resident),
        ],
        out_specs=pl.BlockSpec((tm, d), lambda i: (i, 0)),
        out_shape=jax.ShapeDtypeStruct((t, d), F32),
        scratch_shapes=[pltpu.VMEM((tm, 2 * SGU_WIDTH), BF16), pltpu.VMEM((tm, SGU_WIDTH), BF16),
                        pltpu.VMEM((tm, SGU_WIDTH), BF16)],
        compiler_params=_params("parallel"),
        name="sgu_prompt",
    )(x, row(gain), w_in, row(ln_g), row(ln_b), w_s, bias, w_out)


def _sgu_sample_kernel(u_ref, v_ref, lng_ref, lnb_ref, coef_ref, bias_ref, gate_ref, vn_ref):
    vn = _layernorm(v_ref[...], lng_ref[...], lnb_ref[...])
    vn_ref[...] = vn
    gate_ref[...] = (u_ref[...] * (vn * coef_ref[...] + bias_ref[...])).astype(gate_ref.dtype)


def _sgu_sample(zc, ln_g, ln_b, coef, bias):
    b = zc.shape[0]
    row = lambda a: a.reshape(1, -1)
    pspec = pl.BlockSpec((1, SGU_WIDTH), lambda i: (0, 0))
    return pl.pallas_call(
        _sgu_sample_kernel,
        grid=(1,),
        in_specs=[pl.BlockSpec((b, SGU_WIDTH), lambda i: (0, 0)), pl.BlockSpec((b, SGU_WIDTH), lambda i: (0, 1)),
                  pspec, pspec, pspec, pspec],
        out_specs=[pl.BlockSpec((b, SGU_WIDTH), lambda i: (0, 0))] * 2,
        out_shape=[jax.ShapeDtypeStruct((b, SGU_WIDTH), BF16), jax.ShapeDtypeStruct((b, SGU_WIDTH), F32)],
        compiler_params=_params("arbitrary"),
        name="sgu_sample",
    )(zc, zc, row(ln_g), row(ln_b), row(coef), row(bias))


SCAN_TT = 64
RELAYOUT_BLOCKS = 4
RELAYOUT_T = RELAYOUT_BLOCKS * SCAN_TT
V_PITCH = 72
HALF = RWKV_HEAD // 2


def _split_heads(z_ref, b, x_ref, c):
    xt = x_ref[b, c * LANES:(c + 1) * LANES, :].T
    for h in range(RWKV_HEADS):
        z_ref[b, h * V_PITCH:h * V_PITCH + RWKV_HEAD, :] = xt[h * RWKV_HEAD:(h + 1) * RWKV_HEAD]


def _head_rows(z_ref, b, n):
    return z_ref[b, pl.ds(n, RWKV_HEADS, stride=V_PITCH), :]


def _store_slab(o_ref, c, n, wt):
    pad = jnp.zeros((V_PITCH - SCAN_TT, LANES), F32)
    for k in range(LANES // SCAN_TT):
        blk = c * (LANES // SCAN_TT) + k
        o_ref[blk, n * V_PITCH:n * V_PITCH + SCAN_TT, :] = wt[k * SCAN_TT:(k + 1) * SCAN_TT]
        o_ref[blk, n * V_PITCH + SCAN_TT:(n + 1) * V_PITCH, :] = pad


def _key_to_lanes_kernel(x_ref, o_ref, z_ref):
    nb = x_ref.shape[0]
    for c in range(RELAYOUT_T // LANES):
        for b in range(nb):
            _split_heads(z_ref, b, x_ref, c)
        for j in range(RWKV_HEAD):
            rows = [_head_rows(z_ref, b, j) for b in range(nb)]
            _store_slab(o_ref, c, j, jnp.concatenate(rows + rows, axis=0).T)


def _value_to_lanes_kernel(x_ref, o_ref, z_ref):
    nb = x_ref.shape[0]
    for c in range(RELAYOUT_T // LANES):
        for b in range(nb):
            _split_heads(z_ref, b, x_ref, c)
        for i in range(HALF):
            rows = [_head_rows(z_ref, b, half * HALF + i) for half in range(2) for b in range(nb)]
            _store_slab(o_ref, c, i, jnp.concatenate(rows, axis=0).T)


def _ab_back_kernel(y_ref, bonus_ref, g_ref, pool_ref, x_ref, gng_ref, gnb_ref, w_ref, o_ref, z_ref, lhs_ref):
    nb = o_ref.shape[0]
    per = LANES // SCAN_TT
    for i in range(HALF):
        w = jnp.concatenate([y_ref[k, i * V_PITCH:i * V_PITCH + SCAN_TT, :] for k in range(per)], axis=0)
        wt = w.T
        for half in range(2):
            for b in range(nb):
                r0 = (half * nb + b) * RWKV_HEADS
                z_ref[b, pl.ds(half * HALF + i, RWKV_HEADS, stride=V_PITCH), :] = wt[r0:r0 + RWKV_HEADS]
    ones = _head_ones(RWKV_WIDTH, RWKV_HEAD)
    for b in range(nb):
        zt = jnp.concatenate([z_ref[b, h * V_PITCH:h * V_PITCH + RWKV_HEAD, :] for h in range(RWKV_HEADS)], axis=0)
        y = zt.T
        m = _head_sum(y, ones, split=False) * (1.0 / RWKV_HEAD)
        c = y - m
        var = _head_sum(c * c, ones, split=False) * (1.0 / RWKV_HEAD)
        yn = c * lax.rsqrt(var + GN_EPS) * gng_ref[...] + gnb_ref[...]
        rows = slice(b * LANES, (b + 1) * LANES)
        lhs_ref[rows, :POOL_WIDTH] = pool_ref[b]
        lhs_ref[rows, POOL_WIDTH:] = ((yn + bonus_ref[b]) * g_ref[b]).astype(BF16)
    mix = jnp.dot(lhs_ref[...], w_ref[...], preferred_element_type=F32)
    for b in range(nb):
        o_ref[b] = x_ref[b] + mix[b * LANES:(b + 1) * LANES]


def _key_to_lanes(x):
    b, l, w = x.shape
    assert 2 * b * RWKV_HEADS == LANES and l % RELAYOUT_T == 0 and w == RWKV_WIDTH
    out = pl.pallas_call(
        _key_to_lanes_kernel,
        grid=(l // RELAYOUT_T,),
        in_specs=[pl.BlockSpec((b, RELAYOUT_T, w), lambda i: (0, i, 0))],
        out_specs=pl.BlockSpec((RELAYOUT_BLOCKS, RWKV_HEAD * V_PITCH, LANES), lambda i: (i, 0, 0)),
        out_shape=jax.ShapeDtypeStruct((l // SCAN_TT, RWKV_HEAD * V_PITCH, LANES), F32),
        scratch_shapes=[pltpu.VMEM((b, RWKV_HEADS * V_PITCH, LANES), F32)],
        compiler_params=_params("parallel"),
        name="key_to_lanes",
    )(x)
    return out[None]


def _value_to_lanes(v):
    b, l, w = v.shape
    assert 2 * b * RWKV_HEADS == LANES and l % RELAYOUT_T == 0 and w == RWKV_WIDTH
    out = pl.pallas_call(
        _value_to_lanes_kernel,
        grid=(l // RELAYOUT_T,),
        in_specs=[pl.BlockSpec((b, RELAYOUT_T, w), lambda i: (0, i, 0))],
        out_specs=pl.BlockSpec((RELAYOUT_BLOCKS, HALF * V_PITCH, LANES), lambda i: (i, 0, 0)),
        out_shape=jax.ShapeDtypeStruct((l // SCAN_TT, HALF * V_PITCH, LANES), F32),
        scratch_shapes=[pltpu.VMEM((b, RWKV_HEADS * V_PITCH, LANES), F32)],
        compiler_params=_params("parallel"),
        name="value_to_lanes",
    )(v)
    return out[None]


def _ab_back_prompt(y, bonus, g, y_pool, x, gn_g, gn_b, w_out):
    b, l, d = x.shape
    per = LANES // SCAN_TT
    tok = lambda width: pl.BlockSpec((b, LANES, width), lambda i: (0, i, 0))
    pspec = pl.BlockSpec((1, RWKV_WIDTH), lambda i: (0, 0))
    return pl.pallas_call(
        _ab_back_kernel,
        grid=(l // LANES,),
        in_specs=[pl.BlockSpec((per, HALF * V_PITCH, LANES), lambda i: (i, 0, 0)),
                  tok(RWKV_WIDTH), tok(RWKV_WIDTH), tok(POOL_WIDTH), tok(d), pspec, pspec,
                  pl.BlockSpec(w_out.shape, lambda i: (0, 0), pipeline_mode=pl.Buffered(1))],
        out_specs=tok(d),
        out_shape=jax.ShapeDtypeStruct((b, l, d), F32),
        scratch_shapes=[pltpu.VMEM((b, RWKV_HEADS * V_PITCH, LANES), F32),
                        pltpu.VMEM((b * LANES, POOL_WIDTH + RWKV_WIDTH), BF16)],
        compiler_params=_params("parallel"),
        name="ab_back_prompt",
    )(y[0], bonus, g, y_pool, x, gn_g.reshape(1, -1), gn_b.reshape(1, -1), w_out)


def _state_from_lanes_prompt(s, b):
    s = s.reshape(HALF // F32_SUBLANES, RWKV_HEAD, F32_SUBLANES, 2, b, RWKV_HEADS).transpose(4, 5, 3, 0, 2, 1)
    return s.reshape(b, RWKV_HEADS, RWKV_HEAD, RWKV_HEAD)


def _to_lanes_sample(x):
    b = x.shape[0]
    g = b * RWKV_HEADS // LANES
    x = x.reshape(g, LANES // RWKV_HEADS, RWKV_HEADS, RWKV_HEAD).transpose(0, 3, 1, 2)
    return x.reshape(g, 1, RWKV_HEAD, LANES)


def _from_lanes_sample(y):
    g = y.shape[0]
    y = y.reshape(g, RWKV_HEAD, LANES // RWKV_HEADS, RWKV_HEADS).transpose(0, 2, 3, 1)
    return y.reshape(g * LANES // RWKV_HEADS, RWKV_WIDTH)


def _state_to_lanes_sample(s):
    b = s.shape[0]
    g = b * RWKV_HEADS // LANES
    nib = RWKV_HEAD // F32_SUBLANES
    s = s.reshape(g, LANES // RWKV_HEADS, RWKV_HEADS, nib, F32_SUBLANES, RWKV_HEAD).transpose(0, 3, 5, 4, 1, 2)
    return s.reshape(g, nib, RWKV_HEAD, F32_SUBLANES, LANES)


def _state_from_lanes_sample(s):
    g, nib = s.shape[:2]
    s = s.reshape(g, nib, RWKV_HEAD, F32_SUBLANES, LANES // RWKV_HEADS, RWKV_HEADS).transpose(0, 4, 5, 1, 3, 2)
    return s.reshape(g * LANES // RWKV_HEADS, RWKV_HEADS, RWKV_HEAD, RWKV_HEAD)


def kernel(x_prompt, x_sample, mem_prompt, cache_mem_k, cache_mem_v, state_pool, state_shift, state_wkv, norm_mix_g, norm_xa_g, norm_mem_g, norm_ffn_g, norm_final_g, w_in_ab, w_out_ab, pool_w, pool_scale, rwkv_mu, rwkv_w0, rwkv_w2, rwkv_a0, rwkv_a2, rwkv_g2, rwkv_k_k, rwkv_k_a, rwkv_r_k, rwkv_gn_g, rwkv_gn_b, w_in_c, sgu_ln_g, sgu_ln_b, sgu_w_s, sgu_b_s, w_out_c, w_xq, w_xk, w_xv, w_xo, w_ff_up, w_ff_down):
    bp, lp, d = x_prompt.shape
    bs = x_sample.shape[0]
    tp = bp * lp
    bf = lambda w: w.astype(BF16)

    per_layer = lambda w: [bf(w[k]) for k in range(w.shape[0])]
    w_in_ab_b, w_out_ab_b, pool_w_b = per_layer(w_in_ab), per_layer(w_out_ab), per_layer(pool_w)
    w_in_c_b, w_out_c_b = per_layer(w_in_c), per_layer(w_out_c)
    w_xq_b, w_xo_b = per_layer(w_xq), per_layer(w_xo)
    w_xk_b, w_xv_b = bf(w_xk), bf(w_xv)
    w_up_b, w_down_b = bf(w_ff_up), bf(w_ff_down)

    mem_k_p, mem_v_p, mem_k_out, mem_v_out = _mem_kv(mem_prompt, norm_mem_g, w_xk_b, w_xv_b, bb=4)

    xp = x_prompt.reshape(tp, d)
    xs = x_sample.reshape(bs, d)
    pool_out_p, pool_out_s, shift_out_p, shift_out_s, wkv_out_p, wkv_out_s, sgu_v_s = [], [], [], [], [], [], []

    for l in range(DEPTH):
        j = l // 2
        if l % 2 == 0:
            w_la = jnp.zeros((W_LORA + A_LORA, 2 * RWKV_WIDTH), F32)
            w_la = w_la.at[:W_LORA, :RWKV_WIDTH].set(rwkv_w2[j]).at[W_LORA:, RWKV_WIDTH:].set(rwkv_a2[j])
            params = _prep_param_args(rwkv_mu[j], rwkv_w0[j], rwkv_a0[j], rwkv_k_k[j], rwkv_k_a[j],
                                      rwkv_r_k[j].reshape(-1), bf(w_la), bf(rwkv_g2[j]))
            y_pool, r, k2, v, kap, bb, dec, g, bonus, pool_tail, shift_tail = _ab_front_prompt(
                xp.reshape(bp, lp, d), norm_mix_g[l], w_in_ab_b[j], pool_w_b[j], pool_scale[j], params, tm=512)
            s0 = jnp.zeros((1, HALF // F32_SUBLANES, RWKV_HEAD, F32_SUBLANES, LANES), F32)
            y_l, s_l = _rwkv_scan(_key_to_lanes(kap), _key_to_lanes(dec), _key_to_lanes(bb), _key_to_lanes(k2),
                                  _key_to_lanes(r), _value_to_lanes(v), s0, tt=SCAN_TT, vp=V_PITCH)
            xp = _ab_back_prompt(y_l, bonus, g, y_pool, xp.reshape(bp, lp, d), rwkv_gn_g[j], rwkv_gn_b[j],
                                 w_out_ab_b[j]).reshape(tp, d)
            pool_out_p.append(pool_tail[:, HALO - POOL_BUF:])
            shift_out_p.append(shift_tail[:, F32_SUBLANES - 1])
            wkv_out_p.append(_state_from_lanes_prompt(s_l, bp))
            zs = _linear([xs], [w_in_ab_b[j]], gain=norm_mix_g[l], tm=bs, tn=256, name="ab_in_sample")
            y_pool = _pool_sample(zs, state_pool[j], pool_w_b[j], pool_scale[j])
            r, k2, v, kap, bb, dec, g, bonus = _rwkv_prep_sample(zs, state_shift[j], params)
            y_l, s_l = _rwkv_scan(_to_lanes_sample(kap), _to_lanes_sample(dec), _to_lanes_sample(bb),
                                  _to_lanes_sample(k2), _to_lanes_sample(r), _to_lanes_sample(v),
                                  _state_to_lanes_sample(state_wkv[j]), tt=1, vp=1)
            y_rwkv = _rwkv_post(_from_lanes_sample(y_l), bonus, g, rwkv_gn_g[j], rwkv_gn_b[j], tm=bs)
            xs = _linear([y_pool, y_rwkv], [w_out_ab_b[j][:POOL_WIDTH], w_out_ab_b[j][POOL_WIDTH:]], res=xs,
                         tm=bs, tn=512, name="ab_out_sample")
            pool_out_s.append(jnp.concatenate([state_pool[j][:, 1:], zs[:, None, :POOL_WIDTH]], axis=1))
            shift_out_s.append(zs[:, POOL_WIDTH:])
            wkv_out_s.append(_state_from_lanes_sample(s_l))
        else:
            gw = SGU_WIDTH // SGU_GROUPS
            bias = jnp.repeat(sgu_b_s[j].T, gw, axis=1)
            xp = _sgu_prompt(xp, norm_mix_g[l], w_in_c_b[j], sgu_ln_g[j], sgu_ln_b[j], sgu_w_s[j], bias, w_out_c_b[j],
                             tm=512, ts=128)
            zc = _linear([xs], [w_in_c_b[j]], gain=norm_mix_g[l], act="gelu", tm=bs, tn=512, name="sgu_in_sample")
            gate, vn = _sgu_sample(zc, sgu_ln_g[j], sgu_ln_b[j], jnp.repeat(sgu_w_s[j][:, 0, 0], gw),
                                   jnp.repeat(sgu_b_s[j][:, 0], gw))
            xs = _linear([gate], [w_out_c_b[j]], res=xs, tm=bs, tn=512, name="sgu_out_sample")
            sgu_v_s.append(vn.reshape(bs, 1, SGU_WIDTH))

        xp = _xattn_prompt(xp.reshape(bp, lp, d), norm_xa_g[l], mem_k_p, mem_v_p, l, w_xq_b[l], w_xo_b[l],
                           tm=1024, ts=512).reshape(tp, d)
        q = _linear([xs], [w_xq_b[l]], gain=norm_xa_g[l], tm=bs, tn=512, name="xattn_q_sample")
        att = _xattn_sample_core(q, cache_mem_k, cache_mem_v, l, bs=8)
        xs = _linear([att], [w_xo_b[l]], res=xs, tm=bs, tn=512, name="xattn_o_sample")

        fg = norm_final_g if l == DEPTH - 1 else None
        xp = _mlp(xp, norm_ffn_g[l], w_up_b, w_down_b, l, final_gain=fg, tm=1024, tf=1024)
        xs = _mlp(xs, norm_ffn_g[l], w_up_b, w_down_b, l, final_gain=fg, tm=bs, tf=512)

    return (xp.reshape(bp, lp, d),
            xs.reshape(bs, 1, d),
            mem_k_out,
            mem_v_out,
            jnp.stack(pool_out_p),
            jnp.stack(pool_out_s),
            jnp.stack(shift_out_p),
            jnp.stack(shift_out_s),
            jnp.stack(wkv_out_p),
            jnp.stack(wkv_out_s),
            jnp.stack(sgu_v_s))
```
